```python
import math
import jax
import jax.numpy as jnp
from jax import lax
import numpy as np

D_MODEL = 1024
BATCH = 4
SEQ = 4096
DEPTH = 4
DEC_BATCH = 128
DEC_SEQ = 1
PAST_LEN = 8192
PAGE_SIZE = 128

N_A = DEPTH // 2
N_B = DEPTH - N_A
MEM_TOKENS = 256
MEM_HEADS = 4
MEM_W = D_MODEL // 4
MEM_HD = MEM_W // MEM_HEADS
MIX_MAIN = D_MODEL - MEM_W
RET_HEADS = 6
RET_HD = MIX_MAIN // RET_HEADS
RET_CHUNK = 128
SWA_HEADS = 12
SWA_KV_HEADS = 4
SWA_HD = MIX_MAIN // SWA_HEADS
WINDOW = 128
D_FF = 4 * D_MODEL
EPS = 1e-6

kernel_name = 'yoco_retention_swa_sink_memory_decoder'

F32 = jnp.float32


def rmsnorm(x, g):
    xf = x.astype(F32)
    y = xf * lax.rsqrt(jnp.mean(xf * xf, axis=-1, keepdims=True) + EPS)
    return (y * g.astype(F32)).astype(x.dtype)


def sq_relu_mlp(x, g, w_up, w_down):
    h = rmsnorm(x, g) @ w_up
    return jnp.square(jax.nn.relu(h)) @ w_down


def head_groupnorm(o):
    mu = jnp.mean(o, axis=-1, keepdims=True)
    var = jnp.mean(jnp.square(o - mu), axis=-1, keepdims=True)
    return (o - mu) * lax.rsqrt(var + EPS)


def retention_log_decay():
    return jnp.asarray(np.log1p(-(2.0 ** (-5.0 - np.arange(RET_HEADS)))), F32)


def alibi_slopes(n):
    def pow2(m):
        start = 2.0 ** (-8.0 / m)
        return [start ** (i + 1) for i in range(m)]
    if math.log2(n).is_integer():
        s = pow2(n)
    else:
        c = 2 ** int(math.floor(math.log2(n)))
        s = pow2(c) + pow2(2 * c)[0::2][: n - c]
    return np.asarray(s, np.float32)


def retention(q, k, v, s0):
    b, L, h, dk = q.shape
    dv = v.shape[-1]
    c = math.gcd(L, RET_CHUNK)
    n = L // c
    log_g = retention_log_decay()
    qf = q.astype(F32).reshape(b, n, c, h, dk)
    kf = (k.astype(F32) * dk ** -0.5).reshape(b, n, c, h, dk)
    vf = v.astype(F32).reshape(b, n, c, h, dv)
    idx = jnp.arange(c, dtype=F32)
    diff = idx[:, None] - idx[None, :]
    decay = jnp.where(diff >= 0, jnp.exp(jnp.maximum(diff, 0.0)[None] * log_g[:, None, None]), 0.0)
    scores = jnp.einsum('bnqhd,bnkhd->bnhqk', qf, kf) * decay
    inner = jnp.einsum('bnhqk,bnkhe->bnqhe', scores, vf)
    w_k = jnp.exp((c - 1.0 - idx)[:, None] * log_g[None, :])
    kv_chunk = jnp.einsum('bnkhd,bnkhe->bnhde', kf * w_k[:, :, None], vf)
    g_c = jnp.exp(c * log_g)[:, None, None]

    def step(s, kvc):
        return g_c * s + kvc, s

    s_last, s_prev = lax.scan(step, s0.astype(F32), jnp.moveaxis(kv_chunk, 1, 0))
    s_prev = jnp.moveaxis(s_prev, 0, 1)
    w_q = jnp.exp((idx + 1.0)[:, None] * log_g[None, :])
    cross = jnp.einsum('bnqhd,bnhde->bnqhe', qf * w_q[:, :, None], s_prev)
    return (inner + cross).reshape(b, L, h, dv), s_last


def memory_kv(mem, g, w):
    b, m, _ = mem.shape
    k, v = jnp.split(rmsnorm(mem, g) @ w, 2, axis=-1)
    return k.reshape(b, m, MEM_HEADS, MEM_HD), v.reshape(b, m, MEM_HEADS, MEM_HD)


def memory_attention(q, mk, mv):
    s = jnp.einsum('blhd,bmhd->bhlm', q.astype(F32), mk.astype(F32)) * MEM_HD ** -0.5
    p = jax.nn.softmax(s, axis=-1)
    return jnp.einsum('bhlm,bmhd->blhd', p, mv.astype(F32)).astype(q.dtype)


def sink_softmax(s, sink):
    m = jnp.maximum(jnp.max(s, axis=-1, keepdims=True), sink)
    e = jnp.exp(s - m)
    return e / (jnp.sum(e, axis=-1, keepdims=True) + jnp.exp(sink - m))


def windowed_sink_attention(q, k, v, qpos, kpos, sinks):
    g, r, hd = q.shape[3:]
    slopes = jnp.asarray(alibi_slopes(g * r)).reshape(g, r)
    dist = qpos[:, :, None] - kpos[:, None, :]
    valid = (kpos[:, None, :] >= 0) & (dist >= 0) & (dist <= WINDOW)
    s = jnp.einsum('bnqgrd,bnkgd->bngrqk', q.astype(F32), k.astype(F32)) * hd ** -0.5
    s = s - slopes[None, None, :, :, None, None] * dist.astype(F32)[None, :, None, None]
    s = jnp.where(valid[None, :, None, None], s, -jnp.inf)
    sink = sinks.astype(F32).reshape(g, r)[None, None, :, :, None, None]
    p = sink_softmax(s, sink)
    return jnp.einsum('bngrqk,bnkgd->bnqgrd', p, v.astype(F32)).astype(q.dtype)


def window_attention_prompt(q, k, v, sinks):
    b, s_len, hq, hd = q.shape
    hkv = k.shape[2]
    nb = s_len // WINDOW
    qb = q.reshape(b, nb, WINDOW, hkv, hq // hkv, hd)
    kb = k.reshape(b, nb, WINDOW, hkv, hd)
    vb = v.reshape(b, nb, WINDOW, hkv, hd)
    shift = lambda t: jnp.concatenate([jnp.zeros_like(t[:, :1]), t[:, :-1]], axis=1)
    kk = jnp.concatenate([shift(kb), kb], axis=2)
    vv = jnp.concatenate([shift(vb), vb], axis=2)
    start = jnp.arange(nb)[:, None] * WINDOW
    qpos = start + jnp.arange(WINDOW)[None, :]
    kpos = start - WINDOW + jnp.arange(2 * WINDOW)[None, :]
    o = windowed_sink_attention(qb, kk, vv, qpos, kpos, sinks)
    return o.reshape(b, s_len, hq, hd)


def window_attention_decode(q, k_ctx, v_ctx, sinks):
    b, L, hq, hd = q.shape
    hkv = k_ctx.shape[2]
    qpos = (PAST_LEN + jnp.arange(L))[None, :]
    kpos = (PAST_LEN - WINDOW + jnp.arange(WINDOW + L))[None, :]
    o = windowed_sink_attention(q.reshape(b, 1, L, hkv, hq // hkv, hd), k_ctx[:, None], v_ctx[:, None], qpos, kpos, sinks)
    return o.reshape(b, L, hq, hd)


def retention_layer(x, mem_k, mem_v, s0, g_mix, w_in, w_out, g_mlp, w_up, w_down):
    b, L, _ = x.shape
    proj = rmsnorm(x, g_mix) @ w_in
    q, k, v, gate, q_mem = jnp.split(proj, [MIX_MAIN, 2 * MIX_MAIN, 3 * MIX_MAIN, 4 * MIX_MAIN], axis=-1)
    hs = lambda t: t.reshape(b, L, RET_HEADS, RET_HD)
    o_ret, s_new = retention(hs(q), hs(k), hs(v), s0)
    o_ret = (jax.nn.silu(gate.astype(F32)) * head_groupnorm(o_ret).reshape(b, L, MIX_MAIN)).astype(x.dtype)
    o_mem = memory_attention(q_mem.reshape(b, L, MEM_HEADS, MEM_HD), mem_k, mem_v).reshape(b, L, MEM_W)
    x = x + jnp.concatenate([o_ret, o_mem], axis=-1) @ w_out
    return x + sq_relu_mlp(x, g_mlp, w_up, w_down), s_new


def window_layer(x, k_ctx, v_ctx, decode, mem_k, mem_v, sinks, g_mix, w_in, w_out, g_mlp, w_up, w_down):
    b, L, _ = x.shape
    proj = rmsnorm(x, g_mix) @ w_in
    q, q_mem = jnp.split(proj, [MIX_MAIN], axis=-1)
    q = q.reshape(b, L, SWA_HEADS, SWA_HD)
    attn = window_attention_decode if decode else window_attention_prompt
    o_swa = attn(q, k_ctx, v_ctx, sinks).reshape(b, L, MIX_MAIN)
    o_mem = memory_attention(q_mem.reshape(b, L, MEM_HEADS, MEM_HD), mem_k, mem_v).reshape(b, L, MEM_W)
    x = x + jnp.concatenate([o_swa, o_mem], axis=-1) @ w_out
    return x + sq_relu_mlp(x, g_mlp, w_up, w_down)


def trunk(x, mem_k, mem_v, ret_s0, buf_k, buf_v, norm_mix, w_in_a, w_out_a, w_in_b, w_out_b,
          attn_sinks, norm_kv, w_kv, norm_mlp, w_up, w_down, norm_final):
    decode = buf_k is not None
    b, L, _ = x.shape
    ret_states = []
    k_ctx = v_ctx = None
    for l in range(DEPTH):
        if l < N_A:
            x, s = retention_layer(x, mem_k[l], mem_v[l], ret_s0[l], norm_mix[l], w_in_a[l], w_out_a[l],
                                   norm_mlp[l], w_up[l], w_down[l])
            ret_states.append(s)
            continue
        j = l - N_A
        if j == 0:
            k_new, v_new = jnp.split(rmsnorm(x, norm_kv) @ w_kv, 2, axis=-1)
            k_new = k_new.reshape(b, L, SWA_KV_HEADS, SWA_HD)
            v_new = v_new.reshape(b, L, SWA_KV_HEADS, SWA_HD)
            if decode:
                k_ctx = jnp.concatenate([buf_k.astype(x.dtype), k_new], axis=1)
                v_ctx = jnp.concatenate([buf_v.astype(x.dtype), v_new], axis=1)
            else:
                k_ctx, v_ctx = k_new, v_new
        x = window_layer(x, k_ctx, v_ctx, decode, mem_k[l], mem_v[l], attn_sinks[j], norm_mix[l],
                         w_in_b[j], w_out_b[j], norm_mlp[l], w_up[l], w_down[l])
    y = rmsnorm(x, norm_final)
    return y, jnp.stack(ret_states).astype(ret_s0.dtype), k_ctx[:, -WINDOW:], v_ctx[:, -WINDOW:]


def setup_inputs(seed: int = 0) -> dict:
    key = jax.random.key(seed)
    ks = jax.random.split(key, 24)
    nrm = lambda k, shape, s: s * jax.random.normal(k, shape, jnp.float32)
    gain = lambda k, shape: 1.0 + 0.02 * jax.random.normal(k, shape, jnp.float32)
    return {
        'x_prompt': nrm(ks[0], (BATCH, SEQ, D_MODEL), 1.0),
        'x_sample': nrm(ks[1], (DEC_BATCH, DEC_SEQ, D_MODEL), 1.0),
        'cache_mem_k': nrm(ks[2], (DEPTH, DEC_BATCH, MEM_TOKENS, MEM_HEADS, MEM_HD), 1.0),
        'cache_mem_v': nrm(ks[3], (DEPTH, DEC_BATCH, MEM_TOKENS, MEM_HEADS, MEM_HD), 1.0),
        'state_ret': nrm(ks[4], (N_A, DEC_BATCH, RET_HEADS, RET_HD, RET_HD), 0.3),
        'cache_swa_k': nrm(ks[5], (DEC_BATCH, WINDOW, SWA_KV_HEADS, SWA_HD), 1.0),
        'cache_swa_v': nrm(ks[6], (DEC_BATCH, WINDOW, SWA_KV_HEADS, SWA_HD), 1.0),
        'mem_prompt': nrm(ks[7], (BATCH, MEM_TOKENS, D_MODEL), 1.0),
        'norm_mix': gain(ks[8], (DEPTH, D_MODEL)),
        'w_in_a': nrm(ks[9], (N_A, D_MODEL, 4 * MIX_MAIN + MEM_W), D_MODEL ** -0.5),
        'w_out_a': nrm(ks[10], (N_A, MIX_MAIN + MEM_W, D_MODEL), (MIX_MAIN + MEM_W) ** -0.5),
        'w_in_b': nrm(ks[11], (N_B, D_MODEL, MIX_MAIN + MEM_W), D_MODEL ** -0.5),
        'w_out_b': nrm(ks[12], (N_B, MIX_MAIN + MEM_W, D_MODEL), (MIX_MAIN + MEM_W) ** -0.5),
        'attn_sinks': nrm(ks[13], (N_B, SWA_HEADS), 1.0),
        'norm_mem': gain(ks[14], (DEPTH, D_MODEL)),
        'w_mem_kv': nrm(ks[15], (DEPTH, D_MODEL, 2 * MEM_W), D_MODEL ** -0.5),
        'norm_kv': gain(ks[16], (D_MODEL,)),
        'w_kv': nrm(ks[17], (D_MODEL, 2 * SWA_KV_HEADS * SWA_HD), D_MODEL ** -0.5),
        'norm_mlp': gain(ks[18], (DEPTH, D_MODEL)),
        'w_up': nrm(ks[19], (DEPTH, D_MODEL, D_FF), D_MODEL ** -0.5),
        'w_down': nrm(ks[20], (DEPTH, D_FF, D_MODEL), D_FF ** -0.5),
        'norm_final': gain(ks[21], (D_MODEL,)),
    }


def reference(x_prompt, x_sample, cache_mem_k, cache_mem_v, state_ret, cache_swa_k, cache_swa_v, mem_prompt,
              norm_mix, w_in_a, w_out_a, w_in_b, w_out_b, attn_sinks, norm_mem, w_mem_kv, norm_kv, w_kv,
              norm_mlp, w_up, w_down, norm_final):
    mem_pairs = [memory_kv(mem_prompt, norm_mem[l], w_mem_kv[l]) for l in range(DEPTH)]
    mem_k_prompt = jnp.stack([p[0] for p in mem_pairs])
    mem_v_prompt = jnp.stack([p[1] for p in mem_pairs])
    ret0 = jnp.zeros((N_A, x_prompt.shape[0], RET_HEADS, RET_HD, RET_HD), state_ret.dtype)
    y_prompt, ret_prompt, swa_k_prompt, swa_v_prompt = trunk(
        x_prompt, mem_k_prompt, mem_v_prompt, ret0, None, None,
        norm_mix, w_in_a, w_out_a, w_in_b, w_out_b, attn_sinks, norm_kv, w_kv, norm_mlp, w_up, w_down, norm_final)
    y_sample, ret_sample, swa_k_sample, swa_v_sample = trunk(
        x_sample, cache_mem_k, cache_mem_v, state_ret, cache_swa_k, cache_swa_v,
        norm_mix, w_in_a, w_out_a, w_in_b, w_out_b, attn_sinks, norm_kv, w_kv, norm_mlp, w_up, w_down, norm_final)
    return (y_prompt, y_sample, ret_prompt, ret_sample, swa_k_prompt, swa_v_prompt, swa_k_sample, swa_v_sample,
            mem_k_prompt.astype(cache_mem_k.dtype), mem_v_prompt.astype(cache_mem_v.dtype))
```

```python
import functools
import math

import jax
import jax.numpy as jnp
import numpy as np
from jax import lax
from jax.experimental import pallas as pl
from jax.experimental.pallas import tpu as pltpu

F32 = jnp.float32
BF16 = jnp.bfloat16

D_MODEL = 1024
DEPTH = 4
N_A = 2
N_B = 2
MEM_TOKENS = 256
MEM_HEADS = 4
MEM_W = 256
MEM_HD = 64
MIX_MAIN = 768
RET_HEADS = 6
RET_HD = 128
CHUNK = 128
SWA_HEADS = 12
SWA_KV_HEADS = 4
SWA_REP = SWA_HEADS // SWA_KV_HEADS
SWA_HD = 64
KV_W = SWA_KV_HEADS * SWA_HD
WINDOW = 128
D_FF = 4096
EPS = 1e-6
NEG = -1e30

VMEM_LIMIT = 56 * 1024 * 1024

NT_DIMS = (((1,), (1,)), ((), ()))
TN_DIMS = (((0,), (0,)), ((), ()))


def _alibi_slopes(n):
    def pow2(m):
        start = 2.0 ** (-8.0 / m)
        return [start ** (i + 1) for i in range(m)]

    if math.log2(n).is_integer():
        s = pow2(n)
    else:
        c = 2 ** int(math.floor(math.log2(n)))
        s = pow2(c) + pow2(2 * c)[0::2][: n - c]
    return np.asarray(s, np.float32)


_LOG_G = np.log1p(-(2.0 ** (-5.0 - np.arange(RET_HEADS)))).astype(np.float32).astype(np.float64)
_SLOPES = _alibi_slopes(SWA_HEADS).astype(np.float64)


def _retention_consts():
    idx = np.arange(CHUNK, dtype=np.float64)
    diff = idx[:, None] - idx[None, :]
    scale = RET_HD ** -0.5
    decay = np.where(diff >= 0, np.exp(np.maximum(diff, 0.0)[None] * _LOG_G[:, None, None]), 0.0) * scale
    w_q = np.exp((idx + 1.0)[None, :] * _LOG_G[:, None])
    w_k = np.exp((CHUNK - 1.0 - idx)[None, :] * _LOG_G[:, None]) * scale
    w_q = np.broadcast_to(w_q[:, :, None], (RET_HEADS, CHUNK, RET_HD))
    w_k = np.broadcast_to(w_k[:, :, None], (RET_HEADS, CHUNK, RET_HD))
    g_c = np.exp(CHUNK * _LOG_G)
    return (decay.astype(np.float32), np.ascontiguousarray(w_q, np.float32),
            np.ascontiguousarray(w_k, np.float32), [float(v) for v in g_c])


def _swa_bias():
    i = np.arange(WINDOW)[:, None]
    j = np.arange(2 * WINDOW)[None, :]
    dist = i + WINDOW - j
    valid = (dist >= 0) & (dist <= WINDOW)
    out = np.zeros((SWA_KV_HEADS, SWA_REP * WINDOW, 2 * WINDOW), np.float32)
    for g in range(SWA_KV_HEADS):
        for r in range(SWA_REP):
            b = np.where(valid, -_SLOPES[g * SWA_REP + r] * dist, NEG)
            out[g, r * WINDOW:(r + 1) * WINDOW] = b
    return out


def _rmsnorm(x, g):
    ms = jnp.mean(x * x, axis=-1, keepdims=True)
    return (x * lax.rsqrt(ms + EPS)) * g


def _lane_block64(lane):
    return lax.shift_right_logical(lane, 6)


def _cparams(n_axes):
    return pltpu.CompilerParams(dimension_semantics=("arbitrary",) * n_axes, vmem_limit_bytes=VMEM_LIMIT)


def _norm_matmul_kernel(x_ref, g_ref, w_ref, o_ref, *, n_chunk):
    xn = _rmsnorm(x_ref[...], g_ref[...]).astype(BF16)
    n = w_ref.shape[1]
    for n0 in range(0, n, n_chunk):
        n1 = min(n0 + n_chunk, n)
        o_ref[:, n0:n1] = jnp.dot(xn, w_ref[:, n0:n1], preferred_element_type=F32).astype(o_ref.dtype)


def norm_matmul(x, g, w, out_dtype, tm, n_chunk=512):
    m, k = x.shape
    n = w.shape[1]
    return pl.pallas_call(
        functools.partial(_norm_matmul_kernel, n_chunk=n_chunk),
        grid=(m // tm,),
        in_specs=[pl.BlockSpec((tm, k), lambda i: (i, 0)),
                  pl.BlockSpec((1, k), lambda i: (0, 0)),
                  pl.BlockSpec((k, n), lambda i: (0, 0))],
        out_specs=pl.BlockSpec((tm, n), lambda i: (i, 0)),
        out_shape=jax.ShapeDtypeStruct((m, n), out_dtype),
        compiler_params=_cparams(1),
        name="norm_matmul",
    )(x, g.reshape(1, k), w)


def _matmul_residual_kernel(a_ref, w_ref, x_ref, o_ref):
    o_ref[...] = x_ref[...] + jnp.dot(a_ref[...], w_ref[...], preferred_element_type=F32)


def matmul_residual(a, w, x, tm):
    m, k = a.shape
    n = w.shape[1]
    return pl.pallas_call(
        _matmul_residual_kernel,
        grid=(m // tm,),
        in_specs=[pl.BlockSpec((tm, k), lambda i: (i, 0)),
                  pl.BlockSpec((k, n), lambda i: (0, 0)),
                  pl.BlockSpec((tm, n), lambda i: (i, 0))],
        out_specs=pl.BlockSpec((tm, n), lambda i: (i, 0)),
        out_shape=jax.ShapeDtypeStruct((m, n), F32),
        compiler_params=_cparams(1),
        name="matmul_residual",
    )(a, w, x)


def _mlp_kernel(x_ref, g_ref, wu_ref, wd_ref, gf_ref, o_ref, act_ref, *, fc, final_norm):
    xn = _rmsnorm(x_ref[...], g_ref[...]).astype(BF16)
    for c0 in range(0, D_FF, fc):
        h = jnp.dot(xn, wu_ref[:, c0:c0 + fc], preferred_element_type=F32)
        act_ref[:, c0:c0 + fc] = jnp.square(jnp.maximum(h, 0.0)).astype(BF16)
    y = x_ref[...] + jnp.dot(act_ref[...], wd_ref[...], preferred_element_type=F32)
    if final_norm:
        y = _rmsnorm(y, gf_ref[...])
    o_ref[...] = y


def mlp_block(x, g, w_up, w_down, g_final, final_norm, tm, fc=512):
    m, d = x.shape
    const = lambda i: (0, 0)
    return pl.pallas_call(
        functools.partial(_mlp_kernel, fc=fc, final_norm=final_norm),
        grid=(m // tm,),
        in_specs=[pl.BlockSpec((tm, d), lambda i: (i, 0)),
                  pl.BlockSpec((1, d), const),
                  pl.BlockSpec((d, D_FF), const, pipeline_mode=pl.Buffered(1)),
                  pl.BlockSpec((D_FF, d), const, pipeline_mode=pl.Buffered(1)),
                  pl.BlockSpec((1, d), const)],
        out_specs=pl.BlockSpec((tm, d), lambda i: (i, 0)),
        out_shape=jax.ShapeDtypeStruct((m, d), F32),
        scratch_shapes=[pltpu.VMEM((tm, D_FF), BF16)],
        compiler_params=_cparams(1),
        name="mlp_block",
    )(x, g.reshape(1, d), w_up, w_down, g_final.reshape(1, d))


def _mem_attention_tile(qm, mk, mv):
    t = qm.shape[0]
    lane_head = _lane_block64(lax.broadcasted_iota(jnp.int32, (t, MEM_W), 1))
    acc = jnp.zeros((t, MEM_W), F32)
    for h in range(MEM_HEADS):
        sel = lane_head == h
        qh = jnp.where(sel, qm, jnp.zeros_like(qm))
        s = lax.dot_general(qh, mk, NT_DIMS, preferred_element_type=F32) * (MEM_HD ** -0.5)
        m = jnp.max(s, axis=-1, keepdims=True)
        e = jnp.exp(s - m)
        den = jnp.sum(e, axis=-1, keepdims=True)
        o = jnp.dot(e.astype(BF16), mv, preferred_element_type=F32) / den
        acc = jnp.where(sel, o, acc)
    return acc


def _mixer_a_kernel(proj_ref, mk_ref, mv_ref, decay_ref, wq_ref, wk_ref, o_ref, st_ref, s_scr, *, ts, g_c):
    @pl.when(pl.program_id(1) == 0)
    def _():
        s_scr[...] = jnp.zeros_like(s_scr)

    def chunk_body(c, carry):
        r0 = pl.multiple_of(c * CHUNK, CHUNK)
        rows = pl.ds(r0, CHUNK)
        for h in range(RET_HEADS):
            lo = h * RET_HD
            q = proj_ref[rows, lo:lo + RET_HD]
            k = proj_ref[rows, MIX_MAIN + lo:MIX_MAIN + lo + RET_HD]
            v = proj_ref[rows, 2 * MIX_MAIN + lo:2 * MIX_MAIN + lo + RET_HD]
            gate = proj_ref[rows, 3 * MIX_MAIN + lo:3 * MIX_MAIN + lo + RET_HD].astype(F32)
            s_prev = s_scr[h]
            sc = lax.dot_general(q, k, NT_DIMS, preferred_element_type=F32) * decay_ref[h]
            inner = jnp.dot(sc.astype(BF16), v, preferred_element_type=F32)
            cross = jnp.dot(q, s_prev.astype(BF16), preferred_element_type=F32) * wq_ref[h]
            o = inner + cross
            kw = (k.astype(F32) * wk_ref[h]).astype(BF16)
            kv = lax.dot_general(kw, v, TN_DIMS, preferred_element_type=F32)
            s_scr[h] = g_c[h] * s_prev + kv
            mu = jnp.mean(o, axis=-1, keepdims=True)
            oc = o - mu
            var = jnp.mean(oc * oc, axis=-1, keepdims=True)
            on = oc * lax.rsqrt(var + EPS)
            o_ref[rows, lo:lo + RET_HD] = (gate * jax.nn.sigmoid(gate) * on).astype(BF16)
        return carry

    lax.fori_loop(0, ts // CHUNK, chunk_body, 0)
    st_ref[...] = s_scr[...]
    qm = proj_ref[:, 4 * MIX_MAIN:4 * MIX_MAIN + MEM_W]
    o_ref[:, MIX_MAIN:] = _mem_attention_tile(qm, mk_ref[...], mv_ref[...]).astype(BF16)


def mixer_a_prompt(proj, mem_k, mem_v, batch, seq, ts):
    ns = seq // ts
    decay, w_q, w_k, g_c = _retention_consts()
    c3 = lambda b, s: (0, 0, 0)
    pw = proj.shape[1]
    return pl.pallas_call(
        functools.partial(_mixer_a_kernel, ts=ts, g_c=g_c),
        grid=(batch, ns),
        in_specs=[pl.BlockSpec((ts, pw), lambda b, s: (b * ns + s, 0)),
                  pl.BlockSpec((None, MEM_TOKENS, MEM_W), lambda b, s: (b, 0, 0)),
                  pl.BlockSpec((None, MEM_TOKENS, MEM_W), lambda b, s: (b, 0, 0)),
                  pl.BlockSpec((RET_HEADS, CHUNK, CHUNK), c3),
                  pl.BlockSpec((RET_HEADS, CHUNK, RET_HD), c3),
                  pl.BlockSpec((RET_HEADS, CHUNK, RET_HD), c3)],
        out_specs=[pl.BlockSpec((ts, D_MODEL), lambda b, s: (b * ns + s, 0)),
                   pl.BlockSpec((None, RET_HEADS, RET_HD, RET_HD), lambda b, s: (b, 0, 0, 0))],
        out_shape=[jax.ShapeDtypeStruct((batch * seq, D_MODEL), BF16),
                   jax.ShapeDtypeStruct((batch, RET_HEADS, RET_HD, RET_HD), F32)],
        scratch_shapes=[pltpu.VMEM((RET_HEADS, RET_HD, RET_HD), F32)],
        compiler_params=_cparams(2),
        name="mixer_a_prompt",
    )(proj, mem_k, mem_v, jnp.asarray(decay), jnp.asarray(w_q), jnp.asarray(w_k))


def _mixer_b_kernel(q_ref, kvp_ref, kvc_ref, mk_ref, mv_ref, bias_ref, sink_ref, o_ref, *, ts):
    is_first = pl.program_id(1) == 0
    nblk = ts // WINDOW
    lane_group = _lane_block64(lax.broadcasted_iota(jnp.int32, (WINDOW, KV_W), 1))
    key_col = lax.broadcasted_iota(jnp.int32, (SWA_REP * WINDOW, 2 * WINDOW), 1)
    for i in range(nblk):
        r0 = i * WINDOW
        if i == 0:
            kv2 = jnp.concatenate([kvp_ref[...], kvc_ref[0:WINDOW, :]], axis=0)
        else:
            kv2 = kvc_ref[r0 - WINDOW:r0 + WINDOW, :]
        k2 = kv2[:, :KV_W]
        v2 = kv2[:, KV_W:]
        qs = [q_ref[r0:r0 + WINDOW, r * KV_W:(r + 1) * KV_W] for r in range(SWA_REP)]
        accs = [jnp.zeros((WINDOW, KV_W), F32) for _ in range(SWA_REP)]
        for g in range(SWA_KV_HEADS):
            sel = lane_group == g
            qg = jnp.concatenate([jnp.where(sel, qr, jnp.zeros_like(qr)) for qr in qs], axis=0)
            s = lax.dot_general(qg, k2, NT_DIMS, preferred_element_type=F32) * (SWA_HD ** -0.5) + bias_ref[g]
            if i == 0:
                s = jnp.where(jnp.logical_and(is_first, key_col < WINDOW), NEG, s)
            sink = sink_ref[g]
            m = jnp.maximum(jnp.max(s, axis=-1, keepdims=True), sink)
            e = jnp.exp(s - m)
            den = jnp.sum(e, axis=-1, keepdims=True) + jnp.exp(sink - m)
            o = jnp.dot(e.astype(BF16), v2, preferred_element_type=F32) / den
            for r in range(SWA_REP):
                accs[r] = jnp.where(sel, o[r * WINDOW:(r + 1) * WINDOW], accs[r])
        for r in range(SWA_REP):
            o_ref[r0:r0 + WINDOW, r * KV_W:(r + 1) * KV_W] = accs[r].astype(BF16)
    qm = q_ref[:, MIX_MAIN:MIX_MAIN + MEM_W]
    o_ref[:, MIX_MAIN:] = _mem_attention_tile(qm, mk_ref[...], mv_ref[...]).astype(BF16)


def mixer_b_prompt(qproj, kv, mem_k, mem_v, sink_rows, batch, seq, ts):
    ns = seq // ts
    nw = ts // WINDOW
    bias = jnp.asarray(_swa_bias())

    def prev_map(b, s):
        return (jnp.maximum(b * (seq // WINDOW) + s * nw - 1, b * (seq // WINDOW)), 0)

    return pl.pallas_call(
        functools.partial(_mixer_b_kernel, ts=ts),
        grid=(batch, ns),
        in_specs=[pl.BlockSpec((ts, D_MODEL), lambda b, s: (b * ns + s, 0)),
                  pl.BlockSpec((WINDOW, 2 * KV_W), prev_map),
                  pl.BlockSpec((ts, 2 * KV_W), lambda b, s: (b * ns + s, 0)),
                  pl.BlockSpec((None, MEM_TOKENS, MEM_W), lambda b, s: (b, 0, 0)),
                  pl.BlockSpec((None, MEM_TOKENS, MEM_W), lambda b, s: (b, 0, 0)),
                  pl.BlockSpec((SWA_KV_HEADS, SWA_REP * WINDOW, 2 * WINDOW), lambda b, s: (0, 0, 0)),
                  pl.BlockSpec((SWA_KV_HEADS, SWA_REP * WINDOW, 1), lambda b, s: (0, 0, 0))],
        out_specs=pl.BlockSpec((ts, D_MODEL), lambda b, s: (b * ns + s, 0)),
        out_shape=jax.ShapeDtypeStruct((batch * seq, D_MODEL), BF16),
        compiler_params=_cparams(2),
        name="mixer_b_prompt",
    )(qproj, kv, kv, mem_k, mem_v, bias, sink_rows)


def _ret_decode_kernel(proj_ref, st_ref, o_ref, nst_ref, *, bb):
    scale = RET_HD ** -0.5
    for h in range(RET_HEADS):
        lo = h * RET_HD
        q8 = proj_ref[:, lo:lo + RET_HD]
        k8 = proj_ref[:, MIX_MAIN + lo:MIX_MAIN + lo + RET_HD] * scale
        v8 = proj_ref[:, 2 * MIX_MAIN + lo:2 * MIX_MAIN + lo + RET_HD]
        gate = proj_ref[:, 3 * MIX_MAIN + lo:3 * MIX_MAIN + lo + RET_HD]
        qt = q8.T
        kt = k8.T
        g = float(np.exp(_LOG_G[h]))
        rows = []
        for j in range(bb):
            qcol = jnp.broadcast_to(qt[:, j:j + 1], (RET_HD, RET_HD))
            kcol = jnp.broadcast_to(kt[:, j:j + 1], (RET_HD, RET_HD))
            s_new = g * st_ref[j, h] + kcol * v8[j:j + 1, :]
            nst_ref[j, h] = s_new
            rows.append(jnp.sum(qcol * s_new, axis=0, keepdims=True))
        o = jnp.concatenate(rows, axis=0)
        mu = jnp.mean(o, axis=-1, keepdims=True)
        oc = o - mu
        var = jnp.mean(oc * oc, axis=-1, keepdims=True)
        on = oc * lax.rsqrt(var + EPS)
        o_ref[:, lo:lo + RET_HD] = gate * jax.nn.sigmoid(gate) * on


def ret_decode(proj, state, layer, bb=8):
    nb = proj.shape[0]
    pw = proj.shape[1]
    return pl.pallas_call(
        functools.partial(_ret_decode_kernel, bb=bb),
        grid=(nb // bb,),
        in_specs=[pl.BlockSpec((bb, pw), lambda i: (i, 0)),
                  pl.BlockSpec((None, bb, RET_HEADS, RET_HD, RET_HD), lambda i: (layer, i, 0, 0, 0))],
        out_specs=[pl.BlockSpec((bb, MIX_MAIN), lambda i: (i, 0)),
                   pl.BlockSpec((bb, RET_HEADS, RET_HD, RET_HD), lambda i: (i, 0, 0, 0))],
        out_shape=[jax.ShapeDtypeStruct((nb, MIX_MAIN), F32),
                   jax.ShapeDtypeStruct((nb, RET_HEADS, RET_HD, RET_HD), F32)],
        compiler_params=_cparams(1),
        name="ret_decode",
    )(proj, state)


def _mem_decode_kernel(q_ref, mk_ref, mv_ref, o_ref, *, bb, q_off):
    row = lax.broadcasted_iota(jnp.int32, (8, MEM_W), 0)
    lane_head = _lane_block64(lax.broadcasted_iota(jnp.int32, (8, MEM_W), 1))
    sel = row == lane_head
    outs = []
    for j in range(bb):
        qrow = q_ref[j:j + 1, q_off:q_off + MEM_W]
        qexp = jnp.where(sel, jnp.broadcast_to(qrow, (8, MEM_W)), 0.0).astype(BF16)
        kb = mk_ref[j].astype(BF16)
        vb = mv_ref[j].astype(BF16)
        s = lax.dot_general(qexp, kb, NT_DIMS, preferred_element_type=F32) * (MEM_HD ** -0.5)
        m = jnp.max(s, axis=-1, keepdims=True)
        e = jnp.exp(s - m)
        den = jnp.sum(e, axis=-1, keepdims=True)
        o = jnp.dot(e.astype(BF16), vb, preferred_element_type=F32) / den
        outs.append(jnp.sum(jnp.where(sel, o, 0.0), axis=0, keepdims=True))
    o_ref[...] = jnp.concatenate(outs, axis=0)


def mem_decode(proj, cache_k, cache_v, layer, q_off, bb=8):
    nb = proj.shape[0]
    pw = proj.shape[1]
    cmap = lambda i: (layer, i, 0, 0)
    return pl.pallas_call(
        functools.partial(_mem_decode_kernel, bb=bb, q_off=q_off),
        grid=(nb // bb,),
        in_specs=[pl.BlockSpec((bb, pw), lambda i: (i, 0)),
                  pl.BlockSpec((None, bb, MEM_TOKENS, MEM_W), cmap),
                  pl.BlockSpec((None, bb, MEM_TOKENS, MEM_W), cmap)],
        out_specs=pl.BlockSpec((bb, MEM_W), lambda i: (i, 0)),
        out_shape=jax.ShapeDtypeStruct((nb, MEM_W), F32),
        compiler_params=_cparams(1),
        name="mem_decode",
    )(proj, cache_k, cache_v)


def _swa_decode_kernel(q_ref, kn_ref, vn_ref, ck_ref, cv_ref, slope_ref, sink_ref, o_ref, *, bb):
    nrow = 16
    row = lax.broadcasted_iota(jnp.int32, (nrow, KV_W), 0)
    lane_group = _lane_block64(lax.broadcasted_iota(jnp.int32, (nrow, KV_W), 1))
    dist = (WINDOW - lax.broadcasted_iota(jnp.int32, (nrow, WINDOW), 1)).astype(F32)
    bias = -slope_ref[...] * dist
    sink = sink_ref[...]
    scale = SWA_HD ** -0.5
    outs = [[] for _ in range(SWA_REP)]
    for j in range(bb):
        qexp = jnp.zeros((nrow, KV_W), F32)
        for r in range(SWA_REP):
            qr = jnp.broadcast_to(q_ref[j:j + 1, r * KV_W:(r + 1) * KV_W], (nrow, KV_W))
            qexp = jnp.where(row == SWA_KV_HEADS * r + lane_group, qr, qexp)
        kb = ck_ref[j].astype(BF16)
        vb = cv_ref[j].astype(BF16)
        s = lax.dot_general(qexp.astype(BF16), kb, NT_DIMS, preferred_element_type=F32) * scale + bias
        s_new = jnp.sum(qexp * kn_ref[j:j + 1, :], axis=-1, keepdims=True) * scale
        m = jnp.maximum(jnp.maximum(jnp.max(s, axis=-1, keepdims=True), s_new), sink)
        e = jnp.exp(s - m)
        e_new = jnp.exp(s_new - m)
        den = jnp.sum(e, axis=-1, keepdims=True) + e_new + jnp.exp(sink - m)
        o = (jnp.dot(e.astype(BF16), vb, preferred_element_type=F32) + e_new * vn_ref[j:j + 1, :]) / den
        for r in range(SWA_REP):
            keep = row == SWA_KV_HEADS * r + lane_group
            outs[r].append(jnp.sum(jnp.where(keep, o, 0.0), axis=0, keepdims=True))
    for r in range(SWA_REP):
        o_ref[:, r * KV_W:(r + 1) * KV_W] = jnp.concatenate(outs[r], axis=0)


def swa_decode(qproj, k_new, v_new, cache_k, cache_v, slope_rows, sink_rows, bb=8):
    nb = qproj.shape[0]
    pw = qproj.shape[1]
    return pl.pallas_call(
        functools.partial(_swa_decode_kernel, bb=bb),
        grid=(nb // bb,),
        in_specs=[pl.BlockSpec((bb, pw), lambda i: (i, 0)),
                  pl.BlockSpec((bb, KV_W), lambda i: (i, 0)),
                  pl.BlockSpec((bb, KV_W), lambda i: (i, 0)),
                  pl.BlockSpec((bb, WINDOW, KV_W), lambda i: (i, 0, 0)),
                  pl.BlockSpec((bb, WINDOW, KV_W), lambda i: (i, 0, 0)),
                  pl.BlockSpec((16, 1), lambda i: (0, 0)),
                  pl.BlockSpec((16, 1), lambda i: (0, 0))],
        out_specs=pl.BlockSpec((bb, MIX_MAIN), lambda i: (i, 0)),
        out_shape=jax.ShapeDtypeStruct((nb, MIX_MAIN), F32),
        compiler_params=_cparams(1),
        name="swa_decode",
    )(qproj, k_new, v_new, cache_k, cache_v, slope_rows, sink_rows)


def _permute_q_cols(w):
    lead = w.shape[:-1]
    return w.reshape(*lead, SWA_KV_HEADS, SWA_REP, SWA_HD).swapaxes(-3, -2).reshape(*lead, MIX_MAIN)


def kernel(x_prompt, x_sample, cache_mem_k, cache_mem_v, state_ret, cache_swa_k, cache_swa_v, mem_prompt,
           norm_mix, w_in_a, w_out_a, w_in_b, w_out_b, attn_sinks, norm_mem, w_mem_kv, norm_kv, w_kv,
           norm_mlp, w_up, w_down, norm_final):
    batch, seq, d = x_prompt.shape
    nb = x_sample.shape[0]
    tm = 512

    w_in_a16 = w_in_a.astype(BF16)
    w_out_a16 = w_out_a.astype(BF16)
    w_in_b16 = jnp.concatenate([_permute_q_cols(w_in_b[:, :, :MIX_MAIN]), w_in_b[:, :, MIX_MAIN:]], axis=-1).astype(BF16)
    w_out_b16 = jnp.concatenate(
        [_permute_q_cols(w_out_b[:, :MIX_MAIN, :].swapaxes(1, 2)).swapaxes(1, 2), w_out_b[:, MIX_MAIN:, :]],
        axis=1).astype(BF16)
    w_mem16 = w_mem_kv.astype(BF16)
    w_kv16 = w_kv.astype(BF16)
    w_up16 = w_up.astype(BF16)
    w_down16 = w_down.astype(BF16)

    sinks_gr = attn_sinks.reshape(N_B, SWA_KV_HEADS, SWA_REP)
    sink_prompt = jnp.broadcast_to(sinks_gr[:, :, :, None, None],
                                   (N_B, SWA_KV_HEADS, SWA_REP, WINDOW, 1)).reshape(N_B, SWA_KV_HEADS, SWA_REP * WINDOW, 1)
    sink_dec = jnp.concatenate([sinks_gr.swapaxes(1, 2).reshape(N_B, SWA_HEADS), jnp.zeros((N_B, 4), F32)],
                               axis=1).reshape(N_B, 16, 1)
    slope_dec = np.zeros((16, 1), np.float32)
    for r in range(SWA_REP):
        for g in range(SWA_KV_HEADS):
            slope_dec[SWA_KV_HEADS * r + g, 0] = _SLOPES[g * SWA_REP + r]
    slope_dec = jnp.asarray(slope_dec)

    memx = mem_prompt.reshape(batch * MEM_TOKENS, d)
    mem_kv = [norm_matmul(memx, norm_mem[l], w_mem16[l], F32, tm=256) for l in range(DEPTH)]
    mem_k_prompt = jnp.stack([m[:, :MEM_W] for m in mem_kv]).reshape(DEPTH, batch, MEM_TOKENS, MEM_HEADS, MEM_HD)
    mem_v_prompt = jnp.stack([m[:, MEM_W:] for m in mem_kv]).reshape(DEPTH, batch, MEM_TOKENS, MEM_HEADS, MEM_HD)
    mk16 = mem_k_prompt.reshape(DEPTH, batch, MEM_TOKENS, MEM_W).astype(BF16)
    mv16 = mem_v_prompt.reshape(DEPTH, batch, MEM_TOKENS, MEM_W).astype(BF16)

    x = x_prompt.reshape(batch * seq, d)
    ret_states = []
    kv_p = None
    for l in range(DEPTH):
        if l < N_A:
            proj = norm_matmul(x, norm_mix[l], w_in_a16[l], BF16, tm=tm, n_chunk=MIX_MAIN)
            attn, st = mixer_a_prompt(proj, mk16[l], mv16[l], batch, seq, ts=512)
            ret_states.append(st)
            x = matmul_residual(attn, w_out_a16[l], x, tm=tm)
        else:
            j = l - N_A
            if j == 0:
                kv_p = norm_matmul(x, norm_kv, w_kv16, F32, tm=tm)
                kv_p16 = kv_p.astype(BF16)
            qproj = norm_matmul(x, norm_mix[l], w_in_b16[j], BF16, tm=tm)
            attn = mixer_b_prompt(qproj, kv_p16, mk16[l], mv16[l], sink_prompt[j], batch, seq, ts=512)
            x = matmul_residual(attn, w_out_b16[j], x, tm=tm)
        x = mlp_block(x, norm_mlp[l], w_up16[l], w_down16[l], norm_final, l == DEPTH - 1, tm=tm)
    y_prompt = x.reshape(batch, seq, d)
    ret_prompt = jnp.stack(ret_states)
    kv_p3 = kv_p.reshape(batch, seq, 2 * KV_W)
    swa_k_prompt = kv_p3[:, -WINDOW:, :KV_W].reshape(batch, WINDOW, SWA_KV_HEADS, SWA_HD)
    swa_v_prompt = kv_p3[:, -WINDOW:, KV_W:].reshape(batch, WINDOW, SWA_KV_HEADS, SWA_HD)

    cmk = cache_mem_k.reshape(DEPTH, nb, MEM_TOKENS, MEM_W)
    cmv = cache_mem_v.reshape(DEPTH, nb, MEM_TOKENS, MEM_W)
    csk = cache_swa_k.reshape(nb, WINDOW, KV_W)
    csv = cache_swa_v.reshape(nb, WINDOW, KV_W)
    xs = x_sample.reshape(nb, d)
    ret_states_s = []
    k_new = v_new = None
    for l in range(DEPTH):
        if l < N_A:
            proj = norm_matmul(xs, norm_mix[l], w_in_a16[l], F32, tm=nb, n_chunk=MIX_MAIN)
            o_ret, st = ret_decode(proj, state_ret, l)
            ret_states_s.append(st)
            o_mem = mem_decode(proj, cmk, cmv, l, 4 * MIX_MAIN)
            attn = jnp.concatenate([o_ret, o_mem], axis=-1).astype(BF16)
            xs = matmul_residual(attn, w_out_a16[l], xs, tm=nb)
        else:
            j = l - N_A
            if j == 0:
                kv_s = norm_matmul(xs, norm_kv, w_kv16, F32, tm=nb)
                k_new, v_new = kv_s[:, :KV_W], kv_s[:, KV_W:]
            qproj = norm_matmul(xs, norm_mix[l], w_in_b16[j], F32, tm=nb)
            o_swa = swa_decode(qproj, k_new, v_new, csk, csv, slope_dec, sink_dec[j])
            o_mem = mem_decode(qproj, cmk, cmv, l, MIX_MAIN)
            attn = jnp.concatenate([o_swa, o_mem], axis=-1).astype(BF16)
            xs = matmul_residual(attn, w_out_b16[j], xs, tm=nb)
        xs = mlp_block(xs, norm_mlp[l], w_up16[l], w_down16[l], norm_final, l == DEPTH - 1, tm=nb)
    y_sample = xs.reshape(nb, 1, d)
    ret_sample = jnp.stack(ret_states_s)
    swa_k_sample = jnp.concatenate([csk[:, 1:], k_new[:, None, :]], axis=1).reshape(nb, WINDOW, SWA_KV_HEADS, SWA_HD)
    swa_v_sample = jnp.concatenate([csv[:, 1:], v_new[:, None, :]], axis=1).reshape(nb, WINDOW, SWA_KV_HEADS, SWA_HD)

    return (y_prompt, y_sample, ret_prompt, ret_sample, swa_k_prompt, swa_v_prompt, swa_k_sample, swa_v_sample,
            mem_k_prompt, mem_v_prompt)
```

```python
import functools
import math

import jax
import jax.numpy as jnp
import numpy as np
from jax import lax
from jax.experimental import pallas as pl
from jax.experimental.pallas import tpu as pltpu

F32 = jnp.float32
BF16 = jnp.bfloat16

D_MODEL = 1024
DEPTH = 4
N_A = 2
N_B = 2
MEM_TOKENS = 256
MEM_HEADS = 4
MEM_W = 256
MEM_HD = 64
MIX_MAIN = 768
RET_HEADS = 6
RET_HD = 128
CHUNK = 128
SWA_HEADS = 12
SWA_KV_HEADS = 4
SWA_REP = SWA_HEADS // SWA_KV_HEADS
SWA_HD = 64
KV_W = SWA_KV_HEADS * SWA_HD
WINDOW = 128
D_FF = 4096
EPS = 1e-6
NEG = -1e30
ATTN_SCALE = 0.125
LANES = 128
DEC_ROWS = 16

VMEM_LIMIT = 56 * 1024 * 1024

NT_DIMS = (((1,), (1,)), ((), ()))
TN_DIMS = (((0,), (0,)), ((), ()))


def _alibi_slopes(n):
    def pow2(m):
        start = 2.0 ** (-8.0 / m)
        return [start ** (i + 1) for i in range(m)]

    if math.log2(n).is_integer():
        s = pow2(n)
    else:
        c = 2 ** int(math.floor(math.log2(n)))
        s = pow2(c) + pow2(2 * c)[0::2][: n - c]
    return np.asarray(s, np.float32)


_LOG_G = np.log1p(-(2.0 ** (-5.0 - np.arange(RET_HEADS)))).astype(np.float32).astype(np.float64)
_SLOPES = _alibi_slopes(SWA_HEADS).astype(np.float64)


def _retention_consts():
    idx = np.arange(CHUNK, dtype=np.float64)
    diff = idx[:, None] - idx[None, :]
    scale = RET_HD ** -0.5
    decay = np.where(diff >= 0, np.exp(np.maximum(diff, 0.0)[None] * _LOG_G[:, None, None]), 0.0) * scale
    w_q = np.exp((idx + 1.0)[None, :] * _LOG_G[:, None])
    w_k = np.exp((CHUNK - 1.0 - idx)[None, :] * _LOG_G[:, None]) * scale
    w_q = np.broadcast_to(w_q[:, :, None], (RET_HEADS, CHUNK, RET_HD))
    w_k = np.broadcast_to(w_k[:, :, None], (RET_HEADS, CHUNK, RET_HD))
    g_c = np.exp(CHUNK * _LOG_G)
    return (decay.astype(np.float32), np.ascontiguousarray(w_q, np.float32),
            np.ascontiguousarray(w_k, np.float32), [float(v) for v in g_c])


def _swa_bias():
    i = np.arange(WINDOW)[:, None]
    j = np.arange(2 * WINDOW)[None, :]
    dist = i + WINDOW - j
    valid = (dist >= 0) & (dist <= WINDOW)
    out = np.zeros((SWA_KV_HEADS, SWA_REP * WINDOW, 2 * WINDOW), np.float32)
    for g in range(SWA_KV_HEADS):
        for r in range(SWA_REP):
            b = np.where(valid, -_SLOPES[g * SWA_REP + r] * dist, NEG)
            out[g, r * WINDOW:(r + 1) * WINDOW] = b
    return out


def _rmsnorm(x, g):
    ms = jnp.mean(x * x, axis=-1, keepdims=True)
    return (x * lax.rsqrt(ms + EPS)) * g


def _lane_block64(lane):
    return lax.shift_right_logical(lane, 6)


def _cparams(n_axes):
    return pltpu.CompilerParams(dimension_semantics=("arbitrary",) * n_axes, vmem_limit_bytes=VMEM_LIMIT)


def _layer_block(tail, layer, n_grid):
    zeros = (0,) * len(tail)
    if n_grid == 1:
        return pl.BlockSpec((None,) + tuple(tail), lambda i: (layer,) + zeros)
    return pl.BlockSpec((None,) + tuple(tail), lambda i, j: (layer,) + zeros)


def _norm_matmul_kernel(x_ref, g_ref, w_ref, o_ref, *, n_chunk):
    xn = _rmsnorm(x_ref[...], g_ref[...]).astype(BF16)
    n = w_ref.shape[1]
    for n0 in range(0, n, n_chunk):
        n1 = min(n0 + n_chunk, n)
        o_ref[:, n0:n1] = jnp.dot(xn, w_ref[:, n0:n1], preferred_element_type=F32).astype(o_ref.dtype)


def norm_matmul(x, g, w, layer, out_dtype, tm, n_chunk=512):
    m, k = x.shape
    n = w.shape[2]
    return pl.pallas_call(
        functools.partial(_norm_matmul_kernel, n_chunk=n_chunk),
        grid=(m // tm,),
        in_specs=[pl.BlockSpec((tm, k), lambda i: (i, 0)),
                  _layer_block((1, k), layer, 1),
                  _layer_block((k, n), layer, 1)],
        out_specs=pl.BlockSpec((tm, n), lambda i: (i, 0)),
        out_shape=jax.ShapeDtypeStruct((m, n), out_dtype),
        compiler_params=_cparams(1),
        name="norm_matmul",
    )(x, g.reshape(g.shape[0], 1, k), w)


def _matmul_residual_kernel(a_ref, w_ref, x_ref, o_ref):
    o_ref[...] = x_ref[...] + jnp.dot(a_ref[...], w_ref[...], preferred_element_type=F32)


def matmul_residual(a, w, layer, x, tm):
    m, k = a.shape
    n = w.shape[2]
    return pl.pallas_call(
        _matmul_residual_kernel,
        grid=(m // tm,),
        in_specs=[pl.BlockSpec((tm, k), lambda i: (i, 0)),
                  _layer_block((k, n), layer, 1),
                  pl.BlockSpec((tm, n), lambda i: (i, 0))],
        out_specs=pl.BlockSpec((tm, n), lambda i: (i, 0)),
        out_shape=jax.ShapeDtypeStruct((m, n), F32),
        compiler_params=_cparams(1),
        name="matmul_residual",
    )(a, w, x)


def _mlp_kernel(x_ref, g_ref, wu_ref, wd_ref, gf_ref, o_ref, act_ref, *, fc, final_norm):
    xn = _rmsnorm(x_ref[...], g_ref[...]).astype(BF16)
    for c0 in range(0, D_FF, fc):
        h = jnp.dot(xn, wu_ref[:, c0:c0 + fc], preferred_element_type=F32)
        act_ref[:, c0:c0 + fc] = jnp.square(jnp.maximum(h, 0.0)).astype(BF16)
    y = x_ref[...] + jnp.dot(act_ref[...], wd_ref[...], preferred_element_type=F32)
    if final_norm:
        y = _rmsnorm(y, gf_ref[...])
    o_ref[...] = y


def mlp_block(x, g, w_up, w_down, layer, g_final, final_norm, tm, fc=512):
    m, d = x.shape
    return pl.pallas_call(
        functools.partial(_mlp_kernel, fc=fc, final_norm=final_norm),
        grid=(m // tm,),
        in_specs=[pl.BlockSpec((tm, d), lambda i: (i, 0)),
                  _layer_block((1, d), layer, 1),
                  pl.BlockSpec((None, d, D_FF), lambda i: (layer, 0, 0), pipeline_mode=pl.Buffered(1)),
                  pl.BlockSpec((None, D_FF, d), lambda i: (layer, 0, 0), pipeline_mode=pl.Buffered(1)),
                  pl.BlockSpec((1, d), lambda i: (0, 0))],
        out_specs=pl.BlockSpec((tm, d), lambda i: (i, 0)),
        out_shape=jax.ShapeDtypeStruct((m, d), F32),
        scratch_shapes=[pltpu.VMEM((tm, D_FF), BF16)],
        compiler_params=_cparams(1),
        name="mlp_block",
    )(x, g.reshape(g.shape[0], 1, d), w_up, w_down, g_final.reshape(1, d))


def _mem_attention_tile(qm, mk, mv):
    t = qm.shape[0]
    lane_head = _lane_block64(lax.broadcasted_iota(jnp.int32, (t, MEM_W), 1))
    tok_head = _lane_block64(lax.broadcasted_iota(jnp.int32, (MEM_TOKENS, MEM_W), 1))
    probs = []
    vals = []
    for h in range(MEM_HEADS):
        qh = jnp.where(lane_head == h, qm, jnp.zeros_like(qm))
        s = lax.dot_general(qh, mk, NT_DIMS, preferred_element_type=F32)
        m = jnp.max(s, axis=-1, keepdims=True)
        e = jnp.exp(s - m)
        inv = 1.0 / jnp.sum(e, axis=-1, keepdims=True)
        probs.append((e * inv).astype(BF16))
        vals.append(jnp.where(tok_head == h, mv, jnp.zeros_like(mv)))
    return jnp.dot(jnp.concatenate(probs, axis=1), jnp.concatenate(vals, axis=0), preferred_element_type=F32)


def _mixer_a_kernel(proj_ref, mk_ref, mv_ref, decay_ref, wq_ref, wk_ref, o_ref, st_ref, s_scr, *, ts, g_c):
    @pl.when(pl.program_id(1) == 0)
    def _():
        s_scr[...] = jnp.zeros_like(s_scr)

    for c in range(ts // CHUNK):
        rows = slice(c * CHUNK, (c + 1) * CHUNK)
        for h in range(RET_HEADS):
            lo = h * RET_HD
            q = proj_ref[rows, lo:lo + RET_HD]
            k = proj_ref[rows, MIX_MAIN + lo:MIX_MAIN + lo + RET_HD]
            v = proj_ref[rows, 2 * MIX_MAIN + lo:2 * MIX_MAIN + lo + RET_HD]
            gate = proj_ref[rows, 3 * MIX_MAIN + lo:3 * MIX_MAIN + lo + RET_HD].astype(F32)
            s_prev = s_scr[h]
            sc = lax.dot_general(q, k, NT_DIMS, preferred_element_type=F32) * decay_ref[h]
            inner = jnp.dot(sc.astype(BF16), v, preferred_element_type=F32)
            cross = jnp.dot(q, s_prev.astype(BF16), preferred_element_type=F32) * wq_ref[h]
            o = inner + cross
            kw = (k.astype(F32) * wk_ref[h]).astype(BF16)
            kv = lax.dot_general(kw, v, TN_DIMS, preferred_element_type=F32)
            s_scr[h] = g_c[h] * s_prev + kv
            mu = jnp.mean(o, axis=-1, keepdims=True)
            oc = o - mu
            var = jnp.mean(oc * oc, axis=-1, keepdims=True)
            on = oc * lax.rsqrt(var + EPS)
            o_ref[rows, lo:lo + RET_HD] = (gate * jax.nn.sigmoid(gate) * on).astype(BF16)

    st_ref[...] = s_scr[...]
    qm = proj_ref[:, 4 * MIX_MAIN:4 * MIX_MAIN + MEM_W]
    o_ref[:, MIX_MAIN:] = _mem_attention_tile(qm, mk_ref[...], mv_ref[...]).astype(BF16)


def mixer_a_prompt(proj, mem_k, mem_v, layer, batch, seq, ts):
    ns = seq // ts
    decay, w_q, w_k, g_c = _retention_consts()
    c3 = lambda b, s: (0, 0, 0)
    pw = proj.shape[1]
    mem_spec = pl.BlockSpec((None, None, MEM_TOKENS, MEM_W), lambda b, s: (layer, b, 0, 0))
    return pl.pallas_call(
        functools.partial(_mixer_a_kernel, ts=ts, g_c=g_c),
        grid=(batch, ns),
        in_specs=[pl.BlockSpec((ts, pw), lambda b, s: (b * ns + s, 0)),
                  mem_spec, mem_spec,
                  pl.BlockSpec((RET_HEADS, CHUNK, CHUNK), c3),
                  pl.BlockSpec((RET_HEADS, CHUNK, RET_HD), c3),
                  pl.BlockSpec((RET_HEADS, CHUNK, RET_HD), c3)],
        out_specs=[pl.BlockSpec((ts, D_MODEL), lambda b, s: (b * ns + s, 0)),
                   pl.BlockSpec((None, RET_HEADS, RET_HD, RET_HD), lambda b, s: (b, 0, 0, 0))],
        out_shape=[jax.ShapeDtypeStruct((batch * seq, D_MODEL), BF16),
                   jax.ShapeDtypeStruct((batch, RET_HEADS, RET_HD, RET_HD), F32)],
        scratch_shapes=[pltpu.VMEM((RET_HEADS, RET_HD, RET_HD), F32)],
        compiler_params=_cparams(2),
        name="mixer_a_prompt",
    )(proj, mem_k, mem_v, jnp.asarray(decay), jnp.asarray(w_q), jnp.asarray(w_k))


def _mixer_b_kernel(q_ref, kvp_ref, kvc_ref, mk_ref, mv_ref, bias_ref, sink_ref, o_ref, *, ts):
    is_first = pl.program_id(1) == 0
    nblk = ts // WINDOW
    lane_group = _lane_block64(lax.broadcasted_iota(jnp.int32, (WINDOW, KV_W), 1))
    key_group = _lane_block64(lax.broadcasted_iota(jnp.int32, (2 * WINDOW, KV_W), 1))
    for i in range(nblk):
        r0 = i * WINDOW
        if i == 0:
            kv2 = jnp.concatenate([kvp_ref[...], kvc_ref[0:WINDOW, :]], axis=0)
        else:
            kv2 = kvc_ref[r0 - WINDOW:r0 + WINDOW, :]
        k2 = kv2[:, :KV_W]
        v2 = kv2[:, KV_W:]
        qs = [q_ref[r0:r0 + WINDOW, r * KV_W:(r + 1) * KV_W] for r in range(SWA_REP)]
        probs = []
        vals = []
        for g in range(SWA_KV_HEADS):
            sel = lane_group == g
            qg = jnp.concatenate([jnp.where(sel, qr, jnp.zeros_like(qr)) for qr in qs], axis=0)
            s = lax.dot_general(qg, k2, NT_DIMS, preferred_element_type=F32) + bias_ref[g]
            s_prev = s[:, :WINDOW]
            s_cur = s[:, WINDOW:]
            if i == 0:
                s_prev = jnp.where(is_first, NEG, s_prev)
            sink = sink_ref[g]
            m = jnp.maximum(jnp.max(jnp.maximum(s_prev, s_cur), axis=-1, keepdims=True), sink)
            e_prev = jnp.exp(s_prev - m)
            e_cur = jnp.exp(s_cur - m)
            inv = 1.0 / (jnp.sum(e_prev + e_cur, axis=-1, keepdims=True) + jnp.exp(sink - m))
            probs.append((e_prev * inv).astype(BF16))
            probs.append((e_cur * inv).astype(BF16))
            vals.append(jnp.where(key_group == g, v2, jnp.zeros_like(v2)))
        o = jnp.dot(jnp.concatenate(probs, axis=1), jnp.concatenate(vals, axis=0), preferred_element_type=F32)
        for r in range(SWA_REP):
            o_ref[r0:r0 + WINDOW, r * KV_W:(r + 1) * KV_W] = o[r * WINDOW:(r + 1) * WINDOW].astype(BF16)
    qm = q_ref[:, MIX_MAIN:MIX_MAIN + MEM_W]
    o_ref[:, MIX_MAIN:] = _mem_attention_tile(qm, mk_ref[...], mv_ref[...]).astype(BF16)


def mixer_b_prompt(qproj, kv, mem_k, mem_v, layer, sink_rows, sink_layer, batch, seq, ts):
    ns = seq // ts
    nw = ts // WINDOW
    bias = jnp.asarray(_swa_bias())
    mem_spec = pl.BlockSpec((None, None, MEM_TOKENS, MEM_W), lambda b, s: (layer, b, 0, 0))

    def prev_map(b, s):
        return (jnp.maximum(b * (seq // WINDOW) + s * nw - 1, b * (seq // WINDOW)), 0)

    return pl.pallas_call(
        functools.partial(_mixer_b_kernel, ts=ts),
        grid=(batch, ns),
        in_specs=[pl.BlockSpec((ts, D_MODEL), lambda b, s: (b * ns + s, 0)),
                  pl.BlockSpec((WINDOW, 2 * KV_W), prev_map),
                  pl.BlockSpec((ts, 2 * KV_W), lambda b, s: (b * ns + s, 0)),
                  mem_spec, mem_spec,
                  pl.BlockSpec((SWA_KV_HEADS, SWA_REP * WINDOW, 2 * WINDOW), lambda b, s: (0, 0, 0)),
                  pl.BlockSpec((None, SWA_KV_HEADS, SWA_REP * WINDOW, LANES), lambda b, s: (sink_layer, 0, 0, 0))],
        out_specs=pl.BlockSpec((ts, D_MODEL), lambda b, s: (b * ns + s, 0)),
        out_shape=jax.ShapeDtypeStruct((batch * seq, D_MODEL), BF16),
        compiler_params=_cparams(2),
        name="mixer_b_prompt",
    )(qproj, kv, kv, mem_k, mem_v, bias, sink_rows)


def _ret_decode_kernel(proj_ref, st_ref, o_ref, nst_ref, *, bb):
    scale = RET_HD ** -0.5
    for h in range(RET_HEADS):
        lo = h * RET_HD
        q8 = proj_ref[:, lo:lo + RET_HD]
        k8 = proj_ref[:, MIX_MAIN + lo:MIX_MAIN + lo + RET_HD] * scale
        v8 = proj_ref[:, 2 * MIX_MAIN + lo:2 * MIX_MAIN + lo + RET_HD]
        gate = proj_ref[:, 3 * MIX_MAIN + lo:3 * MIX_MAIN + lo + RET_HD]
        qt = q8.T
        kt = k8.T
        g = float(np.exp(_LOG_G[h]))
        rows = []
        for j in range(bb):
            qcol = jnp.broadcast_to(qt[:, j:j + 1], (RET_HD, RET_HD))
            kcol = jnp.broadcast_to(kt[:, j:j + 1], (RET_HD, RET_HD))
            s_new = g * st_ref[j, h] + kcol * v8[j:j + 1, :]
            nst_ref[j, h] = s_new
            rows.append(jnp.sum(qcol * s_new, axis=0, keepdims=True))
        o = jnp.concatenate(rows, axis=0)
        mu = jnp.mean(o, axis=-1, keepdims=True)
        oc = o - mu
        var = jnp.mean(oc * oc, axis=-1, keepdims=True)
        on = oc * lax.rsqrt(var + EPS)
        o_ref[:, lo:lo + RET_HD] = gate * jax.nn.sigmoid(gate) * on


def ret_decode(proj, state, layer, bb=8):
    nb = proj.shape[0]
    pw = proj.shape[1]
    return pl.pallas_call(
        functools.partial(_ret_decode_kernel, bb=bb),
        grid=(nb // bb,),
        in_specs=[pl.BlockSpec((bb, pw), lambda i: (i, 0)),
                  pl.BlockSpec((None, bb, RET_HEADS, RET_HD, RET_HD), lambda i: (layer, i, 0, 0, 0))],
        out_specs=[pl.BlockSpec((bb, MIX_MAIN), lambda i: (i, 0)),
                   pl.BlockSpec((bb, RET_HEADS, RET_HD, RET_HD), lambda i: (i, 0, 0, 0))],
        out_shape=[jax.ShapeDtypeStruct((nb, MIX_MAIN), F32),
                   jax.ShapeDtypeStruct((nb, RET_HEADS, RET_HD, RET_HD), F32)],
        compiler_params=_cparams(1),
        name="ret_decode",
    )(proj, state)


def _mem_decode_kernel(q_ref, kt_ref, vt_ref, o_ref, *, bb, q_off):
    row = lax.broadcasted_iota(jnp.int32, (8, MEM_W), 0)
    lane_head = _lane_block64(lax.broadcasted_iota(jnp.int32, (8, MEM_W), 1))
    sel = row == lane_head
    outs = []
    for j in range(bb):
        qrow = q_ref[j:j + 1, q_off:q_off + MEM_W]
        qexp = jnp.where(sel, jnp.broadcast_to(qrow, (8, MEM_W)), 0.0).astype(BF16)
        s = jnp.dot(qexp, kt_ref[j].astype(BF16), preferred_element_type=F32)
        m = jnp.max(s, axis=-1, keepdims=True)
        e = jnp.exp(s - m)
        den = jnp.sum(e, axis=-1, keepdims=True)
        o = lax.dot_general(e.astype(BF16), vt_ref[j].astype(BF16), NT_DIMS, preferred_element_type=F32) / den
        outs.append(jnp.sum(jnp.where(sel, o, 0.0), axis=0, keepdims=True))
    o_ref[...] = jnp.concatenate(outs, axis=0)


def mem_decode(proj, cache_kt, cache_vt, layer, q_off, bb=8):
    nb = proj.shape[0]
    pw = proj.shape[1]
    cmap = lambda i: (layer, i, 0, 0)
    return pl.pallas_call(
        functools.partial(_mem_decode_kernel, bb=bb, q_off=q_off),
        grid=(nb // bb,),
        in_specs=[pl.BlockSpec((bb, pw), lambda i: (i, 0)),
                  pl.BlockSpec((None, bb, MEM_W, MEM_TOKENS), cmap),
                  pl.BlockSpec((None, bb, MEM_W, MEM_TOKENS), cmap)],
        out_specs=pl.BlockSpec((bb, MEM_W), lambda i: (i, 0)),
        out_shape=jax.ShapeDtypeStruct((nb, MEM_W), F32),
        compiler_params=_cparams(1),
        name="mem_decode",
    )(proj, cache_kt, cache_vt)


def _swa_decode_kernel(q_ref, kn_ref, vn_ref, ckt_ref, cvt_ref, slope_ref, sink_ref, o_ref, *cache_out, bb):
    row = lax.broadcasted_iota(jnp.int32, (DEC_ROWS, KV_W), 0)
    lane_group = _lane_block64(lax.broadcasted_iota(jnp.int32, (DEC_ROWS, KV_W), 1))
    dist = (WINDOW - lax.broadcasted_iota(jnp.int32, (DEC_ROWS, WINDOW), 1)).astype(F32)
    bias = -slope_ref[...] * dist
    sink = sink_ref[...]
    pos = lax.broadcasted_iota(jnp.int32, (KV_W, WINDOW), 1)
    knt = kn_ref[...].T
    vnt = vn_ref[...].T
    outs = [[] for _ in range(SWA_REP)]
    for j in range(bb):
        qexp = jnp.zeros((DEC_ROWS, KV_W), F32)
        for r in range(SWA_REP):
            qr = jnp.broadcast_to(q_ref[j:j + 1, r * KV_W:(r + 1) * KV_W], (DEC_ROWS, KV_W))
            qexp = jnp.where(row == SWA_KV_HEADS * r + lane_group, qr, qexp)
        kt = ckt_ref[j]
        vt = cvt_ref[j]
        s = jnp.dot(qexp.astype(BF16), kt.astype(BF16), preferred_element_type=F32) + bias
        s_new = jnp.sum(qexp * kn_ref[j:j + 1, :], axis=-1, keepdims=True)
        m = jnp.maximum(jnp.maximum(jnp.max(s, axis=-1, keepdims=True), s_new), sink)
        e = jnp.exp(s - m)
        e_new = jnp.exp(s_new - m)
        inv = 1.0 / (jnp.sum(e, axis=-1, keepdims=True) + e_new + jnp.exp(sink - m))
        o = lax.dot_general(e.astype(BF16), vt.astype(BF16), NT_DIMS, preferred_element_type=F32)
        o = (o + jnp.concatenate([e_new, e_new], axis=1) * vn_ref[j:j + 1, :]) * jnp.concatenate([inv, inv], axis=1)
        for r in range(SWA_REP):
            keep = row == SWA_KV_HEADS * r + lane_group
            outs[r].append(jnp.sum(jnp.where(keep, o, 0.0), axis=0, keepdims=True))
        if cache_out:
            nkt_ref, nvt_ref = cache_out
            last = pos == WINDOW - 1
            nkt_ref[j] = jnp.where(last, jnp.broadcast_to(knt[:, j:j + 1], (KV_W, WINDOW)), pltpu.roll(kt, WINDOW - 1, 1))
            nvt_ref[j] = jnp.where(last, jnp.broadcast_to(vnt[:, j:j + 1], (KV_W, WINDOW)), pltpu.roll(vt, WINDOW - 1, 1))
    for r in range(SWA_REP):
        o_ref[:, r * KV_W:(r + 1) * KV_W] = jnp.concatenate(outs[r], axis=0)


def swa_decode(qproj, k_new, v_new, cache_kt, cache_vt, slope_rows, sink_rows, sink_layer, emit_cache, bb=8):
    nb = qproj.shape[0]
    pw = qproj.shape[1]
    cache_spec = pl.BlockSpec((bb, KV_W, WINDOW), lambda i: (i, 0, 0))
    cache_shape = jax.ShapeDtypeStruct((nb, KV_W, WINDOW), F32)
    n_cache = 2 if emit_cache else 0
    return pl.pallas_call(
        functools.partial(_swa_decode_kernel, bb=bb),
        grid=(nb // bb,),
        in_specs=[pl.BlockSpec((bb, pw), lambda i: (i, 0)),
                  pl.BlockSpec((bb, KV_W), lambda i: (i, 0)),
                  pl.BlockSpec((bb, KV_W), lambda i: (i, 0)),
                  cache_spec, cache_spec,
                  pl.BlockSpec((DEC_ROWS, LANES), lambda i: (0, 0)),
                  pl.BlockSpec((None, DEC_ROWS, LANES), lambda i: (sink_layer, 0, 0))],
        out_specs=[pl.BlockSpec((bb, MIX_MAIN), lambda i: (i, 0))] + [cache_spec] * n_cache,
        out_shape=[jax.ShapeDtypeStruct((nb, MIX_MAIN), F32)] + [cache_shape] * n_cache,
        compiler_params=_cparams(1),
        name="swa_decode",
    )(qproj, k_new, v_new, cache_kt, cache_vt, slope_rows, sink_rows)


def _permute_q_cols(w):
    lead = w.shape[:-1]
    return w.reshape(*lead, SWA_KV_HEADS, SWA_REP, SWA_HD).swapaxes(-3, -2).reshape(*lead, MIX_MAIN)


def kernel(x_prompt, x_sample, cache_mem_k, cache_mem_v, state_ret, cache_swa_k, cache_swa_v, mem_prompt,
           norm_mix, w_in_a, w_out_a, w_in_b, w_out_b, attn_sinks, norm_mem, w_mem_kv, norm_kv, w_kv,
           norm_mlp, w_up, w_down, norm_final):
    batch, seq, d = x_prompt.shape
    nb = x_sample.shape[0]
    tm = 512

    w_in_a16 = jnp.concatenate([w_in_a[:, :, :4 * MIX_MAIN], w_in_a[:, :, 4 * MIX_MAIN:] * ATTN_SCALE], axis=-1).astype(BF16)
    w_out_a16 = w_out_a.astype(BF16)
    w_in_b16 = (jnp.concatenate([_permute_q_cols(w_in_b[:, :, :MIX_MAIN]), w_in_b[:, :, MIX_MAIN:]], axis=-1)
                * ATTN_SCALE).astype(BF16)
    w_out_b16 = jnp.concatenate(
        [_permute_q_cols(w_out_b[:, :MIX_MAIN, :].swapaxes(1, 2)).swapaxes(1, 2), w_out_b[:, MIX_MAIN:, :]],
        axis=1).astype(BF16)
    w_mem16 = w_mem_kv.astype(BF16)
    w_kv16 = w_kv.astype(BF16).reshape(1, d, 2 * KV_W)
    norm_kv1 = norm_kv.reshape(1, d)
    w_up16 = w_up.astype(BF16)
    w_down16 = w_down.astype(BF16)

    sinks_gr = attn_sinks.reshape(N_B, SWA_KV_HEADS, SWA_REP)
    sink_prompt = jnp.broadcast_to(sinks_gr[:, :, :, None, None], (N_B, SWA_KV_HEADS, SWA_REP, WINDOW, LANES)
                                   ).reshape(N_B, SWA_KV_HEADS, SWA_REP * WINDOW, LANES)
    sink_dec = jnp.concatenate([sinks_gr.swapaxes(1, 2).reshape(N_B, SWA_HEADS),
                                jnp.zeros((N_B, DEC_ROWS - SWA_HEADS), F32)], axis=1)
    sink_dec = jnp.broadcast_to(sink_dec[:, :, None], (N_B, DEC_ROWS, LANES))
    slope_dec = np.zeros((DEC_ROWS, LANES), np.float32)
    for r in range(SWA_REP):
        for g in range(SWA_KV_HEADS):
            slope_dec[SWA_KV_HEADS * r + g, :] = _SLOPES[g * SWA_REP + r]
    slope_dec = jnp.asarray(slope_dec)

    memx = mem_prompt.reshape(batch * MEM_TOKENS, d)
    mem_kv = jnp.stack([norm_matmul(memx, norm_mem, w_mem16, l, F32, tm=256) for l in range(DEPTH)])
    mem_kv = mem_kv.reshape(DEPTH, batch, MEM_TOKENS, 2 * MEM_W)
    mem_k_prompt = mem_kv[..., :MEM_W].reshape(DEPTH, batch, MEM_TOKENS, MEM_HEADS, MEM_HD)
    mem_v_prompt = mem_kv[..., MEM_W:].reshape(DEPTH, batch, MEM_TOKENS, MEM_HEADS, MEM_HD)
    mk16 = mem_kv[..., :MEM_W].astype(BF16)
    mv16 = mem_kv[..., MEM_W:].astype(BF16)

    x = x_prompt.reshape(batch * seq, d)
    ret_states = []
    kv_p = kv_p16 = None
    for l in range(DEPTH):
        if l < N_A:
            proj = norm_matmul(x, norm_mix, w_in_a16, l, BF16, tm=tm, n_chunk=MIX_MAIN)
            attn, st = mixer_a_prompt(proj, mk16, mv16, l, batch, seq, ts=512)
            ret_states.append(st)
            x = matmul_residual(attn, w_out_a16, l, x, tm=tm)
        else:
            j = l - N_A
            if j == 0:
                kv_p = norm_matmul(x, norm_kv1, w_kv16, 0, F32, tm=tm)
                kv_p16 = kv_p.astype(BF16)
            qproj = norm_matmul(x, norm_mix[N_A:], w_in_b16, j, BF16, tm=tm)
            attn = mixer_b_prompt(qproj, kv_p16, mk16, mv16, l, sink_prompt, j, batch, seq, ts=512)
            x = matmul_residual(attn, w_out_b16, j, x, tm=tm)
        x = mlp_block(x, norm_mlp, w_up16, w_down16, l, norm_final, l == DEPTH - 1, tm=tm)
    y_prompt = x.reshape(batch, seq, d)
    ret_prompt = jnp.stack(ret_states)
    kv_p3 = kv_p.reshape(batch, seq, 2 * KV_W)
    swa_k_prompt = kv_p3[:, -WINDOW:, :KV_W].reshape(batch, WINDOW, SWA_KV_HEADS, SWA_HD)
    swa_v_prompt = kv_p3[:, -WINDOW:, KV_W:].reshape(batch, WINDOW, SWA_KV_HEADS, SWA_HD)

    cmkt = jnp.transpose(cache_mem_k, (0, 1, 3, 4, 2)).reshape(DEPTH, nb, MEM_W, MEM_TOKENS)
    cmvt = jnp.transpose(cache_mem_v, (0, 1, 3, 4, 2)).reshape(DEPTH, nb, MEM_W, MEM_TOKENS)
    cskt = jnp.transpose(cache_swa_k, (0, 2, 3, 1)).reshape(nb, KV_W, WINDOW)
    csvt = jnp.transpose(cache_swa_v, (0, 2, 3, 1)).reshape(nb, KV_W, WINDOW)
    xs = x_sample.reshape(nb, d)
    ret_states_s = []
    k_new = v_new = nkt = nvt = None
    for l in range(DEPTH):
        if l < N_A:
            proj = norm_matmul(xs, norm_mix, w_in_a16, l, F32, tm=nb, n_chunk=MIX_MAIN)
            o_ret, st = ret_decode(proj, state_ret, l)
            ret_states_s.append(st)
            o_mem = mem_decode(proj, cmkt, cmvt, l, 4 * MIX_MAIN)
            attn = jnp.concatenate([o_ret, o_mem], axis=-1).astype(BF16)
            xs = matmul_residual(attn, w_out_a16, l, xs, tm=nb)
        else:
            j = l - N_A
            if j == 0:
                kv_s = norm_matmul(xs, norm_kv1, w_kv16, 0, F32, tm=nb)
                k_new, v_new = kv_s[:, :KV_W], kv_s[:, KV_W:]
            qproj = norm_matmul(xs, norm_mix[N_A:], w_in_b16, j, F32, tm=nb)
            res = swa_decode(qproj, k_new, v_new, cskt, csvt, slope_dec, sink_dec, j, emit_cache=(j == 0))
            o_swa = res[0]
            if j == 0:
                nkt, nvt = res[1], res[2]
            o_mem = mem_decode(qproj, cmkt, cmvt, l, MIX_MAIN)
            attn = jnp.concatenate([o_swa, o_mem], axis=-1).astype(BF16)
            xs = matmul_residual(attn, w_out_b16, j, xs, tm=nb)
        xs = mlp_block(xs, norm_mlp, w_up16, w_down16, l, norm_final, l == DEPTH - 1, tm=nb)
    y_sample = xs.reshape(nb, 1, d)
    ret_sample = jnp.stack(ret_states_s)
    swa_k_sample = jnp.transpose(nkt.reshape(nb, SWA_KV_HEADS, SWA_HD, WINDOW), (0, 3, 1, 2))
    swa_v_sample = jnp.transpose(nvt.reshape(nb, SWA_KV_HEADS, SWA_HD, WINDOW), (0, 3, 1, 2))

    return (y_prompt, y_sample, ret_prompt, ret_sample, swa_k_prompt, swa_v_prompt, swa_k_sample, swa_v_sample,
            mem_k_prompt, mem_v_prompt)
```

```python
import functools
import math

import jax
import jax.numpy as jnp
import numpy as np
from jax import lax
from jax.experimental import pallas as pl
from jax.experimental.pallas import tpu as pltpu

F32 = jnp.float32
BF16 = jnp.bfloat16

D_MODEL = 1024
DEPTH = 4
N_A = 2
N_B = 2
MEM_TOKENS = 256
MEM_HEADS = 4
MEM_W = 256
MEM_HD = 64
MIX_MAIN = 768
RET_HEADS = 6
RET_HD = 128
CHUNK = 128
SWA_HEADS = 12
SWA_KV_HEADS = 4
SWA_REP = SWA_HEADS // SWA_KV_HEADS
SWA_HD = 64
KV_W = SWA_KV_HEADS * SWA_HD
WINDOW = 128
D_FF = 4096
EPS = 1e-6
NEG = -1e30
ATTN_SCALE = 0.125
LANES = 128
TILE = 512
DEC_ROWS = 16

VMEM_LIMIT = 56 * 1024 * 1024

NT_DIMS = (((1,), (1,)), ((), ()))
TN_DIMS = (((0,), (0,)), ((), ()))


def _alibi_slopes(n):
    def pow2(m):
        start = 2.0 ** (-8.0 / m)
        return [start ** (i + 1) for i in range(m)]

    if math.log2(n).is_integer():
        s = pow2(n)
    else:
        c = 2 ** int(math.floor(math.log2(n)))
        s = pow2(c) + pow2(2 * c)[0::2][: n - c]
    return np.asarray(s, np.float32)


_LOG_G = np.log1p(-(2.0 ** (-5.0 - np.arange(RET_HEADS)))).astype(np.float32).astype(np.float64)
_SLOPES = _alibi_slopes(SWA_HEADS).astype(np.float64)


def _retention_consts():
    idx = np.arange(CHUNK, dtype=np.float64)
    diff = idx[:, None] - idx[None, :]
    scale = RET_HD ** -0.5
    decay = np.where(diff >= 0, np.exp(np.maximum(diff, 0.0)[None] * _LOG_G[:, None, None]), 0.0) * scale
    w_q = np.exp((idx + 1.0)[None, :] * _LOG_G[:, None])
    w_k = np.exp((CHUNK - 1.0 - idx)[None, :] * _LOG_G[:, None]) * scale
    w_q = np.broadcast_to(w_q[:, :, None], (RET_HEADS, CHUNK, RET_HD))
    w_k = np.broadcast_to(w_k[:, :, None], (RET_HEADS, CHUNK, RET_HD))
    g_c = np.exp(CHUNK * _LOG_G)
    pair = lambda t: np.ascontiguousarray(np.concatenate([t[0::2], t[1::2]], axis=2), np.float32)
    return pair(decay), pair(w_q), pair(w_k), [float(v) for v in g_c]


def _swa_bias():
    i = np.arange(WINDOW)[:, None]
    j = np.arange(2 * WINDOW)[None, :]
    dist = i + WINDOW - j
    valid = (dist >= 0) & (dist <= WINDOW)
    out = np.zeros((SWA_KV_HEADS, SWA_REP * WINDOW, 2 * WINDOW), np.float32)
    for g in range(SWA_KV_HEADS):
        for r in range(SWA_REP):
            b = np.where(valid, -_SLOPES[g * SWA_REP + r] * dist, NEG)
            out[g, r * WINDOW:(r + 1) * WINDOW] = b
    return out


def _rmsnorm(x, g):
    ms = jnp.mean(x * x, axis=-1, keepdims=True)
    return (x * lax.rsqrt(ms + EPS)) * g


def _lane_block64(lane):
    return lax.shift_right_logical(lane, 6)


def _cparams(n_axes):
    return pltpu.CompilerParams(dimension_semantics=("arbitrary",) * n_axes, vmem_limit_bytes=VMEM_LIMIT)


def _layer_block(tail, layer, n_grid):
    zeros = (0,) * len(tail)
    if n_grid == 1:
        return pl.BlockSpec((None,) + tuple(tail), lambda i: (layer,) + zeros)
    return pl.BlockSpec((None,) + tuple(tail), lambda i, j: (layer,) + zeros)


def _norm_matmul_kernel(x_ref, g_ref, w_ref, o_ref, *, n_chunk):
    xn = _rmsnorm(x_ref[...], g_ref[...]).astype(BF16)
    n = w_ref.shape[1]
    for n0 in range(0, n, n_chunk):
        n1 = min(n0 + n_chunk, n)
        o_ref[:, n0:n1] = jnp.dot(xn, w_ref[:, n0:n1], preferred_element_type=F32).astype(o_ref.dtype)


def norm_matmul(x, g, w, layer, out_dtype, tm, n_chunk=512):
    m, k = x.shape
    n = w.shape[2]
    return pl.pallas_call(
        functools.partial(_norm_matmul_kernel, n_chunk=n_chunk),
        grid=(m // tm,),
        in_specs=[pl.BlockSpec((tm, k), lambda i: (i, 0)),
                  _layer_block((1, k), layer, 1),
                  _layer_block((k, n), layer, 1)],
        out_specs=pl.BlockSpec((tm, n), lambda i: (i, 0)),
        out_shape=jax.ShapeDtypeStruct((m, n), out_dtype),
        compiler_params=_cparams(1),
        name="norm_matmul",
    )(x, g.reshape(g.shape[0], 1, k), w)


def _matmul_residual_kernel(a_ref, w_ref, x_ref, o_ref):
    o_ref[...] = x_ref[...] + jnp.dot(a_ref[...], w_ref[...], preferred_element_type=F32)


def matmul_residual(a, w, layer, x, tm):
    m, k = a.shape
    n = w.shape[2]
    return pl.pallas_call(
        _matmul_residual_kernel,
        grid=(m // tm,),
        in_specs=[pl.BlockSpec((tm, k), lambda i: (i, 0)),
                  _layer_block((k, n), layer, 1),
                  pl.BlockSpec((tm, n), lambda i: (i, 0))],
        out_specs=pl.BlockSpec((tm, n), lambda i: (i, 0)),
        out_shape=jax.ShapeDtypeStruct((m, n), F32),
        compiler_params=_cparams(1),
        name="matmul_residual",
    )(a, w, x)


def _mlp_kernel(x_ref, g_ref, wu_ref, wd_ref, gf_ref, o_ref, act_ref, *, fc, final_norm):
    xn = _rmsnorm(x_ref[...], g_ref[...]).astype(BF16)
    for c0 in range(0, D_FF, fc):
        h = jnp.dot(xn, wu_ref[:, c0:c0 + fc], preferred_element_type=F32)
        act_ref[:, c0:c0 + fc] = jnp.square(jnp.maximum(h, 0.0)).astype(BF16)
    y = x_ref[...] + jnp.dot(act_ref[...], wd_ref[...], preferred_element_type=F32)
    if final_norm:
        y = _rmsnorm(y, gf_ref[...])
    o_ref[...] = y


def mlp_block(x, g, w_up, w_down, layer, g_final, final_norm, tm, fc=512):
    m, d = x.shape
    return pl.pallas_call(
        functools.partial(_mlp_kernel, fc=fc, final_norm=final_norm),
        grid=(m // tm,),
        in_specs=[pl.BlockSpec((tm, d), lambda i: (i, 0)),
                  _layer_block((1, d), layer, 1),
                  pl.BlockSpec((None, d, D_FF), lambda i: (layer, 0, 0), pipeline_mode=pl.Buffered(1)),
                  pl.BlockSpec((None, D_FF, d), lambda i: (layer, 0, 0), pipeline_mode=pl.Buffered(1)),
                  pl.BlockSpec((1, d), lambda i: (0, 0))],
        out_specs=pl.BlockSpec((tm, d), lambda i: (i, 0)),
        out_shape=jax.ShapeDtypeStruct((m, d), F32),
        scratch_shapes=[pltpu.VMEM((tm, D_FF), BF16)],
        compiler_params=_cparams(1),
        name="mlp_block",
    )(x, g.reshape(g.shape[0], 1, d), w_up, w_down, g_final.reshape(1, d))


def _mem_attention_tile(qm, mk, mv):
    t = qm.shape[0]
    lane_head = _lane_block64(lax.broadcasted_iota(jnp.int32, (t, MEM_W), 1))
    tok_head = _lane_block64(lax.broadcasted_iota(jnp.int32, (MEM_TOKENS, MEM_W), 1))
    probs = []
    vals = []
    for h in range(MEM_HEADS):
        qh = jnp.where(lane_head == h, qm, jnp.zeros_like(qm))
        s = lax.dot_general(qh, mk, NT_DIMS, preferred_element_type=F32)
        m = jnp.max(s, axis=-1, keepdims=True)
        e = jnp.exp(s - m)
        inv = 1.0 / jnp.sum(e, axis=-1, keepdims=True)
        probs.append((e * inv).astype(BF16))
        vals.append(jnp.where(tok_head == h, mv, jnp.zeros_like(mv)))
    return jnp.dot(jnp.concatenate(probs, axis=1), jnp.concatenate(vals, axis=0), preferred_element_type=F32)


def _project_tile(x_ref, g_ref, w_ref, p_ref, n_chunk):
    xn = _rmsnorm(x_ref[...], g_ref[...]).astype(BF16)
    n = w_ref.shape[1]
    for n0 in range(0, n, n_chunk):
        n1 = min(n0 + n_chunk, n)
        p_ref[:, n0:n1] = jnp.dot(xn, w_ref[:, n0:n1], preferred_element_type=F32).astype(BF16)


def _block_diag2(a, b):
    z = jnp.zeros_like(a)
    return jnp.concatenate([jnp.concatenate([a, z], axis=1), jnp.concatenate([z, b], axis=1)], axis=0)


def _group_norm_gate(o, gate):
    mu = jnp.mean(o, axis=-1, keepdims=True)
    oc = o - mu
    var = jnp.mean(oc * oc, axis=-1, keepdims=True)
    return (gate * jax.nn.sigmoid(gate) * (oc * lax.rsqrt(var + EPS))).astype(BF16)


def _retention_tile(p_ref, a_ref, s_scr, decay_ref, wq_ref, wk_ref, g_c):
    pw = 2 * RET_HD
    for c in range(TILE // CHUNK):
        rows = slice(c * CHUNK, (c + 1) * CHUNK)
        for hp in range(RET_HEADS // 2):
            lo = hp * pw
            ha, hb = 2 * hp, 2 * hp + 1
            q2 = p_ref[rows, lo:lo + pw]
            k2 = p_ref[rows, MIX_MAIN + lo:MIX_MAIN + lo + pw]
            v2 = p_ref[rows, 2 * MIX_MAIN + lo:2 * MIX_MAIN + lo + pw]
            gate2 = p_ref[rows, 3 * MIX_MAIN + lo:3 * MIX_MAIN + lo + pw].astype(F32)
            s_a = s_scr[ha]
            s_b = s_scr[hb]
            kbd = _block_diag2(k2[:, :RET_HD], k2[:, RET_HD:])
            sc = lax.dot_general(q2, kbd, NT_DIMS, preferred_element_type=F32) * decay_ref[hp]
            vbd = _block_diag2(v2[:, :RET_HD], v2[:, RET_HD:])
            sbd = _block_diag2(s_a.astype(BF16), s_b.astype(BF16))
            qw = (q2.astype(F32) * wq_ref[hp]).astype(BF16)
            o2 = jnp.dot(jnp.concatenate([sc.astype(BF16), qw], axis=1), jnp.concatenate([vbd, sbd], axis=0),
                         preferred_element_type=F32)
            kw = (k2.astype(F32) * wk_ref[hp]).astype(BF16)
            kv = lax.dot_general(kw, v2, TN_DIMS, preferred_element_type=F32)
            s_scr[ha] = g_c[ha] * s_a + kv[:RET_HD, :RET_HD]
            s_scr[hb] = g_c[hb] * s_b + kv[RET_HD:, RET_HD:]
            a_ref[rows, lo:lo + RET_HD] = _group_norm_gate(o2[:, :RET_HD], gate2[:, :RET_HD])
            a_ref[rows, lo + RET_HD:lo + pw] = _group_norm_gate(o2[:, RET_HD:], gate2[:, RET_HD:])


def _fused_a_kernel(xc_ref, xn_ref, g_ref, win_ref, wout_ref, mk_ref, mv_ref, decay_ref, wq_ref, wk_ref,
                    o_ref, st_ref, p0, p1, a_scr, s_scr, *, g_c, tiles_per_seq):
    t = pl.program_id(0)

    @pl.when(t == 0)
    def _():
        _project_tile(xc_ref, g_ref, win_ref, p0, MIX_MAIN)

    @pl.when(t % tiles_per_seq == 0)
    def _():
        s_scr[...] = jnp.zeros_like(s_scr)

    def step(p_cur, p_next):
        _project_tile(xn_ref, g_ref, win_ref, p_next, MIX_MAIN)
        _retention_tile(p_cur, a_scr, s_scr, decay_ref, wq_ref, wk_ref, g_c)
        st_ref[...] = s_scr[...]
        qm = p_cur[:, 4 * MIX_MAIN:4 * MIX_MAIN + MEM_W]
        a_scr[:, MIX_MAIN:] = _mem_attention_tile(qm, mk_ref[...], mv_ref[...]).astype(BF16)
        o_ref[...] = xc_ref[...] + jnp.dot(a_scr[...], wout_ref[...], preferred_element_type=F32)

    @pl.when(t % 2 == 0)
    def _():
        step(p0, p1)

    @pl.when(t % 2 == 1)
    def _():
        step(p1, p0)


def fused_mixer_a(x, g, w_in, w_out, mem_k, mem_v, layer, batch, seq):
    n_tiles = batch * seq // TILE
    tps = seq // TILE
    decay, w_q, w_k, g_c = _retention_consts()
    pw = w_in.shape[2]
    c3 = lambda t: (0, 0, 0)
    mem_spec = pl.BlockSpec((None, None, MEM_TOKENS, MEM_W), lambda t: (layer, t // tps, 0, 0))
    return pl.pallas_call(
        functools.partial(_fused_a_kernel, g_c=g_c, tiles_per_seq=tps),
        grid=(n_tiles,),
        in_specs=[pl.BlockSpec((TILE, D_MODEL), lambda t: (t, 0)),
                  pl.BlockSpec((TILE, D_MODEL), lambda t: (jnp.minimum(t + 1, n_tiles - 1), 0)),
                  _layer_block((1, D_MODEL), layer, 1),
                  pl.BlockSpec((None, D_MODEL, pw), lambda t: (layer, 0, 0), pipeline_mode=pl.Buffered(1)),
                  pl.BlockSpec((None, D_MODEL, D_MODEL), lambda t: (layer, 0, 0), pipeline_mode=pl.Buffered(1)),
                  mem_spec, mem_spec,
                  pl.BlockSpec((RET_HEADS // 2, CHUNK, 2 * CHUNK), c3),
                  pl.BlockSpec((RET_HEADS // 2, CHUNK, 2 * RET_HD), c3),
                  pl.BlockSpec((RET_HEADS // 2, CHUNK, 2 * RET_HD), c3)],
        out_specs=[pl.BlockSpec((TILE, D_MODEL), lambda t: (t, 0)),
                   pl.BlockSpec((None, RET_HEADS, RET_HD, RET_HD), lambda t: (t // tps, 0, 0, 0))],
        out_shape=[jax.ShapeDtypeStruct((batch * seq, D_MODEL), F32),
                   jax.ShapeDtypeStruct((batch, RET_HEADS, RET_HD, RET_HD), F32)],
        scratch_shapes=[pltpu.VMEM((TILE, pw), BF16), pltpu.VMEM((TILE, pw), BF16),
                        pltpu.VMEM((TILE, D_MODEL), BF16),
                        pltpu.VMEM((RET_HEADS, RET_HD, RET_HD), F32)],
        compiler_params=_cparams(1),
        name="fused_mixer_a",
    )(x, x, g.reshape(g.shape[0], 1, D_MODEL), w_in, w_out, mem_k, mem_v,
      jnp.asarray(decay), jnp.asarray(w_q), jnp.asarray(w_k))


def _swa_tile(p_ref, kvp_ref, kvc_ref, a_ref, bias_ref, sink_ref, is_first):
    lane_group = _lane_block64(lax.broadcasted_iota(jnp.int32, (WINDOW, KV_W), 1))
    key_group = _lane_block64(lax.broadcasted_iota(jnp.int32, (2 * WINDOW, KV_W), 1))
    for i in range(TILE // WINDOW):
        r0 = i * WINDOW
        if i == 0:
            kv2 = jnp.concatenate([kvp_ref[...], kvc_ref[0:WINDOW, :]], axis=0)
        else:
            kv2 = kvc_ref[r0 - WINDOW:r0 + WINDOW, :]
        k2 = kv2[:, :KV_W]
        v2 = kv2[:, KV_W:]
        qs = [p_ref[r0:r0 + WINDOW, r * KV_W:(r + 1) * KV_W] for r in range(SWA_REP)]
        probs = []
        vals = []
        for g in range(SWA_KV_HEADS):
            sel = lane_group == g
            qg = jnp.concatenate([jnp.where(sel, qr, jnp.zeros_like(qr)) for qr in qs], axis=0)
            s = lax.dot_general(qg, k2, NT_DIMS, preferred_element_type=F32) + bias_ref[g]
            s_prev = s[:, :WINDOW]
            s_cur = s[:, WINDOW:]
            if i == 0:
                s_prev = jnp.where(is_first, NEG, s_prev)
            sink = sink_ref[g]
            m = jnp.maximum(jnp.max(jnp.maximum(s_prev, s_cur), axis=-1, keepdims=True), sink)
            e_prev = jnp.exp(s_prev - m)
            e_cur = jnp.exp(s_cur - m)
            inv = 1.0 / (jnp.sum(e_prev + e_cur, axis=-1, keepdims=True) + jnp.exp(sink - m))
            probs.append((e_prev * inv).astype(BF16))
            probs.append((e_cur * inv).astype(BF16))
            vals.append(jnp.where(key_group == g, v2, jnp.zeros_like(v2)))
        o = jnp.dot(jnp.concatenate(probs, axis=1), jnp.concatenate(vals, axis=0), preferred_element_type=F32)
        for r in range(SWA_REP):
            a_ref[r0:r0 + WINDOW, r * KV_W:(r + 1) * KV_W] = o[r * WINDOW:(r + 1) * WINDOW].astype(BF16)


def _fused_b_kernel(xc_ref, xn_ref, g_ref, win_ref, wout_ref, kvp_ref, kvc_ref, mk_ref, mv_ref, bias_ref, sink_ref,
                    o_ref, p0, p1, a_scr, *, tiles_per_seq):
    t = pl.program_id(0)
    is_first = t % tiles_per_seq == 0

    @pl.when(t == 0)
    def _():
        _project_tile(xc_ref, g_ref, win_ref, p0, 512)

    def step(p_cur, p_next):
        _project_tile(xn_ref, g_ref, win_ref, p_next, 512)
        _swa_tile(p_cur, kvp_ref, kvc_ref, a_scr, bias_ref, sink_ref, is_first)
        qm = p_cur[:, MIX_MAIN:MIX_MAIN + MEM_W]
        a_scr[:, MIX_MAIN:] = _mem_attention_tile(qm, mk_ref[...], mv_ref[...]).astype(BF16)
        o_ref[...] = xc_ref[...] + jnp.dot(a_scr[...], wout_ref[...], preferred_element_type=F32)

    @pl.when(t % 2 == 0)
    def _():
        step(p0, p1)

    @pl.when(t % 2 == 1)
    def _():
        step(p1, p0)


def fused_mixer_b(x, g, w_in, w_out, kv, mem_k, mem_v, layer, sink_rows, sub_layer, batch, seq):
    n_tiles = batch * seq // TILE
    tps = seq // TILE
    nw = TILE // WINDOW
    bias = jnp.asarray(_swa_bias())
    mem_spec = pl.BlockSpec((None, None, MEM_TOKENS, MEM_W), lambda t: (layer, t // tps, 0, 0))

    def prev_map(t):
        return (jnp.maximum(t * nw - 1, (t // tps) * (seq // WINDOW)), 0)

    return pl.pallas_call(
        functools.partial(_fused_b_kernel, tiles_per_seq=tps),
        grid=(n_tiles,),
        in_specs=[pl.BlockSpec((TILE, D_MODEL), lambda t: (t, 0)),
                  pl.BlockSpec((TILE, D_MODEL), lambda t: (jnp.minimum(t + 1, n_tiles - 1), 0)),
                  _layer_block((1, D_MODEL), layer, 1),
                  pl.BlockSpec((None, D_MODEL, D_MODEL), lambda t: (sub_layer, 0, 0), pipeline_mode=pl.Buffered(1)),
                  pl.BlockSpec((None, D_MODEL, D_MODEL), lambda t: (sub_layer, 0, 0), pipeline_mode=pl.Buffered(1)),
                  pl.BlockSpec((WINDOW, 2 * KV_W), prev_map),
                  pl.BlockSpec((TILE, 2 * KV_W), lambda t: (t, 0)),
                  mem_spec, mem_spec,
                  pl.BlockSpec((SWA_KV_HEADS, SWA_REP * WINDOW, 2 * WINDOW), lambda t: (0, 0, 0)),
                  pl.BlockSpec((None, SWA_KV_HEADS, SWA_REP * WINDOW, LANES), lambda t: (sub_layer, 0, 0, 0))],
        out_specs=pl.BlockSpec((TILE, D_MODEL), lambda t: (t, 0)),
        out_shape=jax.ShapeDtypeStruct((batch * seq, D_MODEL), F32),
        scratch_shapes=[pltpu.VMEM((TILE, D_MODEL), BF16), pltpu.VMEM((TILE, D_MODEL), BF16),
                        pltpu.VMEM((TILE, D_MODEL), BF16)],
        compiler_params=_cparams(1),
        name="fused_mixer_b",
    )(x, x, g.reshape(g.shape[0], 1, D_MODEL), w_in, w_out, kv, kv, mem_k, mem_v, bias, sink_rows)


def _ret_decode_kernel(proj_ref, st_ref, o_ref, nst_ref, *, bb):
    scale = RET_HD ** -0.5
    for h in range(RET_HEADS):
        lo = h * RET_HD
        q8 = proj_ref[:, lo:lo + RET_HD]
        k8 = proj_ref[:, MIX_MAIN + lo:MIX_MAIN + lo + RET_HD] * scale
        v8 = proj_ref[:, 2 * MIX_MAIN + lo:2 * MIX_MAIN + lo + RET_HD]
        gate = proj_ref[:, 3 * MIX_MAIN + lo:3 * MIX_MAIN + lo + RET_HD]
        qt = q8.T
        kt = k8.T
        g = float(np.exp(_LOG_G[h]))
        rows = []
        for j in range(bb):
            qcol = jnp.broadcast_to(qt[:, j:j + 1], (RET_HD, RET_HD))
            kcol = jnp.broadcast_to(kt[:, j:j + 1], (RET_HD, RET_HD))
            s_new = g * st_ref[j, h] + kcol * v8[j:j + 1, :]
            nst_ref[j, h] = s_new
            rows.append(jnp.sum(qcol * s_new, axis=0, keepdims=True))
        o = jnp.concatenate(rows, axis=0)
        mu = jnp.mean(o, axis=-1, keepdims=True)
        oc = o - mu
        var = jnp.mean(oc * oc, axis=-1, keepdims=True)
        on = oc * lax.rsqrt(var + EPS)
        o_ref[:, lo:lo + RET_HD] = gate * jax.nn.sigmoid(gate) * on


def ret_decode(proj, state, layer, bb=8):
    nb = proj.shape[0]
    pw = proj.shape[1]
    return pl.pallas_call(
        functools.partial(_ret_decode_kernel, bb=bb),
        grid=(nb // bb,),
        in_specs=[pl.BlockSpec((bb, pw), lambda i: (i, 0)),
                  pl.BlockSpec((None, bb, RET_HEADS, RET_HD, RET_HD), lambda i: (layer, i, 0, 0, 0))],
        out_specs=[pl.BlockSpec((bb, MIX_MAIN), lambda i: (i, 0)),
                   pl.BlockSpec((bb, RET_HEADS, RET_HD, RET_HD), lambda i: (i, 0, 0, 0))],
        out_shape=[jax.ShapeDtypeStruct((nb, MIX_MAIN), F32),
                   jax.ShapeDtypeStruct((nb, RET_HEADS, RET_HD, RET_HD), F32)],
        compiler_params=_cparams(1),
        name="ret_decode",
    )(proj, state)


def _mem_decode_kernel(q_ref, kt_ref, vt_ref, o_ref, *, bb, q_off):
    row = lax.broadcasted_iota(jnp.int32, (8, MEM_W), 0)
    lane_head = _lane_block64(lax.broadcasted_iota(jnp.int32, (8, MEM_W), 1))
    sel = row == lane_head
    outs = []
    for j in range(bb):
        qrow = q_ref[j:j + 1, q_off:q_off + MEM_W]
        qexp = jnp.where(sel, jnp.broadcast_to(qrow, (8, MEM_W)), 0.0).astype(BF16)
        s = jnp.dot(qexp, kt_ref[j].astype(BF16), preferred_element_type=F32)
        m = jnp.max(s, axis=-1, keepdims=True)
        e = jnp.exp(s - m)
        den = jnp.sum(e, axis=-1, keepdims=True)
        o = lax.dot_general(e.astype(BF16), vt_ref[j].astype(BF16), NT_DIMS, preferred_element_type=F32) / den
        outs.append(jnp.sum(jnp.where(sel, o, 0.0), axis=0, keepdims=True))
    o_ref[...] = jnp.concatenate(outs, axis=0)


def mem_decode(proj, cache_kt, cache_vt, layer, q_off, bb=8):
    nb = proj.shape[0]
    pw = proj.shape[1]
    cmap = lambda i: (layer, i, 0, 0)
    return pl.pallas_call(
        functools.partial(_mem_decode_kernel, bb=bb, q_off=q_off),
        grid=(nb // bb,),
        in_specs=[pl.BlockSpec((bb, pw), lambda i: (i, 0)),
                  pl.BlockSpec((None, bb, MEM_W, MEM_TOKENS), cmap),
                  pl.BlockSpec((None, bb, MEM_W, MEM_TOKENS), cmap)],
        out_specs=pl.BlockSpec((bb, MEM_W), lambda i: (i, 0)),
        out_shape=jax.ShapeDtypeStruct((nb, MEM_W), F32),
        compiler_params=_cparams(1),
        name="mem_decode",
    )(proj, cache_kt, cache_vt)


def _swa_decode_kernel(q_ref, kn_ref, vn_ref, ckt_ref, cvt_ref, slope_ref, sink_ref, o_ref, *cache_out, bb):
    row = lax.broadcasted_iota(jnp.int32, (DEC_ROWS, KV_W), 0)
    lane_group = _lane_block64(lax.broadcasted_iota(jnp.int32, (DEC_ROWS, KV_W), 1))
    dist = (WINDOW - lax.broadcasted_iota(jnp.int32, (DEC_ROWS, WINDOW), 1)).astype(F32)
    bias = -slope_ref[...] * dist
    sink = sink_ref[...]
    pos = lax.broadcasted_iota(jnp.int32, (KV_W, WINDOW), 1)
    knt = kn_ref[...].T
    vnt = vn_ref[...].T
    outs = [[] for _ in range(SWA_REP)]
    for j in range(bb):
        qexp = jnp.zeros((DEC_ROWS, KV_W), F32)
        for r in range(SWA_REP):
            qr = jnp.broadcast_to(q_ref[j:j + 1, r * KV_W:(r + 1) * KV_W], (DEC_ROWS, KV_W))
            qexp = jnp.where(row == SWA_KV_HEADS * r + lane_group, qr, qexp)
        kt = ckt_ref[j]
        vt = cvt_ref[j]
        s = jnp.dot(qexp.astype(BF16), kt.astype(BF16), preferred_element_type=F32) + bias
        s_new = jnp.sum(qexp * kn_ref[j:j + 1, :], axis=-1, keepdims=True)
        m = jnp.maximum(jnp.maximum(jnp.max(s, axis=-1, keepdims=True), s_new), sink)
        e = jnp.exp(s - m)
        e_new = jnp.exp(s_new - m)
        inv = 1.0 / (jnp.sum(e, axis=-1, keepdims=True) + e_new + jnp.exp(sink - m))
        o = lax.dot_general(e.astype(BF16), vt.astype(BF16), NT_DIMS, preferred_element_type=F32)
        o = (o + jnp.concatenate([e_new, e_new], axis=1) * vn_ref[j:j + 1, :]) * jnp.concatenate([inv, inv], axis=1)
        for r in range(SWA_REP):
            keep = row == SWA_KV_HEADS * r + lane_group
            outs[r].append(jnp.sum(jnp.where(keep, o, 0.0), axis=0, keepdims=True))
        if cache_out:
            nkt_ref, nvt_ref = cache_out
            last = pos == WINDOW - 1
            nkt_ref[j] = jnp.where(last, jnp.broadcast_to(knt[:, j:j + 1], (KV_W, WINDOW)), pltpu.roll(kt, WINDOW - 1, 1))
            nvt_ref[j] = jnp.where(last, jnp.broadcast_to(vnt[:, j:j + 1], (KV_W, WINDOW)), pltpu.roll(vt, WINDOW - 1, 1))
    for r in range(SWA_REP):
        o_ref[:, r * KV_W:(r + 1) * KV_W] = jnp.concatenate(outs[r], axis=0)


def swa_decode(qproj, k_new, v_new, cache_kt, cache_vt, slope_rows, sink_rows, sink_layer, emit_cache, bb=8):
    nb = qproj.shape[0]
    pw = qproj.shape[1]
    cache_spec = pl.BlockSpec((bb, KV_W, WINDOW), lambda i: (i, 0, 0))
    cache_shape = jax.ShapeDtypeStruct((nb, KV_W, WINDOW), F32)
    n_cache = 2 if emit_cache else 0
    return pl.pallas_call(
        functools.partial(_swa_decode_kernel, bb=bb),
        grid=(nb // bb,),
        in_specs=[pl.BlockSpec((bb, pw), lambda i: (i, 0)),
                  pl.BlockSpec((bb, KV_W), lambda i: (i, 0)),
                  pl.BlockSpec((bb, KV_W), lambda i: (i, 0)),
                  cache_spec, cache_spec,
                  pl.BlockSpec((DEC_ROWS, LANES), lambda i: (0, 0)),
                  pl.BlockSpec((None, DEC_ROWS, LANES), lambda i: (sink_layer, 0, 0))],
        out_specs=[pl.BlockSpec((bb, MIX_MAIN), lambda i: (i, 0))] + [cache_spec] * n_cache,
        out_shape=[jax.ShapeDtypeStruct((nb, MIX_MAIN), F32)] + [cache_shape] * n_cache,
        compiler_params=_cparams(1),
        name="swa_decode",
    )(qproj, k_new, v_new, cache_kt, cache_vt, slope_rows, sink_rows)


def _permute_q_cols(w):
    lead = w.shape[:-1]
    return w.reshape(*lead, SWA_KV_HEADS, SWA_REP, SWA_HD).swapaxes(-3, -2).reshape(*lead, MIX_MAIN)


def kernel(x_prompt, x_sample, cache_mem_k, cache_mem_v, state_ret, cache_swa_k, cache_swa_v, mem_prompt,
           norm_mix, w_in_a, w_out_a, w_in_b, w_out_b, attn_sinks, norm_mem, w_mem_kv, norm_kv, w_kv,
           norm_mlp, w_up, w_down, norm_final):
    batch, seq, d = x_prompt.shape
    nb = x_sample.shape[0]
    tm = 512

    w_in_a16 = jnp.concatenate([w_in_a[:, :, :4 * MIX_MAIN], w_in_a[:, :, 4 * MIX_MAIN:] * ATTN_SCALE], axis=-1).astype(BF16)
    w_out_a16 = w_out_a.astype(BF16)
    w_in_b16 = (jnp.concatenate([_permute_q_cols(w_in_b[:, :, :MIX_MAIN]), w_in_b[:, :, MIX_MAIN:]], axis=-1)
                * ATTN_SCALE).astype(BF16)
    w_out_b16 = jnp.concatenate(
        [_permute_q_cols(w_out_b[:, :MIX_MAIN, :].swapaxes(1, 2)).swapaxes(1, 2), w_out_b[:, MIX_MAIN:, :]],
        axis=1).astype(BF16)
    w_mem16 = w_mem_kv.astype(BF16)
    w_kv16 = w_kv.astype(BF16).reshape(1, d, 2 * KV_W)
    norm_kv1 = norm_kv.reshape(1, d)
    w_up16 = w_up.astype(BF16)
    w_down16 = w_down.astype(BF16)

    sinks_gr = attn_sinks.reshape(N_B, SWA_KV_HEADS, SWA_REP)
    sink_prompt = jnp.broadcast_to(sinks_gr[:, :, :, None, None], (N_B, SWA_KV_HEADS, SWA_REP, WINDOW, LANES)
                                   ).reshape(N_B, SWA_KV_HEADS, SWA_REP * WINDOW, LANES)
    sink_dec = jnp.concatenate([sinks_gr.swapaxes(1, 2).reshape(N_B, SWA_HEADS),
                                jnp.zeros((N_B, DEC_ROWS - SWA_HEADS), F32)], axis=1)
    sink_dec = jnp.broadcast_to(sink_dec[:, :, None], (N_B, DEC_ROWS, LANES))
    slope_dec = np.zeros((DEC_ROWS, LANES), np.float32)
    for r in range(SWA_REP):
        for g in range(SWA_KV_HEADS):
            slope_dec[SWA_KV_HEADS * r + g, :] = _SLOPES[g * SWA_REP + r]
    slope_dec = jnp.asarray(slope_dec)

    memx = mem_prompt.reshape(batch * MEM_TOKENS, d)
    mem_kv = jnp.stack([norm_matmul(memx, norm_mem, w_mem16, l, F32, tm=256) for l in range(DEPTH)])
    mem_kv = mem_kv.reshape(DEPTH, batch, MEM_TOKENS, 2 * MEM_W)
    mem_k_prompt = mem_kv[..., :MEM_W].reshape(DEPTH, batch, MEM_TOKENS, MEM_HEADS, MEM_HD)
    mem_v_prompt = mem_kv[..., MEM_W:].reshape(DEPTH, batch, MEM_TOKENS, MEM_HEADS, MEM_HD)
    mk16 = mem_kv[..., :MEM_W].astype(BF16)
    mv16 = mem_kv[..., MEM_W:].astype(BF16)

    x = x_prompt.reshape(batch * seq, d)
    ret_states = []
    kv_p = kv_p16 = None
    for l in range(DEPTH):
        if l < N_A:
            x, st = fused_mixer_a(x, norm_mix, w_in_a16, w_out_a16, mk16, mv16, l, batch, seq)
            ret_states.append(st)
        else:
            j = l - N_A
            if j == 0:
                kv_p = norm_matmul(x, norm_kv1, w_kv16, 0, F32, tm=tm)
                kv_p16 = kv_p.astype(BF16)
            x = fused_mixer_b(x, norm_mix, w_in_b16, w_out_b16, kv_p16, mk16, mv16, l, sink_prompt, j, batch, seq)
        x = mlp_block(x, norm_mlp, w_up16, w_down16, l, norm_final, l == DEPTH - 1, tm=tm)
    y_prompt = x.reshape(batch, seq, d)
    ret_prompt = jnp.stack(ret_states)
    kv_p3 = kv_p.reshape(batch, seq, 2 * KV_W)
    swa_k_prompt = kv_p3[:, -WINDOW:, :KV_W].reshape(batch, WINDOW, SWA_KV_HEADS, SWA_HD)
    swa_v_prompt = kv_p3[:, -WINDOW:, KV_W:].reshape(batch, WINDOW, SWA_KV_HEADS, SWA_HD)

    cmkt = jnp.transpose(cache_mem_k, (0, 1, 3, 4, 2)).reshape(DEPTH, nb, MEM_W, MEM_TOKENS)
    cmvt = jnp.transpose(cache_mem_v, (0, 1, 3, 4, 2)).reshape(DEPTH, nb, MEM_W, MEM_TOKENS)
    cskt = jnp.transpose(cache_swa_k, (0, 2, 3, 1)).reshape(nb, KV_W, WINDOW)
    csvt = jnp.transpose(cache_swa_v, (0, 2, 3, 1)).reshape(nb, KV_W, WINDOW)
    xs = x_sample.reshape(nb, d)
    ret_states_s = []
    k_new = v_new = nkt = nvt = None
    for l in range(DEPTH):
        if l < N_A:
            proj = norm_matmul(xs, norm_mix, w_in_a16, l, F32, tm=nb, n_chunk=MIX_MAIN)
            o_ret, st = ret_decode(proj, state_ret, l)
            ret_states_s.append(st)
            o_mem = mem_decode(proj, cmkt, cmvt, l, 4 * MIX_MAIN)
            attn = jnp.concatenate([o_ret, o_mem], axis=-1).astype(BF16)
            xs = matmul_residual(attn, w_out_a16, l, xs, tm=nb)
        else:
            j = l - N_A
            if j == 0:
                kv_s = norm_matmul(xs, norm_kv1, w_kv16, 0, F32, tm=nb)
                k_new, v_new = kv_s[:, :KV_W], kv_s[:, KV_W:]
            qproj = norm_matmul(xs, norm_mix[N_A:], w_in_b16, j, F32, tm=nb)
            res = swa_decode(qproj, k_new, v_new, cskt, csvt, slope_dec, sink_dec, j, emit_cache=(j == 0))
            o_swa = res[0]
            if j == 0:
                nkt, nvt = res[1], res[2]
            o_mem = mem_decode(qproj, cmkt, cmvt, l, MIX_MAIN)
            attn = jnp.concatenate([o_swa, o_mem], axis=-1).astype(BF16)
            xs = matmul_residual(attn, w_out_b16, j, xs, tm=nb)
        xs = mlp_block(xs, norm_mlp, w_up16, w_down16, l, norm_final, l == DEPTH - 1, tm=nb)
    y_sample = xs.reshape(nb, 1, d)
    ret_sample = jnp.stack(ret_states_s)
    swa_k_sample = jnp.transpose(nkt.reshape(nb, SWA_KV_HEADS, SWA_HD, WINDOW), (0, 3, 1, 2))
    swa_v_sample = jnp.transpose(nvt.reshape(nb, SWA_KV_HEADS, SWA_HD, WINDOW), (0, 3, 1, 2))

    return (y_prompt, y_sample, ret_prompt, ret_sample, swa_k_prompt, swa_v_prompt, swa_k_sample, swa_v_sample,
            mem_k_prompt, mem_v_prompt)
```

```python
import functools
import math

import jax
import jax.numpy as jnp
import numpy as np
from jax import lax
from jax.experimental import pallas as pl
from jax.experimental.pallas import tpu as pltpu

F32 = jnp.float32
BF16 = jnp.bfloat16

D_MODEL = 1024
DEPTH = 4
N_A = 2
N_B = 2
MEM_TOKENS = 256
MEM_HEADS = 4
MEM_W = 256
MEM_HD = 64
MIX_MAIN = 768
RET_HEADS = 6
RET_HD = 128
CHUNK = 128
SWA_HEADS = 12
SWA_KV_HEADS = 4
SWA_REP = SWA_HEADS // SWA_KV_HEADS
SWA_HD = 64
KV_W = SWA_KV_HEADS * SWA_HD
WINDOW = 128
D_FF = 4096
EPS = 1e-6
NEG = -1e30
ATTN_SCALE = 0.125
LANES = 128
TILE = 512
DEC_ROWS = 16

VMEM_LIMIT = 56 * 1024 * 1024

NT_DIMS = (((1,), (1,)), ((), ()))
TN_DIMS = (((0,), (0,)), ((), ()))


def _alibi_slopes(n):
    def pow2(m):
        start = 2.0 ** (-8.0 / m)
        return [start ** (i + 1) for i in range(m)]

    if math.log2(n).is_integer():
        s = pow2(n)
    else:
        c = 2 ** int(math.floor(math.log2(n)))
        s = pow2(c) + pow2(2 * c)[0::2][: n - c]
    return np.asarray(s, np.float32)


_LOG_G = np.log1p(-(2.0 ** (-5.0 - np.arange(RET_HEADS)))).astype(np.float32).astype(np.float64)
_SLOPES = _alibi_slopes(SWA_HEADS).astype(np.float64)


def _retention_consts():
    idx = np.arange(CHUNK, dtype=np.float64)
    diff = idx[:, None] - idx[None, :]
    scale = RET_HD ** -0.5
    decay = np.where(diff >= 0, np.exp(np.maximum(diff, 0.0)[None] * _LOG_G[:, None, None]), 0.0) * scale
    w_q = np.exp((idx + 1.0)[None, :] * _LOG_G[:, None])
    w_k = np.exp((CHUNK - 1.0 - idx)[None, :] * _LOG_G[:, None]) * scale
    w_q = np.broadcast_to(w_q[:, :, None], (RET_HEADS, CHUNK, RET_HD))
    w_k = np.broadcast_to(w_k[:, :, None], (RET_HEADS, CHUNK, RET_HD))
    g_c = np.exp(CHUNK * _LOG_G)
    pair = lambda t: np.ascontiguousarray(np.concatenate([t[0::2], t[1::2]], axis=2), np.float32)
    return pair(decay), pair(w_q), pair(w_k), [float(v) for v in g_c]


def _swa_bias():
    i = np.arange(WINDOW)[:, None]
    j = np.arange(2 * WINDOW)[None, :]
    dist = i + WINDOW - j
    valid = (dist >= 0) & (dist <= WINDOW)
    out = np.zeros((SWA_KV_HEADS, SWA_REP * WINDOW, 2 * WINDOW), np.float32)
    for g in range(SWA_KV_HEADS):
        for r in range(SWA_REP):
            b = np.where(valid, -_SLOPES[g * SWA_REP + r] * dist, NEG)
            out[g, r * WINDOW:(r + 1) * WINDOW] = b
    return out


def _rmsnorm(x, g):
    ms = jnp.mean(x * x, axis=-1, keepdims=True)
    return (x * lax.rsqrt(ms + EPS)) * g


def _lane_block64(lane):
    return lax.shift_right_logical(lane, 6)


def _cparams(n_axes):
    return pltpu.CompilerParams(dimension_semantics=("arbitrary",) * n_axes, vmem_limit_bytes=VMEM_LIMIT)


def _layer_block(tail, layer, n_grid):
    zeros = (0,) * len(tail)
    if n_grid == 1:
        return pl.BlockSpec((None,) + tuple(tail), lambda i: (layer,) + zeros)
    return pl.BlockSpec((None,) + tuple(tail), lambda i, j: (layer,) + zeros)


def _norm_matmul_kernel(x_ref, g_ref, w_ref, o_ref, *, n_chunk):
    xn = _rmsnorm(x_ref[...], g_ref[...]).astype(BF16)
    n = w_ref.shape[1]
    for n0 in range(0, n, n_chunk):
        n1 = min(n0 + n_chunk, n)
        o_ref[:, n0:n1] = jnp.dot(xn, w_ref[:, n0:n1], preferred_element_type=F32).astype(o_ref.dtype)


def norm_matmul(x, g, w, layer, out_dtype, tm, n_chunk=512):
    m, k = x.shape
    n = w.shape[2]
    return pl.pallas_call(
        functools.partial(_norm_matmul_kernel, n_chunk=n_chunk),
        grid=(m // tm,),
        in_specs=[pl.BlockSpec((tm, k), lambda i: (i, 0)),
                  _layer_block((1, k), layer, 1),
                  _layer_block((k, n), layer, 1)],
        out_specs=pl.BlockSpec((tm, n), lambda i: (i, 0)),
        out_shape=jax.ShapeDtypeStruct((m, n), out_dtype),
        compiler_params=_cparams(1),
        name="norm_matmul",
    )(x, g.reshape(g.shape[0], 1, k), w)


def _matmul_residual_kernel(a_ref, w_ref, x_ref, o_ref):
    o_ref[...] = x_ref[...] + jnp.dot(a_ref[...], w_ref[...], preferred_element_type=F32)


def matmul_residual(a, w, layer, x, tm):
    m, k = a.shape
    n = w.shape[2]
    return pl.pallas_call(
        _matmul_residual_kernel,
        grid=(m // tm,),
        in_specs=[pl.BlockSpec((tm, k), lambda i: (i, 0)),
                  _layer_block((k, n), layer, 1),
                  pl.BlockSpec((tm, n), lambda i: (i, 0))],
        out_specs=pl.BlockSpec((tm, n), lambda i: (i, 0)),
        out_shape=jax.ShapeDtypeStruct((m, n), F32),
        compiler_params=_cparams(1),
        name="matmul_residual",
    )(a, w, x)


def _mlp_kernel(x_ref, g_ref, wu_ref, wd_ref, gf_ref, *rest, fc, final_norm, with_kv):
    if with_kv:
        gkv_ref, wkv_ref, o_ref, kv_ref, kv16_ref, act_ref = rest
    else:
        o_ref, act_ref = rest
    xn = _rmsnorm(x_ref[...], g_ref[...]).astype(BF16)
    for c0 in range(0, D_FF, fc):
        h = jnp.dot(xn, wu_ref[:, c0:c0 + fc], preferred_element_type=F32)
        act_ref[:, c0:c0 + fc] = jnp.square(jnp.maximum(h, 0.0)).astype(BF16)
    y = x_ref[...] + jnp.dot(act_ref[...], wd_ref[...], preferred_element_type=F32)
    if with_kv:
        kv = jnp.dot(_rmsnorm(y, gkv_ref[...]).astype(BF16), wkv_ref[...], preferred_element_type=F32)
        kv_ref[...] = kv
        kv16_ref[...] = kv.astype(BF16)
    if final_norm:
        y = _rmsnorm(y, gf_ref[...])
    o_ref[...] = y


def mlp_block(x, g, w_up, w_down, layer, g_final, final_norm, tm, kv_proj=None, fc=512):
    m, d = x.shape
    with_kv = kv_proj is not None
    in_specs = [pl.BlockSpec((tm, d), lambda i: (i, 0)),
                _layer_block((1, d), layer, 1),
                pl.BlockSpec((None, d, D_FF), lambda i: (layer, 0, 0), pipeline_mode=pl.Buffered(1)),
                pl.BlockSpec((None, D_FF, d), lambda i: (layer, 0, 0), pipeline_mode=pl.Buffered(1)),
                pl.BlockSpec((1, d), lambda i: (0, 0))]
    args = [x, g.reshape(g.shape[0], 1, d), w_up, w_down, g_final.reshape(1, d)]
    out_specs = [pl.BlockSpec((tm, d), lambda i: (i, 0))]
    out_shape = [jax.ShapeDtypeStruct((m, d), F32)]
    if with_kv:
        in_specs += [pl.BlockSpec((1, d), lambda i: (0, 0)),
                     pl.BlockSpec((d, 2 * KV_W), lambda i: (0, 0), pipeline_mode=pl.Buffered(1))]
        args += list(kv_proj)
        out_specs += [pl.BlockSpec((tm, 2 * KV_W), lambda i: (i, 0))] * 2
        out_shape += [jax.ShapeDtypeStruct((m, 2 * KV_W), F32), jax.ShapeDtypeStruct((m, 2 * KV_W), BF16)]
    res = pl.pallas_call(
        functools.partial(_mlp_kernel, fc=fc, final_norm=final_norm, with_kv=with_kv),
        grid=(m // tm,),
        in_specs=in_specs,
        out_specs=out_specs,
        out_shape=out_shape,
        scratch_shapes=[pltpu.VMEM((tm, D_FF), BF16)],
        compiler_params=_cparams(1),
        name="mlp_block",
    )(*args)
    return res if with_kv else res[0]


def _mem_attention_tile(qm, mk, mv):
    t = qm.shape[0]
    lane_head = _lane_block64(lax.broadcasted_iota(jnp.int32, (t, MEM_W), 1))
    tok_head = _lane_block64(lax.broadcasted_iota(jnp.int32, (MEM_TOKENS, MEM_W), 1))
    probs = []
    vals = []
    for h in range(MEM_HEADS):
        qh = jnp.where(lane_head == h, qm, jnp.zeros_like(qm))
        s = lax.dot_general(qh, mk, NT_DIMS, preferred_element_type=F32)
        m = jnp.max(s, axis=-1, keepdims=True)
        e = jnp.exp(s - m)
        inv = 1.0 / jnp.sum(e, axis=-1, keepdims=True)
        probs.append((e * inv).astype(BF16))
        vals.append(jnp.where(tok_head == h, mv, jnp.zeros_like(mv)))
    return jnp.dot(jnp.concatenate(probs, axis=1), jnp.concatenate(vals, axis=0), preferred_element_type=F32)


def _project_tile(x_ref, g_ref, w_ref, p_ref, n_chunk):
    xn = _rmsnorm(x_ref[...], g_ref[...]).astype(BF16)
    n = w_ref.shape[1]
    for n0 in range(0, n, n_chunk):
        n1 = min(n0 + n_chunk, n)
        p_ref[:, n0:n1] = jnp.dot(xn, w_ref[:, n0:n1], preferred_element_type=F32).astype(BF16)


def _block_diag2(a, b):
    z = jnp.zeros_like(a)
    return jnp.concatenate([jnp.concatenate([a, z], axis=1), jnp.concatenate([z, b], axis=1)], axis=0)


def _group_norm_gate(o, gate):
    mu = jnp.mean(o, axis=-1, keepdims=True)
    oc = o - mu
    var = jnp.mean(oc * oc, axis=-1, keepdims=True)
    return (gate * jax.nn.sigmoid(gate) * (oc * lax.rsqrt(var + EPS))).astype(BF16)


def _retention_tile(p_ref, a_ref, s_scr, decay_ref, wq_ref, wk_ref, g_c):
    pw = 2 * RET_HD
    for c in range(TILE // CHUNK):
        rows = slice(c * CHUNK, (c + 1) * CHUNK)
        for hp in range(RET_HEADS // 2):
            lo = hp * pw
            ha, hb = 2 * hp, 2 * hp + 1
            q2 = p_ref[rows, lo:lo + pw]
            k2 = p_ref[rows, MIX_MAIN + lo:MIX_MAIN + lo + pw]
            v2 = p_ref[rows, 2 * MIX_MAIN + lo:2 * MIX_MAIN + lo + pw]
            gate2 = p_ref[rows, 3 * MIX_MAIN + lo:3 * MIX_MAIN + lo + pw].astype(F32)
            s_a = s_scr[ha]
            s_b = s_scr[hb]
            kbd = _block_diag2(k2[:, :RET_HD], k2[:, RET_HD:])
            sc = lax.dot_general(q2, kbd, NT_DIMS, preferred_element_type=F32) * decay_ref[hp]
            vbd = _block_diag2(v2[:, :RET_HD], v2[:, RET_HD:])
            sbd = _block_diag2(s_a.astype(BF16), s_b.astype(BF16))
            qw = (q2.astype(F32) * wq_ref[hp]).astype(BF16)
            o2 = jnp.dot(jnp.concatenate([sc.astype(BF16), qw], axis=1), jnp.concatenate([vbd, sbd], axis=0),
                         preferred_element_type=F32)
            kw = (k2.astype(F32) * wk_ref[hp]).astype(BF16)
            kv = lax.dot_general(kw, v2, TN_DIMS, preferred_element_type=F32)
            s_scr[ha] = g_c[ha] * s_a + kv[:RET_HD, :RET_HD]
            s_scr[hb] = g_c[hb] * s_b + kv[RET_HD:, RET_HD:]
            a_ref[rows, lo:lo + RET_HD] = _group_norm_gate(o2[:, :RET_HD], gate2[:, :RET_HD])
            a_ref[rows, lo + RET_HD:lo + pw] = _group_norm_gate(o2[:, RET_HD:], gate2[:, RET_HD:])


def _fused_a_kernel(xc_ref, xn_ref, g_ref, win_ref, wout_ref, mk_ref, mv_ref, decay_ref, wq_ref, wk_ref,
                    o_ref, st_ref, p0, p1, a_scr, s_scr, *, g_c, tiles_per_seq):
    t = pl.program_id(0)

    @pl.when(t == 0)
    def _():
        _project_tile(xc_ref, g_ref, win_ref, p0, MIX_MAIN)

    @pl.when(t % tiles_per_seq == 0)
    def _():
        s_scr[...] = jnp.zeros_like(s_scr)

    def step(p_cur, p_next):
        _project_tile(xn_ref, g_ref, win_ref, p_next, MIX_MAIN)
        _retention_tile(p_cur, a_scr, s_scr, decay_ref, wq_ref, wk_ref, g_c)
        st_ref[...] = s_scr[...]
        qm = p_cur[:, 4 * MIX_MAIN:4 * MIX_MAIN + MEM_W]
        a_scr[:, MIX_MAIN:] = _mem_attention_tile(qm, mk_ref[...], mv_ref[...]).astype(BF16)
        o_ref[...] = xc_ref[...] + jnp.dot(a_scr[...], wout_ref[...], preferred_element_type=F32)

    @pl.when(t % 2 == 0)
    def _():
        step(p0, p1)

    @pl.when(t % 2 == 1)
    def _():
        step(p1, p0)


def fused_mixer_a(x, g, w_in, w_out, mem_k, mem_v, layer, batch, seq):
    n_tiles = batch * seq // TILE
    tps = seq // TILE
    decay, w_q, w_k, g_c = _retention_consts()
    pw = w_in.shape[2]
    c3 = lambda t: (0, 0, 0)
    mem_spec = pl.BlockSpec((None, None, MEM_TOKENS, MEM_W), lambda t: (layer, t // tps, 0, 0))
    return pl.pallas_call(
        functools.partial(_fused_a_kernel, g_c=g_c, tiles_per_seq=tps),
        grid=(n_tiles,),
        in_specs=[pl.BlockSpec((TILE, D_MODEL), lambda t: (t, 0)),
                  pl.BlockSpec((TILE, D_MODEL), lambda t: (jnp.minimum(t + 1, n_tiles - 1), 0)),
                  _layer_block((1, D_MODEL), layer, 1),
                  pl.BlockSpec((None, D_MODEL, pw), lambda t: (layer, 0, 0), pipeline_mode=pl.Buffered(1)),
                  pl.BlockSpec((None, D_MODEL, D_MODEL), lambda t: (layer, 0, 0), pipeline_mode=pl.Buffered(1)),
                  mem_spec, mem_spec,
                  pl.BlockSpec((RET_HEADS // 2, CHUNK, 2 * CHUNK), c3),
                  pl.BlockSpec((RET_HEADS // 2, CHUNK, 2 * RET_HD), c3),
                  pl.BlockSpec((RET_HEADS // 2, CHUNK, 2 * RET_HD), c3)],
        out_specs=[pl.BlockSpec((TILE, D_MODEL), lambda t: (t, 0)),
                   pl.BlockSpec((None, RET_HEADS, RET_HD, RET_HD), lambda t: (t // tps, 0, 0, 0))],
        out_shape=[jax.ShapeDtypeStruct((batch * seq, D_MODEL), F32),
                   jax.ShapeDtypeStruct((batch, RET_HEADS, RET_HD, RET_HD), F32)],
        scratch_shapes=[pltpu.VMEM((TILE, pw), BF16), pltpu.VMEM((TILE, pw), BF16),
                        pltpu.VMEM((TILE, D_MODEL), BF16),
                        pltpu.VMEM((RET_HEADS, RET_HD, RET_HD), F32)],
        compiler_params=_cparams(1),
        name="fused_mixer_a",
    )(x, x, g.reshape(g.shape[0], 1, D_MODEL), w_in, w_out, mem_k, mem_v,
      jnp.asarray(decay), jnp.asarray(w_q), jnp.asarray(w_k))


def _swa_tile(p_ref, kvp_ref, kvc_ref, a_ref, bias_ref, sink_ref, is_first):
    lane_group = _lane_block64(lax.broadcasted_iota(jnp.int32, (WINDOW, KV_W), 1))
    key_group = _lane_block64(lax.broadcasted_iota(jnp.int32, (2 * WINDOW, KV_W), 1))
    for i in range(TILE // WINDOW):
        r0 = i * WINDOW
        if i == 0:
            kv2 = jnp.concatenate([kvp_ref[...], kvc_ref[0:WINDOW, :]], axis=0)
        else:
            kv2 = kvc_ref[r0 - WINDOW:r0 + WINDOW, :]
        k2 = kv2[:, :KV_W]
        v2 = kv2[:, KV_W:]
        qs = [p_ref[r0:r0 + WINDOW, r * KV_W:(r + 1) * KV_W] for r in range(SWA_REP)]
        probs = []
        vals = []
        for g in range(SWA_KV_HEADS):
            sel = lane_group == g
            qg = jnp.concatenate([jnp.where(sel, qr, jnp.zeros_like(qr)) for qr in qs], axis=0)
            s = lax.dot_general(qg, k2, NT_DIMS, preferred_element_type=F32) + bias_ref[g]
            s_prev = s[:, :WINDOW]
            s_cur = s[:, WINDOW:]
            if i == 0:
                s_prev = jnp.where(is_first, NEG, s_prev)
            sink = sink_ref[g]
            m = jnp.maximum(jnp.max(jnp.maximum(s_prev, s_cur), axis=-1, keepdims=True), sink)
            e_prev = jnp.exp(s_prev - m)
            e_cur = jnp.exp(s_cur - m)
            inv = 1.0 / (jnp.sum(e_prev + e_cur, axis=-1, keepdims=True) + jnp.exp(sink - m))
            probs.append((e_prev * inv).astype(BF16))
            probs.append((e_cur * inv).astype(BF16))
            vals.append(jnp.where(key_group == g, v2, jnp.zeros_like(v2)))
        o = jnp.dot(jnp.concatenate(probs, axis=1), jnp.concatenate(vals, axis=0), preferred_element_type=F32)
        for r in range(SWA_REP):
            a_ref[r0:r0 + WINDOW, r * KV_W:(r + 1) * KV_W] = o[r * WINDOW:(r + 1) * WINDOW].astype(BF16)


def _fused_b_kernel(xc_ref, xn_ref, g_ref, win_ref, wout_ref, kvp_ref, kvc_ref, mk_ref, mv_ref, bias_ref, sink_ref,
                    o_ref, p0, p1, a_scr, *, tiles_per_seq):
    t = pl.program_id(0)
    is_first = t % tiles_per_seq == 0

    @pl.when(t == 0)
    def _():
        _project_tile(xc_ref, g_ref, win_ref, p0, 512)

    def step(p_cur, p_next):
        _project_tile(xn_ref, g_ref, win_ref, p_next, 512)
        _swa_tile(p_cur, kvp_ref, kvc_ref, a_scr, bias_ref, sink_ref, is_first)
        qm = p_cur[:, MIX_MAIN:MIX_MAIN + MEM_W]
        a_scr[:, MIX_MAIN:] = _mem_attention_tile(qm, mk_ref[...], mv_ref[...]).astype(BF16)
        o_ref[...] = xc_ref[...] + jnp.dot(a_scr[...], wout_ref[...], preferred_element_type=F32)

    @pl.when(t % 2 == 0)
    def _():
        step(p0, p1)

    @pl.when(t % 2 == 1)
    def _():
        step(p1, p0)


def fused_mixer_b(x, g, w_in, w_out, kv, mem_k, mem_v, layer, sink_rows, sub_layer, batch, seq):
    n_tiles = batch * seq // TILE
    tps = seq // TILE
    nw = TILE // WINDOW
    bias = jnp.asarray(_swa_bias())
    mem_spec = pl.BlockSpec((None, None, MEM_TOKENS, MEM_W), lambda t: (layer, t // tps, 0, 0))

    def prev_map(t):
        return (jnp.maximum(t * nw - 1, (t // tps) * (seq // WINDOW)), 0)

    return pl.pallas_call(
        functools.partial(_fused_b_kernel, tiles_per_seq=tps),
        grid=(n_tiles,),
        in_specs=[pl.BlockSpec((TILE, D_MODEL), lambda t: (t, 0)),
                  pl.BlockSpec((TILE, D_MODEL), lambda t: (jnp.minimum(t + 1, n_tiles - 1), 0)),
                  _layer_block((1, D_MODEL), layer, 1),
                  pl.BlockSpec((None, D_MODEL, D_MODEL), lambda t: (sub_layer, 0, 0), pipeline_mode=pl.Buffered(1)),
                  pl.BlockSpec((None, D_MODEL, D_MODEL), lambda t: (sub_layer, 0, 0), pipeline_mode=pl.Buffered(1)),
                  pl.BlockSpec((WINDOW, 2 * KV_W), prev_map),
                  pl.BlockSpec((TILE, 2 * KV_W), lambda t: (t, 0)),
                  mem_spec, mem_spec,
                  pl.BlockSpec((SWA_KV_HEADS, SWA_REP * WINDOW, 2 * WINDOW), lambda t: (0, 0, 0)),
                  pl.BlockSpec((None, SWA_KV_HEADS, SWA_REP * WINDOW, LANES), lambda t: (sub_layer, 0, 0, 0))],
        out_specs=pl.BlockSpec((TILE, D_MODEL), lambda t: (t, 0)),
        out_shape=jax.ShapeDtypeStruct((batch * seq, D_MODEL), F32),
        scratch_shapes=[pltpu.VMEM((TILE, D_MODEL), BF16), pltpu.VMEM((TILE, D_MODEL), BF16),
                        pltpu.VMEM((TILE, D_MODEL), BF16)],
        compiler_params=_cparams(1),
        name="fused_mixer_b",
    )(x, x, g.reshape(g.shape[0], 1, D_MODEL), w_in, w_out, kv, kv, mem_k, mem_v, bias, sink_rows)


def _ret_decode_kernel(proj_ref, st_ref, *rest, bb):
    o_ref, nst_ref = rest[-2:]
    scale = RET_HD ** -0.5
    row = lax.broadcasted_iota(jnp.int32, (bb, bb * RET_HD), 0)
    lane_seq = lax.shift_right_logical(lax.broadcasted_iota(jnp.int32, (bb, bb * RET_HD), 1), 7)
    own = row == lane_seq
    for h in range(RET_HEADS):
        lo = h * RET_HD
        q8 = proj_ref[:, lo:lo + RET_HD]
        k8 = proj_ref[:, MIX_MAIN + lo:MIX_MAIN + lo + RET_HD] * scale
        v8 = proj_ref[:, 2 * MIX_MAIN + lo:2 * MIX_MAIN + lo + RET_HD]
        gate = proj_ref[:, 3 * MIX_MAIN + lo:3 * MIX_MAIN + lo + RET_HD]
        g = float(np.exp(_LOG_G[h]))
        vexp = jnp.where(own, jnp.concatenate([v8] * bb, axis=1), 0.0).astype(BF16)
        outer = lax.dot_general(k8.astype(BF16), vexp, TN_DIMS, preferred_element_type=F32)
        cols = []
        for j in range(bb):
            s_new = g * st_ref[j, h] + outer[:, j * RET_HD:(j + 1) * RET_HD]
            nst_ref[j, h] = s_new
            cols.append(s_new.astype(BF16))
        o_all = jnp.dot(q8.astype(BF16), jnp.concatenate(cols, axis=1), preferred_element_type=F32)
        o_all = jnp.where(own, o_all, 0.0)
        o = o_all[:, :RET_HD]
        for j in range(1, bb):
            o = o + o_all[:, j * RET_HD:(j + 1) * RET_HD]
        mu = jnp.mean(o, axis=-1, keepdims=True)
        oc = o - mu
        var = jnp.mean(oc * oc, axis=-1, keepdims=True)
        on = oc * lax.rsqrt(var + EPS)
        o_ref[:, lo:lo + RET_HD] = gate * jax.nn.sigmoid(gate) * on


def ret_decode(proj, state, layer, stacked_out=None, bb=8):
    nb = proj.shape[0]
    pw = proj.shape[1]
    in_specs = [pl.BlockSpec((bb, pw), lambda i: (i, 0)),
                pl.BlockSpec((None, bb, RET_HEADS, RET_HD, RET_HD), lambda i: (layer, i, 0, 0, 0))]
    args = [proj, state]
    aliases = {}
    if stacked_out is not None:
        in_specs.append(pl.BlockSpec(memory_space=pl.ANY))
        args.append(stacked_out)
        aliases = {2: 1}
    return pl.pallas_call(
        functools.partial(_ret_decode_kernel, bb=bb),
        grid=(nb // bb,),
        in_specs=in_specs,
        out_specs=[pl.BlockSpec((bb, MIX_MAIN), lambda i: (i, 0)),
                   pl.BlockSpec((None, bb, RET_HEADS, RET_HD, RET_HD), lambda i: (layer, i, 0, 0, 0))],
        out_shape=[jax.ShapeDtypeStruct((nb, MIX_MAIN), F32),
                   jax.ShapeDtypeStruct(state.shape, F32)],
        input_output_aliases=aliases,
        compiler_params=_cparams(1),
        name="ret_decode",
    )(*args)


def _mem_decode_kernel(q_ref, kt_ref, vt_ref, o_ref, *, bb, q_off):
    row = lax.broadcasted_iota(jnp.int32, (8, MEM_W), 0)
    lane_head = _lane_block64(lax.broadcasted_iota(jnp.int32, (8, MEM_W), 1))
    sel = row == lane_head
    outs = []
    for j in range(bb):
        qrow = q_ref[j:j + 1, q_off:q_off + MEM_W]
        qexp = jnp.where(sel, jnp.broadcast_to(qrow, (8, MEM_W)), 0.0).astype(BF16)
        s = jnp.dot(qexp, kt_ref[j].astype(BF16), preferred_element_type=F32)
        m = jnp.max(s, axis=-1, keepdims=True)
        e = jnp.exp(s - m)
        den = jnp.sum(e, axis=-1, keepdims=True)
        o = lax.dot_general(e.astype(BF16), vt_ref[j].astype(BF16), NT_DIMS, preferred_element_type=F32) / den
        outs.append(jnp.sum(jnp.where(sel, o, 0.0), axis=0, keepdims=True))
    o_ref[...] = jnp.concatenate(outs, axis=0)


def mem_decode(proj, cache_kt, cache_vt, layer, q_off, bb=16):
    nb = proj.shape[0]
    pw = proj.shape[1]
    cmap = lambda i: (layer, i, 0, 0)
    return pl.pallas_call(
        functools.partial(_mem_decode_kernel, bb=bb, q_off=q_off),
        grid=(nb // bb,),
        in_specs=[pl.BlockSpec((bb, pw), lambda i: (i, 0)),
                  pl.BlockSpec((None, bb, MEM_W, MEM_TOKENS), cmap),
                  pl.BlockSpec((None, bb, MEM_W, MEM_TOKENS), cmap)],
        out_specs=pl.BlockSpec((bb, MEM_W), lambda i: (i, 0)),
        out_shape=jax.ShapeDtypeStruct((nb, MEM_W), F32),
        compiler_params=_cparams(1),
        name="mem_decode",
    )(proj, cache_kt, cache_vt)


def _swa_decode_kernel(q_ref, kn_ref, vn_ref, ckt_ref, cvt_ref, slope_ref, sink_ref, o_ref, *cache_out, bb):
    row = lax.broadcasted_iota(jnp.int32, (DEC_ROWS, KV_W), 0)
    lane_group = _lane_block64(lax.broadcasted_iota(jnp.int32, (DEC_ROWS, KV_W), 1))
    dist = (WINDOW - lax.broadcasted_iota(jnp.int32, (DEC_ROWS, WINDOW), 1)).astype(F32)
    bias = -slope_ref[...] * dist
    sink = sink_ref[...]
    pos = lax.broadcasted_iota(jnp.int32, (KV_W, WINDOW), 1)
    knt = kn_ref[...].T
    vnt = vn_ref[...].T
    outs = [[] for _ in range(SWA_REP)]
    for j in range(bb):
        qexp = jnp.zeros((DEC_ROWS, KV_W), F32)
        for r in range(SWA_REP):
            qr = jnp.broadcast_to(q_ref[j:j + 1, r * KV_W:(r + 1) * KV_W], (DEC_ROWS, KV_W))
            qexp = jnp.where(row == SWA_KV_HEADS * r + lane_group, qr, qexp)
        kt = ckt_ref[j]
        vt = cvt_ref[j]
        s = jnp.dot(qexp.astype(BF16), kt.astype(BF16), preferred_element_type=F32) + bias
        s_new = jnp.sum(qexp * kn_ref[j:j + 1, :], axis=-1, keepdims=True)
        m = jnp.maximum(jnp.maximum(jnp.max(s, axis=-1, keepdims=True), s_new), sink)
        e = jnp.exp(s - m)
        e_new = jnp.exp(s_new - m)
        inv = 1.0 / (jnp.sum(e, axis=-1, keepdims=True) + e_new + jnp.exp(sink - m))
        o = lax.dot_general(e.astype(BF16), vt.astype(BF16), NT_DIMS, preferred_element_type=F32)
        o = (o + jnp.concatenate([e_new, e_new], axis=1) * vn_ref[j:j + 1, :]) * jnp.concatenate([inv, inv], axis=1)
        for r in range(SWA_REP):
            keep = row == SWA_KV_HEADS * r + lane_group
            outs[r].append(jnp.sum(jnp.where(keep, o, 0.0), axis=0, keepdims=True))
        if cache_out:
            nkt_ref, nvt_ref = cache_out
            last = pos == WINDOW - 1
            nkt_ref[j] = jnp.where(last, jnp.broadcast_to(knt[:, j:j + 1], (KV_W, WINDOW)), pltpu.roll(kt, WINDOW - 1, 1))
            nvt_ref[j] = jnp.where(last, jnp.broadcast_to(vnt[:, j:j + 1], (KV_W, WINDOW)), pltpu.roll(vt, WINDOW - 1, 1))
    for r in range(SWA_REP):
        o_ref[:, r * KV_W:(r + 1) * KV_W] = jnp.concatenate(outs[r], axis=0)


def swa_decode(qproj, k_new, v_new, cache_kt, cache_vt, slope_rows, sink_rows, sink_layer, emit_cache, bb=8):
    nb = qproj.shape[0]
    pw = qproj.shape[1]
    cache_spec = pl.BlockSpec((bb, KV_W, WINDOW), lambda i: (i, 0, 0))
    cache_shape = jax.ShapeDtypeStruct((nb, KV_W, WINDOW), F32)
    n_cache = 2 if emit_cache else 0
    return pl.pallas_call(
        functools.partial(_swa_decode_kernel, bb=bb),
        grid=(nb // bb,),
        in_specs=[pl.BlockSpec((bb, pw), lambda i: (i, 0)),
                  pl.BlockSpec((bb, KV_W), lambda i: (i, 0)),
                  pl.BlockSpec((bb, KV_W), lambda i: (i, 0)),
                  cache_spec, cache_spec,
                  pl.BlockSpec((DEC_ROWS, LANES), lambda i: (0, 0)),
                  pl.BlockSpec((None, DEC_ROWS, LANES), lambda i: (sink_layer, 0, 0))],
        out_specs=[pl.BlockSpec((bb, MIX_MAIN), lambda i: (i, 0))] + [cache_spec] * n_cache,
        out_shape=[jax.ShapeDtypeStruct((nb, MIX_MAIN), F32)] + [cache_shape] * n_cache,
        compiler_params=_cparams(1),
        name="swa_decode",
    )(qproj, k_new, v_new, cache_kt, cache_vt, slope_rows, sink_rows)


def _permute_q_cols(w):
    lead = w.shape[:-1]
    return w.reshape(*lead, SWA_KV_HEADS, SWA_REP, SWA_HD).swapaxes(-3, -2).reshape(*lead, MIX_MAIN)


def kernel(x_prompt, x_sample, cache_mem_k, cache_mem_v, state_ret, cache_swa_k, cache_swa_v, mem_prompt,
           norm_mix, w_in_a, w_out_a, w_in_b, w_out_b, attn_sinks, norm_mem, w_mem_kv, norm_kv, w_kv,
           norm_mlp, w_up, w_down, norm_final):
    batch, seq, d = x_prompt.shape
    nb = x_sample.shape[0]
    tm = 512

    w_in_a16 = jnp.concatenate([w_in_a[:, :, :4 * MIX_MAIN], w_in_a[:, :, 4 * MIX_MAIN:] * ATTN_SCALE], axis=-1).astype(BF16)
    w_out_a16 = w_out_a.astype(BF16)
    w_in_b16 = (jnp.concatenate([_permute_q_cols(w_in_b[:, :, :MIX_MAIN]), w_in_b[:, :, MIX_MAIN:]], axis=-1)
                * ATTN_SCALE).astype(BF16)
    w_out_b16 = jnp.concatenate(
        [_permute_q_cols(w_out_b[:, :MIX_MAIN, :].swapaxes(1, 2)).swapaxes(1, 2), w_out_b[:, MIX_MAIN:, :]],
        axis=1).astype(BF16)
    w_mem16 = w_mem_kv.astype(BF16)
    w_kv16 = w_kv.astype(BF16).reshape(1, d, 2 * KV_W)
    norm_kv1 = norm_kv.reshape(1, d)
    w_up16 = w_up.astype(BF16)
    w_down16 = w_down.astype(BF16)

    sinks_gr = attn_sinks.reshape(N_B, SWA_KV_HEADS, SWA_REP)
    sink_prompt = jnp.broadcast_to(sinks_gr[:, :, :, None, None], (N_B, SWA_KV_HEADS, SWA_REP, WINDOW, LANES)
                                   ).reshape(N_B, SWA_KV_HEADS, SWA_REP * WINDOW, LANES)
    sink_dec = jnp.concatenate([sinks_gr.swapaxes(1, 2).reshape(N_B, SWA_HEADS),
                                jnp.zeros((N_B, DEC_ROWS - SWA_HEADS), F32)], axis=1)
    sink_dec = jnp.broadcast_to(sink_dec[:, :, None], (N_B, DEC_ROWS, LANES))
    slope_dec = np.zeros((DEC_ROWS, LANES), np.float32)
    for r in range(SWA_REP):
        for g in range(SWA_KV_HEADS):
            slope_dec[SWA_KV_HEADS * r + g, :] = _SLOPES[g * SWA_REP + r]
    slope_dec = jnp.asarray(slope_dec)

    memx = mem_prompt.reshape(batch * MEM_TOKENS, d)
    mem_kv = jnp.stack([norm_matmul(memx, norm_mem, w_mem16, l, F32, tm=256) for l in range(DEPTH)])
    mem_kv = mem_kv.reshape(DEPTH, batch, MEM_TOKENS, 2 * MEM_W)
    mem_k_prompt = mem_kv[..., :MEM_W].reshape(DEPTH, batch, MEM_TOKENS, MEM_HEADS, MEM_HD)
    mem_v_prompt = mem_kv[..., MEM_W:].reshape(DEPTH, batch, MEM_TOKENS, MEM_HEADS, MEM_HD)
    mk16 = mem_kv[..., :MEM_W].astype(BF16)
    mv16 = mem_kv[..., MEM_W:].astype(BF16)

    x = x_prompt.reshape(batch * seq, d)
    ret_states = []
    kv_p = kv_p16 = None
    for l in range(DEPTH):
        if l < N_A:
            x, st = fused_mixer_a(x, norm_mix, w_in_a16, w_out_a16, mk16, mv16, l, batch, seq)
            ret_states.append(st)
        else:
            j = l - N_A
            x = fused_mixer_b(x, norm_mix, w_in_b16, w_out_b16, kv_p16, mk16, mv16, l, sink_prompt, j, batch, seq)
        if l == N_A - 1:
            x, kv_p, kv_p16 = mlp_block(x, norm_mlp, w_up16, w_down16, l, norm_final, False, tm=tm,
                                        kv_proj=(norm_kv1, w_kv16[0]))
        else:
            x = mlp_block(x, norm_mlp, w_up16, w_down16, l, norm_final, l == DEPTH - 1, tm=tm)
    y_prompt = x.reshape(batch, seq, d)
    ret_prompt = jnp.stack(ret_states)
    kv_p3 = kv_p.reshape(batch, seq, 2 * KV_W)
    swa_k_prompt = kv_p3[:, -WINDOW:, :KV_W].reshape(batch, WINDOW, SWA_KV_HEADS, SWA_HD)
    swa_v_prompt = kv_p3[:, -WINDOW:, KV_W:].reshape(batch, WINDOW, SWA_KV_HEADS, SWA_HD)

    cmkt = jnp.transpose(cache_mem_k, (0, 1, 3, 4, 2)).reshape(DEPTH, nb, MEM_W, MEM_TOKENS)
    cmvt = jnp.transpose(cache_mem_v, (0, 1, 3, 4, 2)).reshape(DEPTH, nb, MEM_W, MEM_TOKENS)
    cskt = jnp.transpose(cache_swa_k, (0, 2, 3, 1)).reshape(nb, KV_W, WINDOW)
    csvt = jnp.transpose(cache_swa_v, (0, 2, 3, 1)).reshape(nb, KV_W, WINDOW)
    xs = x_sample.reshape(nb, d)
    ret_sample = None
    k_new = v_new = nkt = nvt = None
    for l in range(DEPTH):
        if l < N_A:
            proj = norm_matmul(xs, norm_mix, w_in_a16, l, F32, tm=nb, n_chunk=MIX_MAIN)
            o_ret, ret_sample = ret_decode(proj, state_ret, l, stacked_out=ret_sample)
            o_mem = mem_decode(proj, cmkt, cmvt, l, 4 * MIX_MAIN)
            attn = jnp.concatenate([o_ret, o_mem], axis=-1).astype(BF16)
            xs = matmul_residual(attn, w_out_a16, l, xs, tm=nb)
        else:
            j = l - N_A
            if j == 0:
                kv_s = norm_matmul(xs, norm_kv1, w_kv16, 0, F32, tm=nb)
                k_new, v_new = kv_s[:, :KV_W], kv_s[:, KV_W:]
            qproj = norm_matmul(xs, norm_mix[N_A:], w_in_b16, j, F32, tm=nb)
            res = swa_decode(qproj, k_new, v_new, cskt, csvt, slope_dec, sink_dec, j, emit_cache=(j == 0))
            o_swa = res[0]
            if j == 0:
                nkt, nvt = res[1], res[2]
            o_mem = mem_decode(qproj, cmkt, cmvt, l, MIX_MAIN)
            attn = jnp.concatenate([o_swa, o_mem], axis=-1).astype(BF16)
            xs = matmul_residual(attn, w_out_b16, j, xs, tm=nb)
        xs = mlp_block(xs, norm_mlp, w_up16, w_down16, l, norm_final, l == DEPTH - 1, tm=nb)
    y_sample = xs.reshape(nb, 1, d)
    swa_k_sample = jnp.transpose(nkt.reshape(nb, SWA_KV_HEADS, SWA_HD, WINDOW), (0, 3, 1, 2))
    swa_v_sample = jnp.transpose(nvt.reshape(nb, SWA_KV_HEADS, SWA_HD, WINDOW), (0, 3, 1, 2))

    return (y_prompt, y_sample, ret_prompt, ret_sample, swa_k_prompt, swa_v_prompt, swa_k_sample, swa_v_sample,
            mem_k_prompt, mem_v_prompt)
```

```python
import functools
import math

import jax
import jax.numpy as jnp
import numpy as np
from jax import lax
from jax.experimental import pallas as pl
from jax.experimental.pallas import tpu as pltpu

F32 = jnp.float32
BF16 = jnp.bfloat16

D_MODEL = 1024
DEPTH = 4
N_A = 2
N_B = 2
MEM_TOKENS = 256
MEM_HEADS = 4
MEM_W = 256
MEM_HD = 64
MIX_MAIN = 768
RET_HEADS = 6
RET_HD = 128
CHUNK = 128
SWA_HEADS = 12
SWA_KV_HEADS = 4
SWA_REP = SWA_HEADS // SWA_KV_HEADS
SWA_HD = 64
KV_W = SWA_KV_HEADS * SWA_HD
WINDOW = 128
D_FF = 4096
EPS = 1e-6
NEG = -1e30
ATTN_SCALE = 0.125
LANES = 128
TILE = 512
DEC_ROWS = 16

VMEM_LIMIT = 56 * 1024 * 1024

NT_DIMS = (((1,), (1,)), ((), ()))
TN_DIMS = (((0,), (0,)), ((), ()))


def _alibi_slopes(n):
    def pow2(m):
        start = 2.0 ** (-8.0 / m)
        return [start ** (i + 1) for i in range(m)]

    if math.log2(n).is_integer():
        s = pow2(n)
    else:
        c = 2 ** int(math.floor(math.log2(n)))
        s = pow2(c) + pow2(2 * c)[0::2][: n - c]
    return np.asarray(s, np.float32)


_LOG_G = np.log1p(-(2.0 ** (-5.0 - np.arange(RET_HEADS)))).astype(np.float32).astype(np.float64)
_SLOPES = _alibi_slopes(SWA_HEADS).astype(np.float64)


def _retention_consts():
    idx = np.arange(CHUNK, dtype=np.float64)
    diff = idx[:, None] - idx[None, :]
    scale = RET_HD ** -0.5
    decay = np.where(diff >= 0, np.exp(np.maximum(diff, 0.0)[None] * _LOG_G[:, None, None]), 0.0) * scale
    w_q = np.exp((idx + 1.0)[None, :] * _LOG_G[:, None])
    w_k = np.exp((CHUNK - 1.0 - idx)[None, :] * _LOG_G[:, None]) * scale
    w_q = np.broadcast_to(w_q[:, :, None], (RET_HEADS, CHUNK, RET_HD))
    w_k = np.broadcast_to(w_k[:, :, None], (RET_HEADS, CHUNK, RET_HD))
    g_c = np.exp(CHUNK * _LOG_G)
    pair = lambda t: np.ascontiguousarray(np.concatenate([t[0::2], t[1::2]], axis=2), np.float32)
    return pair(decay), pair(w_q), pair(w_k), [float(v) for v in g_c]


def _swa_bias():
    i = np.arange(WINDOW)[:, None]
    j = np.arange(2 * WINDOW)[None, :]
    dist = i + WINDOW - j
    valid = (dist >= 0) & (dist <= WINDOW)
    out = np.zeros((SWA_KV_HEADS, SWA_REP * WINDOW, 2 * WINDOW), np.float32)
    for g in range(SWA_KV_HEADS):
        for r in range(SWA_REP):
            b = np.where(valid, -_SLOPES[g * SWA_REP + r] * dist, NEG)
            out[g, r * WINDOW:(r + 1) * WINDOW] = b
    return out


def _rmsnorm(x, g):
    ms = jnp.mean(x * x, axis=-1, keepdims=True)
    return (x * lax.rsqrt(ms + EPS)) * g


def _lane_block64(lane):
    return lax.shift_right_logical(lane, 6)


def _cparams(n_axes):
    return pltpu.CompilerParams(dimension_semantics=("arbitrary",) * n_axes, vmem_limit_bytes=VMEM_LIMIT)


def _layer_block(tail, layer, n_grid):
    zeros = (0,) * len(tail)
    if n_grid == 1:
        return pl.BlockSpec((None,) + tuple(tail), lambda i: (layer,) + zeros)
    return pl.BlockSpec((None,) + tuple(tail), lambda i, j: (layer,) + zeros)


def _norm_matmul_kernel(x_ref, g_ref, w_ref, o_ref, *, n_chunk):
    xn = _rmsnorm(x_ref[...], g_ref[...]).astype(BF16)
    n = w_ref.shape[1]
    for n0 in range(0, n, n_chunk):
        n1 = min(n0 + n_chunk, n)
        o_ref[:, n0:n1] = jnp.dot(xn, w_ref[:, n0:n1], preferred_element_type=F32).astype(o_ref.dtype)


def norm_matmul(x, g, w, layer, out_dtype, tm, n_chunk=512):
    m, k = x.shape
    n = w.shape[2]
    return pl.pallas_call(
        functools.partial(_norm_matmul_kernel, n_chunk=n_chunk),
        grid=(m // tm,),
        in_specs=[pl.BlockSpec((tm, k), lambda i: (i, 0)),
                  _layer_block((1, k), layer, 1),
                  _layer_block((k, n), layer, 1)],
        out_specs=pl.BlockSpec((tm, n), lambda i: (i, 0)),
        out_shape=jax.ShapeDtypeStruct((m, n), out_dtype),
        compiler_params=_cparams(1),
        name="norm_matmul",
    )(x, g.reshape(g.shape[0], 1, k), w)


def _matmul_residual_kernel(a_ref, w_ref, x_ref, o_ref):
    o_ref[...] = x_ref[...] + jnp.dot(a_ref[...], w_ref[...], preferred_element_type=F32)


def matmul_residual(a, w, layer, x, tm):
    m, k = a.shape
    n = w.shape[2]
    return pl.pallas_call(
        _matmul_residual_kernel,
        grid=(m // tm,),
        in_specs=[pl.BlockSpec((tm, k), lambda i: (i, 0)),
                  _layer_block((k, n), layer, 1),
                  pl.BlockSpec((tm, n), lambda i: (i, 0))],
        out_specs=pl.BlockSpec((tm, n), lambda i: (i, 0)),
        out_shape=jax.ShapeDtypeStruct((m, n), F32),
        compiler_params=_cparams(1),
        name="matmul_residual",
    )(a, w, x)


def _mlp_kernel(x_ref, g_ref, wu_ref, wd_ref, gf_ref, *rest, fc, final_norm, with_kv):
    if with_kv:
        gkv_ref, wkv_ref, o_ref, kv_ref, kv16_ref, act_ref = rest
    else:
        o_ref, act_ref = rest
    xn = _rmsnorm(x_ref[...], g_ref[...]).astype(BF16)
    for c0 in range(0, D_FF, fc):
        h = jnp.dot(xn, wu_ref[:, c0:c0 + fc].astype(BF16), preferred_element_type=F32)
        act_ref[:, c0:c0 + fc] = jnp.square(jnp.maximum(h, 0.0)).astype(BF16)
    y = x_ref[...] + jnp.dot(act_ref[...], wd_ref[...].astype(BF16), preferred_element_type=F32)
    if with_kv:
        kv = jnp.dot(_rmsnorm(y, gkv_ref[...]).astype(BF16), wkv_ref[...], preferred_element_type=F32)
        kv_ref[...] = kv
        kv16_ref[...] = kv.astype(BF16)
    if final_norm:
        y = _rmsnorm(y, gf_ref[...])
    o_ref[...] = y


def mlp_block(x, g, w_up, w_down, layer, g_final, final_norm, tm, kv_proj=None, fc=512):
    m, d = x.shape
    with_kv = kv_proj is not None
    in_specs = [pl.BlockSpec((tm, d), lambda i: (i, 0)),
                _layer_block((1, d), layer, 1),
                pl.BlockSpec((None, d, D_FF), lambda i: (layer, 0, 0), pipeline_mode=pl.Buffered(1)),
                pl.BlockSpec((None, D_FF, d), lambda i: (layer, 0, 0), pipeline_mode=pl.Buffered(1)),
                pl.BlockSpec((1, d), lambda i: (0, 0))]
    args = [x, g.reshape(g.shape[0], 1, d), w_up, w_down, g_final.reshape(1, d)]
    out_specs = [pl.BlockSpec((tm, d), lambda i: (i, 0))]
    out_shape = [jax.ShapeDtypeStruct((m, d), F32)]
    if with_kv:
        in_specs += [pl.BlockSpec((1, d), lambda i: (0, 0)),
                     pl.BlockSpec((d, 2 * KV_W), lambda i: (0, 0), pipeline_mode=pl.Buffered(1))]
        args += list(kv_proj)
        out_specs += [pl.BlockSpec((tm, 2 * KV_W), lambda i: (i, 0))] * 2
        out_shape += [jax.ShapeDtypeStruct((m, 2 * KV_W), F32), jax.ShapeDtypeStruct((m, 2 * KV_W), BF16)]
    res = pl.pallas_call(
        functools.partial(_mlp_kernel, fc=fc, final_norm=final_norm, with_kv=with_kv),
        grid=(m // tm,),
        in_specs=in_specs,
        out_specs=out_specs,
        out_shape=out_shape,
        scratch_shapes=[pltpu.VMEM((tm, D_FF), BF16)],
        compiler_params=_cparams(1),
        name="mlp_block",
    )(*args)
    return res if with_kv else res[0]


def _mem_attention_tile(qm, mk, mv):
    t = qm.shape[0]
    lane_head = _lane_block64(lax.broadcasted_iota(jnp.int32, (t, MEM_W), 1))
    tok_head = _lane_block64(lax.broadcasted_iota(jnp.int32, (MEM_TOKENS, MEM_W), 1))
    probs = []
    vals = []
    for h in range(MEM_HEADS):
        qh = jnp.where(lane_head == h, qm, jnp.zeros_like(qm))
        s = lax.dot_general(qh, mk, NT_DIMS, preferred_element_type=F32)
        m = jnp.max(s, axis=-1, keepdims=True)
        e = jnp.exp(s - m)
        inv = 1.0 / jnp.sum(e, axis=-1, keepdims=True)
        probs.append((e * inv).astype(BF16))
        vals.append(jnp.where(tok_head == h, mv, jnp.zeros_like(mv)))
    return jnp.dot(jnp.concatenate(probs, axis=1), jnp.concatenate(vals, axis=0), preferred_element_type=F32)


def _project_tile(x_ref, g_ref, w_ref, p_ref, n_chunk):
    xn = _rmsnorm(x_ref[...], g_ref[...]).astype(BF16)
    n = w_ref.shape[1]
    for n0 in range(0, n, n_chunk):
        n1 = min(n0 + n_chunk, n)
        p_ref[:, n0:n1] = jnp.dot(xn, w_ref[:, n0:n1], preferred_element_type=F32).astype(BF16)


def _block_diag2(a, b):
    z = jnp.zeros_like(a)
    return jnp.concatenate([jnp.concatenate([a, z], axis=1), jnp.concatenate([z, b], axis=1)], axis=0)


def _group_norm_gate(o, gate):
    mu = jnp.mean(o, axis=-1, keepdims=True)
    oc = o - mu
    var = jnp.mean(oc * oc, axis=-1, keepdims=True)
    return (gate * jax.nn.sigmoid(gate) * (oc * lax.rsqrt(var + EPS))).astype(BF16)


def _retention_tile(p_ref, a_ref, s_scr, decay_ref, wq_ref, wk_ref, g_c):
    pw = 2 * RET_HD
    for c in range(TILE // CHUNK):
        rows = slice(c * CHUNK, (c + 1) * CHUNK)
        for hp in range(RET_HEADS // 2):
            lo = hp * pw
            ha, hb = 2 * hp, 2 * hp + 1
            q2 = p_ref[rows, lo:lo + pw]
            k2 = p_ref[rows, MIX_MAIN + lo:MIX_MAIN + lo + pw]
            v2 = p_ref[rows, 2 * MIX_MAIN + lo:2 * MIX_MAIN + lo + pw]
            gate2 = p_ref[rows, 3 * MIX_MAIN + lo:3 * MIX_MAIN + lo + pw].astype(F32)
            s_a = s_scr[ha]
            s_b = s_scr[hb]
            kbd = _block_diag2(k2[:, :RET_HD], k2[:, RET_HD:])
            sc = lax.dot_general(q2, kbd, NT_DIMS, preferred_element_type=F32) * decay_ref[hp]
            vbd = _block_diag2(v2[:, :RET_HD], v2[:, RET_HD:])
            sbd = _block_diag2(s_a.astype(BF16), s_b.astype(BF16))
            qw = (q2.astype(F32) * wq_ref[hp]).astype(BF16)
            o2 = jnp.dot(jnp.concatenate([sc.astype(BF16), qw], axis=1), jnp.concatenate([vbd, sbd], axis=0),
                         preferred_element_type=F32)
            kw = (k2.astype(F32) * wk_ref[hp]).astype(BF16)
            kv = lax.dot_general(kw, v2, TN_DIMS, preferred_element_type=F32)
            s_scr[ha] = g_c[ha] * s_a + kv[:RET_HD, :RET_HD]
            s_scr[hb] = g_c[hb] * s_b + kv[RET_HD:, RET_HD:]
            a_ref[rows, lo:lo + RET_HD] = _group_norm_gate(o2[:, :RET_HD], gate2[:, :RET_HD])
            a_ref[rows, lo + RET_HD:lo + pw] = _group_norm_gate(o2[:, RET_HD:], gate2[:, RET_HD:])


def _fused_a_kernel(xc_ref, xn_ref, g_ref, win_ref, wout_ref, mk_ref, mv_ref, decay_ref, wq_ref, wk_ref,
                    o_ref, st_ref, p0, p1, a_scr, s_scr, *, g_c, tiles_per_seq):
    t = pl.program_id(0)

    @pl.when(t == 0)
    def _():
        _project_tile(xc_ref, g_ref, win_ref, p0, MIX_MAIN)

    @pl.when(t % tiles_per_seq == 0)
    def _():
        s_scr[...] = jnp.zeros_like(s_scr)

    def step(p_cur, p_next):
        _project_tile(xn_ref, g_ref, win_ref, p_next, MIX_MAIN)
        _retention_tile(p_cur, a_scr, s_scr, decay_ref, wq_ref, wk_ref, g_c)
        st_ref[...] = s_scr[...]
        qm = p_cur[:, 4 * MIX_MAIN:4 * MIX_MAIN + MEM_W]
        a_scr[:, MIX_MAIN:] = _mem_attention_tile(qm, mk_ref[...], mv_ref[...]).astype(BF16)
        o_ref[...] = xc_ref[...] + jnp.dot(a_scr[...], wout_ref[...], preferred_element_type=F32)

    @pl.when(t % 2 == 0)
    def _():
        step(p0, p1)

    @pl.when(t % 2 == 1)
    def _():
        step(p1, p0)


def fused_mixer_a(x, g, w_in, w_out, mem_k, mem_v, layer, batch, seq):
    n_tiles = batch * seq // TILE
    tps = seq // TILE
    decay, w_q, w_k, g_c = _retention_consts()
    pw = w_in.shape[2]
    c3 = lambda t: (0, 0, 0)
    mem_spec = pl.BlockSpec((None, None, MEM_TOKENS, MEM_W), lambda t: (layer, t // tps, 0, 0))
    return pl.pallas_call(
        functools.partial(_fused_a_kernel, g_c=g_c, tiles_per_seq=tps),
        grid=(n_tiles,),
        in_specs=[pl.BlockSpec((TILE, D_MODEL), lambda t: (t, 0)),
                  pl.BlockSpec((TILE, D_MODEL), lambda t: (jnp.minimum(t + 1, n_tiles - 1), 0)),
                  _layer_block((1, D_MODEL), layer, 1),
                  pl.BlockSpec((None, D_MODEL, pw), lambda t: (layer, 0, 0), pipeline_mode=pl.Buffered(1)),
                  pl.BlockSpec((None, D_MODEL, D_MODEL), lambda t: (layer, 0, 0), pipeline_mode=pl.Buffered(1)),
                  mem_spec, mem_spec,
                  pl.BlockSpec((RET_HEADS // 2, CHUNK, 2 * CHUNK), c3),
                  pl.BlockSpec((RET_HEADS // 2, CHUNK, 2 * RET_HD), c3),
                  pl.BlockSpec((RET_HEADS // 2, CHUNK, 2 * RET_HD), c3)],
        out_specs=[pl.BlockSpec((TILE, D_MODEL), lambda t: (t, 0)),
                   pl.BlockSpec((None, RET_HEADS, RET_HD, RET_HD), lambda t: (t // tps, 0, 0, 0))],
        out_shape=[jax.ShapeDtypeStruct((batch * seq, D_MODEL), F32),
                   jax.ShapeDtypeStruct((batch, RET_HEADS, RET_HD, RET_HD), F32)],
        scratch_shapes=[pltpu.VMEM((TILE, pw), BF16), pltpu.VMEM((TILE, pw), BF16),
                        pltpu.VMEM((TILE, D_MODEL), BF16),
                        pltpu.VMEM((RET_HEADS, RET_HD, RET_HD), F32)],
        compiler_params=_cparams(1),
        name="fused_mixer_a",
    )(x, x, g.reshape(g.shape[0], 1, D_MODEL), w_in, w_out, mem_k, mem_v,
      jnp.asarray(decay), jnp.asarray(w_q), jnp.asarray(w_k))


def _swa_tile(p_ref, kvp_ref, kvc_ref, a_ref, bias_ref, sink_ref, is_first):
    lane_group = _lane_block64(lax.broadcasted_iota(jnp.int32, (WINDOW, KV_W), 1))
    key_group = _lane_block64(lax.broadcasted_iota(jnp.int32, (2 * WINDOW, KV_W), 1))
    for i in range(TILE // WINDOW):
        r0 = i * WINDOW
        if i == 0:
            kv2 = jnp.concatenate([kvp_ref[...], kvc_ref[0:WINDOW, :]], axis=0)
        else:
            kv2 = kvc_ref[r0 - WINDOW:r0 + WINDOW, :]
        k2 = kv2[:, :KV_W]
        v2 = kv2[:, KV_W:]
        qs = [p_ref[r0:r0 + WINDOW, r * KV_W:(r + 1) * KV_W] for r in range(SWA_REP)]
        probs = []
        vals = []
        for g in range(SWA_KV_HEADS):
            sel = lane_group == g
            qg = jnp.concatenate([jnp.where(sel, qr, jnp.zeros_like(qr)) for qr in qs], axis=0)
            s = lax.dot_general(qg, k2, NT_DIMS, preferred_element_type=F32) + bias_ref[g]
            s_prev = s[:, :WINDOW]
            s_cur = s[:, WINDOW:]
            if i == 0:
                s_prev = jnp.where(is_first, NEG, s_prev)
            sink = sink_ref[g]
            m = jnp.maximum(jnp.max(jnp.maximum(s_prev, s_cur), axis=-1, keepdims=True), sink)
            e_prev = jnp.exp(s_prev - m)
            e_cur = jnp.exp(s_cur - m)
            inv = 1.0 / (jnp.sum(e_prev + e_cur, axis=-1, keepdims=True) + jnp.exp(sink - m))
            probs.append((e_prev * inv).astype(BF16))
            probs.append((e_cur * inv).astype(BF16))
            vals.append(jnp.where(key_group == g, v2, jnp.zeros_like(v2)))
        o = jnp.dot(jnp.concatenate(probs, axis=1), jnp.concatenate(vals, axis=0), preferred_element_type=F32)
        for r in range(SWA_REP):
            a_ref[r0:r0 + WINDOW, r * KV_W:(r + 1) * KV_W] = o[r * WINDOW:(r + 1) * WINDOW].astype(BF16)


def _fused_b_kernel(xc_ref, xn_ref, g_ref, win_ref, wout_ref, kvp_ref, kvc_ref, mk_ref, mv_ref, bias_ref, sink_ref,
                    o_ref, p0, p1, a_scr, *, tiles_per_seq):
    t = pl.program_id(0)
    is_first = t % tiles_per_seq == 0

    @pl.when(t == 0)
    def _():
        _project_tile(xc_ref, g_ref, win_ref, p0, 512)

    def step(p_cur, p_next):
        _project_tile(xn_ref, g_ref, win_ref, p_next, 512)
        _swa_tile(p_cur, kvp_ref, kvc_ref, a_scr, bias_ref, sink_ref, is_first)
        qm = p_cur[:, MIX_MAIN:MIX_MAIN + MEM_W]
        a_scr[:, MIX_MAIN:] = _mem_attention_tile(qm, mk_ref[...], mv_ref[...]).astype(BF16)
        o_ref[...] = xc_ref[...] + jnp.dot(a_scr[...], wout_ref[...], preferred_element_type=F32)

    @pl.when(t % 2 == 0)
    def _():
        step(p0, p1)

    @pl.when(t % 2 == 1)
    def _():
        step(p1, p0)


def fused_mixer_b(x, g, w_in, w_out, kv, mem_k, mem_v, layer, sink_rows, sub_layer, batch, seq):
    n_tiles = batch * seq // TILE
    tps = seq // TILE
    nw = TILE // WINDOW
    bias = jnp.asarray(_swa_bias())
    mem_spec = pl.BlockSpec((None, None, MEM_TOKENS, MEM_W), lambda t: (layer, t // tps, 0, 0))

    def prev_map(t):
        return (jnp.maximum(t * nw - 1, (t // tps) * (seq // WINDOW)), 0)

    return pl.pallas_call(
        functools.partial(_fused_b_kernel, tiles_per_seq=tps),
        grid=(n_tiles,),
        in_specs=[pl.BlockSpec((TILE, D_MODEL), lambda t: (t, 0)),
                  pl.BlockSpec((TILE, D_MODEL), lambda t: (jnp.minimum(t + 1, n_tiles - 1), 0)),
                  _layer_block((1, D_MODEL), layer, 1),
                  pl.BlockSpec((None, D_MODEL, D_MODEL), lambda t: (sub_layer, 0, 0), pipeline_mode=pl.Buffered(1)),
                  pl.BlockSpec((None, D_MODEL, D_MODEL), lambda t: (sub_layer, 0, 0), pipeline_mode=pl.Buffered(1)),
                  pl.BlockSpec((WINDOW, 2 * KV_W), prev_map),
                  pl.BlockSpec((TILE, 2 * KV_W), lambda t: (t, 0)),
                  mem_spec, mem_spec,
                  pl.BlockSpec((SWA_KV_HEADS, SWA_REP * WINDOW, 2 * WINDOW), lambda t: (0, 0, 0)),
                  pl.BlockSpec((None, SWA_KV_HEADS, SWA_REP * WINDOW, LANES), lambda t: (sub_layer, 0, 0, 0))],
        out_specs=pl.BlockSpec((TILE, D_MODEL), lambda t: (t, 0)),
        out_shape=jax.ShapeDtypeStruct((batch * seq, D_MODEL), F32),
        scratch_shapes=[pltpu.VMEM((TILE, D_MODEL), BF16), pltpu.VMEM((TILE, D_MODEL), BF16),
                        pltpu.VMEM((TILE, D_MODEL), BF16)],
        compiler_params=_cparams(1),
        name="fused_mixer_b",
    )(x, x, g.reshape(g.shape[0], 1, D_MODEL), w_in, w_out, kv, kv, mem_k, mem_v, bias, sink_rows)


def _ret_decode_kernel(proj_ref, st_ref, *rest, bb):
    o_ref, nst_ref = rest[-2:]
    scale = RET_HD ** -0.5
    row = lax.broadcasted_iota(jnp.int32, (bb, bb * RET_HD), 0)
    lane_seq = lax.shift_right_logical(lax.broadcasted_iota(jnp.int32, (bb, bb * RET_HD), 1), 7)
    own = row == lane_seq
    for h in range(RET_HEADS):
        lo = h * RET_HD
        q8 = proj_ref[:, lo:lo + RET_HD]
        k8 = proj_ref[:, MIX_MAIN + lo:MIX_MAIN + lo + RET_HD] * scale
        v8 = proj_ref[:, 2 * MIX_MAIN + lo:2 * MIX_MAIN + lo + RET_HD]
        gate = proj_ref[:, 3 * MIX_MAIN + lo:3 * MIX_MAIN + lo + RET_HD]
        g = float(np.exp(_LOG_G[h]))
        vexp = jnp.where(own, jnp.concatenate([v8] * bb, axis=1), 0.0).astype(BF16)
        outer = lax.dot_general(k8.astype(BF16), vexp, TN_DIMS, preferred_element_type=F32)
        cols = []
        for j in range(bb):
            s_new = g * st_ref[j, h] + outer[:, j * RET_HD:(j + 1) * RET_HD]
            nst_ref[j, h] = s_new
            cols.append(s_new.astype(BF16))
        o_all = jnp.dot(q8.astype(BF16), jnp.concatenate(cols, axis=1), preferred_element_type=F32)
        o_all = jnp.where(own, o_all, 0.0)
        o = o_all[:, :RET_HD]
        for j in range(1, bb):
            o = o + o_all[:, j * RET_HD:(j + 1) * RET_HD]
        mu = jnp.mean(o, axis=-1, keepdims=True)
        oc = o - mu
        var = jnp.mean(oc * oc, axis=-1, keepdims=True)
        on = oc * lax.rsqrt(var + EPS)
        o_ref[:, lo:lo + RET_HD] = gate * jax.nn.sigmoid(gate) * on


def ret_decode(proj, state, layer, stacked_out=None, bb=16):
    nb = proj.shape[0]
    pw = proj.shape[1]
    in_specs = [pl.BlockSpec((bb, pw), lambda i: (i, 0)),
                pl.BlockSpec((None, bb, RET_HEADS, RET_HD, RET_HD), lambda i: (layer, i, 0, 0, 0))]
    args = [proj, state]
    aliases = {}
    if stacked_out is not None:
        in_specs.append(pl.BlockSpec(memory_space=pl.ANY))
        args.append(stacked_out)
        aliases = {2: 1}
    return pl.pallas_call(
        functools.partial(_ret_decode_kernel, bb=bb),
        grid=(nb // bb,),
        in_specs=in_specs,
        out_specs=[pl.BlockSpec((bb, MIX_MAIN), lambda i: (i, 0)),
                   pl.BlockSpec((None, bb, RET_HEADS, RET_HD, RET_HD), lambda i: (layer, i, 0, 0, 0))],
        out_shape=[jax.ShapeDtypeStruct((nb, MIX_MAIN), F32),
                   jax.ShapeDtypeStruct(state.shape, F32)],
        input_output_aliases=aliases,
        compiler_params=_cparams(1),
        name="ret_decode",
    )(*args)


def _mem_decode_kernel(q_ref, kt_ref, vt_ref, o_ref, *, bb, q_off):
    row = lax.broadcasted_iota(jnp.int32, (8, MEM_W), 0)
    lane_head = _lane_block64(lax.broadcasted_iota(jnp.int32, (8, MEM_W), 1))
    sel = row == lane_head
    outs = []
    for j in range(bb):
        qrow = q_ref[j:j + 1, q_off:q_off + MEM_W]
        qexp = jnp.where(sel, jnp.broadcast_to(qrow, (8, MEM_W)), 0.0).astype(BF16)
        s = jnp.dot(qexp, kt_ref[j].astype(BF16), preferred_element_type=F32)
        m = jnp.max(s, axis=-1, keepdims=True)
        e = jnp.exp(s - m)
        den = jnp.sum(e, axis=-1, keepdims=True)
        o = lax.dot_general(e.astype(BF16), vt_ref[j].astype(BF16), NT_DIMS, preferred_element_type=F32) / den
        outs.append(jnp.sum(jnp.where(sel, o, 0.0), axis=0, keepdims=True))
    o_ref[...] = jnp.concatenate(outs, axis=0)


def mem_decode(proj, cache_kt, cache_vt, layer, q_off, bb=16):
    nb = proj.shape[0]
    pw = proj.shape[1]
    cmap = lambda i: (layer, i, 0, 0)
    return pl.pallas_call(
        functools.partial(_mem_decode_kernel, bb=bb, q_off=q_off),
        grid=(nb // bb,),
        in_specs=[pl.BlockSpec((bb, pw), lambda i: (i, 0)),
                  pl.BlockSpec((None, bb, MEM_W, MEM_TOKENS), cmap),
                  pl.BlockSpec((None, bb, MEM_W, MEM_TOKENS), cmap)],
        out_specs=pl.BlockSpec((bb, MEM_W), lambda i: (i, 0)),
        out_shape=jax.ShapeDtypeStruct((nb, MEM_W), F32),
        compiler_params=_cparams(1),
        name="mem_decode",
    )(proj, cache_kt, cache_vt)


def _swa_decode_kernel(q_ref, kn_ref, vn_ref, ckt_ref, cvt_ref, slope_ref, sink_ref, o_ref, *cache_out, bb):
    row = lax.broadcasted_iota(jnp.int32, (DEC_ROWS, KV_W), 0)
    lane_group = _lane_block64(lax.broadcasted_iota(jnp.int32, (DEC_ROWS, KV_W), 1))
    dist = (WINDOW - lax.broadcasted_iota(jnp.int32, (DEC_ROWS, WINDOW), 1)).astype(F32)
    bias = -slope_ref[...] * dist
    sink = sink_ref[...]
    pos = lax.broadcasted_iota(jnp.int32, (KV_W, WINDOW), 1)
    knt = kn_ref[...].T
    vnt = vn_ref[...].T
    outs = [[] for _ in range(SWA_REP)]
    for j in range(bb):
        qexp = jnp.zeros((DEC_ROWS, KV_W), F32)
        for r in range(SWA_REP):
            qr = jnp.broadcast_to(q_ref[j:j + 1, r * KV_W:(r + 1) * KV_W], (DEC_ROWS, KV_W))
            qexp = jnp.where(row == SWA_KV_HEADS * r + lane_group, qr, qexp)
        kt = ckt_ref[j]
        vt = cvt_ref[j]
        s = jnp.dot(qexp.astype(BF16), kt.astype(BF16), preferred_element_type=F32) + bias
        s_new = jnp.sum(qexp * kn_ref[j:j + 1, :], axis=-1, keepdims=True)
        m = jnp.maximum(jnp.maximum(jnp.max(s, axis=-1, keepdims=True), s_new), sink)
        e = jnp.exp(s - m)
        e_new = jnp.exp(s_new - m)
        inv = 1.0 / (jnp.sum(e, axis=-1, keepdims=True) + e_new + jnp.exp(sink - m))
        o = lax.dot_general(e.astype(BF16), vt.astype(BF16), NT_DIMS, preferred_element_type=F32)
        o = (o + jnp.concatenate([e_new, e_new], axis=1) * vn_ref[j:j + 1, :]) * jnp.concatenate([inv, inv], axis=1)
        for r in range(SWA_REP):
            keep = row == SWA_KV_HEADS * r + lane_group
            outs[r].append(jnp.sum(jnp.where(keep, o, 0.0), axis=0, keepdims=True))
        if cache_out:
            nkt_ref, nvt_ref = cache_out
            last = pos == WINDOW - 1
            nkt_ref[j] = jnp.where(last, jnp.broadcast_to(knt[:, j:j + 1], (KV_W, WINDOW)), pltpu.roll(kt, WINDOW - 1, 1))
            nvt_ref[j] = jnp.where(last, jnp.broadcast_to(vnt[:, j:j + 1], (KV_W, WINDOW)), pltpu.roll(vt, WINDOW - 1, 1))
    for r in range(SWA_REP):
        o_ref[:, r * KV_W:(r + 1) * KV_W] = jnp.concatenate(outs[r], axis=0)


def swa_decode(qproj, k_new, v_new, cache_kt, cache_vt, slope_rows, sink_rows, sink_layer, emit_cache, bb=8):
    nb = qproj.shape[0]
    pw = qproj.shape[1]
    cache_spec = pl.BlockSpec((bb, KV_W, WINDOW), lambda i: (i, 0, 0))
    cache_shape = jax.ShapeDtypeStruct((nb, KV_W, WINDOW), F32)
    n_cache = 2 if emit_cache else 0
    return pl.pallas_call(
        functools.partial(_swa_decode_kernel, bb=bb),
        grid=(nb // bb,),
        in_specs=[pl.BlockSpec((bb, pw), lambda i: (i, 0)),
                  pl.BlockSpec((bb, KV_W), lambda i: (i, 0)),
                  pl.BlockSpec((bb, KV_W), lambda i: (i, 0)),
                  cache_spec, cache_spec,
                  pl.BlockSpec((DEC_ROWS, LANES), lambda i: (0, 0)),
                  pl.BlockSpec((None, DEC_ROWS, LANES), lambda i: (sink_layer, 0, 0))],
        out_specs=[pl.BlockSpec((bb, MIX_MAIN), lambda i: (i, 0))] + [cache_spec] * n_cache,
        out_shape=[jax.ShapeDtypeStruct((nb, MIX_MAIN), F32)] + [cache_shape] * n_cache,
        compiler_params=_cparams(1),
        name="swa_decode",
    )(qproj, k_new, v_new, cache_kt, cache_vt, slope_rows, sink_rows)


def _permute_q_cols(w):
    lead = w.shape[:-1]
    return w.reshape(*lead, SWA_KV_HEADS, SWA_REP, SWA_HD).swapaxes(-3, -2).reshape(*lead, MIX_MAIN)


def kernel(x_prompt, x_sample, cache_mem_k, cache_mem_v, state_ret, cache_swa_k, cache_swa_v, mem_prompt,
           norm_mix, w_in_a, w_out_a, w_in_b, w_out_b, attn_sinks, norm_mem, w_mem_kv, norm_kv, w_kv,
           norm_mlp, w_up, w_down, norm_final):
    batch, seq, d = x_prompt.shape
    nb = x_sample.shape[0]
    tm = 512

    w_in_a16 = jnp.concatenate([w_in_a[:, :, :4 * MIX_MAIN], w_in_a[:, :, 4 * MIX_MAIN:] * ATTN_SCALE], axis=-1).astype(BF16)
    w_out_a16 = w_out_a.astype(BF16)
    w_in_b16 = (jnp.concatenate([_permute_q_cols(w_in_b[:, :, :MIX_MAIN]), w_in_b[:, :, MIX_MAIN:]], axis=-1)
                * ATTN_SCALE).astype(BF16)
    w_out_b16 = jnp.concatenate(
        [_permute_q_cols(w_out_b[:, :MIX_MAIN, :].swapaxes(1, 2)).swapaxes(1, 2), w_out_b[:, MIX_MAIN:, :]],
        axis=1).astype(BF16)
    w_mem16 = w_mem_kv.astype(BF16)
    w_kv16 = w_kv.astype(BF16).reshape(1, d, 2 * KV_W)
    norm_kv1 = norm_kv.reshape(1, d)

    sinks_gr = attn_sinks.reshape(N_B, SWA_KV_HEADS, SWA_REP)
    sink_prompt = jnp.broadcast_to(sinks_gr[:, :, :, None, None], (N_B, SWA_KV_HEADS, SWA_REP, WINDOW, LANES)
                                   ).reshape(N_B, SWA_KV_HEADS, SWA_REP * WINDOW, LANES)
    sink_dec = jnp.concatenate([sinks_gr.swapaxes(1, 2).reshape(N_B, SWA_HEADS),
                                jnp.zeros((N_B, DEC_ROWS - SWA_HEADS), F32)], axis=1)
    sink_dec = jnp.broadcast_to(sink_dec[:, :, None], (N_B, DEC_ROWS, LANES))
    slope_dec = np.zeros((DEC_ROWS, LANES), np.float32)
    for r in range(SWA_REP):
        for g in range(SWA_KV_HEADS):
            slope_dec[SWA_KV_HEADS * r + g, :] = _SLOPES[g * SWA_REP + r]
    slope_dec = jnp.asarray(slope_dec)

    memx = mem_prompt.reshape(batch * MEM_TOKENS, d)
    mem_kv = jnp.stack([norm_matmul(memx, norm_mem, w_mem16, l, F32, tm=256) for l in range(DEPTH)])
    mem_kv = mem_kv.reshape(DEPTH, batch, MEM_TOKENS, 2 * MEM_W)
    mem_k_prompt = mem_kv[..., :MEM_W].reshape(DEPTH, batch, MEM_TOKENS, MEM_HEADS, MEM_HD)
    mem_v_prompt = mem_kv[..., MEM_W:].reshape(DEPTH, batch, MEM_TOKENS, MEM_HEADS, MEM_HD)
    mk16 = mem_kv[..., :MEM_W].astype(BF16)
    mv16 = mem_kv[..., MEM_W:].astype(BF16)

    x = x_prompt.reshape(batch * seq, d)
    ret_states = []
    kv_p = kv_p16 = None
    for l in range(DEPTH):
        if l < N_A:
            x, st = fused_mixer_a(x, norm_mix, w_in_a16, w_out_a16, mk16, mv16, l, batch, seq)
            ret_states.append(st)
        else:
            j = l - N_A
            x = fused_mixer_b(x, norm_mix, w_in_b16, w_out_b16, kv_p16, mk16, mv16, l, sink_prompt, j, batch, seq)
        if l == N_A - 1:
            x, kv_p, kv_p16 = mlp_block(x, norm_mlp, w_up, w_down, l, norm_final, False, tm=tm,
                                        kv_proj=(norm_kv1, w_kv16[0]))
        else:
            x = mlp_block(x, norm_mlp, w_up, w_down, l, norm_final, l == DEPTH - 1, tm=tm)
    y_prompt = x.reshape(batch, seq, d)
    ret_prompt = jnp.stack(ret_states)
    kv_p3 = kv_p.reshape(batch, seq, 2 * KV_W)
    swa_k_prompt = kv_p3[:, -WINDOW:, :KV_W].reshape(batch, WINDOW, SWA_KV_HEADS, SWA_HD)
    swa_v_prompt = kv_p3[:, -WINDOW:, KV_W:].reshape(batch, WINDOW, SWA_KV_HEADS, SWA_HD)

    cmkt = jnp.transpose(cache_mem_k, (0, 1, 3, 4, 2)).reshape(DEPTH, nb, MEM_W, MEM_TOKENS)
    cmvt = jnp.transpose(cache_mem_v, (0, 1, 3, 4, 2)).reshape(DEPTH, nb, MEM_W, MEM_TOKENS)
    cskt = jnp.transpose(cache_swa_k, (0, 2, 3, 1)).reshape(nb, KV_W, WINDOW)
    csvt = jnp.transpose(cache_swa_v, (0, 2, 3, 1)).reshape(nb, KV_W, WINDOW)
    xs = x_sample.reshape(nb, d)
    ret_sample = None
    k_new = v_new = nkt = nvt = None
    for l in range(DEPTH):
        if l < N_A:
            proj = norm_matmul(xs, norm_mix, w_in_a16, l, F32, tm=nb, n_chunk=MIX_MAIN)
            o_ret, ret_sample = ret_decode(proj, state_ret, l, stacked_out=ret_sample)
            o_mem = mem_decode(proj, cmkt, cmvt, l, 4 * MIX_MAIN)
            attn = jnp.concatenate([o_ret, o_mem], axis=-1).astype(BF16)
            xs = matmul_residual(attn, w_out_a16, l, xs, tm=nb)
        else:
            j = l - N_A
            if j == 0:
                kv_s = norm_matmul(xs, norm_kv1, w_kv16, 0, F32, tm=nb)
                k_new, v_new = kv_s[:, :KV_W], kv_s[:, KV_W:]
            qproj = norm_matmul(xs, norm_mix[N_A:], w_in_b16, j, F32, tm=nb)
            res = swa_decode(qproj, k_new, v_new, cskt, csvt, slope_dec, sink_dec, j, emit_cache=(j == 0))
            o_swa = res[0]
            if j == 0:
                nkt, nvt = res[1], res[2]
            o_mem = mem_decode(qproj, cmkt, cmvt, l, MIX_MAIN)
            attn = jnp.concatenate([o_swa, o_mem], axis=-1).astype(BF16)
            xs = matmul_residual(attn, w_out_b16, j, xs, tm=nb)
        xs = mlp_block(xs, norm_mlp, w_up, w_down, l, norm_final, l == DEPTH - 1, tm=nb)
    y_sample = xs.reshape(nb, 1, d)
    swa_k_sample = jnp.transpose(nkt.reshape(nb, SWA_KV_HEADS, SWA_HD, WINDOW), (0, 3, 1, 2))
    swa_v_sample = jnp.transpose(nvt.reshape(nb, SWA_KV_HEADS, SWA_HD, WINDOW), (0, 3, 1, 2))

    return (y_prompt, y_sample, ret_prompt, ret_sample, swa_k_prompt, swa_v_prompt, swa_k_sample, swa_v_sample,
            mem_k_prompt, mem_v_prompt)
```

```python
import functools
import math

import jax
import jax.numpy as jnp
import numpy as np
from jax import lax
from jax.experimental import pallas as pl
from jax.experimental.pallas import tpu as pltpu

F32 = jnp.float32
BF16 = jnp.bfloat16

D_MODEL = 1024
DEPTH = 4
N_A = 2
N_B = 2
MEM_TOKENS = 256
MEM_HEADS = 4
MEM_W = 256
MEM_HD = 64
MIX_MAIN = 768
RET_HEADS = 6
RET_HD = 128
CHUNK = 128
SWA_HEADS = 12
SWA_KV_HEADS = 4
SWA_REP = SWA_HEADS // SWA_KV_HEADS
SWA_HD = 64
KV_W = SWA_KV_HEADS * SWA_HD
WINDOW = 128
D_FF = 4096
EPS = 1e-6
NEG = -1e30
ATTN_SCALE = 0.125
LANES = 128
TILE = 512
DEC_ROWS = 16

VMEM_LIMIT = 56 * 1024 * 1024

NT_DIMS = (((1,), (1,)), ((), ()))
TN_DIMS = (((0,), (0,)), ((), ()))


def _alibi_slopes(n):
    def pow2(m):
        start = 2.0 ** (-8.0 / m)
        return [start ** (i + 1) for i in range(m)]

    if math.log2(n).is_integer():
        s = pow2(n)
    else:
        c = 2 ** int(math.floor(math.log2(n)))
        s = pow2(c) + pow2(2 * c)[0::2][: n - c]
    return np.asarray(s, np.float32)


_LOG_G = np.log1p(-(2.0 ** (-5.0 - np.arange(RET_HEADS)))).astype(np.float32).astype(np.float64)
_SLOPES = _alibi_slopes(SWA_HEADS).astype(np.float64)


def _retention_consts():
    idx = np.arange(CHUNK, dtype=np.float64)
    diff = idx[:, None] - idx[None, :]
    scale = RET_HD ** -0.5
    decay = np.where(diff >= 0, np.exp(np.maximum(diff, 0.0)[None] * _LOG_G[:, None, None]), 0.0) * scale
    w_q = np.exp((idx + 1.0)[None, :] * _LOG_G[:, None])
    w_k = np.exp((CHUNK - 1.0 - idx)[None, :] * _LOG_G[:, None]) * scale
    w_q = np.broadcast_to(w_q[:, :, None], (RET_HEADS, CHUNK, RET_HD))
    w_k = np.broadcast_to(w_k[:, :, None], (RET_HEADS, CHUNK, RET_HD))
    g_c = np.exp(CHUNK * _LOG_G)
    pair = lambda t: np.ascontiguousarray(np.concatenate([t[0::2], t[1::2]], axis=2), np.float32)
    return pair(decay), pair(w_q), pair(w_k), [float(v) for v in g_c]


def _swa_bias():
    i = np.arange(WINDOW)[:, None]
    j = np.arange(2 * WINDOW)[None, :]
    dist = i + WINDOW - j
    valid = (dist >= 0) & (dist <= WINDOW)
    out = np.zeros((SWA_KV_HEADS, SWA_REP * WINDOW, 2 * WINDOW), np.float32)
    for g in range(SWA_KV_HEADS):
        for r in range(SWA_REP):
            b = np.where(valid, -_SLOPES[g * SWA_REP + r] * dist, NEG)
            out[g, r * WINDOW:(r + 1) * WINDOW] = b
    return out


def _rmsnorm(x, g):
    ms = jnp.mean(x * x, axis=-1, keepdims=True)
    return (x * lax.rsqrt(ms + EPS)) * g


def _lane_block64(lane):
    return lax.shift_right_logical(lane, 6)


def _cparams(n_axes):
    return pltpu.CompilerParams(dimension_semantics=("arbitrary",) * n_axes, vmem_limit_bytes=VMEM_LIMIT)


def _layer_block(tail, layer, n_grid):
    zeros = (0,) * len(tail)
    if n_grid == 1:
        return pl.BlockSpec((None,) + tuple(tail), lambda i: (layer,) + zeros)
    return pl.BlockSpec((None,) + tuple(tail), lambda i, j: (layer,) + zeros)


def _norm_matmul_kernel(x_ref, g_ref, w_ref, o_ref, *, n_chunk):
    xn = _rmsnorm(x_ref[...], g_ref[...]).astype(BF16)
    n = w_ref.shape[1]
    for n0 in range(0, n, n_chunk):
        n1 = min(n0 + n_chunk, n)
        o_ref[:, n0:n1] = jnp.dot(xn, w_ref[:, n0:n1], preferred_element_type=F32).astype(o_ref.dtype)


def norm_matmul(x, g, w, layer, out_dtype, tm, n_chunk=512):
    m, k = x.shape
    n = w.shape[2]
    return pl.pallas_call(
        functools.partial(_norm_matmul_kernel, n_chunk=n_chunk),
        grid=(m // tm,),
        in_specs=[pl.BlockSpec((tm, k), lambda i: (i, 0)),
                  _layer_block((1, k), layer, 1),
                  _layer_block((k, n), layer, 1)],
        out_specs=pl.BlockSpec((tm, n), lambda i: (i, 0)),
        out_shape=jax.ShapeDtypeStruct((m, n), out_dtype),
        compiler_params=_cparams(1),
        name="norm_matmul",
    )(x, g.reshape(g.shape[0], 1, k), w)


def _matmul_residual_kernel(a_ref, w_ref, x_ref, o_ref):
    o_ref[...] = x_ref[...] + jnp.dot(a_ref[...], w_ref[...], preferred_element_type=F32)


def matmul_residual(a, w, layer, x, tm):
    m, k = a.shape
    n = w.shape[2]
    return pl.pallas_call(
        _matmul_residual_kernel,
        grid=(m // tm,),
        in_specs=[pl.BlockSpec((tm, k), lambda i: (i, 0)),
                  _layer_block((k, n), layer, 1),
                  pl.BlockSpec((tm, n), lambda i: (i, 0))],
        out_specs=pl.BlockSpec((tm, n), lambda i: (i, 0)),
        out_shape=jax.ShapeDtypeStruct((m, n), F32),
        compiler_params=_cparams(1),
        name="matmul_residual",
    )(a, w, x)


def _mlp_kernel(x_ref, g_ref, wu_ref, wd_ref, gf_ref, *rest, fc, final_norm, with_kv):
    if with_kv:
        gkv_ref, wkv_ref, o_ref, kv_ref, kv16_ref, act_ref = rest
    else:
        o_ref, act_ref = rest
    xn = _rmsnorm(x_ref[...], g_ref[...]).astype(BF16)
    for c0 in range(0, D_FF, fc):
        h = jnp.dot(xn, wu_ref[:, c0:c0 + fc].astype(BF16), preferred_element_type=F32)
        act_ref[:, c0:c0 + fc] = jnp.square(jnp.maximum(h, 0.0)).astype(BF16)
    y = x_ref[...] + jnp.dot(act_ref[...], wd_ref[...].astype(BF16), preferred_element_type=F32)
    if with_kv:
        kv = jnp.dot(_rmsnorm(y, gkv_ref[...]).astype(BF16), wkv_ref[...], preferred_element_type=F32)
        kv_ref[...] = kv
        kv16_ref[...] = kv.astype(BF16)
    if final_norm:
        y = _rmsnorm(y, gf_ref[...])
    o_ref[...] = y


def mlp_block(x, g, w_up, w_down, layer, g_final, final_norm, tm, kv_proj=None, fc=512):
    m, d = x.shape
    with_kv = kv_proj is not None
    in_specs = [pl.BlockSpec((tm, d), lambda i: (i, 0)),
                _layer_block((1, d), layer, 1),
                pl.BlockSpec((None, d, D_FF), lambda i: (layer, 0, 0), pipeline_mode=pl.Buffered(1)),
                pl.BlockSpec((None, D_FF, d), lambda i: (layer, 0, 0), pipeline_mode=pl.Buffered(1)),
                pl.BlockSpec((1, d), lambda i: (0, 0))]
    args = [x, g.reshape(g.shape[0], 1, d), w_up, w_down, g_final.reshape(1, d)]
    out_specs = [pl.BlockSpec((tm, d), lambda i: (i, 0))]
    out_shape = [jax.ShapeDtypeStruct((m, d), F32)]
    if with_kv:
        in_specs += [pl.BlockSpec((1, d), lambda i: (0, 0)),
                     pl.BlockSpec((d, 2 * KV_W), lambda i: (0, 0), pipeline_mode=pl.Buffered(1))]
        args += list(kv_proj)
        out_specs += [pl.BlockSpec((tm, 2 * KV_W), lambda i: (i, 0))] * 2
        out_shape += [jax.ShapeDtypeStruct((m, 2 * KV_W), F32), jax.ShapeDtypeStruct((m, 2 * KV_W), BF16)]
    res = pl.pallas_call(
        functools.partial(_mlp_kernel, fc=fc, final_norm=final_norm, with_kv=with_kv),
        grid=(m // tm,),
        in_specs=in_specs,
        out_specs=out_specs,
        out_shape=out_shape,
        scratch_shapes=[pltpu.VMEM((tm, D_FF), BF16)],
        compiler_params=_cparams(1),
        name="mlp_block",
    )(*args)
    return res if with_kv else res[0]


def _mem_attention_tile(qm, mk, mv, fillers=()):
    t = qm.shape[0]
    lane_head = _lane_block64(lax.broadcasted_iota(jnp.int32, (t, MEM_W), 1))
    tok_head = _lane_block64(lax.broadcasted_iota(jnp.int32, (MEM_TOKENS, MEM_W), 1))
    probs = []
    vals = []
    for h in range(MEM_HEADS):
        qh = jnp.where(lane_head == h, qm, jnp.zeros_like(qm))
        s = lax.dot_general(qh, mk, NT_DIMS, preferred_element_type=F32)
        if h < len(fillers):
            fillers[h]()
        m = jnp.max(s, axis=-1, keepdims=True)
        e = jnp.exp(s - m)
        inv = 1.0 / jnp.sum(e, axis=-1, keepdims=True)
        probs.append((e * inv).astype(BF16))
        vals.append(jnp.where(tok_head == h, mv, jnp.zeros_like(mv)))
    return jnp.dot(jnp.concatenate(probs, axis=1), jnp.concatenate(vals, axis=0), preferred_element_type=F32)


def _project_tile(x_ref, g_ref, w_ref, p_ref, n_chunk):
    xn = _rmsnorm(x_ref[...], g_ref[...]).astype(BF16)
    n = w_ref.shape[1]
    for n0 in range(0, n, n_chunk):
        n1 = min(n0 + n_chunk, n)
        p_ref[:, n0:n1] = jnp.dot(xn, w_ref[:, n0:n1], preferred_element_type=F32).astype(BF16)


PIECE = 256


def _projection_pieces(x_ref, g_ref, w_ref, p_ref):
    xn = _rmsnorm(x_ref[...], g_ref[...]).astype(BF16)

    def piece(c):
        def run():
            cols = slice(c * PIECE, (c + 1) * PIECE)
            p_ref[:, cols] = jnp.dot(xn, w_ref[:, cols], preferred_element_type=F32).astype(BF16)
        return run

    return [piece(c) for c in range(w_ref.shape[1] // PIECE)]


def _block_diag2(a, b):
    z = jnp.zeros_like(a)
    return jnp.concatenate([jnp.concatenate([a, z], axis=1), jnp.concatenate([z, b], axis=1)], axis=0)


def _group_norm_gate(o, gate):
    mu = jnp.mean(o, axis=-1, keepdims=True)
    oc = o - mu
    var = jnp.mean(oc * oc, axis=-1, keepdims=True)
    return (gate * jax.nn.sigmoid(gate) * (oc * lax.rsqrt(var + EPS))).astype(BF16)


def _retention_tile(p_ref, a_ref, s_scr, decay_ref, wq_ref, wk_ref, g_c, fillers=()):
    pw = 2 * RET_HD
    pairs = range(RET_HEADS // 2)
    for c in range(TILE // CHUNK):
        rows = slice(c * CHUNK, (c + 1) * CHUNK)
        q2 = [p_ref[rows, hp * pw:(hp + 1) * pw] for hp in pairs]
        k2 = [p_ref[rows, MIX_MAIN + hp * pw:MIX_MAIN + (hp + 1) * pw] for hp in pairs]
        v2 = [p_ref[rows, 2 * MIX_MAIN + hp * pw:2 * MIX_MAIN + (hp + 1) * pw] for hp in pairs]
        sc = [lax.dot_general(q2[hp], _block_diag2(k2[hp][:, :RET_HD], k2[hp][:, RET_HD:]), NT_DIMS,
                              preferred_element_type=F32) for hp in pairs]
        kw = [(k2[hp].astype(F32) * wk_ref[hp]).astype(BF16) for hp in pairs]
        kv = [lax.dot_general(kw[hp], v2[hp], TN_DIMS, preferred_element_type=F32) for hp in pairs]
        if 2 * c < len(fillers):
            fillers[2 * c]()
        s_old = [s_scr[h] for h in range(RET_HEADS)]
        lhs = [jnp.concatenate([(sc[hp] * decay_ref[hp]).astype(BF16),
                                (q2[hp].astype(F32) * wq_ref[hp]).astype(BF16)], axis=1) for hp in pairs]
        rhs = [jnp.concatenate([_block_diag2(v2[hp][:, :RET_HD], v2[hp][:, RET_HD:]),
                                _block_diag2(s_old[2 * hp].astype(BF16), s_old[2 * hp + 1].astype(BF16))], axis=0)
               for hp in pairs]
        o2 = [jnp.dot(lhs[hp], rhs[hp], preferred_element_type=F32) for hp in pairs]
        if 2 * c + 1 < len(fillers):
            fillers[2 * c + 1]()
        for hp in pairs:
            s_scr[2 * hp] = g_c[2 * hp] * s_old[2 * hp] + kv[hp][:RET_HD, :RET_HD]
            s_scr[2 * hp + 1] = g_c[2 * hp + 1] * s_old[2 * hp + 1] + kv[hp][RET_HD:, RET_HD:]
        for hp in pairs:
            lo = hp * pw
            gate2 = p_ref[rows, 3 * MIX_MAIN + lo:3 * MIX_MAIN + lo + pw].astype(F32)
            a_ref[rows, lo:lo + RET_HD] = _group_norm_gate(o2[hp][:, :RET_HD], gate2[:, :RET_HD])
            a_ref[rows, lo + RET_HD:lo + pw] = _group_norm_gate(o2[hp][:, RET_HD:], gate2[:, RET_HD:])


def _fused_a_kernel(xc_ref, xn_ref, g_ref, win_ref, wout_ref, mk_ref, mv_ref, decay_ref, wq_ref, wk_ref,
                    o_ref, st_ref, p0, p1, a_scr, s_scr, *, g_c, tiles_per_seq):
    t = pl.program_id(0)

    @pl.when(t == 0)
    def _():
        _project_tile(xc_ref, g_ref, win_ref, p0, MIX_MAIN)

    @pl.when(t % tiles_per_seq == 0)
    def _():
        s_scr[...] = jnp.zeros_like(s_scr)

    def step(p_cur, p_next):
        pieces = _projection_pieces(xn_ref, g_ref, win_ref, p_next)
        pieces[8]()
        _retention_tile(p_cur, a_scr, s_scr, decay_ref, wq_ref, wk_ref, g_c, pieces[:8])
        st_ref[...] = s_scr[...]
        qm = p_cur[:, 4 * MIX_MAIN:4 * MIX_MAIN + MEM_W]
        a_scr[:, MIX_MAIN:] = _mem_attention_tile(qm, mk_ref[...], mv_ref[...], pieces[9:]).astype(BF16)
        o_ref[...] = xc_ref[...] + jnp.dot(a_scr[...], wout_ref[...], preferred_element_type=F32)

    @pl.when(t % 2 == 0)
    def _():
        step(p0, p1)

    @pl.when(t % 2 == 1)
    def _():
        step(p1, p0)


def fused_mixer_a(x, g, w_in, w_out, mem_k, mem_v, layer, batch, seq):
    n_tiles = batch * seq // TILE
    tps = seq // TILE
    decay, w_q, w_k, g_c = _retention_consts()
    pw = w_in.shape[2]
    c3 = lambda t: (0, 0, 0)
    mem_spec = pl.BlockSpec((None, None, MEM_TOKENS, MEM_W), lambda t: (layer, t // tps, 0, 0))
    return pl.pallas_call(
        functools.partial(_fused_a_kernel, g_c=g_c, tiles_per_seq=tps),
        grid=(n_tiles,),
        in_specs=[pl.BlockSpec((TILE, D_MODEL), lambda t: (t, 0)),
                  pl.BlockSpec((TILE, D_MODEL), lambda t: (jnp.minimum(t + 1, n_tiles - 1), 0)),
                  _layer_block((1, D_MODEL), layer, 1),
                  pl.BlockSpec((None, D_MODEL, pw), lambda t: (layer, 0, 0), pipeline_mode=pl.Buffered(1)),
                  pl.BlockSpec((None, D_MODEL, D_MODEL), lambda t: (layer, 0, 0), pipeline_mode=pl.Buffered(1)),
                  mem_spec, mem_spec,
                  pl.BlockSpec((RET_HEADS // 2, CHUNK, 2 * CHUNK), c3),
                  pl.BlockSpec((RET_HEADS // 2, CHUNK, 2 * RET_HD), c3),
                  pl.BlockSpec((RET_HEADS // 2, CHUNK, 2 * RET_HD), c3)],
        out_specs=[pl.BlockSpec((TILE, D_MODEL), lambda t: (t, 0)),
                   pl.BlockSpec((None, RET_HEADS, RET_HD, RET_HD), lambda t: (t // tps, 0, 0, 0))],
        out_shape=[jax.ShapeDtypeStruct((batch * seq, D_MODEL), F32),
                   jax.ShapeDtypeStruct((batch, RET_HEADS, RET_HD, RET_HD), F32)],
        scratch_shapes=[pltpu.VMEM((TILE, pw), BF16), pltpu.VMEM((TILE, pw), BF16),
                        pltpu.VMEM((TILE, D_MODEL), BF16),
                        pltpu.VMEM((RET_HEADS, RET_HD, RET_HD), F32)],
        compiler_params=_cparams(1),
        name="fused_mixer_a",
    )(x, x, g.reshape(g.shape[0], 1, D_MODEL), w_in, w_out, mem_k, mem_v,
      jnp.asarray(decay), jnp.asarray(w_q), jnp.asarray(w_k))


def _swa_tile(p_ref, kvp_ref, kvc_ref, a_ref, bias_ref, sink_ref, is_first, fillers=()):
    lane_group = _lane_block64(lax.broadcasted_iota(jnp.int32, (WINDOW, KV_W), 1))
    key_group = _lane_block64(lax.broadcasted_iota(jnp.int32, (2 * WINDOW, KV_W), 1))
    for i in range(TILE // WINDOW):
        r0 = i * WINDOW
        if i == 0:
            kv2 = jnp.concatenate([kvp_ref[...], kvc_ref[0:WINDOW, :]], axis=0)
        else:
            kv2 = kvc_ref[r0 - WINDOW:r0 + WINDOW, :]
        k2 = kv2[:, :KV_W]
        v2 = kv2[:, KV_W:]
        qs = [p_ref[r0:r0 + WINDOW, r * KV_W:(r + 1) * KV_W] for r in range(SWA_REP)]
        probs = []
        vals = []
        for g in range(SWA_KV_HEADS):
            sel = lane_group == g
            qg = jnp.concatenate([jnp.where(sel, qr, jnp.zeros_like(qr)) for qr in qs], axis=0)
            s = lax.dot_general(qg, k2, NT_DIMS, preferred_element_type=F32) + bias_ref[g]
            if g == 1 and i < len(fillers):
                fillers[i]()
            s_prev = s[:, :WINDOW]
            s_cur = s[:, WINDOW:]
            if i == 0:
                s_prev = jnp.where(is_first, NEG, s_prev)
            sink = sink_ref[g]
            m = jnp.maximum(jnp.max(jnp.maximum(s_prev, s_cur), axis=-1, keepdims=True), sink)
            e_prev = jnp.exp(s_prev - m)
            e_cur = jnp.exp(s_cur - m)
            inv = 1.0 / (jnp.sum(e_prev + e_cur, axis=-1, keepdims=True) + jnp.exp(sink - m))
            probs.append((e_prev * inv).astype(BF16))
            probs.append((e_cur * inv).astype(BF16))
            vals.append(jnp.where(key_group == g, v2, jnp.zeros_like(v2)))
        o = jnp.dot(jnp.concatenate(probs, axis=1), jnp.concatenate(vals, axis=0), preferred_element_type=F32)
        for r in range(SWA_REP):
            a_ref[r0:r0 + WINDOW, r * KV_W:(r + 1) * KV_W] = o[r * WINDOW:(r + 1) * WINDOW].astype(BF16)


def _fused_b_kernel(xc_ref, xn_ref, g_ref, win_ref, wout_ref, kvp_ref, kvc_ref, mk_ref, mv_ref, bias_ref, sink_ref,
                    o_ref, p0, p1, a_scr, *, tiles_per_seq):
    t = pl.program_id(0)
    is_first = t % tiles_per_seq == 0

    @pl.when(t == 0)
    def _():
        _project_tile(xc_ref, g_ref, win_ref, p0, 512)

    def out_piece(c):
        def run():
            cols = slice(c * PIECE, (c + 1) * PIECE)
            o_ref[:, cols] = xc_ref[:, cols] + jnp.dot(a_scr[:, :MIX_MAIN], wout_ref[:MIX_MAIN, cols],
                                                       preferred_element_type=F32)
        return run

    def step(p_cur, p_next):
        pieces = _projection_pieces(xn_ref, g_ref, win_ref, p_next)
        _swa_tile(p_cur, kvp_ref, kvc_ref, a_scr, bias_ref, sink_ref, is_first, pieces[:2])
        qm = p_cur[:, MIX_MAIN:MIX_MAIN + MEM_W]
        a_mem = _mem_attention_tile(qm, mk_ref[...], mv_ref[...],
                                    [pieces[2], pieces[3], out_piece(0), out_piece(1)]).astype(BF16)
        out_piece(2)()
        out_piece(3)()
        o_ref[...] += jnp.dot(a_mem, wout_ref[MIX_MAIN:, :], preferred_element_type=F32)

    @pl.when(t % 2 == 0)
    def _():
        step(p0, p1)

    @pl.when(t % 2 == 1)
    def _():
        step(p1, p0)


def fused_mixer_b(x, g, w_in, w_out, kv, mem_k, mem_v, layer, sink_rows, sub_layer, batch, seq):
    n_tiles = batch * seq // TILE
    tps = seq // TILE
    nw = TILE // WINDOW
    bias = jnp.asarray(_swa_bias())
    mem_spec = pl.BlockSpec((None, None, MEM_TOKENS, MEM_W), lambda t: (layer, t // tps, 0, 0))

    def prev_map(t):
        return (jnp.maximum(t * nw - 1, (t // tps) * (seq // WINDOW)), 0)

    return pl.pallas_call(
        functools.partial(_fused_b_kernel, tiles_per_seq=tps),
        grid=(n_tiles,),
        in_specs=[pl.BlockSpec((TILE, D_MODEL), lambda t: (t, 0)),
                  pl.BlockSpec((TILE, D_MODEL), lambda t: (jnp.minimum(t + 1, n_tiles - 1), 0)),
                  _layer_block((1, D_MODEL), layer, 1),
                  pl.BlockSpec((None, D_MODEL, D_MODEL), lambda t: (sub_layer, 0, 0), pipeline_mode=pl.Buffered(1)),
                  pl.BlockSpec((None, D_MODEL, D_MODEL), lambda t: (sub_layer, 0, 0), pipeline_mode=pl.Buffered(1)),
                  pl.BlockSpec((WINDOW, 2 * KV_W), prev_map),
                  pl.BlockSpec((TILE, 2 * KV_W), lambda t: (t, 0)),
                  mem_spec, mem_spec,
                  pl.BlockSpec((SWA_KV_HEADS, SWA_REP * WINDOW, 2 * WINDOW), lambda t: (0, 0, 0)),
                  pl.BlockSpec((None, SWA_KV_HEADS, SWA_REP * WINDOW, LANES), lambda t: (sub_layer, 0, 0, 0))],
        out_specs=pl.BlockSpec((TILE, D_MODEL), lambda t: (t, 0)),
        out_shape=jax.ShapeDtypeStruct((batch * seq, D_MODEL), F32),
        scratch_shapes=[pltpu.VMEM((TILE, D_MODEL), BF16), pltpu.VMEM((TILE, D_MODEL), BF16),
                        pltpu.VMEM((TILE, D_MODEL), BF16)],
        compiler_params=_cparams(1),
        name="fused_mixer_b",
    )(x, x, g.reshape(g.shape[0], 1, D_MODEL), w_in, w_out, kv, kv, mem_k, mem_v, bias, sink_rows)


def _ret_decode_kernel(proj_ref, st_ref, *rest, bb):
    o_ref, nst_ref = rest[-2:]
    scale = RET_HD ** -0.5
    row = lax.broadcasted_iota(jnp.int32, (bb, bb * RET_HD), 0)
    lane_seq = lax.shift_right_logical(lax.broadcasted_iota(jnp.int32, (bb, bb * RET_HD), 1), 7)
    own = row == lane_seq
    for h in range(RET_HEADS):
        lo = h * RET_HD
        q8 = proj_ref[:, lo:lo + RET_HD]
        k8 = proj_ref[:, MIX_MAIN + lo:MIX_MAIN + lo + RET_HD] * scale
        v8 = proj_ref[:, 2 * MIX_MAIN + lo:2 * MIX_MAIN + lo + RET_HD]
        gate = proj_ref[:, 3 * MIX_MAIN + lo:3 * MIX_MAIN + lo + RET_HD]
        g = float(np.exp(_LOG_G[h]))
        vexp = jnp.where(own, jnp.concatenate([v8] * bb, axis=1), 0.0).astype(BF16)
        outer = lax.dot_general(k8.astype(BF16), vexp, TN_DIMS, preferred_element_type=F32)
        cols = []
        for j in range(bb):
            s_new = g * st_ref[j, h] + outer[:, j * RET_HD:(j + 1) * RET_HD]
            nst_ref[j, h] = s_new
            cols.append(s_new.astype(BF16))
        o_all = jnp.dot(q8.astype(BF16), jnp.concatenate(cols, axis=1), preferred_element_type=F32)
        o_all = jnp.where(own, o_all, 0.0)
        o = o_all[:, :RET_HD]
        for j in range(1, bb):
            o = o + o_all[:, j * RET_HD:(j + 1) * RET_HD]
        mu = jnp.mean(o, axis=-1, keepdims=True)
        oc = o - mu
        var = jnp.mean(oc * oc, axis=-1, keepdims=True)
        on = oc * lax.rsqrt(var + EPS)
        o_ref[:, lo:lo + RET_HD] = gate * jax.nn.sigmoid(gate) * on


def ret_decode(proj, state, layer, stacked_out=None, bb=16):
    nb = proj.shape[0]
    pw = proj.shape[1]
    in_specs = [pl.BlockSpec((bb, pw), lambda i: (i, 0)),
                pl.BlockSpec((None, bb, RET_HEADS, RET_HD, RET_HD), lambda i: (layer, i, 0, 0, 0))]
    args = [proj, state]
    aliases = {}
    if stacked_out is not None:
        in_specs.append(pl.BlockSpec(memory_space=pl.ANY))
        args.append(stacked_out)
        aliases = {2: 1}
    return pl.pallas_call(
        functools.partial(_ret_decode_kernel, bb=bb),
        grid=(nb // bb,),
        in_specs=in_specs,
        out_specs=[pl.BlockSpec((bb, MIX_MAIN), lambda i: (i, 0)),
                   pl.BlockSpec((None, bb, RET_HEADS, RET_HD, RET_HD), lambda i: (layer, i, 0, 0, 0))],
        out_shape=[jax.ShapeDtypeStruct((nb, MIX_MAIN), F32),
                   jax.ShapeDtypeStruct(state.shape, F32)],
        input_output_aliases=aliases,
        compiler_params=_cparams(1),
        name="ret_decode",
    )(*args)


def _mem_decode_kernel(q_ref, kt_ref, vt_ref, o_ref, *, bb, q_off):
    row = lax.broadcasted_iota(jnp.int32, (8, MEM_W), 0)
    lane_head = _lane_block64(lax.broadcasted_iota(jnp.int32, (8, MEM_W), 1))
    sel = row == lane_head
    outs = []
    for j in range(bb):
        qrow = q_ref[j:j + 1, q_off:q_off + MEM_W]
        qexp = jnp.where(sel, jnp.broadcast_to(qrow, (8, MEM_W)), 0.0).astype(BF16)
        s = jnp.dot(qexp, kt_ref[j].astype(BF16), preferred_element_type=F32)
        m = jnp.max(s, axis=-1, keepdims=True)
        e = jnp.exp(s - m)
        den = jnp.sum(e, axis=-1, keepdims=True)
        o = lax.dot_general(e.astype(BF16), vt_ref[j].astype(BF16), NT_DIMS, preferred_element_type=F32) / den
        outs.append(jnp.sum(jnp.where(sel, o, 0.0), axis=0, keepdims=True))
    o_ref[...] = jnp.concatenate(outs, axis=0)


def mem_decode(proj, cache_kt, cache_vt, layer, q_off, bb=16):
    nb = proj.shape[0]
    pw = proj.shape[1]
    cmap = lambda i: (layer, i, 0, 0)
    return pl.pallas_call(
        functools.partial(_mem_decode_kernel, bb=bb, q_off=q_off),
        grid=(nb // bb,),
        in_specs=[pl.BlockSpec((bb, pw), lambda i: (i, 0)),
                  pl.BlockSpec((None, bb, MEM_W, MEM_TOKENS), cmap),
                  pl.BlockSpec((None, bb, MEM_W, MEM_TOKENS), cmap)],
        out_specs=pl.BlockSpec((bb, MEM_W), lambda i: (i, 0)),
        out_shape=jax.ShapeDtypeStruct((nb, MEM_W), F32),
        compiler_params=_cparams(1),
        name="mem_decode",
    )(proj, cache_kt, cache_vt)


def _swa_decode_kernel(q_ref, kn_ref, vn_ref, ckt_ref, cvt_ref, slope_ref, sink_ref, o_ref, *cache_out, bb):
    row = lax.broadcasted_iota(jnp.int32, (DEC_ROWS, KV_W), 0)
    lane_group = _lane_block64(lax.broadcasted_iota(jnp.int32, (DEC_ROWS, KV_W), 1))
    dist = (WINDOW - lax.broadcasted_iota(jnp.int32, (DEC_ROWS, WINDOW), 1)).astype(F32)
    bias = -slope_ref[...] * dist
    sink = sink_ref[...]
    pos = lax.broadcasted_iota(jnp.int32, (KV_W, WINDOW), 1)
    knt = kn_ref[...].T
    vnt = vn_ref[...].T
    outs = [[] for _ in range(SWA_REP)]
    for j in range(bb):
        qexp = jnp.zeros((DEC_ROWS, KV_W), F32)
        for r in range(SWA_REP):
            qr = jnp.broadcast_to(q_ref[j:j + 1, r * KV_W:(r + 1) * KV_W], (DEC_ROWS, KV_W))
            qexp = jnp.where(row == SWA_KV_HEADS * r + lane_group, qr, qexp)
        kt = ckt_ref[j]
        vt = cvt_ref[j]
        s = jnp.dot(qexp.astype(BF16), kt.astype(BF16), preferred_element_type=F32) + bias
        s_new = jnp.sum(qexp * kn_ref[j:j + 1, :], axis=-1, keepdims=True)
        m = jnp.maximum(jnp.maximum(jnp.max(s, axis=-1, keepdims=True), s_new), sink)
        e = jnp.exp(s - m)
        e_new = jnp.exp(s_new - m)
        inv = 1.0 / (jnp.sum(e, axis=-1, keepdims=True) + e_new + jnp.exp(sink - m))
        o = lax.dot_general(e.astype(BF16), vt.astype(BF16), NT_DIMS, preferred_element_type=F32)
        o = (o + jnp.concatenate([e_new, e_new], axis=1) * vn_ref[j:j + 1, :]) * jnp.concatenate([inv, inv], axis=1)
        for r in range(SWA_REP):
            keep = row == SWA_KV_HEADS * r + lane_group
            outs[r].append(jnp.sum(jnp.where(keep, o, 0.0), axis=0, keepdims=True))
        if cache_out:
            nkt_ref, nvt_ref = cache_out
            last = pos == WINDOW - 1
            nkt_ref[j] = jnp.where(last, jnp.broadcast_to(knt[:, j:j + 1], (KV_W, WINDOW)), pltpu.roll(kt, WINDOW - 1, 1))
            nvt_ref[j] = jnp.where(last, jnp.broadcast_to(vnt[:, j:j + 1], (KV_W, WINDOW)), pltpu.roll(vt, WINDOW - 1, 1))
    for r in range(SWA_REP):
        o_ref[:, r * KV_W:(r + 1) * KV_W] = jnp.concatenate(outs[r], axis=0)


def swa_decode(qproj, k_new, v_new, cache_kt, cache_vt, slope_rows, sink_rows, sink_layer, emit_cache, bb=8):
    nb = qproj.shape[0]
    pw = qproj.shape[1]
    cache_spec = pl.BlockSpec((bb, KV_W, WINDOW), lambda i: (i, 0, 0))
    cache_shape = jax.ShapeDtypeStruct((nb, KV_W, WINDOW), F32)
    n_cache = 2 if emit_cache else 0
    return pl.pallas_call(
        functools.partial(_swa_decode_kernel, bb=bb),
        grid=(nb // bb,),
        in_specs=[pl.BlockSpec((bb, pw), lambda i: (i, 0)),
                  pl.BlockSpec((bb, KV_W), lambda i: (i, 0)),
                  pl.BlockSpec((bb, KV_W), lambda i: (i, 0)),
                  cache_spec, cache_spec,
                  pl.BlockSpec((DEC_ROWS, LANES), lambda i: (0, 0)),
                  pl.BlockSpec((None, DEC_ROWS, LANES), lambda i: (sink_layer, 0, 0))],
        out_specs=[pl.BlockSpec((bb, MIX_MAIN), lambda i: (i, 0))] + [cache_spec] * n_cache,
        out_shape=[jax.ShapeDtypeStruct((nb, MIX_MAIN), F32)] + [cache_shape] * n_cache,
        compiler_params=_cparams(1),
        name="swa_decode",
    )(qproj, k_new, v_new, cache_kt, cache_vt, slope_rows, sink_rows)


def _permute_q_cols(w):
    lead = w.shape[:-1]
    return w.reshape(*lead, SWA_KV_HEADS, SWA_REP, SWA_HD).swapaxes(-3, -2).reshape(*lead, MIX_MAIN)


def kernel(x_prompt, x_sample, cache_mem_k, cache_mem_v, state_ret, cache_swa_k, cache_swa_v, mem_prompt,
           norm_mix, w_in_a, w_out_a, w_in_b, w_out_b, attn_sinks, norm_mem, w_mem_kv, norm_kv, w_kv,
           norm_mlp, w_up, w_down, norm_final):
    batch, seq, d = x_prompt.shape
    nb = x_sample.shape[0]
    tm = 512

    w_in_a16 = jnp.concatenate([w_in_a[:, :, :4 * MIX_MAIN], w_in_a[:, :, 4 * MIX_MAIN:] * ATTN_SCALE], axis=-1).astype(BF16)
    w_out_a16 = w_out_a.astype(BF16)
    w_in_b16 = (jnp.concatenate([_permute_q_cols(w_in_b[:, :, :MIX_MAIN]), w_in_b[:, :, MIX_MAIN:]], axis=-1)
                * ATTN_SCALE).astype(BF16)
    w_out_b16 = jnp.concatenate(
        [_permute_q_cols(w_out_b[:, :MIX_MAIN, :].swapaxes(1, 2)).swapaxes(1, 2), w_out_b[:, MIX_MAIN:, :]],
        axis=1).astype(BF16)
    w_mem16 = w_mem_kv.astype(BF16)
    w_kv16 = w_kv.astype(BF16).reshape(1, d, 2 * KV_W)
    norm_kv1 = norm_kv.reshape(1, d)

    sinks_gr = attn_sinks.reshape(N_B, SWA_KV_HEADS, SWA_REP)
    sink_prompt = jnp.broadcast_to(sinks_gr[:, :, :, None, None], (N_B, SWA_KV_HEADS, SWA_REP, WINDOW, LANES)
                                   ).reshape(N_B, SWA_KV_HEADS, SWA_REP * WINDOW, LANES)
    sink_dec = jnp.concatenate([sinks_gr.swapaxes(1, 2).reshape(N_B, SWA_HEADS),
                                jnp.zeros((N_B, DEC_ROWS - SWA_HEADS), F32)], axis=1)
    sink_dec = jnp.broadcast_to(sink_dec[:, :, None], (N_B, DEC_ROWS, LANES))
    slope_dec = np.zeros((DEC_ROWS, LANES), np.float32)
    for r in range(SWA_REP):
        for g in range(SWA_KV_HEADS):
            slope_dec[SWA_KV_HEADS * r + g, :] = _SLOPES[g * SWA_REP + r]
    slope_dec = jnp.asarray(slope_dec)

    memx = mem_prompt.reshape(batch * MEM_TOKENS, d)
    mem_kv = jnp.stack([norm_matmul(memx, norm_mem, w_mem16, l, F32, tm=256) for l in range(DEPTH)])
    mem_kv = mem_kv.reshape(DEPTH, batch, MEM_TOKENS, 2 * MEM_W)
    mem_k_prompt = mem_kv[..., :MEM_W].reshape(DEPTH, batch, MEM_TOKENS, MEM_HEADS, MEM_HD)
    mem_v_prompt = mem_kv[..., MEM_W:].reshape(DEPTH, batch, MEM_TOKENS, MEM_HEADS, MEM_HD)
    mk16 = mem_kv[..., :MEM_W].astype(BF16)
    mv16 = mem_kv[..., MEM_W:].astype(BF16)

    x = x_prompt.reshape(batch * seq, d)
    ret_states = []
    kv_p = kv_p16 = None
    for l in range(DEPTH):
        if l < N_A:
            x, st = fused_mixer_a(x, norm_mix, w_in_a16, w_out_a16, mk16, mv16, l, batch, seq)
            ret_states.append(st)
        else:
            j = l - N_A
            x = fused_mixer_b(x, norm_mix, w_in_b16, w_out_b16, kv_p16, mk16, mv16, l, sink_prompt, j, batch, seq)
        if l == N_A - 1:
            x, kv_p, kv_p16 = mlp_block(x, norm_mlp, w_up, w_down, l, norm_final, False, tm=tm,
                                        kv_proj=(norm_kv1, w_kv16[0]))
        else:
            x = mlp_block(x, norm_mlp, w_up, w_down, l, norm_final, l == DEPTH - 1, tm=tm)
    y_prompt = x.reshape(batch, seq, d)
    ret_prompt = jnp.stack(ret_states)
    kv_p3 = kv_p.reshape(batch, seq, 2 * KV_W)
    swa_k_prompt = kv_p3[:, -WINDOW:, :KV_W].reshape(batch, WINDOW, SWA_KV_HEADS, SWA_HD)
    swa_v_prompt = kv_p3[:, -WINDOW:, KV_W:].reshape(batch, WINDOW, SWA_KV_HEADS, SWA_HD)

    cmkt = jnp.transpose(cache_mem_k, (0, 1, 3, 4, 2)).reshape(DEPTH, nb, MEM_W, MEM_TOKENS)
    cmvt = jnp.transpose(cache_mem_v, (0, 1, 3, 4, 2)).reshape(DEPTH, nb, MEM_W, MEM_TOKENS)
    cskt = jnp.transpose(cache_swa_k, (0, 2, 3, 1)).reshape(nb, KV_W, WINDOW)
    csvt = jnp.transpose(cache_swa_v, (0, 2, 3, 1)).reshape(nb, KV_W, WINDOW)
    xs = x_sample.reshape(nb, d)
    ret_sample = None
    k_new = v_new = nkt = nvt = None
    for l in range(DEPTH):
        if l < N_A:
            proj = norm_matmul(xs, norm_mix, w_in_a16, l, F32, tm=nb, n_chunk=MIX_MAIN)
            o_ret, ret_sample = ret_decode(proj, state_ret, l, stacked_out=ret_sample)
            o_mem = mem_decode(proj, cmkt, cmvt, l, 4 * MIX_MAIN)
            attn = jnp.concatenate([o_ret, o_mem], axis=-1).astype(BF16)
            xs = matmul_residual(attn, w_out_a16, l, xs, tm=nb)
        else:
            j = l - N_A
            if j == 0:
                kv_s = norm_matmul(xs, norm_kv1, w_kv16, 0, F32, tm=nb)
                k_new, v_new = kv_s[:, :KV_W], kv_s[:, KV_W:]
            qproj = norm_matmul(xs, norm_mix[N_A:], w_in_b16, j, F32, tm=nb)
            res = swa_decode(qproj, k_new, v_new, cskt, csvt, slope_dec, sink_dec, j, emit_cache=(j == 0))
            o_swa = res[0]
            if j == 0:
                nkt, nvt = res[1], res[2]
            o_mem = mem_decode(qproj, cmkt, cmvt, l, MIX_MAIN)
            attn = jnp.concatenate([o_swa, o_mem], axis=-1).astype(BF16)
            xs = matmul_residual(attn, w_out_b16, j, xs, tm=nb)
        xs = mlp_block(xs, norm_mlp, w_up, w_down, l, norm_final, l == DEPTH - 1, tm=nb)
    y_sample = xs.reshape(nb, 1, d)
    swa_k_sample = jnp.transpose(nkt.reshape(nb, SWA_KV_HEADS, SWA_HD, WINDOW), (0, 3, 1, 2))
    swa_v_sample = jnp.transpose(nvt.reshape(nb, SWA_KV_HEADS, SWA_HD, WINDOW), (0, 3, 1, 2))

    return (y_prompt, y_sample, ret_prompt, ret_sample, swa_k_prompt, swa_v_prompt, swa_k_sample, swa_v_sample,
            mem_k_prompt, mem_v_prompt)
```

```python
import functools
import math

import jax
import jax.numpy as jnp
import numpy as np
from jax import lax
from jax.experimental import pallas as pl
from jax.experimental.pallas import tpu as pltpu

F32 = jnp.float32
BF16 = jnp.bfloat16

D_MODEL = 1024
DEPTH = 4
N_A = 2
N_B = 2
MEM_TOKENS = 256
MEM_HEADS = 4
MEM_W = 256
MEM_HD = 64
MIX_MAIN = 768
RET_HEADS = 6
RET_HD = 128
CHUNK = 128
SWA_HEADS = 12
SWA_KV_HEADS = 4
SWA_REP = SWA_HEADS // SWA_KV_HEADS
SWA_HD = 64
KV_W = SWA_KV_HEADS * SWA_HD
WINDOW = 128
D_FF = 4096
EPS = 1e-6
NEG = -1e30
ATTN_SCALE = 0.125
LANES = 128
TILE = 512
DEC_ROWS = 16

VMEM_LIMIT = 56 * 1024 * 1024

NT_DIMS = (((1,), (1,)), ((), ()))
TN_DIMS = (((0,), (0,)), ((), ()))


def _alibi_slopes(n):
    def pow2(m):
        start = 2.0 ** (-8.0 / m)
        return [start ** (i + 1) for i in range(m)]

    if math.log2(n).is_integer():
        s = pow2(n)
    else:
        c = 2 ** int(math.floor(math.log2(n)))
        s = pow2(c) + pow2(2 * c)[0::2][: n - c]
    return np.asarray(s, np.float32)


_LOG_G = np.log1p(-(2.0 ** (-5.0 - np.arange(RET_HEADS)))).astype(np.float32).astype(np.float64)
_SLOPES = _alibi_slopes(SWA_HEADS).astype(np.float64)


def _retention_consts():
    idx = np.arange(CHUNK, dtype=np.float64)
    diff = idx[:, None] - idx[None, :]
    scale = RET_HD ** -0.5
    decay = np.where(diff >= 0, np.exp(np.maximum(diff, 0.0)[None] * _LOG_G[:, None, None]), 0.0) * scale
    w_q = np.exp((idx + 1.0)[None, :] * _LOG_G[:, None])
    w_k = np.exp((CHUNK - 1.0 - idx)[None, :] * _LOG_G[:, None]) * scale
    w_q = np.broadcast_to(w_q[:, :, None], (RET_HEADS, CHUNK, RET_HD))
    w_k = np.broadcast_to(w_k[:, :, None], (RET_HEADS, CHUNK, RET_HD))
    g_c = np.exp(CHUNK * _LOG_G)
    pair = lambda t: np.ascontiguousarray(np.concatenate([t[0::2], t[1::2]], axis=2), np.float32)
    return pair(decay), pair(w_q), pair(w_k), [float(v) for v in g_c]


def _swa_bias():
    i = np.arange(WINDOW)[:, None]
    j = np.arange(2 * WINDOW)[None, :]
    dist = i + WINDOW - j
    valid = (dist >= 0) & (dist <= WINDOW)
    out = np.zeros((SWA_KV_HEADS, SWA_REP * WINDOW, 2 * WINDOW), np.float32)
    for g in range(SWA_KV_HEADS):
        for r in range(SWA_REP):
            b = np.where(valid, -_SLOPES[g * SWA_REP + r] * dist, NEG)
            out[g, r * WINDOW:(r + 1) * WINDOW] = b
    return out


def _rmsnorm(x, g):
    ms = jnp.mean(x * x, axis=-1, keepdims=True)
    return (x * lax.rsqrt(ms + EPS)) * g


def _lane_block64(lane):
    return lax.shift_right_logical(lane, 6)


def _cparams(n_axes):
    return pltpu.CompilerParams(dimension_semantics=("arbitrary",) * n_axes, vmem_limit_bytes=VMEM_LIMIT)


def _layer_block(tail, layer, n_grid):
    zeros = (0,) * len(tail)
    if n_grid == 1:
        return pl.BlockSpec((None,) + tuple(tail), lambda i: (layer,) + zeros)
    return pl.BlockSpec((None,) + tuple(tail), lambda i, j: (layer,) + zeros)


def _norm_matmul_kernel(x_ref, g_ref, w_ref, o_ref, *, n_chunk):
    xn = _rmsnorm(x_ref[...], g_ref[...]).astype(BF16)
    n = w_ref.shape[1]
    for n0 in range(0, n, n_chunk):
        n1 = min(n0 + n_chunk, n)
        o_ref[:, n0:n1] = jnp.dot(xn, w_ref[:, n0:n1].astype(BF16), preferred_element_type=F32).astype(o_ref.dtype)


def norm_matmul(x, g, w, layer, out_dtype, tm, n_chunk=512):
    m, k = x.shape
    n = w.shape[2]
    return pl.pallas_call(
        functools.partial(_norm_matmul_kernel, n_chunk=n_chunk),
        grid=(m // tm,),
        in_specs=[pl.BlockSpec((tm, k), lambda i: (i, 0)),
                  _layer_block((1, k), layer, 1),
                  _layer_block((k, n), layer, 1)],
        out_specs=pl.BlockSpec((tm, n), lambda i: (i, 0)),
        out_shape=jax.ShapeDtypeStruct((m, n), out_dtype),
        compiler_params=_cparams(1),
        name="norm_matmul",
    )(x, g.reshape(g.shape[0], 1, k), w)


def _mem_kv_kernel(x_ref, g_ref, w_ref, o_ref):
    xn = _rmsnorm(x_ref[...], g_ref[...]).astype(BF16)
    o_ref[...] = jnp.dot(xn, w_ref[...].astype(BF16), preferred_element_type=F32)


def mem_kv_proj(x, g, w):
    m, k = x.shape
    n_layers, _, n = w.shape
    return pl.pallas_call(
        _mem_kv_kernel,
        grid=(n_layers,),
        in_specs=[pl.BlockSpec((m, k), lambda l: (0, 0)),
                  pl.BlockSpec((None, 1, k), lambda l: (l, 0, 0)),
                  pl.BlockSpec((None, k, n), lambda l: (l, 0, 0))],
        out_specs=pl.BlockSpec((None, m, n), lambda l: (l, 0, 0)),
        out_shape=jax.ShapeDtypeStruct((n_layers, m, n), F32),
        compiler_params=_cparams(1),
        name="mem_kv_proj",
    )(x, g.reshape(n_layers, 1, k), w)


def _matmul_residual_kernel(a_ref, w_ref, x_ref, o_ref):
    o_ref[...] = x_ref[...] + jnp.dot(a_ref[...], w_ref[...].astype(BF16), preferred_element_type=F32)


def matmul_residual(a, w, layer, x, tm):
    m, k = a.shape
    n = w.shape[2]
    return pl.pallas_call(
        _matmul_residual_kernel,
        grid=(m // tm,),
        in_specs=[pl.BlockSpec((tm, k), lambda i: (i, 0)),
                  _layer_block((k, n), layer, 1),
                  pl.BlockSpec((tm, n), lambda i: (i, 0))],
        out_specs=pl.BlockSpec((tm, n), lambda i: (i, 0)),
        out_shape=jax.ShapeDtypeStruct((m, n), F32),
        compiler_params=_cparams(1),
        name="matmul_residual",
    )(a, w, x)


def _mlp_kernel(x_ref, g_ref, wu_ref, wd_ref, gf_ref, *rest, fc, final_norm, with_kv):
    if with_kv:
        gkv_ref, wkv_ref, o_ref, kv_ref, kv16_ref, act_ref = rest
    else:
        o_ref, act_ref = rest
    xn = _rmsnorm(x_ref[...], g_ref[...]).astype(BF16)
    for c0 in range(0, D_FF, fc):
        h = jnp.dot(xn, wu_ref[:, c0:c0 + fc].astype(BF16), preferred_element_type=F32)
        act_ref[:, c0:c0 + fc] = jnp.square(jnp.maximum(h, 0.0)).astype(BF16)
    y = x_ref[...] + jnp.dot(act_ref[...], wd_ref[...].astype(BF16), preferred_element_type=F32)
    if with_kv:
        kv = jnp.dot(_rmsnorm(y, gkv_ref[...]).astype(BF16), wkv_ref[...], preferred_element_type=F32)
        kv_ref[...] = kv
        kv16_ref[...] = kv.astype(BF16)
    if final_norm:
        y = _rmsnorm(y, gf_ref[...])
    o_ref[...] = y


def mlp_block(x, g, w_up, w_down, layer, g_final, final_norm, tm, kv_proj=None, fc=512):
    m, d = x.shape
    with_kv = kv_proj is not None
    in_specs = [pl.BlockSpec((tm, d), lambda i: (i, 0)),
                _layer_block((1, d), layer, 1),
                pl.BlockSpec((None, d, D_FF), lambda i: (layer, 0, 0), pipeline_mode=pl.Buffered(1)),
                pl.BlockSpec((None, D_FF, d), lambda i: (layer, 0, 0), pipeline_mode=pl.Buffered(1)),
                pl.BlockSpec((1, d), lambda i: (0, 0))]
    args = [x, g.reshape(g.shape[0], 1, d), w_up, w_down, g_final.reshape(1, d)]
    out_specs = [pl.BlockSpec((tm, d), lambda i: (i, 0))]
    out_shape = [jax.ShapeDtypeStruct((m, d), F32)]
    if with_kv:
        in_specs += [pl.BlockSpec((1, d), lambda i: (0, 0)),
                     pl.BlockSpec((d, 2 * KV_W), lambda i: (0, 0), pipeline_mode=pl.Buffered(1))]
        args += list(kv_proj)
        out_specs += [pl.BlockSpec((tm, 2 * KV_W), lambda i: (i, 0))] * 2
        out_shape += [jax.ShapeDtypeStruct((m, 2 * KV_W), F32), jax.ShapeDtypeStruct((m, 2 * KV_W), BF16)]
    res = pl.pallas_call(
        functools.partial(_mlp_kernel, fc=fc, final_norm=final_norm, with_kv=with_kv),
        grid=(m // tm,),
        in_specs=in_specs,
        out_specs=out_specs,
        out_shape=out_shape,
        scratch_shapes=[pltpu.VMEM((tm, D_FF), BF16)],
        compiler_params=_cparams(1),
        name="mlp_block",
    )(*args)
    return res if with_kv else res[0]


def _mlp_decode_kernel(x_ref, g_ref, wu_ref, wd_ref, gf_ref, o_ref, xn_scr, acc_scr, *, final_norm):
    k = pl.program_id(0)

    @pl.when(k == 0)
    def _():
        xn_scr[...] = _rmsnorm(x_ref[...], g_ref[...]).astype(BF16)
        acc_scr[...] = jnp.zeros_like(acc_scr)

    h = jnp.dot(xn_scr[...], wu_ref[...].astype(BF16), preferred_element_type=F32)
    act = jnp.square(jnp.maximum(h, 0.0)).astype(BF16)
    acc_scr[...] += jnp.dot(act, wd_ref[...].astype(BF16), preferred_element_type=F32)

    @pl.when(k == pl.num_programs(0) - 1)
    def _():
        y = x_ref[...] + acc_scr[...]
        if final_norm:
            y = _rmsnorm(y, gf_ref[...])
        o_ref[...] = y


def mlp_decode(x, g, w_up, w_down, layer, g_final, final_norm, fc=512):
    m, d = x.shape
    return pl.pallas_call(
        functools.partial(_mlp_decode_kernel, final_norm=final_norm),
        grid=(D_FF // fc,),
        in_specs=[pl.BlockSpec((m, d), lambda k: (0, 0)),
                  _layer_block((1, d), layer, 1),
                  pl.BlockSpec((None, d, fc), lambda k: (layer, 0, k)),
                  pl.BlockSpec((None, fc, d), lambda k: (layer, k, 0)),
                  pl.BlockSpec((1, d), lambda k: (0, 0))],
        out_specs=pl.BlockSpec((m, d), lambda k: (0, 0)),
        out_shape=jax.ShapeDtypeStruct((m, d), F32),
        scratch_shapes=[pltpu.VMEM((m, d), BF16), pltpu.VMEM((m, d), F32)],
        compiler_params=_cparams(1),
        name="mlp_decode",
    )(x, g.reshape(g.shape[0], 1, d), w_up, w_down, g_final.reshape(1, d))


def _mem_attention_tile(qm, mk, mv, fillers=()):
    t = qm.shape[0]
    lane_head = _lane_block64(lax.broadcasted_iota(jnp.int32, (t, MEM_W), 1))
    tok_head = _lane_block64(lax.broadcasted_iota(jnp.int32, (MEM_TOKENS, MEM_W), 1))
    probs = []
    vals = []
    for h in range(MEM_HEADS):
        qh = jnp.where(lane_head == h, qm, jnp.zeros_like(qm))
        s = lax.dot_general(qh, mk, NT_DIMS, preferred_element_type=F32)
        if h < len(fillers):
            fillers[h]()
        m = jnp.max(s, axis=-1, keepdims=True)
        e = jnp.exp(s - m)
        inv = 1.0 / jnp.sum(e, axis=-1, keepdims=True)
        probs.append((e * inv).astype(BF16))
        vals.append(jnp.where(tok_head == h, mv, jnp.zeros_like(mv)))
    return jnp.dot(jnp.concatenate(probs, axis=1), jnp.concatenate(vals, axis=0), preferred_element_type=F32)


def _project_tile(x_ref, g_ref, w_ref, p_ref, n_chunk):
    xn = _rmsnorm(x_ref[...], g_ref[...]).astype(BF16)
    n = w_ref.shape[1]
    for n0 in range(0, n, n_chunk):
        n1 = min(n0 + n_chunk, n)
        p_ref[:, n0:n1] = jnp.dot(xn, w_ref[:, n0:n1].astype(BF16), preferred_element_type=F32).astype(BF16)


PIECE = 256


def _projection_pieces(x_ref, g_ref, w_ref, p_ref):
    xn = _rmsnorm(x_ref[...], g_ref[...]).astype(BF16)

    def piece(c):
        def run():
            cols = slice(c * PIECE, (c + 1) * PIECE)
            p_ref[:, cols] = jnp.dot(xn, w_ref[:, cols].astype(BF16), preferred_element_type=F32).astype(BF16)
        return run

    return [piece(c) for c in range(w_ref.shape[1] // PIECE)]


def _block_diag2(a, b):
    z = jnp.zeros_like(a)
    return jnp.concatenate([jnp.concatenate([a, z], axis=1), jnp.concatenate([z, b], axis=1)], axis=0)


def _group_norm_gate(o, gate):
    mu = jnp.mean(o, axis=-1, keepdims=True)
    oc = o - mu
    var = jnp.mean(oc * oc, axis=-1, keepdims=True)
    return (gate * jax.nn.sigmoid(gate) * (oc * lax.rsqrt(var + EPS))).astype(BF16)


def _retention_tile(p_ref, a_ref, s_scr, decay_ref, wq_ref, wk_ref, g_c, fillers=()):
    pw = 2 * RET_HD
    pairs = range(RET_HEADS // 2)
    for c in range(TILE // CHUNK):
        rows = slice(c * CHUNK, (c + 1) * CHUNK)
        q2 = [p_ref[rows, hp * pw:(hp + 1) * pw] for hp in pairs]
        k2 = [p_ref[rows, MIX_MAIN + hp * pw:MIX_MAIN + (hp + 1) * pw] for hp in pairs]
        v2 = [p_ref[rows, 2 * MIX_MAIN + hp * pw:2 * MIX_MAIN + (hp + 1) * pw] for hp in pairs]
        sc = [lax.dot_general(q2[hp], _block_diag2(k2[hp][:, :RET_HD], k2[hp][:, RET_HD:]), NT_DIMS,
                              preferred_element_type=F32) for hp in pairs]
        kw = [(k2[hp].astype(F32) * wk_ref[hp]).astype(BF16) for hp in pairs]
        kv = [lax.dot_general(kw[hp], v2[hp], TN_DIMS, preferred_element_type=F32) for hp in pairs]
        if 2 * c < len(fillers):
            fillers[2 * c]()
        s_old = [s_scr[h] for h in range(RET_HEADS)]
        lhs = [jnp.concatenate([(sc[hp] * decay_ref[hp]).astype(BF16),
                                (q2[hp].astype(F32) * wq_ref[hp]).astype(BF16)], axis=1) for hp in pairs]
        rhs = [jnp.concatenate([_block_diag2(v2[hp][:, :RET_HD], v2[hp][:, RET_HD:]),
                                _block_diag2(s_old[2 * hp].astype(BF16), s_old[2 * hp + 1].astype(BF16))], axis=0)
               for hp in pairs]
        o2 = [jnp.dot(lhs[hp], rhs[hp], preferred_element_type=F32) for hp in pairs]
        if 2 * c + 1 < len(fillers):
            fillers[2 * c + 1]()
        for hp in pairs:
            s_scr[2 * hp] = g_c[2 * hp] * s_old[2 * hp] + kv[hp][:RET_HD, :RET_HD]
            s_scr[2 * hp + 1] = g_c[2 * hp + 1] * s_old[2 * hp + 1] + kv[hp][RET_HD:, RET_HD:]
        for hp in pairs:
            lo = hp * pw
            gate2 = p_ref[rows, 3 * MIX_MAIN + lo:3 * MIX_MAIN + lo + pw].astype(F32)
            a_ref[rows, lo:lo + RET_HD] = _group_norm_gate(o2[hp][:, :RET_HD], gate2[:, :RET_HD])
            a_ref[rows, lo + RET_HD:lo + pw] = _group_norm_gate(o2[hp][:, RET_HD:], gate2[:, RET_HD:])


def _fused_a_kernel(xc_ref, xn_ref, g_ref, win_ref, wout_ref, mk_ref, mv_ref, decay_ref, wq_ref, wk_ref,
                    o_ref, st_ref, p0, p1, a_scr, s_scr, *, g_c, tiles_per_seq):
    t = pl.program_id(0)

    @pl.when(t == 0)
    def _():
        _project_tile(xc_ref, g_ref, win_ref, p0, MIX_MAIN)

    @pl.when(t % tiles_per_seq == 0)
    def _():
        s_scr[...] = jnp.zeros_like(s_scr)

    def step(p_cur, p_next):
        pieces = _projection_pieces(xn_ref, g_ref, win_ref, p_next)
        pieces[8]()
        _retention_tile(p_cur, a_scr, s_scr, decay_ref, wq_ref, wk_ref, g_c, pieces[:8])
        st_ref[...] = s_scr[...]
        qm = p_cur[:, 4 * MIX_MAIN:4 * MIX_MAIN + MEM_W] * ATTN_SCALE
        a_scr[:, MIX_MAIN:] = _mem_attention_tile(qm, mk_ref[...], mv_ref[...], pieces[9:]).astype(BF16)
        o_ref[...] = xc_ref[...] + jnp.dot(a_scr[...], wout_ref[...].astype(BF16), preferred_element_type=F32)

    @pl.when(t % 2 == 0)
    def _():
        step(p0, p1)

    @pl.when(t % 2 == 1)
    def _():
        step(p1, p0)


def fused_mixer_a(x, g, w_in, w_out, mem_k, mem_v, layer, batch, seq):
    n_tiles = batch * seq // TILE
    tps = seq // TILE
    decay, w_q, w_k, g_c = _retention_consts()
    pw = w_in.shape[2]
    c3 = lambda t: (0, 0, 0)
    mem_spec = pl.BlockSpec((None, None, MEM_TOKENS, MEM_W), lambda t: (layer, t // tps, 0, 0))
    return pl.pallas_call(
        functools.partial(_fused_a_kernel, g_c=g_c, tiles_per_seq=tps),
        grid=(n_tiles,),
        in_specs=[pl.BlockSpec((TILE, D_MODEL), lambda t: (t, 0)),
                  pl.BlockSpec((TILE, D_MODEL), lambda t: (jnp.minimum(t + 1, n_tiles - 1), 0)),
                  _layer_block((1, D_MODEL), layer, 1),
                  pl.BlockSpec((None, D_MODEL, pw), lambda t: (layer, 0, 0), pipeline_mode=pl.Buffered(1)),
                  pl.BlockSpec((None, D_MODEL, D_MODEL), lambda t: (layer, 0, 0), pipeline_mode=pl.Buffered(1)),
                  mem_spec, mem_spec,
                  pl.BlockSpec((RET_HEADS // 2, CHUNK, 2 * CHUNK), c3),
                  pl.BlockSpec((RET_HEADS // 2, CHUNK, 2 * RET_HD), c3),
                  pl.BlockSpec((RET_HEADS // 2, CHUNK, 2 * RET_HD), c3)],
        out_specs=[pl.BlockSpec((TILE, D_MODEL), lambda t: (t, 0)),
                   pl.BlockSpec((None, RET_HEADS, RET_HD, RET_HD), lambda t: (t // tps, 0, 0, 0))],
        out_shape=[jax.ShapeDtypeStruct((batch * seq, D_MODEL), F32),
                   jax.ShapeDtypeStruct((batch, RET_HEADS, RET_HD, RET_HD), F32)],
        scratch_shapes=[pltpu.VMEM((TILE, pw), BF16), pltpu.VMEM((TILE, pw), BF16),
                        pltpu.VMEM((TILE, D_MODEL), BF16),
                        pltpu.VMEM((RET_HEADS, RET_HD, RET_HD), F32)],
        compiler_params=_cparams(1),
        name="fused_mixer_a",
    )(x, x, g.reshape(g.shape[0], 1, D_MODEL), w_in, w_out, mem_k, mem_v,
      jnp.asarray(decay), jnp.asarray(w_q), jnp.asarray(w_k))


def _swa_tile(p_ref, kvp_ref, kvc_ref, a_ref, bias_ref, sink_ref, is_first, fillers=()):
    lane_group = _lane_block64(lax.broadcasted_iota(jnp.int32, (WINDOW, KV_W), 1))
    key_group = _lane_block64(lax.broadcasted_iota(jnp.int32, (2 * WINDOW, KV_W), 1))
    for i in range(TILE // WINDOW):
        r0 = i * WINDOW
        if i == 0:
            kv2 = jnp.concatenate([kvp_ref[...], kvc_ref[0:WINDOW, :]], axis=0)
        else:
            kv2 = kvc_ref[r0 - WINDOW:r0 + WINDOW, :]
        k2 = kv2[:, :KV_W]
        v2 = kv2[:, KV_W:]
        qs = [p_ref[r0:r0 + WINDOW, r * KV_W:(r + 1) * KV_W] for r in range(SWA_REP)]
        probs = []
        vals = []
        for g in range(SWA_KV_HEADS):
            sel = lane_group == g
            qg = jnp.concatenate([jnp.where(sel, qr, jnp.zeros_like(qr)) for qr in qs], axis=0)
            s = lax.dot_general(qg, k2, NT_DIMS, preferred_element_type=F32) + bias_ref[g]
            if g == 1 and i < len(fillers):
                fillers[i]()
            s_prev = s[:, :WINDOW]
            s_cur = s[:, WINDOW:]
            if i == 0:
                s_prev = jnp.where(is_first, NEG, s_prev)
            sink = sink_ref[g]
            m = jnp.maximum(jnp.max(jnp.maximum(s_prev, s_cur), axis=-1, keepdims=True), sink)
            e_prev = jnp.exp(s_prev - m)
            e_cur = jnp.exp(s_cur - m)
            inv = 1.0 / (jnp.sum(e_prev + e_cur, axis=-1, keepdims=True) + jnp.exp(sink - m))
            probs.append((e_prev * inv).astype(BF16))
            probs.append((e_cur * inv).astype(BF16))
            vals.append(jnp.where(key_group == g, v2, jnp.zeros_like(v2)))
        o = jnp.dot(jnp.concatenate(probs, axis=1), jnp.concatenate(vals, axis=0), preferred_element_type=F32)
        for r in range(SWA_REP):
            a_ref[r0:r0 + WINDOW, r * KV_W:(r + 1) * KV_W] = o[r * WINDOW:(r + 1) * WINDOW].astype(BF16)


def _fused_b_kernel(xc_ref, xn_ref, g_ref, win_ref, wout_ref, kvp_ref, kvc_ref, mk_ref, mv_ref, bias_ref, sink_ref,
                    o_ref, p0, p1, a_scr, *, tiles_per_seq):
    t = pl.program_id(0)
    is_first = t % tiles_per_seq == 0

    @pl.when(t == 0)
    def _():
        _project_tile(xc_ref, g_ref, win_ref, p0, 512)

    def out_piece(c):
        def run():
            cols = slice(c * PIECE, (c + 1) * PIECE)
            o_ref[:, cols] = xc_ref[:, cols] + jnp.dot(a_scr[:, :MIX_MAIN], wout_ref[:MIX_MAIN, cols],
                                                       preferred_element_type=F32)
        return run

    def step(p_cur, p_next):
        pieces = _projection_pieces(xn_ref, g_ref, win_ref, p_next)
        _swa_tile(p_cur, kvp_ref, kvc_ref, a_scr, bias_ref, sink_ref, is_first, pieces[:2])
        qm = p_cur[:, MIX_MAIN:MIX_MAIN + MEM_W]
        a_mem = _mem_attention_tile(qm, mk_ref[...], mv_ref[...],
                                    [pieces[2], pieces[3], out_piece(0), out_piece(1)]).astype(BF16)
        out_piece(2)()
        out_piece(3)()
        o_ref[...] += jnp.dot(a_mem, wout_ref[MIX_MAIN:, :], preferred_element_type=F32)

    @pl.when(t % 2 == 0)
    def _():
        step(p0, p1)

    @pl.when(t % 2 == 1)
    def _():
        step(p1, p0)


def fused_mixer_b(x, g, w_in, w_out, kv, mem_k, mem_v, layer, sink_rows, sub_layer, batch, seq):
    n_tiles = batch * seq // TILE
    tps = seq // TILE
    nw = TILE // WINDOW
    bias = jnp.asarray(_swa_bias())
    mem_spec = pl.BlockSpec((None, None, MEM_TOKENS, MEM_W), lambda t: (layer, t // tps, 0, 0))

    def prev_map(t):
        return (jnp.maximum(t * nw - 1, (t // tps) * (seq // WINDOW)), 0)

    return pl.pallas_call(
        functools.partial(_fused_b_kernel, tiles_per_seq=tps),
        grid=(n_tiles,),
        in_specs=[pl.BlockSpec((TILE, D_MODEL), lambda t: (t, 0)),
                  pl.BlockSpec((TILE, D_MODEL), lambda t: (jnp.minimum(t + 1, n_tiles - 1), 0)),
                  _layer_block((1, D_MODEL), layer, 1),
                  pl.BlockSpec((None, D_MODEL, D_MODEL), lambda t: (sub_layer, 0, 0), pipeline_mode=pl.Buffered(1)),
                  pl.BlockSpec((None, D_MODEL, D_MODEL), lambda t: (sub_layer, 0, 0), pipeline_mode=pl.Buffered(1)),
                  pl.BlockSpec((WINDOW, 2 * KV_W), prev_map),
                  pl.BlockSpec((TILE, 2 * KV_W), lambda t: (t, 0)),
                  mem_spec, mem_spec,
                  pl.BlockSpec((SWA_KV_HEADS, SWA_REP * WINDOW, 2 * WINDOW), lambda t: (0, 0, 0)),
                  pl.BlockSpec((None, SWA_KV_HEADS, SWA_REP * WINDOW, LANES), lambda t: (sub_layer, 0, 0, 0))],
        out_specs=pl.BlockSpec((TILE, D_MODEL), lambda t: (t, 0)),
        out_shape=jax.ShapeDtypeStruct((batch * seq, D_MODEL), F32),
        scratch_shapes=[pltpu.VMEM((TILE, D_MODEL), BF16), pltpu.VMEM((TILE, D_MODEL), BF16),
                        pltpu.VMEM((TILE, D_MODEL), BF16)],
        compiler_params=_cparams(1),
        name="fused_mixer_b",
    )(x, x, g.reshape(g.shape[0], 1, D_MODEL), w_in, w_out, kv, kv, mem_k, mem_v, bias, sink_rows)


def _ret_decode_kernel(proj_ref, st_ref, *rest, bb):
    o_ref, nst_ref = rest[-2:]
    scale = RET_HD ** -0.5
    row = lax.broadcasted_iota(jnp.int32, (bb, bb * RET_HD), 0)
    lane_seq = lax.shift_right_logical(lax.broadcasted_iota(jnp.int32, (bb, bb * RET_HD), 1), 7)
    own = row == lane_seq
    for h in range(RET_HEADS):
        lo = h * RET_HD
        q8 = proj_ref[:, lo:lo + RET_HD]
        k8 = proj_ref[:, MIX_MAIN + lo:MIX_MAIN + lo + RET_HD] * scale
        v8 = proj_ref[:, 2 * MIX_MAIN + lo:2 * MIX_MAIN + lo + RET_HD]
        gate = proj_ref[:, 3 * MIX_MAIN + lo:3 * MIX_MAIN + lo + RET_HD]
        g = float(np.exp(_LOG_G[h]))
        vexp = jnp.where(own, jnp.concatenate([v8] * bb, axis=1), 0.0).astype(BF16)
        outer = lax.dot_general(k8.astype(BF16), vexp, TN_DIMS, preferred_element_type=F32)
        cols = []
        for j in range(bb):
            s_new = g * st_ref[j, h] + outer[:, j * RET_HD:(j + 1) * RET_HD]
            nst_ref[j, h] = s_new
            cols.append(s_new.astype(BF16))
        o_all = jnp.dot(q8.astype(BF16), jnp.concatenate(cols, axis=1), preferred_element_type=F32)
        o_all = jnp.where(own, o_all, 0.0)
        o = o_all[:, :RET_HD]
        for j in range(1, bb):
            o = o + o_all[:, j * RET_HD:(j + 1) * RET_HD]
        mu = jnp.mean(o, axis=-1, keepdims=True)
        oc = o - mu
        var = jnp.mean(oc * oc, axis=-1, keepdims=True)
        on = oc * lax.rsqrt(var + EPS)
        o_ref[:, lo:lo + RET_HD] = gate * jax.nn.sigmoid(gate) * on


def ret_decode(proj, state, layer, stacked_out=None, bb=16):
    nb = proj.shape[0]
    pw = proj.shape[1]
    in_specs = [pl.BlockSpec((bb, pw), lambda i: (i, 0)),
                pl.BlockSpec((None, bb, RET_HEADS, RET_HD, RET_HD), lambda i: (layer, i, 0, 0, 0))]
    args = [proj, state]
    aliases = {}
    if stacked_out is not None:
        in_specs.append(pl.BlockSpec(memory_space=pl.ANY))
        args.append(stacked_out)
        aliases = {2: 1}
    return pl.pallas_call(
        functools.partial(_ret_decode_kernel, bb=bb),
        grid=(nb // bb,),
        in_specs=in_specs,
        out_specs=[pl.BlockSpec((bb, MIX_MAIN), lambda i: (i, 0)),
                   pl.BlockSpec((None, bb, RET_HEADS, RET_HD, RET_HD), lambda i: (layer, i, 0, 0, 0))],
        out_shape=[jax.ShapeDtypeStruct((nb, MIX_MAIN), F32),
                   jax.ShapeDtypeStruct(state.shape, F32)],
        input_output_aliases=aliases,
        compiler_params=_cparams(1),
        name="ret_decode",
    )(*args)


def _mem_decode_kernel(q_ref, kt_ref, vt_ref, o_ref, *, bb, q_off, q_scale):
    row = lax.broadcasted_iota(jnp.int32, (8, MEM_W), 0)
    lane_head = _lane_block64(lax.broadcasted_iota(jnp.int32, (8, MEM_W), 1))
    sel = row == lane_head
    seqs = range(bb)
    qexp = [jnp.where(sel, jnp.broadcast_to(q_ref[j:j + 1, q_off:q_off + MEM_W] * q_scale, (8, MEM_W)), 0.0)
            for j in seqs]
    s = [jnp.dot(qexp[j], kt_ref[j], preferred_element_type=F32) for j in seqs]
    e = [jnp.exp(s[j] - jnp.max(s[j], axis=-1, keepdims=True)) for j in seqs]
    den = [jnp.sum(e[j], axis=-1, keepdims=True) for j in seqs]
    o = [lax.dot_general(e[j], vt_ref[j], NT_DIMS, preferred_element_type=F32) / den[j] for j in seqs]
    o_ref[...] = jnp.concatenate([jnp.sum(jnp.where(sel, o[j], 0.0), axis=0, keepdims=True) for j in seqs], axis=0)


def mem_decode(proj, cache_kt, cache_vt, layer, q_off, q_scale, bb=16):
    nb = proj.shape[0]
    pw = proj.shape[1]
    cmap = lambda i: (layer, i, 0, 0)
    return pl.pallas_call(
        functools.partial(_mem_decode_kernel, bb=bb, q_off=q_off, q_scale=q_scale),
        grid=(nb // bb,),
        in_specs=[pl.BlockSpec((bb, pw), lambda i: (i, 0)),
                  pl.BlockSpec((None, bb, MEM_W, MEM_TOKENS), cmap),
                  pl.BlockSpec((None, bb, MEM_W, MEM_TOKENS), cmap)],
        out_specs=pl.BlockSpec((bb, MEM_W), lambda i: (i, 0)),
        out_shape=jax.ShapeDtypeStruct((nb, MEM_W), F32),
        compiler_params=_cparams(1),
        name="mem_decode",
    )(proj, cache_kt, cache_vt)


def _swa_decode_kernel(q_ref, kn_ref, vn_ref, ckt_ref, cvt_ref, slope_ref, sink_ref, o_ref, *cache_out, bb):
    row = lax.broadcasted_iota(jnp.int32, (DEC_ROWS, KV_W), 0)
    lane_group = _lane_block64(lax.broadcasted_iota(jnp.int32, (DEC_ROWS, KV_W), 1))
    dist = (WINDOW - lax.broadcasted_iota(jnp.int32, (DEC_ROWS, WINDOW), 1)).astype(F32)
    bias = -slope_ref[...] * dist
    sink = sink_ref[...]
    own = [row == SWA_KV_HEADS * r + lane_group for r in range(SWA_REP)]
    seqs = range(bb)

    def expand(j):
        qexp = jnp.zeros((DEC_ROWS, KV_W), F32)
        for r in range(SWA_REP):
            qr = jnp.broadcast_to(q_ref[j:j + 1, r * KV_W:(r + 1) * KV_W], (DEC_ROWS, KV_W))
            qexp = jnp.where(own[r], qr, qexp)
        return qexp

    qexp = [expand(j) for j in seqs]
    s = [jnp.dot(qexp[j], ckt_ref[j], preferred_element_type=F32) + bias for j in seqs]
    s_new = [jnp.sum(qexp[j] * kn_ref[j:j + 1, :], axis=-1, keepdims=True) for j in seqs]
    m = [jnp.maximum(jnp.maximum(jnp.max(s[j], axis=-1, keepdims=True), s_new[j]), sink) for j in seqs]
    e = [jnp.exp(s[j] - m[j]) for j in seqs]
    e_new = [jnp.exp(s_new[j] - m[j]) for j in seqs]
    inv = [1.0 / (jnp.sum(e[j], axis=-1, keepdims=True) + e_new[j] + jnp.exp(sink - m[j])) for j in seqs]
    o = [lax.dot_general(e[j], cvt_ref[j], NT_DIMS, preferred_element_type=F32) for j in seqs]
    o = [(o[j] + jnp.concatenate([e_new[j], e_new[j]], axis=1) * vn_ref[j:j + 1, :])
         * jnp.concatenate([inv[j], inv[j]], axis=1) for j in seqs]
    for r in range(SWA_REP):
        o_ref[:, r * KV_W:(r + 1) * KV_W] = jnp.concatenate(
            [jnp.sum(jnp.where(own[r], o[j], 0.0), axis=0, keepdims=True) for j in seqs], axis=0)
    if cache_out:
        nkt_ref, nvt_ref = cache_out
        last = lax.broadcasted_iota(jnp.int32, (KV_W, WINDOW), 1) == WINDOW - 1
        knt = kn_ref[...].T
        vnt = vn_ref[...].T
        for j in seqs:
            nkt_ref[j] = jnp.where(last, jnp.broadcast_to(knt[:, j:j + 1], (KV_W, WINDOW)),
                                   pltpu.roll(ckt_ref[j], WINDOW - 1, 1))
            nvt_ref[j] = jnp.where(last, jnp.broadcast_to(vnt[:, j:j + 1], (KV_W, WINDOW)),
                                   pltpu.roll(cvt_ref[j], WINDOW - 1, 1))


def swa_decode(qproj, k_new, v_new, cache_kt, cache_vt, slope_rows, sink_rows, sink_layer, emit_cache, bb=8):
    nb = qproj.shape[0]
    pw = qproj.shape[1]
    cache_spec = pl.BlockSpec((bb, KV_W, WINDOW), lambda i: (i, 0, 0))
    cache_shape = jax.ShapeDtypeStruct((nb, KV_W, WINDOW), F32)
    n_cache = 2 if emit_cache else 0
    return pl.pallas_call(
        functools.partial(_swa_decode_kernel, bb=bb),
        grid=(nb // bb,),
        in_specs=[pl.BlockSpec((bb, pw), lambda i: (i, 0)),
                  pl.BlockSpec((bb, KV_W), lambda i: (i, 0)),
                  pl.BlockSpec((bb, KV_W), lambda i: (i, 0)),
                  cache_spec, cache_spec,
                  pl.BlockSpec((DEC_ROWS, LANES), lambda i: (0, 0)),
                  pl.BlockSpec((None, DEC_ROWS, LANES), lambda i: (sink_layer, 0, 0))],
        out_specs=[pl.BlockSpec((bb, MIX_MAIN), lambda i: (i, 0))] + [cache_spec] * n_cache,
        out_shape=[jax.ShapeDtypeStruct((nb, MIX_MAIN), F32)] + [cache_shape] * n_cache,
        compiler_params=_cparams(1),
        name="swa_decode",
    )(qproj, k_new, v_new, cache_kt, cache_vt, slope_rows, sink_rows)


def _permute_q_cols(w):
    lead = w.shape[:-1]
    return w.reshape(*lead, SWA_KV_HEADS, SWA_REP, SWA_HD).swapaxes(-3, -2).reshape(*lead, MIX_MAIN)


def kernel(x_prompt, x_sample, cache_mem_k, cache_mem_v, state_ret, cache_swa_k, cache_swa_v, mem_prompt,
           norm_mix, w_in_a, w_out_a, w_in_b, w_out_b, attn_sinks, norm_mem, w_mem_kv, norm_kv, w_kv,
           norm_mlp, w_up, w_down, norm_final):
    batch, seq, d = x_prompt.shape
    nb = x_sample.shape[0]
    tm = 512

    w_in_b16 = (jnp.concatenate([_permute_q_cols(w_in_b[:, :, :MIX_MAIN]), w_in_b[:, :, MIX_MAIN:]], axis=-1)
                * ATTN_SCALE).astype(BF16)
    w_out_b16 = jnp.concatenate(
        [_permute_q_cols(w_out_b[:, :MIX_MAIN, :].swapaxes(1, 2)).swapaxes(1, 2), w_out_b[:, MIX_MAIN:, :]],
        axis=1).astype(BF16)
    w_kv16 = w_kv.astype(BF16).reshape(1, d, 2 * KV_W)
    norm_kv1 = norm_kv.reshape(1, d)

    sinks_gr = attn_sinks.reshape(N_B, SWA_KV_HEADS, SWA_REP)
    sink_prompt = jnp.broadcast_to(sinks_gr[:, :, :, None, None], (N_B, SWA_KV_HEADS, SWA_REP, WINDOW, LANES)
                                   ).reshape(N_B, SWA_KV_HEADS, SWA_REP * WINDOW, LANES)
    sink_dec = jnp.concatenate([sinks_gr.swapaxes(1, 2).reshape(N_B, SWA_HEADS),
                                jnp.zeros((N_B, DEC_ROWS - SWA_HEADS), F32)], axis=1)
    sink_dec = jnp.broadcast_to(sink_dec[:, :, None], (N_B, DEC_ROWS, LANES))
    slope_dec = np.zeros((DEC_ROWS, LANES), np.float32)
    for r in range(SWA_REP):
        for g in range(SWA_KV_HEADS):
            slope_dec[SWA_KV_HEADS * r + g, :] = _SLOPES[g * SWA_REP + r]
    slope_dec = jnp.asarray(slope_dec)

    memx = mem_prompt.reshape(batch * MEM_TOKENS, d)
    mem_kv = mem_kv_proj(memx, norm_mem, w_mem_kv).reshape(DEPTH, batch, MEM_TOKENS, 2 * MEM_W)
    mem_k_prompt = mem_kv[..., :MEM_W].reshape(DEPTH, batch, MEM_TOKENS, MEM_HEADS, MEM_HD)
    mem_v_prompt = mem_kv[..., MEM_W:].reshape(DEPTH, batch, MEM_TOKENS, MEM_HEADS, MEM_HD)
    mk16 = mem_kv[..., :MEM_W].astype(BF16)
    mv16 = mem_kv[..., MEM_W:].astype(BF16)

    x = x_prompt.reshape(batch * seq, d)
    ret_states = []
    kv_p = kv_p16 = None
    for l in range(DEPTH):
        if l < N_A:
            x, st = fused_mixer_a(x, norm_mix, w_in_a, w_out_a, mk16, mv16, l, batch, seq)
            ret_states.append(st)
        else:
            j = l - N_A
            x = fused_mixer_b(x, norm_mix, w_in_b16, w_out_b16, kv_p16, mk16, mv16, l, sink_prompt, j, batch, seq)
        if l == N_A - 1:
            x, kv_p, kv_p16 = mlp_block(x, norm_mlp, w_up, w_down, l, norm_final, False, tm=tm,
                                        kv_proj=(norm_kv1, w_kv16[0]))
        else:
            x = mlp_block(x, norm_mlp, w_up, w_down, l, norm_final, l == DEPTH - 1, tm=tm)
    y_prompt = x.reshape(batch, seq, d)
    ret_prompt = jnp.stack(ret_states)
    kv_p3 = kv_p.reshape(batch, seq, 2 * KV_W)
    swa_k_prompt = kv_p3[:, -WINDOW:, :KV_W].reshape(batch, WINDOW, SWA_KV_HEADS, SWA_HD)
    swa_v_prompt = kv_p3[:, -WINDOW:, KV_W:].reshape(batch, WINDOW, SWA_KV_HEADS, SWA_HD)

    cmkt = jnp.transpose(cache_mem_k, (0, 1, 3, 4, 2)).reshape(DEPTH, nb, MEM_W, MEM_TOKENS)
    cmvt = jnp.transpose(cache_mem_v, (0, 1, 3, 4, 2)).reshape(DEPTH, nb, MEM_W, MEM_TOKENS)
    cskt = jnp.transpose(cache_swa_k, (0, 2, 3, 1)).reshape(nb, KV_W, WINDOW)
    csvt = jnp.transpose(cache_swa_v, (0, 2, 3, 1)).reshape(nb, KV_W, WINDOW)
    xs = x_sample.reshape(nb, d)
    ret_sample = None
    k_new = v_new = nkt = nvt = None
    for l in range(DEPTH):
        if l < N_A:
            proj = norm_matmul(xs, norm_mix, w_in_a, l, F32, tm=nb, n_chunk=MIX_MAIN)
            o_ret, ret_sample = ret_decode(proj, state_ret, l, stacked_out=ret_sample)
            o_mem = mem_decode(proj, cmkt, cmvt, l, 4 * MIX_MAIN, ATTN_SCALE)
            attn = jnp.concatenate([o_ret, o_mem], axis=-1).astype(BF16)
            xs = matmul_residual(attn, w_out_a, l, xs, tm=nb)
        else:
            j = l - N_A
            if j == 0:
                kv_s = norm_matmul(xs, norm_kv1, w_kv16, 0, F32, tm=nb)
                k_new, v_new = kv_s[:, :KV_W], kv_s[:, KV_W:]
            qproj = norm_matmul(xs, norm_mix[N_A:], w_in_b16, j, F32, tm=nb)
            res = swa_decode(qproj, k_new, v_new, cskt, csvt, slope_dec, sink_dec, j, emit_cache=(j == 0))
            o_swa = res[0]
            if j == 0:
                nkt, nvt = res[1], res[2]
            o_mem = mem_decode(qproj, cmkt, cmvt, l, MIX_MAIN, 1.0)
            attn = jnp.concatenate([o_swa, o_mem], axis=-1).astype(BF16)
            xs = matmul_residual(attn, w_out_b16, j, xs, tm=nb)
        xs = mlp_decode(xs, norm_mlp, w_up, w_down, l, norm_final, l == DEPTH - 1)
    y_sample = xs.reshape(nb, 1, d)
    swa_k_sample = jnp.transpose(nkt.reshape(nb, SWA_KV_HEADS, SWA_HD, WINDOW), (0, 3, 1, 2))
    swa_v_sample = jnp.transpose(nvt.reshape(nb, SWA_KV_HEADS, SWA_HD, WINDOW), (0, 3, 1, 2))

    return (y_prompt, y_sample, ret_prompt, ret_sample, swa_k_prompt, swa_v_prompt, swa_k_sample, swa_v_sample,
            mem_k_prompt, mem_v_prompt)
```

```python
import functools
import math

import jax
import jax.numpy as jnp
import numpy as np
from jax import lax
from jax.experimental import pallas as pl
from jax.experimental.pallas import tpu as pltpu

F32 = jnp.float32
BF16 = jnp.bfloat16

D_MODEL = 1024
DEPTH = 4
N_A = 2
N_B = 2
MEM_TOKENS = 256
MEM_HEADS = 4
MEM_W = 256
MEM_HD = 64
MIX_MAIN = 768
RET_HEADS = 6
RET_HD = 128
CHUNK = 128
SWA_HEADS = 12
SWA_KV_HEADS = 4
SWA_REP = SWA_HEADS // SWA_KV_HEADS
SWA_HD = 64
KV_W = SWA_KV_HEADS * SWA_HD
WINDOW = 128
D_FF = 4096
EPS = 1e-6
NEG = -1e30
ATTN_SCALE = 0.125
LANES = 128
TILE = 512
DEC_ROWS = 16

VMEM_LIMIT = 56 * 1024 * 1024

NT_DIMS = (((1,), (1,)), ((), ()))
TN_DIMS = (((0,), (0,)), ((), ()))


def _alibi_slopes(n):
    def pow2(m):
        start = 2.0 ** (-8.0 / m)
        return [start ** (i + 1) for i in range(m)]

    if math.log2(n).is_integer():
        s = pow2(n)
    else:
        c = 2 ** int(math.floor(math.log2(n)))
        s = pow2(c) + pow2(2 * c)[0::2][: n - c]
    return np.asarray(s, np.float32)


_LOG_G = np.log1p(-(2.0 ** (-5.0 - np.arange(RET_HEADS)))).astype(np.float32).astype(np.float64)
_SLOPES = _alibi_slopes(SWA_HEADS).astype(np.float64)


def _retention_consts():
    idx = np.arange(CHUNK, dtype=np.float64)
    diff = idx[:, None] - idx[None, :]
    scale = RET_HD ** -0.5
    decay = np.where(diff >= 0, np.exp(np.maximum(diff, 0.0)[None] * _LOG_G[:, None, None]), 0.0) * scale
    w_q = np.exp((idx + 1.0)[None, :] * _LOG_G[:, None])
    w_k = np.exp((CHUNK - 1.0 - idx)[None, :] * _LOG_G[:, None]) * scale
    w_q = np.broadcast_to(w_q[:, :, None], (RET_HEADS, CHUNK, RET_HD))
    w_k = np.broadcast_to(w_k[:, :, None], (RET_HEADS, CHUNK, RET_HD))
    g_c = np.exp(CHUNK * _LOG_G)
    pair = lambda t: np.ascontiguousarray(np.concatenate([t[0::2], t[1::2]], axis=2), np.float32)
    return pair(decay), pair(w_q), pair(w_k), [float(v) for v in g_c]


def _swa_bias():
    i = np.arange(WINDOW)[:, None]
    j = np.arange(2 * WINDOW)[None, :]
    dist = i + WINDOW - j
    valid = (dist >= 0) & (dist <= WINDOW)
    out = np.zeros((SWA_KV_HEADS, SWA_REP * WINDOW, 2 * WINDOW), np.float32)
    for g in range(SWA_KV_HEADS):
        for r in range(SWA_REP):
            b = np.where(valid, -_SLOPES[g * SWA_REP + r] * dist, NEG)
            out[g, r * WINDOW:(r + 1) * WINDOW] = b
    return out


def _rmsnorm(x, g):
    ms = jnp.mean(x * x, axis=-1, keepdims=True)
    return (x * lax.rsqrt(ms + EPS)) * g


def _lane_block64(lane):
    return lax.shift_right_logical(lane, 6)


def _cparams(n_axes):
    return pltpu.CompilerParams(dimension_semantics=("arbitrary",) * n_axes, vmem_limit_bytes=VMEM_LIMIT)


def _layer_block(tail, layer, n_grid):
    zeros = (0,) * len(tail)
    if n_grid == 1:
        return pl.BlockSpec((None,) + tuple(tail), lambda i: (layer,) + zeros)
    return pl.BlockSpec((None,) + tuple(tail), lambda i, j: (layer,) + zeros)


def _norm_matmul_kernel(x_ref, g_ref, w_ref, o_ref, *, n_chunk):
    xn = _rmsnorm(x_ref[...], g_ref[...]).astype(BF16)
    n = w_ref.shape[1]
    for n0 in range(0, n, n_chunk):
        n1 = min(n0 + n_chunk, n)
        o_ref[:, n0:n1] = jnp.dot(xn, w_ref[:, n0:n1].astype(BF16), preferred_element_type=F32).astype(o_ref.dtype)


def norm_matmul(x, g, w, layer, out_dtype, tm, n_chunk=512):
    m, k = x.shape
    n = w.shape[2]
    return pl.pallas_call(
        functools.partial(_norm_matmul_kernel, n_chunk=n_chunk),
        grid=(m // tm,),
        in_specs=[pl.BlockSpec((tm, k), lambda i: (i, 0)),
                  _layer_block((1, k), layer, 1),
                  _layer_block((k, n), layer, 1)],
        out_specs=pl.BlockSpec((tm, n), lambda i: (i, 0)),
        out_shape=jax.ShapeDtypeStruct((m, n), out_dtype),
        compiler_params=_cparams(1),
        name="norm_matmul",
    )(x, g.reshape(g.shape[0], 1, k), w)


def _mem_kv_kernel(x_ref, g_ref, w_ref, o_ref):
    xn = _rmsnorm(x_ref[...], g_ref[...]).astype(BF16)
    o_ref[...] = jnp.dot(xn, w_ref[...].astype(BF16), preferred_element_type=F32)


def mem_kv_proj(x, g, w):
    m, k = x.shape
    n_layers, _, n = w.shape
    return pl.pallas_call(
        _mem_kv_kernel,
        grid=(n_layers,),
        in_specs=[pl.BlockSpec((m, k), lambda l: (0, 0)),
                  pl.BlockSpec((None, 1, k), lambda l: (l, 0, 0)),
                  pl.BlockSpec((None, k, n), lambda l: (l, 0, 0))],
        out_specs=pl.BlockSpec((None, m, n), lambda l: (l, 0, 0)),
        out_shape=jax.ShapeDtypeStruct((n_layers, m, n), F32),
        compiler_params=_cparams(1),
        name="mem_kv_proj",
    )(x, g.reshape(n_layers, 1, k), w)


def _matmul_residual_kernel(a_ref, w_ref, x_ref, o_ref):
    o_ref[...] = x_ref[...] + jnp.dot(a_ref[...], w_ref[...].astype(BF16), preferred_element_type=F32)


def matmul_residual(a, w, layer, x, tm):
    m, k = a.shape
    n = w.shape[2]
    return pl.pallas_call(
        _matmul_residual_kernel,
        grid=(m // tm,),
        in_specs=[pl.BlockSpec((tm, k), lambda i: (i, 0)),
                  _layer_block((k, n), layer, 1),
                  pl.BlockSpec((tm, n), lambda i: (i, 0))],
        out_specs=pl.BlockSpec((tm, n), lambda i: (i, 0)),
        out_shape=jax.ShapeDtypeStruct((m, n), F32),
        compiler_params=_cparams(1),
        name="matmul_residual",
    )(a, w, x)


def _mlp_kernel(x_ref, g_ref, wu_ref, wd_ref, gf_ref, *rest, fc, final_norm, with_kv):
    if with_kv:
        gkv_ref, wkv_ref, o_ref, kv_ref, kv16_ref, act_ref = rest
    else:
        o_ref, act_ref = rest
    xn = _rmsnorm(x_ref[...], g_ref[...]).astype(BF16)
    for c0 in range(0, D_FF, fc):
        h = jnp.dot(xn, wu_ref[:, c0:c0 + fc].astype(BF16), preferred_element_type=F32)
        act_ref[:, c0:c0 + fc] = jnp.square(jnp.maximum(h, 0.0)).astype(BF16)
    y = x_ref[...] + jnp.dot(act_ref[...], wd_ref[...].astype(BF16), preferred_element_type=F32)
    if with_kv:
        kv = jnp.dot(_rmsnorm(y, gkv_ref[...]).astype(BF16), wkv_ref[...], preferred_element_type=F32)
        kv_ref[...] = kv
        kv16_ref[...] = kv.astype(BF16)
    if final_norm:
        y = _rmsnorm(y, gf_ref[...])
    o_ref[...] = y


def mlp_block(x, g, w_up, w_down, layer, g_final, final_norm, tm, kv_proj=None, fc=512):
    m, d = x.shape
    with_kv = kv_proj is not None
    in_specs = [pl.BlockSpec((tm, d), lambda i: (i, 0)),
                _layer_block((1, d), layer, 1),
                pl.BlockSpec((None, d, D_FF), lambda i: (layer, 0, 0), pipeline_mode=pl.Buffered(1)),
                pl.BlockSpec((None, D_FF, d), lambda i: (layer, 0, 0), pipeline_mode=pl.Buffered(1)),
                pl.BlockSpec((1, d), lambda i: (0, 0))]
    args = [x, g.reshape(g.shape[0], 1, d), w_up, w_down, g_final.reshape(1, d)]
    out_specs = [pl.BlockSpec((tm, d), lambda i: (i, 0))]
    out_shape = [jax.ShapeDtypeStruct((m, d), F32)]
    if with_kv:
        in_specs += [pl.BlockSpec((1, d), lambda i: (0, 0)),
                     pl.BlockSpec((d, 2 * KV_W), lambda i: (0, 0), pipeline_mode=pl.Buffered(1))]
        args += list(kv_proj)
        out_specs += [pl.BlockSpec((tm, 2 * KV_W), lambda i: (i, 0))] * 2
        out_shape += [jax.ShapeDtypeStruct((m, 2 * KV_W), F32), jax.ShapeDtypeStruct((m, 2 * KV_W), BF16)]
    res = pl.pallas_call(
        functools.partial(_mlp_kernel, fc=fc, final_norm=final_norm, with_kv=with_kv),
        grid=(m // tm,),
        in_specs=in_specs,
        out_specs=out_specs,
        out_shape=out_shape,
        scratch_shapes=[pltpu.VMEM((tm, D_FF), BF16)],
        compiler_params=_cparams(1),
        name="mlp_block",
    )(*args)
    return res if with_kv else res[0]


def _mlp_decode_kernel(x_ref, g_ref, wu_ref, wd_ref, gf_ref, o_ref, xn_scr, acc_scr, *, final_norm):
    k = pl.program_id(0)

    @pl.when(k == 0)
    def _():
        xn_scr[...] = _rmsnorm(x_ref[...], g_ref[...]).astype(BF16)
        acc_scr[...] = jnp.zeros_like(acc_scr)

    h = jnp.dot(xn_scr[...], wu_ref[...].astype(BF16), preferred_element_type=F32)
    act = jnp.square(jnp.maximum(h, 0.0)).astype(BF16)
    acc_scr[...] += jnp.dot(act, wd_ref[...].astype(BF16), preferred_element_type=F32)

    @pl.when(k == pl.num_programs(0) - 1)
    def _():
        y = x_ref[...] + acc_scr[...]
        if final_norm:
            y = _rmsnorm(y, gf_ref[...])
        o_ref[...] = y


def mlp_decode(x, g, w_up, w_down, layer, g_final, final_norm, fc=1024):
    m, d = x.shape
    return pl.pallas_call(
        functools.partial(_mlp_decode_kernel, final_norm=final_norm),
        grid=(D_FF // fc,),
        in_specs=[pl.BlockSpec((m, d), lambda k: (0, 0)),
                  _layer_block((1, d), layer, 1),
                  pl.BlockSpec((None, d, fc), lambda k: (layer, 0, k)),
                  pl.BlockSpec((None, fc, d), lambda k: (layer, k, 0)),
                  pl.BlockSpec((1, d), lambda k: (0, 0))],
        out_specs=pl.BlockSpec((m, d), lambda k: (0, 0)),
        out_shape=jax.ShapeDtypeStruct((m, d), F32),
        scratch_shapes=[pltpu.VMEM((m, d), BF16), pltpu.VMEM((m, d), F32)],
        compiler_params=_cparams(1),
        name="mlp_decode",
    )(x, g.reshape(g.shape[0], 1, d), w_up, w_down, g_final.reshape(1, d))


def _mem_attention_tile(qm, mk, mv, fillers=()):
    t = qm.shape[0]
    lane_head = _lane_block64(lax.broadcasted_iota(jnp.int32, (t, MEM_W), 1))
    tok_head = _lane_block64(lax.broadcasted_iota(jnp.int32, (MEM_TOKENS, MEM_W), 1))
    probs = []
    vals = []
    for h in range(MEM_HEADS):
        qh = jnp.where(lane_head == h, qm, jnp.zeros_like(qm))
        s = lax.dot_general(qh, mk, NT_DIMS, preferred_element_type=F32)
        if h < len(fillers):
            fillers[h]()
        m = jnp.max(s, axis=-1, keepdims=True)
        e = jnp.exp(s - m)
        inv = 1.0 / jnp.sum(e, axis=-1, keepdims=True)
        probs.append((e * inv).astype(BF16))
        vals.append(jnp.where(tok_head == h, mv, jnp.zeros_like(mv)))
    return jnp.dot(jnp.concatenate(probs, axis=1), jnp.concatenate(vals, axis=0), preferred_element_type=F32)


def _project_tile(x_ref, g_ref, w_ref, p_ref, n_chunk):
    xn = _rmsnorm(x_ref[...], g_ref[...]).astype(BF16)
    n = w_ref.shape[1]
    for n0 in range(0, n, n_chunk):
        n1 = min(n0 + n_chunk, n)
        p_ref[:, n0:n1] = jnp.dot(xn, w_ref[:, n0:n1].astype(BF16), preferred_element_type=F32).astype(BF16)


PIECE = 256


def _projection_pieces(x_ref, g_ref, w_ref, p_ref):
    xn = _rmsnorm(x_ref[...], g_ref[...]).astype(BF16)

    def piece(c):
        def run():
            cols = slice(c * PIECE, (c + 1) * PIECE)
            p_ref[:, cols] = jnp.dot(xn, w_ref[:, cols].astype(BF16), preferred_element_type=F32).astype(BF16)
        return run

    return [piece(c) for c in range(w_ref.shape[1] // PIECE)]


def _out_projection_piece(x_ref, a_ref, w_ref, o_ref, c):
    def run():
        cols = slice(c * PIECE, (c + 1) * PIECE)
        o_ref[:, cols] = x_ref[:, cols] + jnp.dot(a_ref[:, :MIX_MAIN], w_ref[:MIX_MAIN, cols].astype(BF16),
                                                  preferred_element_type=F32)
    return run


def _both(f, g):
    def run():
        f()
        g()
    return run


def _block_diag2(a, b):
    z = jnp.zeros_like(a)
    return jnp.concatenate([jnp.concatenate([a, z], axis=1), jnp.concatenate([z, b], axis=1)], axis=0)


def _group_norm_gate(o, gate):
    mu = jnp.mean(o, axis=-1, keepdims=True)
    oc = o - mu
    var = jnp.mean(oc * oc, axis=-1, keepdims=True)
    return (gate * jax.nn.sigmoid(gate) * (oc * lax.rsqrt(var + EPS))).astype(BF16)


def _retention_tile(p_ref, a_ref, s_scr, decay_ref, wq_ref, wk_ref, g_c, fillers=()):
    pw = 2 * RET_HD
    pairs = range(RET_HEADS // 2)
    for c in range(TILE // CHUNK):
        rows = slice(c * CHUNK, (c + 1) * CHUNK)
        q2 = [p_ref[rows, hp * pw:(hp + 1) * pw] for hp in pairs]
        k2 = [p_ref[rows, MIX_MAIN + hp * pw:MIX_MAIN + (hp + 1) * pw] for hp in pairs]
        v2 = [p_ref[rows, 2 * MIX_MAIN + hp * pw:2 * MIX_MAIN + (hp + 1) * pw] for hp in pairs]
        sc = [lax.dot_general(q2[hp], _block_diag2(k2[hp][:, :RET_HD], k2[hp][:, RET_HD:]), NT_DIMS,
                              preferred_element_type=F32) for hp in pairs]
        kw = [(k2[hp].astype(F32) * wk_ref[hp]).astype(BF16) for hp in pairs]
        kv = [lax.dot_general(kw[hp], v2[hp], TN_DIMS, preferred_element_type=F32) for hp in pairs]
        if 2 * c < len(fillers):
            fillers[2 * c]()
        s_old = [s_scr[h] for h in range(RET_HEADS)]
        lhs = [jnp.concatenate([(sc[hp] * decay_ref[hp]).astype(BF16),
                                (q2[hp].astype(F32) * wq_ref[hp]).astype(BF16)], axis=1) for hp in pairs]
        rhs = [jnp.concatenate([_block_diag2(v2[hp][:, :RET_HD], v2[hp][:, RET_HD:]),
                                _block_diag2(s_old[2 * hp].astype(BF16), s_old[2 * hp + 1].astype(BF16))], axis=0)
               for hp in pairs]
        o2 = [jnp.dot(lhs[hp], rhs[hp], preferred_element_type=F32) for hp in pairs]
        if 2 * c + 1 < len(fillers):
            fillers[2 * c + 1]()
        for hp in pairs:
            s_scr[2 * hp] = g_c[2 * hp] * s_old[2 * hp] + kv[hp][:RET_HD, :RET_HD]
            s_scr[2 * hp + 1] = g_c[2 * hp + 1] * s_old[2 * hp + 1] + kv[hp][RET_HD:, RET_HD:]
        for hp in pairs:
            lo = hp * pw
            gate2 = p_ref[rows, 3 * MIX_MAIN + lo:3 * MIX_MAIN + lo + pw].astype(F32)
            a_ref[rows, lo:lo + RET_HD] = _group_norm_gate(o2[hp][:, :RET_HD], gate2[:, :RET_HD])
            a_ref[rows, lo + RET_HD:lo + pw] = _group_norm_gate(o2[hp][:, RET_HD:], gate2[:, RET_HD:])


def _fused_a_kernel(xc_ref, xn_ref, g_ref, win_ref, wout_ref, mk_ref, mv_ref, decay_ref, wq_ref, wk_ref,
                    o_ref, st_ref, p0, p1, a_scr, s_scr, *, g_c, tiles_per_seq):
    t = pl.program_id(0)

    @pl.when(t == 0)
    def _():
        _project_tile(xc_ref, g_ref, win_ref, p0, MIX_MAIN)

    @pl.when(t % tiles_per_seq == 0)
    def _():
        s_scr[...] = jnp.zeros_like(s_scr)

    def step(p_cur, p_next):
        pieces = _projection_pieces(xn_ref, g_ref, win_ref, p_next)
        n_slots = 2 * (TILE // CHUNK)
        slots = [_both(pieces[c], pieces[n_slots + c]) if n_slots + c < len(pieces) else pieces[c]
                 for c in range(n_slots)]
        _retention_tile(p_cur, a_scr, s_scr, decay_ref, wq_ref, wk_ref, g_c, slots)
        st_ref[...] = s_scr[...]
        qm = p_cur[:, 4 * MIX_MAIN:4 * MIX_MAIN + MEM_W] * ATTN_SCALE
        out_piece = functools.partial(_out_projection_piece, xc_ref, a_scr, wout_ref, o_ref)
        a_mem = _mem_attention_tile(qm, mk_ref[...], mv_ref[...],
                                    [out_piece(c) for c in range(D_MODEL // PIECE)]).astype(BF16)
        o_ref[...] += jnp.dot(a_mem, wout_ref[MIX_MAIN:, :].astype(BF16), preferred_element_type=F32)

    @pl.when(t % 2 == 0)
    def _():
        step(p0, p1)

    @pl.when(t % 2 == 1)
    def _():
        step(p1, p0)


def fused_mixer_a(x, g, w_in, w_out, mem_k, mem_v, layer, batch, seq):
    n_tiles = batch * seq // TILE
    tps = seq // TILE
    decay, w_q, w_k, g_c = _retention_consts()
    pw = w_in.shape[2]
    c3 = lambda t: (0, 0, 0)
    mem_spec = pl.BlockSpec((None, None, MEM_TOKENS, MEM_W), lambda t: (layer, t // tps, 0, 0))
    return pl.pallas_call(
        functools.partial(_fused_a_kernel, g_c=g_c, tiles_per_seq=tps),
        grid=(n_tiles,),
        in_specs=[pl.BlockSpec((TILE, D_MODEL), lambda t: (t, 0)),
                  pl.BlockSpec((TILE, D_MODEL), lambda t: (jnp.minimum(t + 1, n_tiles - 1), 0)),
                  _layer_block((1, D_MODEL), layer, 1),
                  pl.BlockSpec((None, D_MODEL, pw), lambda t: (layer, 0, 0), pipeline_mode=pl.Buffered(1)),
                  pl.BlockSpec((None, D_MODEL, D_MODEL), lambda t: (layer, 0, 0), pipeline_mode=pl.Buffered(1)),
                  mem_spec, mem_spec,
                  pl.BlockSpec((RET_HEADS // 2, CHUNK, 2 * CHUNK), c3),
                  pl.BlockSpec((RET_HEADS // 2, CHUNK, 2 * RET_HD), c3),
                  pl.BlockSpec((RET_HEADS // 2, CHUNK, 2 * RET_HD), c3)],
        out_specs=[pl.BlockSpec((TILE, D_MODEL), lambda t: (t, 0)),
                   pl.BlockSpec((None, RET_HEADS, RET_HD, RET_HD), lambda t: (t // tps, 0, 0, 0))],
        out_shape=[jax.ShapeDtypeStruct((batch * seq, D_MODEL), F32),
                   jax.ShapeDtypeStruct((batch, RET_HEADS, RET_HD, RET_HD), F32)],
        scratch_shapes=[pltpu.VMEM((TILE, pw), BF16), pltpu.VMEM((TILE, pw), BF16),
                        pltpu.VMEM((TILE, D_MODEL), BF16),
                        pltpu.VMEM((RET_HEADS, RET_HD, RET_HD), F32)],
        compiler_params=_cparams(1),
        name="fused_mixer_a",
    )(x, x, g.reshape(g.shape[0], 1, D_MODEL), w_in, w_out, mem_k, mem_v,
      jnp.asarray(decay), jnp.asarray(w_q), jnp.asarray(w_k))


def _swa_tile(p_ref, kvp_ref, kvc_ref, a_ref, bias_ref, sink_ref, is_first, fillers=()):
    lane_group = _lane_block64(lax.broadcasted_iota(jnp.int32, (WINDOW, KV_W), 1))
    key_group = _lane_block64(lax.broadcasted_iota(jnp.int32, (2 * WINDOW, KV_W), 1))
    for i in range(TILE // WINDOW):
        r0 = i * WINDOW
        if i == 0:
            kv2 = jnp.concatenate([kvp_ref[...], kvc_ref[0:WINDOW, :]], axis=0)
        else:
            kv2 = kvc_ref[r0 - WINDOW:r0 + WINDOW, :]
        k2 = kv2[:, :KV_W]
        v2 = kv2[:, KV_W:]
        qs = [p_ref[r0:r0 + WINDOW, r * KV_W:(r + 1) * KV_W] for r in range(SWA_REP)]
        probs = []
        vals = []
        for g in range(SWA_KV_HEADS):
            sel = lane_group == g
            qg = jnp.concatenate([jnp.where(sel, qr, jnp.zeros_like(qr)) for qr in qs], axis=0)
            s = lax.dot_general(qg, k2, NT_DIMS, preferred_element_type=F32) + bias_ref[g]
            if g == 1 and i < len(fillers):
                fillers[i]()
            s_prev = s[:, :WINDOW]
            s_cur = s[:, WINDOW:]
            if i == 0:
                s_prev = jnp.where(is_first, NEG, s_prev)
            sink = sink_ref[g]
            m = jnp.maximum(jnp.max(jnp.maximum(s_prev, s_cur), axis=-1, keepdims=True), sink)
            e_prev = jnp.exp(s_prev - m)
            e_cur = jnp.exp(s_cur - m)
            inv = 1.0 / (jnp.sum(e_prev + e_cur, axis=-1, keepdims=True) + jnp.exp(sink - m))
            probs.append((e_prev * inv).astype(BF16))
            probs.append((e_cur * inv).astype(BF16))
            vals.append(jnp.where(key_group == g, v2, jnp.zeros_like(v2)))
        o = jnp.dot(jnp.concatenate(probs, axis=1), jnp.concatenate(vals, axis=0), preferred_element_type=F32)
        for r in range(SWA_REP):
            a_ref[r0:r0 + WINDOW, r * KV_W:(r + 1) * KV_W] = o[r * WINDOW:(r + 1) * WINDOW].astype(BF16)


def _fused_b_kernel(xc_ref, xn_ref, g_ref, win_ref, wout_ref, kvp_ref, kvc_ref, mk_ref, mv_ref, bias_ref, sink_ref,
                    o_ref, p0, p1, a_scr, *, tiles_per_seq):
    t = pl.program_id(0)
    is_first = t % tiles_per_seq == 0

    @pl.when(t == 0)
    def _():
        _project_tile(xc_ref, g_ref, win_ref, p0, 512)

    out_piece = functools.partial(_out_projection_piece, xc_ref, a_scr, wout_ref, o_ref)

    def step(p_cur, p_next):
        pieces = _projection_pieces(xn_ref, g_ref, win_ref, p_next)
        _swa_tile(p_cur, kvp_ref, kvc_ref, a_scr, bias_ref, sink_ref, is_first, pieces)
        qm = p_cur[:, MIX_MAIN:MIX_MAIN + MEM_W]
        a_mem = _mem_attention_tile(qm, mk_ref[...], mv_ref[...],
                                    [out_piece(c) for c in range(D_MODEL // PIECE)]).astype(BF16)
        o_ref[...] += jnp.dot(a_mem, wout_ref[MIX_MAIN:, :], preferred_element_type=F32)

    @pl.when(t % 2 == 0)
    def _():
        step(p0, p1)

    @pl.when(t % 2 == 1)
    def _():
        step(p1, p0)


def fused_mixer_b(x, g, w_in, w_out, kv, mem_k, mem_v, layer, sink_rows, sub_layer, batch, seq):
    n_tiles = batch * seq // TILE
    tps = seq // TILE
    nw = TILE // WINDOW
    bias = jnp.asarray(_swa_bias())
    mem_spec = pl.BlockSpec((None, None, MEM_TOKENS, MEM_W), lambda t: (layer, t // tps, 0, 0))

    def prev_map(t):
        return (jnp.maximum(t * nw - 1, (t // tps) * (seq // WINDOW)), 0)

    return pl.pallas_call(
        functools.partial(_fused_b_kernel, tiles_per_seq=tps),
        grid=(n_tiles,),
        in_specs=[pl.BlockSpec((TILE, D_MODEL), lambda t: (t, 0)),
                  pl.BlockSpec((TILE, D_MODEL), lambda t: (jnp.minimum(t + 1, n_tiles - 1), 0)),
                  _layer_block((1, D_MODEL), layer, 1),
                  pl.BlockSpec((None, D_MODEL, D_MODEL), lambda t: (sub_layer, 0, 0), pipeline_mode=pl.Buffered(1)),
                  pl.BlockSpec((None, D_MODEL, D_MODEL), lambda t: (sub_layer, 0, 0), pipeline_mode=pl.Buffered(1)),
                  pl.BlockSpec((WINDOW, 2 * KV_W), prev_map),
                  pl.BlockSpec((TILE, 2 * KV_W), lambda t: (t, 0)),
                  mem_spec, mem_spec,
                  pl.BlockSpec((SWA_KV_HEADS, SWA_REP * WINDOW, 2 * WINDOW), lambda t: (0, 0, 0)),
                  pl.BlockSpec((None, SWA_KV_HEADS, SWA_REP * WINDOW, LANES), lambda t: (sub_layer, 0, 0, 0))],
        out_specs=pl.BlockSpec((TILE, D_MODEL), lambda t: (t, 0)),
        out_shape=jax.ShapeDtypeStruct((batch * seq, D_MODEL), F32),
        scratch_shapes=[pltpu.VMEM((TILE, D_MODEL), BF16), pltpu.VMEM((TILE, D_MODEL), BF16),
                        pltpu.VMEM((TILE, D_MODEL), BF16)],
        compiler_params=_cparams(1),
        name="fused_mixer_b",
    )(x, x, g.reshape(g.shape[0], 1, D_MODEL), w_in, w_out, kv, kv, mem_k, mem_v, bias, sink_rows)


def _ret_decode_kernel(proj_ref, st_ref, *rest, bb):
    o_ref, nst_ref = rest[-2:]
    scale = RET_HD ** -0.5
    row = lax.broadcasted_iota(jnp.int32, (bb, bb * RET_HD), 0)
    lane_seq = lax.shift_right_logical(lax.broadcasted_iota(jnp.int32, (bb, bb * RET_HD), 1), 7)
    own = row == lane_seq
    for h in range(RET_HEADS):
        lo = h * RET_HD
        q8 = proj_ref[:, lo:lo + RET_HD]
        k8 = proj_ref[:, MIX_MAIN + lo:MIX_MAIN + lo + RET_HD] * scale
        v8 = proj_ref[:, 2 * MIX_MAIN + lo:2 * MIX_MAIN + lo + RET_HD]
        gate = proj_ref[:, 3 * MIX_MAIN + lo:3 * MIX_MAIN + lo + RET_HD]
        g = float(np.exp(_LOG_G[h]))
        vexp = jnp.where(own, jnp.concatenate([v8] * bb, axis=1), 0.0).astype(BF16)
        outer = lax.dot_general(k8.astype(BF16), vexp, TN_DIMS, preferred_element_type=F32)
        cols = []
        for j in range(bb):
            s_new = g * st_ref[j, h] + outer[:, j * RET_HD:(j + 1) * RET_HD]
            nst_ref[j, h] = s_new
            cols.append(s_new.astype(BF16))
        o_all = jnp.dot(q8.astype(BF16), jnp.concatenate(cols, axis=1), preferred_element_type=F32)
        o_all = jnp.where(own, o_all, 0.0)
        o = o_all[:, :RET_HD]
        for j in range(1, bb):
            o = o + o_all[:, j * RET_HD:(j + 1) * RET_HD]
        mu = jnp.mean(o, axis=-1, keepdims=True)
        oc = o - mu
        var = jnp.mean(oc * oc, axis=-1, keepdims=True)
        on = oc * lax.rsqrt(var + EPS)
        o_ref[:, lo:lo + RET_HD] = gate * jax.nn.sigmoid(gate) * on


def ret_decode(proj, state, layer, stacked_out=None, bb=16):
    nb = proj.shape[0]
    pw = proj.shape[1]
    in_specs = [pl.BlockSpec((bb, pw), lambda i: (i, 0)),
                pl.BlockSpec((None, bb, RET_HEADS, RET_HD, RET_HD), lambda i: (layer, i, 0, 0, 0))]
    args = [proj, state]
    aliases = {}
    if stacked_out is not None:
        in_specs.append(pl.BlockSpec(memory_space=pl.ANY))
        args.append(stacked_out)
        aliases = {2: 1}
    return pl.pallas_call(
        functools.partial(_ret_decode_kernel, bb=bb),
        grid=(nb // bb,),
        in_specs=in_specs,
        out_specs=[pl.BlockSpec((bb, MIX_MAIN), lambda i: (i, 0)),
                   pl.BlockSpec((None, bb, RET_HEADS, RET_HD, RET_HD), lambda i: (layer, i, 0, 0, 0))],
        out_shape=[jax.ShapeDtypeStruct((nb, MIX_MAIN), F32),
                   jax.ShapeDtypeStruct(state.shape, F32)],
        input_output_aliases=aliases,
        compiler_params=_cparams(1),
        name="ret_decode",
    )(*args)


def _mem_decode_kernel(q_ref, kt_ref, vt_ref, o_ref, *, bb, q_off, q_scale):
    row = lax.broadcasted_iota(jnp.int32, (8, MEM_W), 0)
    lane_head = _lane_block64(lax.broadcasted_iota(jnp.int32, (8, MEM_W), 1))
    sel = row == lane_head
    seqs = range(bb)
    qexp = [jnp.where(sel, jnp.broadcast_to(q_ref[j:j + 1, q_off:q_off + MEM_W] * q_scale, (8, MEM_W)), 0.0)
            for j in seqs]
    s = [jnp.dot(qexp[j], kt_ref[j], preferred_element_type=F32) for j in seqs]
    e = [jnp.exp(s[j] - jnp.max(s[j], axis=-1, keepdims=True)) for j in seqs]
    den = [jnp.sum(e[j], axis=-1, keepdims=True) for j in seqs]
    o = [lax.dot_general(e[j], vt_ref[j], NT_DIMS, preferred_element_type=F32) / den[j] for j in seqs]
    o_ref[...] = jnp.concatenate([jnp.sum(jnp.where(sel, o[j], 0.0), axis=0, keepdims=True) for j in seqs], axis=0)


def mem_decode(proj, cache_kt, cache_vt, layer, q_off, q_scale, bb=16):
    nb = proj.shape[0]
    pw = proj.shape[1]
    cmap = lambda i: (layer, i, 0, 0)
    return pl.pallas_call(
        functools.partial(_mem_decode_kernel, bb=bb, q_off=q_off, q_scale=q_scale),
        grid=(nb // bb,),
        in_specs=[pl.BlockSpec((bb, pw), lambda i: (i, 0)),
                  pl.BlockSpec((None, bb, MEM_W, MEM_TOKENS), cmap),
                  pl.BlockSpec((None, bb, MEM_W, MEM_TOKENS), cmap)],
        out_specs=pl.BlockSpec((bb, MEM_W), lambda i: (i, 0)),
        out_shape=jax.ShapeDtypeStruct((nb, MEM_W), F32),
        compiler_params=_cparams(1),
        name="mem_decode",
    )(proj, cache_kt, cache_vt)


def _swa_decode_kernel(q_ref, kn_ref, vn_ref, ckt_ref, cvt_ref, slope_ref, sink_ref, o_ref, *cache_out, bb):
    row = lax.broadcasted_iota(jnp.int32, (DEC_ROWS, KV_W), 0)
    lane_group = _lane_block64(lax.broadcasted_iota(jnp.int32, (DEC_ROWS, KV_W), 1))
    dist = (WINDOW - lax.broadcasted_iota(jnp.int32, (DEC_ROWS, WINDOW), 1)).astype(F32)
    bias = -slope_ref[...] * dist
    sink = sink_ref[...]
    own = [row == SWA_KV_HEADS * r + lane_group for r in range(SWA_REP)]
    seqs = range(bb)

    def expand(j):
        qexp = jnp.zeros((DEC_ROWS, KV_W), F32)
        for r in range(SWA_REP):
            qr = jnp.broadcast_to(q_ref[j:j + 1, r * KV_W:(r + 1) * KV_W], (DEC_ROWS, KV_W))
            qexp = jnp.where(own[r], qr, qexp)
        return qexp

    qexp = [expand(j) for j in seqs]
    s = [jnp.dot(qexp[j], ckt_ref[j], preferred_element_type=F32) + bias for j in seqs]
    s_new = [jnp.sum(qexp[j] * kn_ref[j:j + 1, :], axis=-1, keepdims=True) for j in seqs]
    m = [jnp.maximum(jnp.maximum(jnp.max(s[j], axis=-1, keepdims=True), s_new[j]), sink) for j in seqs]
    e = [jnp.exp(s[j] - m[j]) for j in seqs]
    e_new = [jnp.exp(s_new[j] - m[j]) for j in seqs]
    inv = [1.0 / (jnp.sum(e[j], axis=-1, keepdims=True) + e_new[j] + jnp.exp(sink - m[j])) for j in seqs]
    o = [lax.dot_general(e[j], cvt_ref[j], NT_DIMS, preferred_element_type=F32) for j in seqs]
    o = [(o[j] + jnp.concatenate([e_new[j], e_new[j]], axis=1) * vn_ref[j:j + 1, :])
         * jnp.concatenate([inv[j], inv[j]], axis=1) for j in seqs]
    for r in range(SWA_REP):
        o_ref[:, r * KV_W:(r + 1) * KV_W] = jnp.concatenate(
            [jnp.sum(jnp.where(own[r], o[j], 0.0), axis=0, keepdims=True) for j in seqs], axis=0)
    if cache_out:
        nkt_ref, nvt_ref = cache_out
        last = lax.broadcasted_iota(jnp.int32, (KV_W, WINDOW), 1) == WINDOW - 1
        knt = kn_ref[...].T
        vnt = vn_ref[...].T
        for j in seqs:
            nkt_ref[j] = jnp.where(last, jnp.broadcast_to(knt[:, j:j + 1], (KV_W, WINDOW)),
                                   pltpu.roll(ckt_ref[j], WINDOW - 1, 1))
            nvt_ref[j] = jnp.where(last, jnp.broadcast_to(vnt[:, j:j + 1], (KV_W, WINDOW)),
                                   pltpu.roll(cvt_ref[j], WINDOW - 1, 1))


def swa_decode(qproj, k_new, v_new, cache_kt, cache_vt, slope_rows, sink_rows, sink_layer, emit_cache, bb=32):
    nb = qproj.shape[0]
    pw = qproj.shape[1]
    cache_spec = pl.BlockSpec((bb, KV_W, WINDOW), lambda i: (i, 0, 0))
    cache_shape = jax.ShapeDtypeStruct((nb, KV_W, WINDOW), F32)
    n_cache = 2 if emit_cache else 0
    return pl.pallas_call(
        functools.partial(_swa_decode_kernel, bb=bb),
        grid=(nb // bb,),
        in_specs=[pl.BlockSpec((bb, pw), lambda i: (i, 0)),
                  pl.BlockSpec((bb, KV_W), lambda i: (i, 0)),
                  pl.BlockSpec((bb, KV_W), lambda i: (i, 0)),
                  cache_spec, cache_spec,
                  pl.BlockSpec((DEC_ROWS, LANES), lambda i: (0, 0)),
                  pl.BlockSpec((None, DEC_ROWS, LANES), lambda i: (sink_layer, 0, 0))],
        out_specs=[pl.BlockSpec((bb, MIX_MAIN), lambda i: (i, 0))] + [cache_spec] * n_cache,
        out_shape=[jax.ShapeDtypeStruct((nb, MIX_MAIN), F32)] + [cache_shape] * n_cache,
        compiler_params=_cparams(1),
        name="swa_decode",
    )(qproj, k_new, v_new, cache_kt, cache_vt, slope_rows, sink_rows)


def _permute_q_cols(w):
    lead = w.shape[:-1]
    return w.reshape(*lead, SWA_KV_HEADS, SWA_REP, SWA_HD).swapaxes(-3, -2).reshape(*lead, MIX_MAIN)


def kernel(x_prompt, x_sample, cache_mem_k, cache_mem_v, state_ret, cache_swa_k, cache_swa_v, mem_prompt,
           norm_mix, w_in_a, w_out_a, w_in_b, w_out_b, attn_sinks, norm_mem, w_mem_kv, norm_kv, w_kv,
           norm_mlp, w_up, w_down, norm_final):
    batch, seq, d = x_prompt.shape
    nb = x_sample.shape[0]
    tm = 512

    w_in_b16 = (jnp.concatenate([_permute_q_cols(w_in_b[:, :, :MIX_MAIN]), w_in_b[:, :, MIX_MAIN:]], axis=-1)
                * ATTN_SCALE).astype(BF16)
    w_out_b16 = jnp.concatenate(
        [_permute_q_cols(w_out_b[:, :MIX_MAIN, :].swapaxes(1, 2)).swapaxes(1, 2), w_out_b[:, MIX_MAIN:, :]],
        axis=1).astype(BF16)
    w_kv16 = w_kv.astype(BF16).reshape(1, d, 2 * KV_W)
    norm_kv1 = norm_kv.reshape(1, d)

    sinks_gr = attn_sinks.reshape(N_B, SWA_KV_HEADS, SWA_REP)
    sink_prompt = jnp.broadcast_to(sinks_gr[:, :, :, None, None], (N_B, SWA_KV_HEADS, SWA_REP, WINDOW, LANES)
                                   ).reshape(N_B, SWA_KV_HEADS, SWA_REP * WINDOW, LANES)
    sink_dec = jnp.concatenate([sinks_gr.swapaxes(1, 2).reshape(N_B, SWA_HEADS),
                                jnp.zeros((N_B, DEC_ROWS - SWA_HEADS), F32)], axis=1)
    sink_dec = jnp.broadcast_to(sink_dec[:, :, None], (N_B, DEC_ROWS, LANES))
    slope_dec = np.zeros((DEC_ROWS, LANES), np.float32)
    for r in range(SWA_REP):
        for g in range(SWA_KV_HEADS):
            slope_dec[SWA_KV_HEADS * r + g, :] = _SLOPES[g * SWA_REP + r]
    slope_dec = jnp.asarray(slope_dec)

    memx = mem_prompt.reshape(batch * MEM_TOKENS, d)
    mem_kv = mem_kv_proj(memx, norm_mem, w_mem_kv).reshape(DEPTH, batch, MEM_TOKENS, 2 * MEM_W)
    mem_k_prompt = mem_kv[..., :MEM_W].reshape(DEPTH, batch, MEM_TOKENS, MEM_HEADS, MEM_HD)
    mem_v_prompt = mem_kv[..., MEM_W:].reshape(DEPTH, batch, MEM_TOKENS, MEM_HEADS, MEM_HD)
    mk16 = mem_kv[..., :MEM_W].astype(BF16)
    mv16 = mem_kv[..., MEM_W:].astype(BF16)

    x = x_prompt.reshape(batch * seq, d)
    ret_states = []
    kv_p = kv_p16 = None
    for l in range(DEPTH):
        if l < N_A:
            x, st = fused_mixer_a(x, norm_mix, w_in_a, w_out_a, mk16, mv16, l, batch, seq)
            ret_states.append(st)
        else:
            j = l - N_A
            x = fused_mixer_b(x, norm_mix, w_in_b16, w_out_b16, kv_p16, mk16, mv16, l, sink_prompt, j, batch, seq)
        if l == N_A - 1:
            x, kv_p, kv_p16 = mlp_block(x, norm_mlp, w_up, w_down, l, norm_final, False, tm=tm,
                                        kv_proj=(norm_kv1, w_kv16[0]))
        else:
            x = mlp_block(x, norm_mlp, w_up, w_down, l, norm_final, l == DEPTH - 1, tm=tm)
    y_prompt = x.reshape(batch, seq, d)
    ret_prompt = jnp.stack(ret_states)
    kv_p3 = kv_p.reshape(batch, seq, 2 * KV_W)
    swa_k_prompt = kv_p3[:, -WINDOW:, :KV_W].reshape(batch, WINDOW, SWA_KV_HEADS, SWA_HD)
    swa_v_prompt = kv_p3[:, -WINDOW:, KV_W:].reshape(batch, WINDOW, SWA_KV_HEADS, SWA_HD)

    cmkt = jnp.transpose(cache_mem_k, (0, 1, 3, 4, 2)).reshape(DEPTH, nb, MEM_W, MEM_TOKENS)
    cmvt = jnp.transpose(cache_mem_v, (0, 1, 3, 4, 2)).reshape(DEPTH, nb, MEM_W, MEM_TOKENS)
    cskt = jnp.transpose(cache_swa_k, (0, 2, 3, 1)).reshape(nb, KV_W, WINDOW)
    csvt = jnp.transpose(cache_swa_v, (0, 2, 3, 1)).reshape(nb, KV_W, WINDOW)
    xs = x_sample.reshape(nb, d)
    ret_sample = None
    k_new = v_new = nkt = nvt = None
    for l in range(DEPTH):
        if l < N_A:
            proj = norm_matmul(xs, norm_mix, w_in_a, l, F32, tm=nb, n_chunk=MIX_MAIN)
            o_ret, ret_sample = ret_decode(proj, state_ret, l, stacked_out=ret_sample)
            o_mem = mem_decode(proj, cmkt, cmvt, l, 4 * MIX_MAIN, ATTN_SCALE)
            attn = jnp.concatenate([o_ret, o_mem], axis=-1).astype(BF16)
            xs = matmul_residual(attn, w_out_a, l, xs, tm=nb)
        else:
            j = l - N_A
            if j == 0:
                kv_s = norm_matmul(xs, norm_kv1, w_kv16, 0, F32, tm=nb)
                k_new, v_new = kv_s[:, :KV_W], kv_s[:, KV_W:]
            qproj = norm_matmul(xs, norm_mix[N_A:], w_in_b16, j, F32, tm=nb)
            res = swa_decode(qproj, k_new, v_new, cskt, csvt, slope_dec, sink_dec, j, emit_cache=(j == 0))
            o_swa = res[0]
            if j == 0:
                nkt, nvt = res[1], res[2]
            o_mem = mem_decode(qproj, cmkt, cmvt, l, MIX_MAIN, 1.0)
            attn = jnp.concatenate([o_swa, o_mem], axis=-1).astype(BF16)
            xs = matmul_residual(attn, w_out_b16, j, xs, tm=nb)
        xs = mlp_decode(xs, norm_mlp, w_up, w_down, l, norm_final, l == DEPTH - 1)
    y_sample = xs.reshape(nb, 1, d)
    swa_k_sample = jnp.transpose(nkt.reshape(nb, SWA_KV_HEADS, SWA_HD, WINDOW), (0, 3, 1, 2))
    swa_v_sample = jnp.transpose(nvt.reshape(nb, SWA_KV_HEADS, SWA_HD, WINDOW), (0, 3, 1, 2))

    return (y_prompt, y_sample, ret_prompt, ret_sample, swa_k_prompt, swa_v_prompt, swa_k_sample, swa_v_sample,
            mem_k_prompt, mem_v_prompt)
```

```python
import functools
import math

import jax
import jax.numpy as jnp
import numpy as np
from jax import lax
from jax.experimental import pallas as pl
from jax.experimental.pallas import tpu as pltpu

F32 = jnp.float32
BF16 = jnp.bfloat16

D_MODEL = 1024
DEPTH = 4
N_A = 2
N_B = 2
MEM_TOKENS = 256
MEM_HEADS = 4
MEM_W = 256
MEM_HD = 64
MIX_MAIN = 768
RET_HEADS = 6
RET_HD = 128
CHUNK = 128
SWA_HEADS = 12
SWA_KV_HEADS = 4
SWA_REP = SWA_HEADS // SWA_KV_HEADS
SWA_HD = 64
KV_W = SWA_KV_HEADS * SWA_HD
WINDOW = 128
D_FF = 4096
EPS = 1e-6
NEG = -1e30
ATTN_SCALE = 0.125
LANES = 128
TILE = 512
DEC_ROWS = 16

VMEM_LIMIT = 56 * 1024 * 1024

NT_DIMS = (((1,), (1,)), ((), ()))
TN_DIMS = (((0,), (0,)), ((), ()))


def _alibi_slopes(n):
    def pow2(m):
        start = 2.0 ** (-8.0 / m)
        return [start ** (i + 1) for i in range(m)]

    if math.log2(n).is_integer():
        s = pow2(n)
    else:
        c = 2 ** int(math.floor(math.log2(n)))
        s = pow2(c) + pow2(2 * c)[0::2][: n - c]
    return np.asarray(s, np.float32)


_LOG_G = np.log1p(-(2.0 ** (-5.0 - np.arange(RET_HEADS)))).astype(np.float32).astype(np.float64)
_SLOPES = _alibi_slopes(SWA_HEADS).astype(np.float64)


def _retention_consts():
    idx = np.arange(CHUNK, dtype=np.float64)
    diff = idx[:, None] - idx[None, :]
    scale = RET_HD ** -0.5
    decay = np.where(diff >= 0, np.exp(np.maximum(diff, 0.0)[None] * _LOG_G[:, None, None]), 0.0) * scale
    w_q = np.exp((idx + 1.0)[None, :] * _LOG_G[:, None])
    w_k = np.exp((CHUNK - 1.0 - idx)[None, :] * _LOG_G[:, None]) * scale
    w_q = np.broadcast_to(w_q[:, :, None], (RET_HEADS, CHUNK, RET_HD))
    w_k = np.broadcast_to(w_k[:, :, None], (RET_HEADS, CHUNK, RET_HD))
    g_c = np.exp(CHUNK * _LOG_G)
    pair = lambda t: np.ascontiguousarray(np.concatenate([t[0::2], t[1::2]], axis=2), np.float32)
    return pair(decay), pair(w_q), pair(w_k), [float(v) for v in g_c]


def _swa_bias():
    i = np.arange(WINDOW)[:, None]
    j = np.arange(2 * WINDOW)[None, :]
    dist = i + WINDOW - j
    valid = (dist >= 0) & (dist <= WINDOW)
    out = np.zeros((SWA_KV_HEADS, SWA_REP * WINDOW, 2 * WINDOW), np.float32)
    for g in range(SWA_KV_HEADS):
        for r in range(SWA_REP):
            b = np.where(valid, -_SLOPES[g * SWA_REP + r] * dist, NEG)
            out[g, r * WINDOW:(r + 1) * WINDOW] = b
    return out


def _rmsnorm(x, g):
    ms = jnp.mean(x * x, axis=-1, keepdims=True)
    return (x * lax.rsqrt(ms + EPS)) * g


def _lane_block64(lane):
    return lax.shift_right_logical(lane, 6)


def _cparams(n_axes):
    return pltpu.CompilerParams(dimension_semantics=("arbitrary",) * n_axes, vmem_limit_bytes=VMEM_LIMIT)


def _layer_block(tail, layer, n_grid):
    zeros = (0,) * len(tail)
    if n_grid == 1:
        return pl.BlockSpec((None,) + tuple(tail), lambda i: (layer,) + zeros)
    return pl.BlockSpec((None,) + tuple(tail), lambda i, j: (layer,) + zeros)


def _norm_matmul_kernel(x_ref, g_ref, w_ref, o_ref, *, n_chunk):
    xn = _rmsnorm(x_ref[...], g_ref[...]).astype(BF16)
    n = w_ref.shape[1]
    for n0 in range(0, n, n_chunk):
        n1 = min(n0 + n_chunk, n)
        o_ref[:, n0:n1] = jnp.dot(xn, w_ref[:, n0:n1].astype(BF16), preferred_element_type=F32).astype(o_ref.dtype)


def norm_matmul(x, g, w, layer, out_dtype, tm, n_chunk=512):
    m, k = x.shape
    n = w.shape[2]
    return pl.pallas_call(
        functools.partial(_norm_matmul_kernel, n_chunk=n_chunk),
        grid=(m // tm,),
        in_specs=[pl.BlockSpec((tm, k), lambda i: (i, 0)),
                  _layer_block((1, k), layer, 1),
                  _layer_block((k, n), layer, 1)],
        out_specs=pl.BlockSpec((tm, n), lambda i: (i, 0)),
        out_shape=jax.ShapeDtypeStruct((m, n), out_dtype),
        compiler_params=_cparams(1),
        name="norm_matmul",
    )(x, g.reshape(g.shape[0], 1, k), w)


def _mem_kv_kernel(x_ref, g_ref, w_ref, o_ref):
    xn = _rmsnorm(x_ref[...], g_ref[...]).astype(BF16)
    o_ref[...] = jnp.dot(xn, w_ref[...].astype(BF16), preferred_element_type=F32)


def mem_kv_proj(x, g, w):
    m, k = x.shape
    n_layers, _, n = w.shape
    return pl.pallas_call(
        _mem_kv_kernel,
        grid=(n_layers,),
        in_specs=[pl.BlockSpec((m, k), lambda l: (0, 0)),
                  pl.BlockSpec((None, 1, k), lambda l: (l, 0, 0)),
                  pl.BlockSpec((None, k, n), lambda l: (l, 0, 0))],
        out_specs=pl.BlockSpec((None, m, n), lambda l: (l, 0, 0)),
        out_shape=jax.ShapeDtypeStruct((n_layers, m, n), F32),
        compiler_params=_cparams(1),
        name="mem_kv_proj",
    )(x, g.reshape(n_layers, 1, k), w)


def _mlp_kernel(x_ref, g_ref, wu_ref, wd_ref, gf_ref, *rest, fc, final_norm, with_kv):
    if with_kv:
        gkv_ref, wkv_ref, o_ref, kv_ref, kv16_ref, act_ref = rest
    else:
        o_ref, act_ref = rest
    xn = _rmsnorm(x_ref[...], g_ref[...]).astype(BF16)
    for c0 in range(0, D_FF, fc):
        h = jnp.dot(xn, wu_ref[:, c0:c0 + fc].astype(BF16), preferred_element_type=F32)
        act_ref[:, c0:c0 + fc] = jnp.square(jnp.maximum(h, 0.0)).astype(BF16)
    y = x_ref[...] + jnp.dot(act_ref[...], wd_ref[...].astype(BF16), preferred_element_type=F32)
    if with_kv:
        kv = jnp.dot(_rmsnorm(y, gkv_ref[...]).astype(BF16), wkv_ref[...], preferred_element_type=F32)
        kv_ref[...] = kv
        kv16_ref[...] = kv.astype(BF16)
    if final_norm:
        y = _rmsnorm(y, gf_ref[...])
    o_ref[...] = y


def mlp_block(x, g, w_up, w_down, layer, g_final, final_norm, tm, kv_proj=None, fc=512):
    m, d = x.shape
    with_kv = kv_proj is not None
    in_specs = [pl.BlockSpec((tm, d), lambda i: (i, 0)),
                _layer_block((1, d), layer, 1),
                pl.BlockSpec((None, d, D_FF), lambda i: (layer, 0, 0), pipeline_mode=pl.Buffered(1)),
                pl.BlockSpec((None, D_FF, d), lambda i: (layer, 0, 0), pipeline_mode=pl.Buffered(1)),
                pl.BlockSpec((1, d), lambda i: (0, 0))]
    args = [x, g.reshape(g.shape[0], 1, d), w_up, w_down, g_final.reshape(1, d)]
    out_specs = [pl.BlockSpec((tm, d), lambda i: (i, 0))]
    out_shape = [jax.ShapeDtypeStruct((m, d), F32)]
    if with_kv:
        in_specs += [pl.BlockSpec((1, d), lambda i: (0, 0)),
                     pl.BlockSpec((d, 2 * KV_W), lambda i: (0, 0), pipeline_mode=pl.Buffered(1))]
        args += list(kv_proj)
        out_specs += [pl.BlockSpec((tm, 2 * KV_W), lambda i: (i, 0))] * 2
        out_shape += [jax.ShapeDtypeStruct((m, 2 * KV_W), F32), jax.ShapeDtypeStruct((m, 2 * KV_W), BF16)]
    res = pl.pallas_call(
        functools.partial(_mlp_kernel, fc=fc, final_norm=final_norm, with_kv=with_kv),
        grid=(m // tm,),
        in_specs=in_specs,
        out_specs=out_specs,
        out_shape=out_shape,
        scratch_shapes=[pltpu.VMEM((tm, D_FF), BF16)],
        compiler_params=_cparams(1),
        name="mlp_block",
    )(*args)
    return res if with_kv else res[0]


def _post_decode_kernel(a_ref, wo_ref, x_ref, g_ref, wu_ref, wd_ref, gf_ref, o_ref, x1_scr, xn_scr, acc_scr, *,
                        final_norm):
    k = pl.program_id(0)

    @pl.when(k == 0)
    def _():
        x1 = x_ref[...] + jnp.dot(a_ref[...], wo_ref[...].astype(BF16), preferred_element_type=F32)
        x1_scr[...] = x1
        xn_scr[...] = _rmsnorm(x1, g_ref[...]).astype(BF16)
        acc_scr[...] = jnp.zeros_like(acc_scr)

    h = jnp.dot(xn_scr[...], wu_ref[...].astype(BF16), preferred_element_type=F32)
    act = jnp.square(jnp.maximum(h, 0.0)).astype(BF16)
    acc_scr[...] += jnp.dot(act, wd_ref[...].astype(BF16), preferred_element_type=F32)

    @pl.when(k == pl.num_programs(0) - 1)
    def _():
        y = x1_scr[...] + acc_scr[...]
        if final_norm:
            y = _rmsnorm(y, gf_ref[...])
        o_ref[...] = y


def post_decode(attn, w_out, out_layer, x, g, w_up, w_down, layer, g_final, final_norm, fc=1024):
    m, d = x.shape
    return pl.pallas_call(
        functools.partial(_post_decode_kernel, final_norm=final_norm),
        grid=(D_FF // fc,),
        in_specs=[pl.BlockSpec((m, d), lambda k: (0, 0)),
                  _layer_block((d, d), out_layer, 1),
                  pl.BlockSpec((m, d), lambda k: (0, 0)),
                  _layer_block((1, d), layer, 1),
                  pl.BlockSpec((None, d, fc), lambda k: (layer, 0, k)),
                  pl.BlockSpec((None, fc, d), lambda k: (layer, k, 0)),
                  pl.BlockSpec((1, d), lambda k: (0, 0))],
        out_specs=pl.BlockSpec((m, d), lambda k: (0, 0)),
        out_shape=jax.ShapeDtypeStruct((m, d), F32),
        scratch_shapes=[pltpu.VMEM((m, d), F32), pltpu.VMEM((m, d), BF16), pltpu.VMEM((m, d), F32)],
        compiler_params=_cparams(1),
        name="post_decode",
    )(attn, w_out, x, g.reshape(g.shape[0], 1, d), w_up, w_down, g_final.reshape(1, d))


def _mem_attention_tile(qm, mk, mv, fillers=()):
    t = qm.shape[0]
    lane_head = _lane_block64(lax.broadcasted_iota(jnp.int32, (t, MEM_W), 1))
    tok_head = _lane_block64(lax.broadcasted_iota(jnp.int32, (MEM_TOKENS, MEM_W), 1))
    probs = []
    vals = []
    for h in range(MEM_HEADS):
        qh = jnp.where(lane_head == h, qm, jnp.zeros_like(qm))
        s = lax.dot_general(qh, mk, NT_DIMS, preferred_element_type=F32)
        if h < len(fillers):
            fillers[h]()
        m = jnp.max(s, axis=-1, keepdims=True)
        e = jnp.exp(s - m)
        inv = 1.0 / jnp.sum(e, axis=-1, keepdims=True)
        probs.append((e * inv).astype(BF16))
        vals.append(jnp.where(tok_head == h, mv, jnp.zeros_like(mv)))
    return jnp.dot(jnp.concatenate(probs, axis=1), jnp.concatenate(vals, axis=0), preferred_element_type=F32)


def _project_tile(x_ref, g_ref, w_ref, p_ref, n_chunk):
    xn = _rmsnorm(x_ref[...], g_ref[...]).astype(BF16)
    n = w_ref.shape[1]
    for n0 in range(0, n, n_chunk):
        n1 = min(n0 + n_chunk, n)
        p_ref[:, n0:n1] = jnp.dot(xn, w_ref[:, n0:n1].astype(BF16), preferred_element_type=F32).astype(BF16)


PIECE = 256


def _projection_pieces(x_ref, g_ref, w_ref, p_ref):
    xn = _rmsnorm(x_ref[...], g_ref[...]).astype(BF16)

    def piece(c):
        def run():
            cols = slice(c * PIECE, (c + 1) * PIECE)
            p_ref[:, cols] = jnp.dot(xn, w_ref[:, cols].astype(BF16), preferred_element_type=F32).astype(BF16)
        return run

    return [piece(c) for c in range(w_ref.shape[1] // PIECE)]


def _out_projection_piece(x_ref, a_ref, w_ref, o_ref, c):
    def run():
        cols = slice(c * PIECE, (c + 1) * PIECE)
        o_ref[:, cols] = x_ref[:, cols] + jnp.dot(a_ref[:, :MIX_MAIN], w_ref[:MIX_MAIN, cols].astype(BF16),
                                                  preferred_element_type=F32)
    return run


def _both(f, g):
    def run():
        f()
        g()
    return run


def _block_diag2(a, b):
    z = jnp.zeros_like(a)
    return jnp.concatenate([jnp.concatenate([a, z], axis=1), jnp.concatenate([z, b], axis=1)], axis=0)


def _group_norm_gate(o, gate):
    mu = jnp.mean(o, axis=-1, keepdims=True)
    oc = o - mu
    var = jnp.mean(oc * oc, axis=-1, keepdims=True)
    return (gate * jax.nn.sigmoid(gate) * (oc * lax.rsqrt(var + EPS))).astype(BF16)


def _retention_tile(p_ref, a_ref, s_scr, decay_ref, wq_ref, wk_ref, g_c, fillers=()):
    pw = 2 * RET_HD
    pairs = range(RET_HEADS // 2)
    for c in range(TILE // CHUNK):
        rows = slice(c * CHUNK, (c + 1) * CHUNK)
        q2 = [p_ref[rows, hp * pw:(hp + 1) * pw] for hp in pairs]
        k2 = [p_ref[rows, MIX_MAIN + hp * pw:MIX_MAIN + (hp + 1) * pw] for hp in pairs]
        v2 = [p_ref[rows, 2 * MIX_MAIN + hp * pw:2 * MIX_MAIN + (hp + 1) * pw] for hp in pairs]
        sc = [lax.dot_general(q2[hp], _block_diag2(k2[hp][:, :RET_HD], k2[hp][:, RET_HD:]), NT_DIMS,
                              preferred_element_type=F32) for hp in pairs]
        kw = [(k2[hp].astype(F32) * wk_ref[hp]).astype(BF16) for hp in pairs]
        kv = [lax.dot_general(kw[hp], v2[hp], TN_DIMS, preferred_element_type=F32) for hp in pairs]
        if 2 * c < len(fillers):
            fillers[2 * c]()
        s_old = [s_scr[h] for h in range(RET_HEADS)]
        lhs = [jnp.concatenate([(sc[hp] * decay_ref[hp]).astype(BF16),
                                (q2[hp].astype(F32) * wq_ref[hp]).astype(BF16)], axis=1) for hp in pairs]
        rhs = [jnp.concatenate([_block_diag2(v2[hp][:, :RET_HD], v2[hp][:, RET_HD:]),
                                _block_diag2(s_old[2 * hp].astype(BF16), s_old[2 * hp + 1].astype(BF16))], axis=0)
               for hp in pairs]
        o2 = [jnp.dot(lhs[hp], rhs[hp], preferred_element_type=F32) for hp in pairs]
        if 2 * c + 1 < len(fillers):
            fillers[2 * c + 1]()
        for hp in pairs:
            s_scr[2 * hp] = g_c[2 * hp] * s_old[2 * hp] + kv[hp][:RET_HD, :RET_HD]
            s_scr[2 * hp + 1] = g_c[2 * hp + 1] * s_old[2 * hp + 1] + kv[hp][RET_HD:, RET_HD:]
        for hp in pairs:
            lo = hp * pw
            gate2 = p_ref[rows, 3 * MIX_MAIN + lo:3 * MIX_MAIN + lo + pw].astype(F32)
            a_ref[rows, lo:lo + RET_HD] = _group_norm_gate(o2[hp][:, :RET_HD], gate2[:, :RET_HD])
            a_ref[rows, lo + RET_HD:lo + pw] = _group_norm_gate(o2[hp][:, RET_HD:], gate2[:, RET_HD:])


def _fused_a_kernel(xc_ref, xn_ref, g_ref, win_ref, wout_ref, mk_ref, mv_ref, decay_ref, wq_ref, wk_ref,
                    o_ref, st_ref, p0, p1, a_scr, s_scr, *, g_c, tiles_per_seq):
    t = pl.program_id(0)

    @pl.when(t == 0)
    def _():
        _project_tile(xc_ref, g_ref, win_ref, p0, MIX_MAIN)

    @pl.when(t % tiles_per_seq == 0)
    def _():
        s_scr[...] = jnp.zeros_like(s_scr)

    def step(p_cur, p_next):
        pieces = _projection_pieces(xn_ref, g_ref, win_ref, p_next)
        n_slots = 2 * (TILE // CHUNK)
        slots = [_both(pieces[c], pieces[n_slots + c]) if n_slots + c < len(pieces) else pieces[c]
                 for c in range(n_slots)]
        _retention_tile(p_cur, a_scr, s_scr, decay_ref, wq_ref, wk_ref, g_c, slots)
        st_ref[...] = s_scr[...]
        qm = p_cur[:, 4 * MIX_MAIN:4 * MIX_MAIN + MEM_W] * ATTN_SCALE
        out_piece = functools.partial(_out_projection_piece, xc_ref, a_scr, wout_ref, o_ref)
        a_mem = _mem_attention_tile(qm, mk_ref[...], mv_ref[...],
                                    [out_piece(c) for c in range(D_MODEL // PIECE)]).astype(BF16)
        o_ref[...] += jnp.dot(a_mem, wout_ref[MIX_MAIN:, :].astype(BF16), preferred_element_type=F32)

    @pl.when(t % 2 == 0)
    def _():
        step(p0, p1)

    @pl.when(t % 2 == 1)
    def _():
        step(p1, p0)


def fused_mixer_a(x, g, w_in, w_out, mem_k, mem_v, layer, batch, seq):
    n_tiles = batch * seq // TILE
    tps = seq // TILE
    decay, w_q, w_k, g_c = _retention_consts()
    pw = w_in.shape[2]
    c3 = lambda t: (0, 0, 0)
    mem_spec = pl.BlockSpec((None, None, MEM_TOKENS, MEM_W), lambda t: (layer, t // tps, 0, 0))
    return pl.pallas_call(
        functools.partial(_fused_a_kernel, g_c=g_c, tiles_per_seq=tps),
        grid=(n_tiles,),
        in_specs=[pl.BlockSpec((TILE, D_MODEL), lambda t: (t, 0)),
                  pl.BlockSpec((TILE, D_MODEL), lambda t: (jnp.minimum(t + 1, n_tiles - 1), 0)),
                  _layer_block((1, D_MODEL), layer, 1),
                  pl.BlockSpec((None, D_MODEL, pw), lambda t: (layer, 0, 0), pipeline_mode=pl.Buffered(1)),
                  pl.BlockSpec((None, D_MODEL, D_MODEL), lambda t: (layer, 0, 0), pipeline_mode=pl.Buffered(1)),
                  mem_spec, mem_spec,
                  pl.BlockSpec((RET_HEADS // 2, CHUNK, 2 * CHUNK), c3),
                  pl.BlockSpec((RET_HEADS // 2, CHUNK, 2 * RET_HD), c3),
                  pl.BlockSpec((RET_HEADS // 2, CHUNK, 2 * RET_HD), c3)],
        out_specs=[pl.BlockSpec((TILE, D_MODEL), lambda t: (t, 0)),
                   pl.BlockSpec((None, RET_HEADS, RET_HD, RET_HD), lambda t: (t // tps, 0, 0, 0))],
        out_shape=[jax.ShapeDtypeStruct((batch * seq, D_MODEL), F32),
                   jax.ShapeDtypeStruct((batch, RET_HEADS, RET_HD, RET_HD), F32)],
        scratch_shapes=[pltpu.VMEM((TILE, pw), BF16), pltpu.VMEM((TILE, pw), BF16),
                        pltpu.VMEM((TILE, D_MODEL), BF16),
                        pltpu.VMEM((RET_HEADS, RET_HD, RET_HD), F32)],
        compiler_params=_cparams(1),
        name="fused_mixer_a",
    )(x, x, g.reshape(g.shape[0], 1, D_MODEL), w_in, w_out, mem_k, mem_v,
      jnp.asarray(decay), jnp.asarray(w_q), jnp.asarray(w_k))


def _swa_tile(p_ref, kvp_ref, kvc_ref, a_ref, bias_ref, sink_ref, is_first, fillers=()):
    lane_group = _lane_block64(lax.broadcasted_iota(jnp.int32, (WINDOW, KV_W), 1))
    key_group = _lane_block64(lax.broadcasted_iota(jnp.int32, (2 * WINDOW, KV_W), 1))
    for i in range(TILE // WINDOW):
        r0 = i * WINDOW
        if i == 0:
            kv2 = jnp.concatenate([kvp_ref[...], kvc_ref[0:WINDOW, :]], axis=0)
        else:
            kv2 = kvc_ref[r0 - WINDOW:r0 + WINDOW, :]
        k2 = kv2[:, :KV_W]
        v2 = kv2[:, KV_W:]
        qs = [p_ref[r0:r0 + WINDOW, r * KV_W:(r + 1) * KV_W] for r in range(SWA_REP)]
        probs = []
        vals = []
        for g in range(SWA_KV_HEADS):
            sel = lane_group == g
            qg = jnp.concatenate([jnp.where(sel, qr, jnp.zeros_like(qr)) for qr in qs], axis=0)
            s = lax.dot_general(qg, k2, NT_DIMS, preferred_element_type=F32) + bias_ref[g]
            if g == 1 and i < len(fillers):
                fillers[i]()
            s_prev = s[:, :WINDOW]
            s_cur = s[:, WINDOW:]
            if i == 0:
                s_prev = jnp.where(is_first, NEG, s_prev)
            sink = sink_ref[g]
            m = jnp.maximum(jnp.max(jnp.maximum(s_prev, s_cur), axis=-1, keepdims=True), sink)
            e_prev = jnp.exp(s_prev - m)
            e_cur = jnp.exp(s_cur - m)
            inv = 1.0 / (jnp.sum(e_prev + e_cur, axis=-1, keepdims=True) + jnp.exp(sink - m))
            probs.append((e_prev * inv).astype(BF16))
            probs.append((e_cur * inv).astype(BF16))
            vals.append(jnp.where(key_group == g, v2, jnp.zeros_like(v2)))
        o = jnp.dot(jnp.concatenate(probs, axis=1), jnp.concatenate(vals, axis=0), preferred_element_type=F32)
        for r in range(SWA_REP):
            a_ref[r0:r0 + WINDOW, r * KV_W:(r + 1) * KV_W] = o[r * WINDOW:(r + 1) * WINDOW].astype(BF16)


def _fused_b_kernel(xc_ref, xn_ref, g_ref, win_ref, wout_ref, kvp_ref, kvc_ref, mk_ref, mv_ref, bias_ref, sink_ref,
                    o_ref, p0, p1, a_scr, *, tiles_per_seq):
    t = pl.program_id(0)
    is_first = t % tiles_per_seq == 0

    @pl.when(t == 0)
    def _():
        _project_tile(xc_ref, g_ref, win_ref, p0, 512)

    out_piece = functools.partial(_out_projection_piece, xc_ref, a_scr, wout_ref, o_ref)

    def step(p_cur, p_next):
        pieces = _projection_pieces(xn_ref, g_ref, win_ref, p_next)
        _swa_tile(p_cur, kvp_ref, kvc_ref, a_scr, bias_ref, sink_ref, is_first, pieces)
        qm = p_cur[:, MIX_MAIN:MIX_MAIN + MEM_W]
        a_mem = _mem_attention_tile(qm, mk_ref[...], mv_ref[...],
                                    [out_piece(c) for c in range(D_MODEL // PIECE)]).astype(BF16)
        o_ref[...] += jnp.dot(a_mem, wout_ref[MIX_MAIN:, :], preferred_element_type=F32)

    @pl.when(t % 2 == 0)
    def _():
        step(p0, p1)

    @pl.when(t % 2 == 1)
    def _():
        step(p1, p0)


def fused_mixer_b(x, g, w_in, w_out, kv, mem_k, mem_v, layer, sink_rows, sub_layer, batch, seq):
    n_tiles = batch * seq // TILE
    tps = seq // TILE
    nw = TILE // WINDOW
    bias = jnp.asarray(_swa_bias())
    mem_spec = pl.BlockSpec((None, None, MEM_TOKENS, MEM_W), lambda t: (layer, t // tps, 0, 0))

    def prev_map(t):
        return (jnp.maximum(t * nw - 1, (t // tps) * (seq // WINDOW)), 0)

    return pl.pallas_call(
        functools.partial(_fused_b_kernel, tiles_per_seq=tps),
        grid=(n_tiles,),
        in_specs=[pl.BlockSpec((TILE, D_MODEL), lambda t: (t, 0)),
                  pl.BlockSpec((TILE, D_MODEL), lambda t: (jnp.minimum(t + 1, n_tiles - 1), 0)),
                  _layer_block((1, D_MODEL), layer, 1),
                  pl.BlockSpec((None, D_MODEL, D_MODEL), lambda t: (sub_layer, 0, 0), pipeline_mode=pl.Buffered(1)),
                  pl.BlockSpec((None, D_MODEL, D_MODEL), lambda t: (sub_layer, 0, 0), pipeline_mode=pl.Buffered(1)),
                  pl.BlockSpec((WINDOW, 2 * KV_W), prev_map),
                  pl.BlockSpec((TILE, 2 * KV_W), lambda t: (t, 0)),
                  mem_spec, mem_spec,
                  pl.BlockSpec((SWA_KV_HEADS, SWA_REP * WINDOW, 2 * WINDOW), lambda t: (0, 0, 0)),
                  pl.BlockSpec((None, SWA_KV_HEADS, SWA_REP * WINDOW, LANES), lambda t: (sub_layer, 0, 0, 0))],
        out_specs=pl.BlockSpec((TILE, D_MODEL), lambda t: (t, 0)),
        out_shape=jax.ShapeDtypeStruct((batch * seq, D_MODEL), F32),
        scratch_shapes=[pltpu.VMEM((TILE, D_MODEL), BF16), pltpu.VMEM((TILE, D_MODEL), BF16),
                        pltpu.VMEM((TILE, D_MODEL), BF16)],
        compiler_params=_cparams(1),
        name="fused_mixer_b",
    )(x, x, g.reshape(g.shape[0], 1, D_MODEL), w_in, w_out, kv, kv, mem_k, mem_v, bias, sink_rows)


def _ret_decode_kernel(proj_ref, st_ref, *rest, bb):
    o_ref, nst_ref = rest[-2:]
    scale = RET_HD ** -0.5
    row = lax.broadcasted_iota(jnp.int32, (bb, bb * RET_HD), 0)
    lane_seq = lax.shift_right_logical(lax.broadcasted_iota(jnp.int32, (bb, bb * RET_HD), 1), 7)
    own = row == lane_seq
    for h in range(RET_HEADS):
        lo = h * RET_HD
        q8 = proj_ref[:, lo:lo + RET_HD]
        k8 = proj_ref[:, MIX_MAIN + lo:MIX_MAIN + lo + RET_HD] * scale
        v8 = proj_ref[:, 2 * MIX_MAIN + lo:2 * MIX_MAIN + lo + RET_HD]
        gate = proj_ref[:, 3 * MIX_MAIN + lo:3 * MIX_MAIN + lo + RET_HD]
        g = float(np.exp(_LOG_G[h]))
        vexp = jnp.where(own, jnp.concatenate([v8] * bb, axis=1), 0.0).astype(BF16)
        outer = lax.dot_general(k8.astype(BF16), vexp, TN_DIMS, preferred_element_type=F32)
        cols = []
        for j in range(bb):
            s_new = g * st_ref[j, h] + outer[:, j * RET_HD:(j + 1) * RET_HD]
            nst_ref[j, h] = s_new
            cols.append(s_new.astype(BF16))
        o_all = jnp.dot(q8.astype(BF16), jnp.concatenate(cols, axis=1), preferred_element_type=F32)
        o_all = jnp.where(own, o_all, 0.0)
        o = o_all[:, :RET_HD]
        for j in range(1, bb):
            o = o + o_all[:, j * RET_HD:(j + 1) * RET_HD]
        mu = jnp.mean(o, axis=-1, keepdims=True)
        oc = o - mu
        var = jnp.mean(oc * oc, axis=-1, keepdims=True)
        on = oc * lax.rsqrt(var + EPS)
        o_ref[:, lo:lo + RET_HD] = gate * jax.nn.sigmoid(gate) * on


def ret_decode(proj, state, layer, stacked_out=None, bb=16):
    nb = proj.shape[0]
    pw = proj.shape[1]
    in_specs = [pl.BlockSpec((bb, pw), lambda i: (i, 0)),
                pl.BlockSpec((None, bb, RET_HEADS, RET_HD, RET_HD), lambda i: (layer, i, 0, 0, 0))]
    args = [proj, state]
    aliases = {}
    if stacked_out is not None:
        in_specs.append(pl.BlockSpec(memory_space=pl.ANY))
        args.append(stacked_out)
        aliases = {2: 1}
    return pl.pallas_call(
        functools.partial(_ret_decode_kernel, bb=bb),
        grid=(nb // bb,),
        in_specs=in_specs,
        out_specs=[pl.BlockSpec((bb, MIX_MAIN), lambda i: (i, 0)),
                   pl.BlockSpec((None, bb, RET_HEADS, RET_HD, RET_HD), lambda i: (layer, i, 0, 0, 0))],
        out_shape=[jax.ShapeDtypeStruct((nb, MIX_MAIN), F32),
                   jax.ShapeDtypeStruct(state.shape, F32)],
        input_output_aliases=aliases,
        compiler_params=_cparams(1),
        name="ret_decode",
    )(*args)


def _mem_decode_kernel(q_ref, kt_ref, vt_ref, o_ref, *, bb, q_off, q_scale):
    row = lax.broadcasted_iota(jnp.int32, (8, MEM_W), 0)
    lane_head = _lane_block64(lax.broadcasted_iota(jnp.int32, (8, MEM_W), 1))
    sel = row == lane_head
    seqs = range(bb)
    qexp = [jnp.where(sel, jnp.broadcast_to(q_ref[j:j + 1, q_off:q_off + MEM_W] * q_scale, (8, MEM_W)), 0.0)
            for j in seqs]
    s = [jnp.dot(qexp[j], kt_ref[j], preferred_element_type=F32) for j in seqs]
    e = [jnp.exp(s[j] - jnp.max(s[j], axis=-1, keepdims=True)) for j in seqs]
    den = [jnp.sum(e[j], axis=-1, keepdims=True) for j in seqs]
    o = [lax.dot_general(e[j], vt_ref[j], NT_DIMS, preferred_element_type=F32) / den[j] for j in seqs]
    o_ref[...] = jnp.concatenate([jnp.sum(jnp.where(sel, o[j], 0.0), axis=0, keepdims=True) for j in seqs], axis=0)


def mem_decode(proj, cache_kt, cache_vt, layer, q_off, q_scale, bb=32):
    nb = proj.shape[0]
    pw = proj.shape[1]
    cmap = lambda i: (layer, i, 0, 0)
    return pl.pallas_call(
        functools.partial(_mem_decode_kernel, bb=bb, q_off=q_off, q_scale=q_scale),
        grid=(nb // bb,),
        in_specs=[pl.BlockSpec((bb, pw), lambda i: (i, 0)),
                  pl.BlockSpec((None, bb, MEM_W, MEM_TOKENS), cmap),
                  pl.BlockSpec((None, bb, MEM_W, MEM_TOKENS), cmap)],
        out_specs=pl.BlockSpec((bb, MEM_W), lambda i: (i, 0)),
        out_shape=jax.ShapeDtypeStruct((nb, MEM_W), F32),
        compiler_params=_cparams(1),
        name="mem_decode",
    )(proj, cache_kt, cache_vt)


def _swa_decode_kernel(q_ref, kn_ref, vn_ref, ckt_ref, cvt_ref, slope_ref, sink_ref, o_ref, *cache_out, bb):
    row = lax.broadcasted_iota(jnp.int32, (DEC_ROWS, KV_W), 0)
    lane_group = _lane_block64(lax.broadcasted_iota(jnp.int32, (DEC_ROWS, KV_W), 1))
    dist = (WINDOW - lax.broadcasted_iota(jnp.int32, (DEC_ROWS, WINDOW), 1)).astype(F32)
    bias = -slope_ref[...] * dist
    sink = sink_ref[...]
    own = [row == SWA_KV_HEADS * r + lane_group for r in range(SWA_REP)]
    seqs = range(bb)

    def expand(j):
        qexp = jnp.zeros((DEC_ROWS, KV_W), F32)
        for r in range(SWA_REP):
            qr = jnp.broadcast_to(q_ref[j:j + 1, r * KV_W:(r + 1) * KV_W], (DEC_ROWS, KV_W))
            qexp = jnp.where(own[r], qr, qexp)
        return qexp

    qexp = [expand(j) for j in seqs]
    s = [jnp.dot(qexp[j], ckt_ref[j], preferred_element_type=F32) + bias for j in seqs]
    s_new = [jnp.sum(qexp[j] * kn_ref[j:j + 1, :], axis=-1, keepdims=True) for j in seqs]
    m = [jnp.maximum(jnp.maximum(jnp.max(s[j], axis=-1, keepdims=True), s_new[j]), sink) for j in seqs]
    e = [jnp.exp(s[j] - m[j]) for j in seqs]
    e_new = [jnp.exp(s_new[j] - m[j]) for j in seqs]
    inv = [1.0 / (jnp.sum(e[j], axis=-1, keepdims=True) + e_new[j] + jnp.exp(sink - m[j])) for j in seqs]
    o = [lax.dot_general(e[j], cvt_ref[j], NT_DIMS, preferred_element_type=F32) for j in seqs]
    o = [(o[j] + jnp.concatenate([e_new[j], e_new[j]], axis=1) * vn_ref[j:j + 1, :])
         * jnp.concatenate([inv[j], inv[j]], axis=1) for j in seqs]
    for r in range(SWA_REP):
        o_ref[:, r * KV_W:(r + 1) * KV_W] = jnp.concatenate(
            [jnp.sum(jnp.where(own[r], o[j], 0.0), axis=0, keepdims=True) for j in seqs], axis=0)
    if cache_out:
        nkt_ref, nvt_ref = cache_out
        last = lax.broadcasted_iota(jnp.int32, (KV_W, WINDOW), 1) == WINDOW - 1
        knt = kn_ref[...].T
        vnt = vn_ref[...].T
        for j in seqs:
            nkt_ref[j] = jnp.where(last, jnp.broadcast_to(knt[:, j:j + 1], (KV_W, WINDOW)),
                                   pltpu.roll(ckt_ref[j], WINDOW - 1, 1))
            nvt_ref[j] = jnp.where(last, jnp.broadcast_to(vnt[:, j:j + 1], (KV_W, WINDOW)),
                                   pltpu.roll(cvt_ref[j], WINDOW - 1, 1))


def swa_decode(qproj, k_new, v_new, cache_kt, cache_vt, slope_rows, sink_rows, sink_layer, emit_cache, bb=32):
    nb = qproj.shape[0]
    pw = qproj.shape[1]
    cache_spec = pl.BlockSpec((bb, KV_W, WINDOW), lambda i: (i, 0, 0))
    cache_shape = jax.ShapeDtypeStruct((nb, KV_W, WINDOW), F32)
    n_cache = 2 if emit_cache else 0
    return pl.pallas_call(
        functools.partial(_swa_decode_kernel, bb=bb),
        grid=(nb // bb,),
        in_specs=[pl.BlockSpec((bb, pw), lambda i: (i, 0)),
                  pl.BlockSpec((bb, KV_W), lambda i: (i, 0)),
                  pl.BlockSpec((bb, KV_W), lambda i: (i, 0)),
                  cache_spec, cache_spec,
                  pl.BlockSpec((DEC_ROWS, LANES), lambda i: (0, 0)),
                  pl.BlockSpec((None, DEC_ROWS, LANES), lambda i: (sink_layer, 0, 0))],
        out_specs=[pl.BlockSpec((bb, MIX_MAIN), lambda i: (i, 0))] + [cache_spec] * n_cache,
        out_shape=[jax.ShapeDtypeStruct((nb, MIX_MAIN), F32)] + [cache_shape] * n_cache,
        compiler_params=_cparams(1),
        name="swa_decode",
    )(qproj, k_new, v_new, cache_kt, cache_vt, slope_rows, sink_rows)


def _permute_q_cols(w):
    lead = w.shape[:-1]
    return w.reshape(*lead, SWA_KV_HEADS, SWA_REP, SWA_HD).swapaxes(-3, -2).reshape(*lead, MIX_MAIN)


def kernel(x_prompt, x_sample, cache_mem_k, cache_mem_v, state_ret, cache_swa_k, cache_swa_v, mem_prompt,
           norm_mix, w_in_a, w_out_a, w_in_b, w_out_b, attn_sinks, norm_mem, w_mem_kv, norm_kv, w_kv,
           norm_mlp, w_up, w_down, norm_final):
    batch, seq, d = x_prompt.shape
    nb = x_sample.shape[0]
    tm = 512

    w_in_b16 = (jnp.concatenate([_permute_q_cols(w_in_b[:, :, :MIX_MAIN]), w_in_b[:, :, MIX_MAIN:]], axis=-1)
                * ATTN_SCALE).astype(BF16)
    w_out_b16 = jnp.concatenate(
        [_permute_q_cols(w_out_b[:, :MIX_MAIN, :].swapaxes(1, 2)).swapaxes(1, 2), w_out_b[:, MIX_MAIN:, :]],
        axis=1).astype(BF16)
    w_kv16 = w_kv.astype(BF16).reshape(1, d, 2 * KV_W)
    norm_kv1 = norm_kv.reshape(1, d)

    sinks_gr = attn_sinks.reshape(N_B, SWA_KV_HEADS, SWA_REP)
    sink_prompt = jnp.broadcast_to(sinks_gr[:, :, :, None, None], (N_B, SWA_KV_HEADS, SWA_REP, WINDOW, LANES)
                                   ).reshape(N_B, SWA_KV_HEADS, SWA_REP * WINDOW, LANES)
    sink_dec = jnp.concatenate([sinks_gr.swapaxes(1, 2).reshape(N_B, SWA_HEADS),
                                jnp.zeros((N_B, DEC_ROWS - SWA_HEADS), F32)], axis=1)
    sink_dec = jnp.broadcast_to(sink_dec[:, :, None], (N_B, DEC_ROWS, LANES))
    slope_dec = np.zeros((DEC_ROWS, LANES), np.float32)
    for r in range(SWA_REP):
        for g in range(SWA_KV_HEADS):
            slope_dec[SWA_KV_HEADS * r + g, :] = _SLOPES[g * SWA_REP + r]
    slope_dec = jnp.asarray(slope_dec)

    memx = mem_prompt.reshape(batch * MEM_TOKENS, d)
    mem_kv = mem_kv_proj(memx, norm_mem, w_mem_kv).reshape(DEPTH, batch, MEM_TOKENS, 2 * MEM_W)
    mem_k_prompt = mem_kv[..., :MEM_W].reshape(DEPTH, batch, MEM_TOKENS, MEM_HEADS, MEM_HD)
    mem_v_prompt = mem_kv[..., MEM_W:].reshape(DEPTH, batch, MEM_TOKENS, MEM_HEADS, MEM_HD)
    mk16 = mem_kv[..., :MEM_W].astype(BF16)
    mv16 = mem_kv[..., MEM_W:].astype(BF16)

    x = x_prompt.reshape(batch * seq, d)
    ret_states = []
    kv_p = kv_p16 = None
    for l in range(DEPTH):
        if l < N_A:
            x, st = fused_mixer_a(x, norm_mix, w_in_a, w_out_a, mk16, mv16, l, batch, seq)
            ret_states.append(st)
        else:
            j = l - N_A
            x = fused_mixer_b(x, norm_mix, w_in_b16, w_out_b16, kv_p16, mk16, mv16, l, sink_prompt, j, batch, seq)
        if l == N_A - 1:
            x, kv_p, kv_p16 = mlp_block(x, norm_mlp, w_up, w_down, l, norm_final, False, tm=tm,
                                        kv_proj=(norm_kv1, w_kv16[0]))
        else:
            x = mlp_block(x, norm_mlp, w_up, w_down, l, norm_final, l == DEPTH - 1, tm=tm)
    y_prompt = x.reshape(batch, seq, d)
    ret_prompt = jnp.stack(ret_states)
    kv_p3 = kv_p.reshape(batch, seq, 2 * KV_W)
    swa_k_prompt = kv_p3[:, -WINDOW:, :KV_W].reshape(batch, WINDOW, SWA_KV_HEADS, SWA_HD)
    swa_v_prompt = kv_p3[:, -WINDOW:, KV_W:].reshape(batch, WINDOW, SWA_KV_HEADS, SWA_HD)

    cmkt = jnp.transpose(cache_mem_k, (0, 1, 3, 4, 2)).reshape(DEPTH, nb, MEM_W, MEM_TOKENS)
    cmvt = jnp.transpose(cache_mem_v, (0, 1, 3, 4, 2)).reshape(DEPTH, nb, MEM_W, MEM_TOKENS)
    cskt = jnp.transpose(cache_swa_k, (0, 2, 3, 1)).reshape(nb, KV_W, WINDOW)
    csvt = jnp.transpose(cache_swa_v, (0, 2, 3, 1)).reshape(nb, KV_W, WINDOW)
    xs = x_sample.reshape(nb, d)
    ret_sample = None
    k_new = v_new = nkt = nvt = None
    for l in range(DEPTH):
        if l < N_A:
            proj = norm_matmul(xs, norm_mix, w_in_a, l, F32, tm=nb, n_chunk=MIX_MAIN)
            o_ret, ret_sample = ret_decode(proj, state_ret, l, stacked_out=ret_sample)
            o_mem = mem_decode(proj, cmkt, cmvt, l, 4 * MIX_MAIN, ATTN_SCALE)
            attn = jnp.concatenate([o_ret, o_mem], axis=-1).astype(BF16)
            w_out, out_layer = w_out_a, l
        else:
            j = l - N_A
            if j == 0:
                kv_s = norm_matmul(xs, norm_kv1, w_kv16, 0, F32, tm=nb)
                k_new, v_new = kv_s[:, :KV_W], kv_s[:, KV_W:]
            qproj = norm_matmul(xs, norm_mix[N_A:], w_in_b16, j, F32, tm=nb)
            res = swa_decode(qproj, k_new, v_new, cskt, csvt, slope_dec, sink_dec, j, emit_cache=(j == 0))
            o_swa = res[0]
            if j == 0:
                nkt, nvt = res[1], res[2]
            o_mem = mem_decode(qproj, cmkt, cmvt, l, MIX_MAIN, 1.0)
            attn = jnp.concatenate([o_swa, o_mem], axis=-1).astype(BF16)
            w_out, out_layer = w_out_b16, j
        xs = post_decode(attn, w_out, out_layer, xs, norm_mlp, w_up, w_down, l, norm_final, l == DEPTH - 1)
    y_sample = xs.reshape(nb, 1, d)
    swa_k_sample = jnp.transpose(nkt.reshape(nb, SWA_KV_HEADS, SWA_HD, WINDOW), (0, 3, 1, 2))
    swa_v_sample = jnp.transpose(nvt.reshape(nb, SWA_KV_HEADS, SWA_HD, WINDOW), (0, 3, 1, 2))

    return (y_prompt, y_sample, ret_prompt, ret_sample, swa_k_prompt, swa_v_prompt, swa_k_sample, swa_v_sample,
            mem_k_prompt, mem_v_prompt)
```

```python
import functools
import math

import jax
import jax.numpy as jnp
import numpy as np
from jax import lax
from jax.experimental import pallas as pl
from jax.experimental.pallas import tpu as pltpu

F32 = jnp.float32
BF16 = jnp.bfloat16

D_MODEL = 1024
DEPTH = 4
N_A = 2
N_B = 2
MEM_TOKENS = 256
MEM_HEADS = 4
MEM_W = 256
MEM_HD = 64
MIX_MAIN = 768
RET_HEADS = 6
RET_HD = 128
CHUNK = 128
SWA_HEADS = 12
SWA_KV_HEADS = 4
SWA_REP = SWA_HEADS // SWA_KV_HEADS
SWA_HD = 64
KV_W = SWA_KV_HEADS * SWA_HD
WINDOW = 128
D_FF = 4096
EPS = 1e-6
NEG = -1e30
ATTN_SCALE = 0.125
LANES = 128
TILE = 512
DEC_ROWS = 16

VMEM_LIMIT = 56 * 1024 * 1024

NT_DIMS = (((1,), (1,)), ((), ()))
TN_DIMS = (((0,), (0,)), ((), ()))


def _alibi_slopes(n):
    def pow2(m):
        start = 2.0 ** (-8.0 / m)
        return [start ** (i + 1) for i in range(m)]

    if math.log2(n).is_integer():
        s = pow2(n)
    else:
        c = 2 ** int(math.floor(math.log2(n)))
        s = pow2(c) + pow2(2 * c)[0::2][: n - c]
    return np.asarray(s, np.float32)


_LOG_G = np.log1p(-(2.0 ** (-5.0 - np.arange(RET_HEADS)))).astype(np.float32).astype(np.float64)
_SLOPES = _alibi_slopes(SWA_HEADS).astype(np.float64)


def _retention_consts():
    idx = np.arange(CHUNK, dtype=np.float64)
    diff = idx[:, None] - idx[None, :]
    scale = RET_HD ** -0.5
    decay = np.where(diff >= 0, np.exp(np.maximum(diff, 0.0)[None] * _LOG_G[:, None, None]), 0.0) * scale
    w_q = np.exp((idx + 1.0)[None, :] * _LOG_G[:, None])
    w_k = np.exp((CHUNK - 1.0 - idx)[None, :] * _LOG_G[:, None]) * scale
    w_q = np.broadcast_to(w_q[:, :, None], (RET_HEADS, CHUNK, RET_HD))
    w_k = np.broadcast_to(w_k[:, :, None], (RET_HEADS, CHUNK, RET_HD))
    g_c = np.exp(CHUNK * _LOG_G)
    pair = lambda t: np.ascontiguousarray(np.concatenate([t[0::2], t[1::2]], axis=2), np.float32)
    return pair(decay), pair(w_q), pair(w_k), [float(v) for v in g_c]


def _swa_bias():
    i = np.arange(WINDOW)[:, None]
    j = np.arange(2 * WINDOW)[None, :]
    dist = i + WINDOW - j
    valid = (dist >= 0) & (dist <= WINDOW)
    out = np.zeros((SWA_KV_HEADS, SWA_REP * WINDOW, 2 * WINDOW), np.float32)
    for g in range(SWA_KV_HEADS):
        for r in range(SWA_REP):
            b = np.where(valid, -_SLOPES[g * SWA_REP + r] * dist, NEG)
            out[g, r * WINDOW:(r + 1) * WINDOW] = b
    return out


def _rmsnorm(x, g):
    ms = jnp.mean(x * x, axis=-1, keepdims=True)
    return (x * lax.rsqrt(ms + EPS)) * g


def _lane_block64(lane):
    return lax.shift_right_logical(lane, 6)


def _cparams(n_axes):
    return pltpu.CompilerParams(dimension_semantics=("arbitrary",) * n_axes, vmem_limit_bytes=VMEM_LIMIT)


def _layer_block(tail, layer, n_grid):
    zeros = (0,) * len(tail)
    if n_grid == 1:
        return pl.BlockSpec((None,) + tuple(tail), lambda i: (layer,) + zeros)
    return pl.BlockSpec((None,) + tuple(tail), lambda i, j: (layer,) + zeros)


def _norm_matmul_kernel(x_ref, g_ref, w_ref, o_ref, *, n_chunk):
    xn = _rmsnorm(x_ref[...], g_ref[...]).astype(BF16)
    n = w_ref.shape[1]
    for n0 in range(0, n, n_chunk):
        n1 = min(n0 + n_chunk, n)
        o_ref[:, n0:n1] = jnp.dot(xn, w_ref[:, n0:n1].astype(BF16), preferred_element_type=F32).astype(o_ref.dtype)


def norm_matmul(x, g, w, layer, out_dtype, tm, n_chunk=512):
    m, k = x.shape
    n = w.shape[2]
    return pl.pallas_call(
        functools.partial(_norm_matmul_kernel, n_chunk=n_chunk),
        grid=(m // tm,),
        in_specs=[pl.BlockSpec((tm, k), lambda i: (i, 0)),
                  _layer_block((1, k), layer, 1),
                  _layer_block((k, n), layer, 1)],
        out_specs=pl.BlockSpec((tm, n), lambda i: (i, 0)),
        out_shape=jax.ShapeDtypeStruct((m, n), out_dtype),
        compiler_params=_cparams(1),
        name="norm_matmul",
    )(x, g.reshape(g.shape[0], 1, k), w)


def _mem_kv_kernel(x_ref, g_ref, w_ref, o_ref):
    xn = _rmsnorm(x_ref[...], g_ref[...]).astype(BF16)
    o_ref[...] = jnp.dot(xn, w_ref[...].astype(BF16), preferred_element_type=F32)


def mem_kv_proj(x, g, w):
    m, k = x.shape
    n_layers, _, n = w.shape
    return pl.pallas_call(
        _mem_kv_kernel,
        grid=(n_layers,),
        in_specs=[pl.BlockSpec((m, k), lambda l: (0, 0)),
                  pl.BlockSpec((None, 1, k), lambda l: (l, 0, 0)),
                  pl.BlockSpec((None, k, n), lambda l: (l, 0, 0))],
        out_specs=pl.BlockSpec((None, m, n), lambda l: (l, 0, 0)),
        out_shape=jax.ShapeDtypeStruct((n_layers, m, n), F32),
        compiler_params=_cparams(1),
        name="mem_kv_proj",
    )(x, g.reshape(n_layers, 1, k), w)


def _mlp_kernel(x_ref, g_ref, wu_ref, wd_ref, gf_ref, *rest, fc, final_norm, with_kv):
    if with_kv:
        gkv_ref, wkv_ref, o_ref, kv_ref, kv16_ref, act_ref = rest
    else:
        o_ref, act_ref = rest
    xn = _rmsnorm(x_ref[...], g_ref[...]).astype(BF16)
    for c0 in range(0, D_FF, fc):
        h = jnp.dot(xn, wu_ref[:, c0:c0 + fc].astype(BF16), preferred_element_type=F32)
        act_ref[:, c0:c0 + fc] = jnp.square(jnp.maximum(h, 0.0)).astype(BF16)
    y = x_ref[...] + jnp.dot(act_ref[...], wd_ref[...].astype(BF16), preferred_element_type=F32)
    if with_kv:
        kv = jnp.dot(_rmsnorm(y, gkv_ref[...]).astype(BF16), wkv_ref[...], preferred_element_type=F32)
        kv_ref[...] = kv
        kv16_ref[...] = kv.astype(BF16)
    if final_norm:
        y = _rmsnorm(y, gf_ref[...])
    o_ref[...] = y


def mlp_block(x, g, w_up, w_down, layer, g_final, final_norm, tm, kv_proj=None, fc=512):
    m, d = x.shape
    with_kv = kv_proj is not None
    in_specs = [pl.BlockSpec((tm, d), lambda i: (i, 0)),
                _layer_block((1, d), layer, 1),
                pl.BlockSpec((None, d, D_FF), lambda i: (layer, 0, 0), pipeline_mode=pl.Buffered(1)),
                pl.BlockSpec((None, D_FF, d), lambda i: (layer, 0, 0), pipeline_mode=pl.Buffered(1)),
                pl.BlockSpec((1, d), lambda i: (0, 0))]
    args = [x, g.reshape(g.shape[0], 1, d), w_up, w_down, g_final.reshape(1, d)]
    out_specs = [pl.BlockSpec((tm, d), lambda i: (i, 0))]
    out_shape = [jax.ShapeDtypeStruct((m, d), F32)]
    if with_kv:
        in_specs += [pl.BlockSpec((1, d), lambda i: (0, 0)),
                     pl.BlockSpec((d, 2 * KV_W), lambda i: (0, 0), pipeline_mode=pl.Buffered(1))]
        args += list(kv_proj)
        out_specs += [pl.BlockSpec((tm, 2 * KV_W), lambda i: (i, 0))] * 2
        out_shape += [jax.ShapeDtypeStruct((m, 2 * KV_W), F32), jax.ShapeDtypeStruct((m, 2 * KV_W), BF16)]
    res = pl.pallas_call(
        functools.partial(_mlp_kernel, fc=fc, final_norm=final_norm, with_kv=with_kv),
        grid=(m // tm,),
        in_specs=in_specs,
        out_specs=out_specs,
        out_shape=out_shape,
        scratch_shapes=[pltpu.VMEM((tm, D_FF), BF16)],
        compiler_params=_cparams(1),
        name="mlp_block",
    )(*args)
    return res if with_kv else res[0]


def _post_decode_kernel(a_ref, wo_ref, x_ref, g_ref, wu_ref, wd_ref, gf_ref, o_ref, x1_scr, xn_scr, acc_scr, *,
                        final_norm):
    k = pl.program_id(0)

    @pl.when(k == 0)
    def _():
        x1 = x_ref[...] + jnp.dot(a_ref[...], wo_ref[...].astype(BF16), preferred_element_type=F32)
        x1_scr[...] = x1
        xn_scr[...] = _rmsnorm(x1, g_ref[...]).astype(BF16)
        acc_scr[...] = jnp.zeros_like(acc_scr)

    h = jnp.dot(xn_scr[...], wu_ref[...].astype(BF16), preferred_element_type=F32)
    act = jnp.square(jnp.maximum(h, 0.0)).astype(BF16)
    acc_scr[...] += jnp.dot(act, wd_ref[...].astype(BF16), preferred_element_type=F32)

    @pl.when(k == pl.num_programs(0) - 1)
    def _():
        y = x1_scr[...] + acc_scr[...]
        if final_norm:
            y = _rmsnorm(y, gf_ref[...])
        o_ref[...] = y


def post_decode(attn, w_out, out_layer, x, g, w_up, w_down, layer, g_final, final_norm, fc=1024):
    m, d = x.shape
    return pl.pallas_call(
        functools.partial(_post_decode_kernel, final_norm=final_norm),
        grid=(D_FF // fc,),
        in_specs=[pl.BlockSpec((m, d), lambda k: (0, 0)),
                  _layer_block((d, d), out_layer, 1),
                  pl.BlockSpec((m, d), lambda k: (0, 0)),
                  _layer_block((1, d), layer, 1),
                  pl.BlockSpec((None, d, fc), lambda k: (layer, 0, k)),
                  pl.BlockSpec((None, fc, d), lambda k: (layer, k, 0)),
                  pl.BlockSpec((1, d), lambda k: (0, 0))],
        out_specs=pl.BlockSpec((m, d), lambda k: (0, 0)),
        out_shape=jax.ShapeDtypeStruct((m, d), F32),
        scratch_shapes=[pltpu.VMEM((m, d), F32), pltpu.VMEM((m, d), BF16), pltpu.VMEM((m, d), F32)],
        compiler_params=_cparams(1),
        name="post_decode",
    )(attn, w_out, x, g.reshape(g.shape[0], 1, d), w_up, w_down, g_final.reshape(1, d))


def _mem_attention_tile(qm, mk, mv, fillers=()):
    t = qm.shape[0]
    lane_head = _lane_block64(lax.broadcasted_iota(jnp.int32, (t, MEM_W), 1))
    tok_head = _lane_block64(lax.broadcasted_iota(jnp.int32, (MEM_TOKENS, MEM_W), 1))
    probs = []
    vals = []
    for h in range(MEM_HEADS):
        qh = jnp.where(lane_head == h, qm, jnp.zeros_like(qm))
        s = lax.dot_general(qh, mk, NT_DIMS, preferred_element_type=F32)
        if h < len(fillers):
            fillers[h]()
        m = jnp.max(s, axis=-1, keepdims=True)
        e = jnp.exp(s - m)
        inv = 1.0 / jnp.sum(e, axis=-1, keepdims=True)
        probs.append((e * inv).astype(BF16))
        vals.append(jnp.where(tok_head == h, mv, jnp.zeros_like(mv)))
    return jnp.dot(jnp.concatenate(probs, axis=1), jnp.concatenate(vals, axis=0), preferred_element_type=F32)


def _project_tile(x_ref, g_ref, w_ref, p_ref, n_chunk):
    xn = _rmsnorm(x_ref[...], g_ref[...]).astype(BF16)
    n = w_ref.shape[1]
    for n0 in range(0, n, n_chunk):
        n1 = min(n0 + n_chunk, n)
        p_ref[:, n0:n1] = jnp.dot(xn, w_ref[:, n0:n1].astype(BF16), preferred_element_type=F32).astype(BF16)


PIECE = 256


def _projection_pieces(x_ref, g_ref, w_ref, p_ref):
    xn = _rmsnorm(x_ref[...], g_ref[...]).astype(BF16)

    def piece(c):
        def run():
            cols = slice(c * PIECE, (c + 1) * PIECE)
            p_ref[:, cols] = jnp.dot(xn, w_ref[:, cols].astype(BF16), preferred_element_type=F32).astype(BF16)
        return run

    return [piece(c) for c in range(w_ref.shape[1] // PIECE)]


def _out_projection_piece(x_ref, a_ref, w_ref, o_ref, c):
    def run():
        cols = slice(c * PIECE, (c + 1) * PIECE)
        o_ref[:, cols] = x_ref[:, cols] + jnp.dot(a_ref[:, :MIX_MAIN], w_ref[:MIX_MAIN, cols].astype(BF16),
                                                  preferred_element_type=F32)
    return run


def _both(f, g):
    def run():
        f()
        g()
    return run


def _block_diag2(a, b):
    z = jnp.zeros_like(a)
    return jnp.concatenate([jnp.concatenate([a, z], axis=1), jnp.concatenate([z, b], axis=1)], axis=0)


def _group_norm_gate(o, gate):
    mu = jnp.mean(o, axis=-1, keepdims=True)
    oc = o - mu
    var = jnp.mean(oc * oc, axis=-1, keepdims=True)
    return (gate * jax.nn.sigmoid(gate) * (oc * lax.rsqrt(var + EPS))).astype(BF16)


def _retention_tile(p_ref, a_ref, s_scr, decay_ref, wq_ref, wk_ref, g_c, fillers=()):
    pw = 2 * RET_HD
    pairs = range(RET_HEADS // 2)
    for c in range(TILE // CHUNK):
        rows = slice(c * CHUNK, (c + 1) * CHUNK)
        q2 = [p_ref[rows, hp * pw:(hp + 1) * pw] for hp in pairs]
        k2 = [p_ref[rows, MIX_MAIN + hp * pw:MIX_MAIN + (hp + 1) * pw] for hp in pairs]
        v2 = [p_ref[rows, 2 * MIX_MAIN + hp * pw:2 * MIX_MAIN + (hp + 1) * pw] for hp in pairs]
        sc = [lax.dot_general(q2[hp], _block_diag2(k2[hp][:, :RET_HD], k2[hp][:, RET_HD:]), NT_DIMS,
                              preferred_element_type=F32) for hp in pairs]
        kw = [(k2[hp].astype(F32) * wk_ref[hp]).astype(BF16) for hp in pairs]
        kv = [lax.dot_general(kw[hp], v2[hp], TN_DIMS, preferred_element_type=F32) for hp in pairs]
        if 2 * c < len(fillers):
            fillers[2 * c]()
        s_old = [s_scr[h] for h in range(RET_HEADS)]
        lhs = [jnp.concatenate([(sc[hp] * decay_ref[hp]).astype(BF16),
                                (q2[hp].astype(F32) * wq_ref[hp]).astype(BF16)], axis=1) for hp in pairs]
        rhs = [jnp.concatenate([_block_diag2(v2[hp][:, :RET_HD], v2[hp][:, RET_HD:]),
                                _block_diag2(s_old[2 * hp].astype(BF16), s_old[2 * hp + 1].astype(BF16))], axis=0)
               for hp in pairs]
        o2 = [jnp.dot(lhs[hp], rhs[hp], preferred_element_type=F32) for hp in pairs]
        if 2 * c + 1 < len(fillers):
            fillers[2 * c + 1]()
        for hp in pairs:
            s_scr[2 * hp] = g_c[2 * hp] * s_old[2 * hp] + kv[hp][:RET_HD, :RET_HD]
            s_scr[2 * hp + 1] = g_c[2 * hp + 1] * s_old[2 * hp + 1] + kv[hp][RET_HD:, RET_HD:]
        for hp in pairs:
            lo = hp * pw
            gate2 = p_ref[rows, 3 * MIX_MAIN + lo:3 * MIX_MAIN + lo + pw].astype(F32)
            a_ref[rows, lo:lo + RET_HD] = _group_norm_gate(o2[hp][:, :RET_HD], gate2[:, :RET_HD])
            a_ref[rows, lo + RET_HD:lo + pw] = _group_norm_gate(o2[hp][:, RET_HD:], gate2[:, RET_HD:])


def _fused_a_kernel(xc_ref, xn_ref, g_ref, win_ref, wout_ref, mk_ref, mv_ref, decay_ref, wq_ref, wk_ref,
                    o_ref, st_ref, p0, p1, a_scr, s_scr, *, g_c, tiles_per_seq):
    t = pl.program_id(0)

    @pl.when(t == 0)
    def _():
        _project_tile(xc_ref, g_ref, win_ref, p0, MIX_MAIN)

    @pl.when(t % tiles_per_seq == 0)
    def _():
        s_scr[...] = jnp.zeros_like(s_scr)

    def step(p_cur, p_next):
        pieces = _projection_pieces(xn_ref, g_ref, win_ref, p_next)
        n_slots = 2 * (TILE // CHUNK)
        slots = [_both(pieces[c], pieces[n_slots + c]) if n_slots + c < len(pieces) else pieces[c]
                 for c in range(n_slots)]
        _retention_tile(p_cur, a_scr, s_scr, decay_ref, wq_ref, wk_ref, g_c, slots)
        st_ref[...] = s_scr[...]
        qm = p_cur[:, 4 * MIX_MAIN:4 * MIX_MAIN + MEM_W] * ATTN_SCALE
        out_piece = functools.partial(_out_projection_piece, xc_ref, a_scr, wout_ref, o_ref)
        a_mem = _mem_attention_tile(qm, mk_ref[...], mv_ref[...],
                                    [out_piece(c) for c in range(D_MODEL // PIECE)]).astype(BF16)
        o_ref[...] += jnp.dot(a_mem, wout_ref[MIX_MAIN:, :].astype(BF16), preferred_element_type=F32)

    @pl.when(t % 2 == 0)
    def _():
        step(p0, p1)

    @pl.when(t % 2 == 1)
    def _():
        step(p1, p0)


def fused_mixer_a(x, g, w_in, w_out, mem_k, mem_v, layer, batch, seq):
    n_tiles = batch * seq // TILE
    tps = seq // TILE
    decay, w_q, w_k, g_c = _retention_consts()
    pw = w_in.shape[2]
    c3 = lambda t: (0, 0, 0)
    mem_spec = pl.BlockSpec((None, None, MEM_TOKENS, MEM_W), lambda t: (layer, t // tps, 0, 0))
    return pl.pallas_call(
        functools.partial(_fused_a_kernel, g_c=g_c, tiles_per_seq=tps),
        grid=(n_tiles,),
        in_specs=[pl.BlockSpec((TILE, D_MODEL), lambda t: (t, 0)),
                  pl.BlockSpec((TILE, D_MODEL), lambda t: (jnp.minimum(t + 1, n_tiles - 1), 0)),
                  _layer_block((1, D_MODEL), layer, 1),
                  pl.BlockSpec((None, D_MODEL, pw), lambda t: (layer, 0, 0), pipeline_mode=pl.Buffered(1)),
                  pl.BlockSpec((None, D_MODEL, D_MODEL), lambda t: (layer, 0, 0), pipeline_mode=pl.Buffered(1)),
                  mem_spec, mem_spec,
                  pl.BlockSpec((RET_HEADS // 2, CHUNK, 2 * CHUNK), c3),
                  pl.BlockSpec((RET_HEADS // 2, CHUNK, 2 * RET_HD), c3),
                  pl.BlockSpec((RET_HEADS // 2, CHUNK, 2 * RET_HD), c3)],
        out_specs=[pl.BlockSpec((TILE, D_MODEL), lambda t: (t, 0)),
                   pl.BlockSpec((None, RET_HEADS, RET_HD, RET_HD), lambda t: (t // tps, 0, 0, 0))],
        out_shape=[jax.ShapeDtypeStruct((batch * seq, D_MODEL), F32),
                   jax.ShapeDtypeStruct((batch, RET_HEADS, RET_HD, RET_HD), F32)],
        scratch_shapes=[pltpu.VMEM((TILE, pw), BF16), pltpu.VMEM((TILE, pw), BF16),
                        pltpu.VMEM((TILE, D_MODEL), BF16),
                        pltpu.VMEM((RET_HEADS, RET_HD, RET_HD), F32)],
        compiler_params=_cparams(1),
        name="fused_mixer_a",
    )(x, x, g.reshape(g.shape[0], 1, D_MODEL), w_in, w_out, mem_k, mem_v,
      jnp.asarray(decay), jnp.asarray(w_q), jnp.asarray(w_k))


def _swa_tile(p_ref, kvp_ref, kvc_ref, a_ref, bias_ref, sink_ref, is_first, fillers=()):
    lane_group = _lane_block64(lax.broadcasted_iota(jnp.int32, (WINDOW, KV_W), 1))
    key_group = _lane_block64(lax.broadcasted_iota(jnp.int32, (2 * WINDOW, KV_W), 1))
    for i in range(TILE // WINDOW):
        r0 = i * WINDOW
        if i == 0:
            kv2 = jnp.concatenate([kvp_ref[...], kvc_ref[0:WINDOW, :]], axis=0)
        else:
            kv2 = kvc_ref[r0 - WINDOW:r0 + WINDOW, :]
        k2 = kv2[:, :KV_W]
        v2 = kv2[:, KV_W:]
        qs = [p_ref[r0:r0 + WINDOW, r * KV_W:(r + 1) * KV_W] for r in range(SWA_REP)]
        probs = []
        vals = []
        for g in range(SWA_KV_HEADS):
            sel = lane_group == g
            qg = jnp.concatenate([jnp.where(sel, qr, jnp.zeros_like(qr)) for qr in qs], axis=0)
            s = lax.dot_general(qg, k2, NT_DIMS, preferred_element_type=F32) + bias_ref[g]
            if g == 1 and i < len(fillers):
                fillers[i]()
            s_prev = s[:, :WINDOW]
            s_cur = s[:, WINDOW:]
            if i == 0:
                s_prev = jnp.where(is_first, NEG, s_prev)
            sink = sink_ref[g]
            m = jnp.maximum(jnp.max(jnp.maximum(s_prev, s_cur), axis=-1, keepdims=True), sink)
            e_prev = jnp.exp(s_prev - m)
            e_cur = jnp.exp(s_cur - m)
            inv = 1.0 / (jnp.sum(e_prev + e_cur, axis=-1, keepdims=True) + jnp.exp(sink - m))
            probs.append((e_prev * inv).astype(BF16))
            probs.append((e_cur * inv).astype(BF16))
            vals.append(jnp.where(key_group == g, v2, jnp.zeros_like(v2)))
        o = jnp.dot(jnp.concatenate(probs, axis=1), jnp.concatenate(vals, axis=0), preferred_element_type=F32)
        for r in range(SWA_REP):
            a_ref[r0:r0 + WINDOW, r * KV_W:(r + 1) * KV_W] = o[r * WINDOW:(r + 1) * WINDOW].astype(BF16)


def _fused_b_kernel(xc_ref, xn_ref, g_ref, win_ref, wout_ref, kvp_ref, kvc_ref, mk_ref, mv_ref, bias_ref, sink_ref,
                    o_ref, p0, p1, a_scr, *, tiles_per_seq):
    t = pl.program_id(0)
    is_first = t % tiles_per_seq == 0

    @pl.when(t == 0)
    def _():
        _project_tile(xc_ref, g_ref, win_ref, p0, 512)

    out_piece = functools.partial(_out_projection_piece, xc_ref, a_scr, wout_ref, o_ref)

    def step(p_cur, p_next):
        pieces = _projection_pieces(xn_ref, g_ref, win_ref, p_next)
        _swa_tile(p_cur, kvp_ref, kvc_ref, a_scr, bias_ref, sink_ref, is_first, pieces)
        qm = p_cur[:, MIX_MAIN:MIX_MAIN + MEM_W]
        a_mem = _mem_attention_tile(qm, mk_ref[...], mv_ref[...],
                                    [out_piece(c) for c in range(D_MODEL // PIECE)]).astype(BF16)
        o_ref[...] += jnp.dot(a_mem, wout_ref[MIX_MAIN:, :], preferred_element_type=F32)

    @pl.when(t % 2 == 0)
    def _():
        step(p0, p1)

    @pl.when(t % 2 == 1)
    def _():
        step(p1, p0)


def fused_mixer_b(x, g, w_in, w_out, kv, mem_k, mem_v, layer, sink_rows, sub_layer, batch, seq):
    n_tiles = batch * seq // TILE
    tps = seq // TILE
    nw = TILE // WINDOW
    bias = jnp.asarray(_swa_bias())
    mem_spec = pl.BlockSpec((None, None, MEM_TOKENS, MEM_W), lambda t: (layer, t // tps, 0, 0))

    def prev_map(t):
        return (jnp.maximum(t * nw - 1, (t // tps) * (seq // WINDOW)), 0)

    return pl.pallas_call(
        functools.partial(_fused_b_kernel, tiles_per_seq=tps),
        grid=(n_tiles,),
        in_specs=[pl.BlockSpec((TILE, D_MODEL), lambda t: (t, 0)),
                  pl.BlockSpec((TILE, D_MODEL), lambda t: (jnp.minimum(t + 1, n_tiles - 1), 0)),
                  _layer_block((1, D_MODEL), layer, 1),
                  pl.BlockSpec((None, D_MODEL, D_MODEL), lambda t: (sub_layer, 0, 0), pipeline_mode=pl.Buffered(1)),
                  pl.BlockSpec((None, D_MODEL, D_MODEL), lambda t: (sub_layer, 0, 0), pipeline_mode=pl.Buffered(1)),
                  pl.BlockSpec((WINDOW, 2 * KV_W), prev_map),
                  pl.BlockSpec((TILE, 2 * KV_W), lambda t: (t, 0)),
                  mem_spec, mem_spec,
                  pl.BlockSpec((SWA_KV_HEADS, SWA_REP * WINDOW, 2 * WINDOW), lambda t: (0, 0, 0)),
                  pl.BlockSpec((None, SWA_KV_HEADS, SWA_REP * WINDOW, LANES), lambda t: (sub_layer, 0, 0, 0))],
        out_specs=pl.BlockSpec((TILE, D_MODEL), lambda t: (t, 0)),
        out_shape=jax.ShapeDtypeStruct((batch * seq, D_MODEL), F32),
        scratch_shapes=[pltpu.VMEM((TILE, D_MODEL), BF16), pltpu.VMEM((TILE, D_MODEL), BF16),
                        pltpu.VMEM((TILE, D_MODEL), BF16)],
        compiler_params=_cparams(1),
        name="fused_mixer_b",
    )(x, x, g.reshape(g.shape[0], 1, D_MODEL), w_in, w_out, kv, kv, mem_k, mem_v, bias, sink_rows)


def _ret_decode_kernel(proj_ref, st_ref, *rest, bb):
    o_ref, nst_ref = rest[-2:]
    scale = RET_HD ** -0.5
    row = lax.broadcasted_iota(jnp.int32, (bb, bb * RET_HD), 0)
    lane_seq = lax.shift_right_logical(lax.broadcasted_iota(jnp.int32, (bb, bb * RET_HD), 1), 7)
    own = row == lane_seq
    for h in range(RET_HEADS):
        lo = h * RET_HD
        q8 = proj_ref[:, lo:lo + RET_HD]
        k8 = proj_ref[:, MIX_MAIN + lo:MIX_MAIN + lo + RET_HD] * scale
        v8 = proj_ref[:, 2 * MIX_MAIN + lo:2 * MIX_MAIN + lo + RET_HD]
        gate = proj_ref[:, 3 * MIX_MAIN + lo:3 * MIX_MAIN + lo + RET_HD]
        g = float(np.exp(_LOG_G[h]))
        vexp = jnp.where(own, jnp.concatenate([v8] * bb, axis=1), 0.0).astype(BF16)
        outer = lax.dot_general(k8.astype(BF16), vexp, TN_DIMS, preferred_element_type=F32)
        cols = []
        for j in range(bb):
            s_new = g * st_ref[j, h] + outer[:, j * RET_HD:(j + 1) * RET_HD]
            nst_ref[j, h] = s_new
            cols.append(s_new.astype(BF16))
        o_all = jnp.dot(q8.astype(BF16), jnp.concatenate(cols, axis=1), preferred_element_type=F32)
        o_all = jnp.where(own, o_all, 0.0)
        o = o_all[:, :RET_HD]
        for j in range(1, bb):
            o = o + o_all[:, j * RET_HD:(j + 1) * RET_HD]
        mu = jnp.mean(o, axis=-1, keepdims=True)
        oc = o - mu
        var = jnp.mean(oc * oc, axis=-1, keepdims=True)
        on = oc * lax.rsqrt(var + EPS)
        o_ref[:, lo:lo + RET_HD] = gate * jax.nn.sigmoid(gate) * on


def ret_decode(proj, state, layer, stacked_out=None, bb=16):
    nb = proj.shape[0]
    pw = proj.shape[1]
    in_specs = [pl.BlockSpec((bb, pw), lambda i: (i, 0)),
                pl.BlockSpec((None, bb, RET_HEADS, RET_HD, RET_HD), lambda i: (layer, i, 0, 0, 0))]
    args = [proj, state]
    aliases = {}
    if stacked_out is not None:
        in_specs.append(pl.BlockSpec(memory_space=pl.ANY))
        args.append(stacked_out)
        aliases = {2: 1}
    return pl.pallas_call(
        functools.partial(_ret_decode_kernel, bb=bb),
        grid=(nb // bb,),
        in_specs=in_specs,
        out_specs=[pl.BlockSpec((bb, MIX_MAIN), lambda i: (i, 0)),
                   pl.BlockSpec((None, bb, RET_HEADS, RET_HD, RET_HD), lambda i: (layer, i, 0, 0, 0))],
        out_shape=[jax.ShapeDtypeStruct((nb, MIX_MAIN), F32),
                   jax.ShapeDtypeStruct(state.shape, F32)],
        input_output_aliases=aliases,
        compiler_params=_cparams(1),
        name="ret_decode",
    )(*args)


def _mem_decode_kernel(q_ref, kt_ref, vt_ref, o_ref, *, bb, q_off, q_scale):
    row = lax.broadcasted_iota(jnp.int32, (8, MEM_W), 0)
    lane_head = _lane_block64(lax.broadcasted_iota(jnp.int32, (8, MEM_W), 1))
    sel = row == lane_head
    seqs = range(bb)
    qexp = [jnp.where(sel, jnp.broadcast_to(q_ref[j:j + 1, q_off:q_off + MEM_W] * q_scale, (8, MEM_W)), 0.0)
            for j in seqs]
    s = [jnp.dot(qexp[j], kt_ref[j], preferred_element_type=F32) for j in seqs]
    e = [jnp.exp(s[j] - jnp.max(s[j], axis=-1, keepdims=True)) for j in seqs]
    den = [jnp.sum(e[j], axis=-1, keepdims=True) for j in seqs]
    o = [lax.dot_general(e[j], vt_ref[j], NT_DIMS, preferred_element_type=F32) / den[j] for j in seqs]
    o_ref[...] = jnp.concatenate([jnp.sum(jnp.where(sel, o[j], 0.0), axis=0, keepdims=True) for j in seqs], axis=0)


def mem_decode(proj, cache_kt, cache_vt, layer, q_off, q_scale, bb=32):
    nb = proj.shape[0]
    pw = proj.shape[1]
    cmap = lambda i: (layer, i, 0, 0)
    return pl.pallas_call(
        functools.partial(_mem_decode_kernel, bb=bb, q_off=q_off, q_scale=q_scale),
        grid=(nb // bb,),
        in_specs=[pl.BlockSpec((bb, pw), lambda i: (i, 0)),
                  pl.BlockSpec((None, bb, MEM_W, MEM_TOKENS), cmap),
                  pl.BlockSpec((None, bb, MEM_W, MEM_TOKENS), cmap)],
        out_specs=pl.BlockSpec((bb, MEM_W), lambda i: (i, 0)),
        out_shape=jax.ShapeDtypeStruct((nb, MEM_W), F32),
        compiler_params=_cparams(1),
        name="mem_decode",
    )(proj, cache_kt, cache_vt)


def _swa_decode_kernel(q_ref, kn_ref, vn_ref, ckt_ref, cvt_ref, slope_ref, sink_ref, o_ref, *cache_out, bb):
    row = lax.broadcasted_iota(jnp.int32, (DEC_ROWS, KV_W), 0)
    lane_group = _lane_block64(lax.broadcasted_iota(jnp.int32, (DEC_ROWS, KV_W), 1))
    dist = (WINDOW - lax.broadcasted_iota(jnp.int32, (DEC_ROWS, WINDOW), 1)).astype(F32)
    bias = -slope_ref[...] * dist
    sink = sink_ref[...]
    own = [row == SWA_KV_HEADS * r + lane_group for r in range(SWA_REP)]
    seqs = range(bb)

    def expand(j):
        qexp = jnp.zeros((DEC_ROWS, KV_W), F32)
        for r in range(SWA_REP):
            qr = jnp.broadcast_to(q_ref[j:j + 1, r * KV_W:(r + 1) * KV_W], (DEC_ROWS, KV_W))
            qexp = jnp.where(own[r], qr, qexp)
        return qexp

    qexp = [expand(j) for j in seqs]
    s = [jnp.dot(qexp[j], ckt_ref[j], preferred_element_type=F32) + bias for j in seqs]
    s_new = [jnp.sum(qexp[j] * kn_ref[j:j + 1, :], axis=-1, keepdims=True) for j in seqs]
    m = [jnp.maximum(jnp.maximum(jnp.max(s[j], axis=-1, keepdims=True), s_new[j]), sink) for j in seqs]
    e = [jnp.exp(s[j] - m[j]) for j in seqs]
    e_new = [jnp.exp(s_new[j] - m[j]) for j in seqs]
    inv = [1.0 / (jnp.sum(e[j], axis=-1, keepdims=True) + e_new[j] + jnp.exp(sink - m[j])) for j in seqs]
    o = [lax.dot_general(e[j], cvt_ref[j], NT_DIMS, preferred_element_type=F32) for j in seqs]
    o = [(o[j] + jnp.concatenate([e_new[j], e_new[j]], axis=1) * vn_ref[j:j + 1, :])
         * jnp.concatenate([inv[j], inv[j]], axis=1) for j in seqs]
    for r in range(SWA_REP):
        o_ref[:, r * KV_W:(r + 1) * KV_W] = jnp.concatenate(
            [jnp.sum(jnp.where(own[r], o[j], 0.0), axis=0, keepdims=True) for j in seqs], axis=0)
    if cache_out:
        nkt_ref, nvt_ref = cache_out
        last = lax.broadcasted_iota(jnp.int32, (KV_W, WINDOW), 1) == WINDOW - 1
        knt = kn_ref[...].T
        vnt = vn_ref[...].T
        for j in seqs:
            nkt_ref[j] = jnp.where(last, jnp.broadcast_to(knt[:, j:j + 1], (KV_W, WINDOW)),
                                   pltpu.roll(ckt_ref[j], WINDOW - 1, 1))
            nvt_ref[j] = jnp.where(last, jnp.broadcast_to(vnt[:, j:j + 1], (KV_W, WINDOW)),
                                   pltpu.roll(cvt_ref[j], WINDOW - 1, 1))


def swa_decode(qproj, k_new, v_new, cache_kt, cache_vt, slope_rows, sink_rows, sink_layer, emit_cache, bb=32):
    nb = qproj.shape[0]
    pw = qproj.shape[1]
    cache_spec = pl.BlockSpec((bb, KV_W, WINDOW), lambda i: (i, 0, 0))
    cache_shape = jax.ShapeDtypeStruct((nb, KV_W, WINDOW), F32)
    n_cache = 2 if emit_cache else 0
    return pl.pallas_call(
        functools.partial(_swa_decode_kernel, bb=bb),
        grid=(nb // bb,),
        in_specs=[pl.BlockSpec((bb, pw), lambda i: (i, 0)),
                  pl.BlockSpec((bb, KV_W), lambda i: (i, 0)),
                  pl.BlockSpec((bb, KV_W), lambda i: (i, 0)),
                  cache_spec, cache_spec,
                  pl.BlockSpec((DEC_ROWS, LANES), lambda i: (0, 0)),
                  pl.BlockSpec((None, DEC_ROWS, LANES), lambda i: (sink_layer, 0, 0))],
        out_specs=[pl.BlockSpec((bb, MIX_MAIN), lambda i: (i, 0))] + [cache_spec] * n_cache,
        out_shape=[jax.ShapeDtypeStruct((nb, MIX_MAIN), F32)] + [cache_shape] * n_cache,
        compiler_params=_cparams(1),
        name="swa_decode",
    )(qproj, k_new, v_new, cache_kt, cache_vt, slope_rows, sink_rows)


def _slot_source(t):
    r, g = divmod(t, SWA_KV_HEADS)
    return SWA_REP * g + r


def _prep_w_in_b_kernel(w_ref, o_ref):
    low = lax.broadcasted_iota(jnp.int32, (D_MODEL, LANES), 1) < SWA_HD

    def half(slot, want_high):
        col = w_ref[:, (slot // 2) * LANES:(slot // 2 + 1) * LANES]
        return pltpu.roll(col, SWA_HD, 1) if (slot % 2 == 1) != want_high else col

    for c in range(MIX_MAIN // LANES):
        tile = jnp.where(low, half(_slot_source(2 * c), False), half(_slot_source(2 * c + 1), True))
        o_ref[:, c * LANES:(c + 1) * LANES] = (tile * ATTN_SCALE).astype(BF16)
    o_ref[:, MIX_MAIN:] = (w_ref[:, MIX_MAIN:] * ATTN_SCALE).astype(BF16)


def _prep_w_out_b_kernel(w_ref, o_ref):
    for t in range(SWA_HEADS):
        s = _slot_source(t)
        o_ref[t * SWA_HD:(t + 1) * SWA_HD, :] = w_ref[s * SWA_HD:(s + 1) * SWA_HD, :].astype(BF16)
    o_ref[MIX_MAIN:, :] = w_ref[MIX_MAIN:, :].astype(BF16)


def _prep_layer_b_weights(w, body, name):
    n_layers = w.shape[0]
    spec = pl.BlockSpec((None, D_MODEL, D_MODEL), lambda l: (l, 0, 0))
    return pl.pallas_call(
        body, grid=(n_layers,), in_specs=[spec], out_specs=spec,
        out_shape=jax.ShapeDtypeStruct(w.shape, BF16), compiler_params=_cparams(1), name=name,
    )(w)


def kernel(x_prompt, x_sample, cache_mem_k, cache_mem_v, state_ret, cache_swa_k, cache_swa_v, mem_prompt,
           norm_mix, w_in_a, w_out_a, w_in_b, w_out_b, attn_sinks, norm_mem, w_mem_kv, norm_kv, w_kv,
           norm_mlp, w_up, w_down, norm_final):
    batch, seq, d = x_prompt.shape
    nb = x_sample.shape[0]
    tm = 512

    w_in_b16 = _prep_layer_b_weights(w_in_b, _prep_w_in_b_kernel, "prep_w_in_b")
    w_out_b16 = _prep_layer_b_weights(w_out_b, _prep_w_out_b_kernel, "prep_w_out_b")
    w_kv16 = w_kv.astype(BF16).reshape(1, d, 2 * KV_W)
    norm_kv1 = norm_kv.reshape(1, d)

    sinks_gr = attn_sinks.reshape(N_B, SWA_KV_HEADS, SWA_REP)
    sink_prompt = jnp.broadcast_to(sinks_gr[:, :, :, None, None], (N_B, SWA_KV_HEADS, SWA_REP, WINDOW, LANES)
                                   ).reshape(N_B, SWA_KV_HEADS, SWA_REP * WINDOW, LANES)
    sink_dec = jnp.concatenate([sinks_gr.swapaxes(1, 2).reshape(N_B, SWA_HEADS),
                                jnp.zeros((N_B, DEC_ROWS - SWA_HEADS), F32)], axis=1)
    sink_dec = jnp.broadcast_to(sink_dec[:, :, None], (N_B, DEC_ROWS, LANES))
    slope_dec = np.zeros((DEC_ROWS, LANES), np.float32)
    for r in range(SWA_REP):
        for g in range(SWA_KV_HEADS):
            slope_dec[SWA_KV_HEADS * r + g, :] = _SLOPES[g * SWA_REP + r]
    slope_dec = jnp.asarray(slope_dec)

    memx = mem_prompt.reshape(batch * MEM_TOKENS, d)
    mem_kv = mem_kv_proj(memx, norm_mem, w_mem_kv).reshape(DEPTH, batch, MEM_TOKENS, 2 * MEM_W)
    mem_k_prompt = mem_kv[..., :MEM_W].reshape(DEPTH, batch, MEM_TOKENS, MEM_HEADS, MEM_HD)
    mem_v_prompt = mem_kv[..., MEM_W:].reshape(DEPTH, batch, MEM_TOKENS, MEM_HEADS, MEM_HD)
    mk16 = mem_kv[..., :MEM_W].astype(BF16)
    mv16 = mem_kv[..., MEM_W:].astype(BF16)

    x = x_prompt.reshape(batch * seq, d)
    ret_states = []
    kv_p = kv_p16 = None
    for l in range(DEPTH):
        if l < N_A:
            x, st = fused_mixer_a(x, norm_mix, w_in_a, w_out_a, mk16, mv16, l, batch, seq)
            ret_states.append(st)
        else:
            j = l - N_A
            x = fused_mixer_b(x, norm_mix, w_in_b16, w_out_b16, kv_p16, mk16, mv16, l, sink_prompt, j, batch, seq)
        if l == N_A - 1:
            x, kv_p, kv_p16 = mlp_block(x, norm_mlp, w_up, w_down, l, norm_final, False, tm=tm,
                                        kv_proj=(norm_kv1, w_kv16[0]))
        else:
            x = mlp_block(x, norm_mlp, w_up, w_down, l, norm_final, l == DEPTH - 1, tm=tm)
    y_prompt = x.reshape(batch, seq, d)
    ret_prompt = jnp.stack(ret_states)
    kv_p3 = kv_p.reshape(batch, seq, 2 * KV_W)
    swa_k_prompt = kv_p3[:, -WINDOW:, :KV_W].reshape(batch, WINDOW, SWA_KV_HEADS, SWA_HD)
    swa_v_prompt = kv_p3[:, -WINDOW:, KV_W:].reshape(batch, WINDOW, SWA_KV_HEADS, SWA_HD)

    cmkt = jnp.transpose(cache_mem_k, (0, 1, 3, 4, 2)).reshape(DEPTH, nb, MEM_W, MEM_TOKENS)
    cmvt = jnp.transpose(cache_mem_v, (0, 1, 3, 4, 2)).reshape(DEPTH, nb, MEM_W, MEM_TOKENS)
    cskt = jnp.transpose(cache_swa_k, (0, 2, 3, 1)).reshape(nb, KV_W, WINDOW)
    csvt = jnp.transpose(cache_swa_v, (0, 2, 3, 1)).reshape(nb, KV_W, WINDOW)
    xs = x_sample.reshape(nb, d)
    ret_sample = None
    k_new = v_new = nkt = nvt = None
    for l in range(DEPTH):
        if l < N_A:
            proj = norm_matmul(xs, norm_mix, w_in_a, l, F32, tm=nb, n_chunk=MIX_MAIN)
            o_ret, ret_sample = ret_decode(proj, state_ret, l, stacked_out=ret_sample)
            o_mem = mem_decode(proj, cmkt, cmvt, l, 4 * MIX_MAIN, ATTN_SCALE)
            attn = jnp.concatenate([o_ret, o_mem], axis=-1).astype(BF16)
            w_out, out_layer = w_out_a, l
        else:
            j = l - N_A
            if j == 0:
                kv_s = norm_matmul(xs, norm_kv1, w_kv16, 0, F32, tm=nb)
                k_new, v_new = kv_s[:, :KV_W], kv_s[:, KV_W:]
            qproj = norm_matmul(xs, norm_mix[N_A:], w_in_b16, j, F32, tm=nb)
            res = swa_decode(qproj, k_new, v_new, cskt, csvt, slope_dec, sink_dec, j, emit_cache=(j == 0))
            o_swa = res[0]
            if j == 0:
                nkt, nvt = res[1], res[2]
            o_mem = mem_decode(qproj, cmkt, cmvt, l, MIX_MAIN, 1.0)
            attn = jnp.concatenate([o_swa, o_mem], axis=-1).astype(BF16)
            w_out, out_layer = w_out_b16, j
        xs = post_decode(attn, w_out, out_layer, xs, norm_mlp, w_up, w_down, l, norm_final, l == DEPTH - 1)
    y_sample = xs.reshape(nb, 1, d)
    swa_k_sample = jnp.transpose(nkt.reshape(nb, SWA_KV_HEADS, SWA_HD, WINDOW), (0, 3, 1, 2))
    swa_v_sample = jnp.transpose(nvt.reshape(nb, SWA_KV_HEADS, SWA_HD, WINDOW), (0, 3, 1, 2))

    return (y_prompt, y_sample, ret_prompt, ret_sample, swa_k_prompt, swa_v_prompt, swa_k_sample, swa_v_sample,
            mem_k_prompt, mem_v_prompt)
```

```python
import functools
import math

import jax
import jax.numpy as jnp
import numpy as np
from jax import lax
from jax.experimental import pallas as pl
from jax.experimental.pallas import tpu as pltpu

F32 = jnp.float32
BF16 = jnp.bfloat16

D_MODEL = 1024
DEPTH = 4
N_A = 2
N_B = 2
MEM_TOKENS = 256
MEM_HEADS = 4
MEM_W = 256
MEM_HD = 64
MIX_MAIN = 768
RET_HEADS = 6
RET_HD = 128
CHUNK = 128
SWA_HEADS = 12
SWA_KV_HEADS = 4
SWA_REP = SWA_HEADS // SWA_KV_HEADS
SWA_HD = 64
KV_W = SWA_KV_HEADS * SWA_HD
WINDOW = 128
D_FF = 4096
EPS = 1e-6
NEG = -1e30
ATTN_SCALE = 0.125
LANES = 128
SUBLANES = 8
PIECE = 256
TILE = 512
DEC_ROWS = 16

VMEM_LIMIT = 56 * 1024 * 1024

NT_DIMS = (((1,), (1,)), ((), ()))
TN_DIMS = (((0,), (0,)), ((), ()))


def _alibi_slopes(n):
    def pow2(m):
        start = 2.0 ** (-8.0 / m)
        return [start ** (i + 1) for i in range(m)]

    if math.log2(n).is_integer():
        s = pow2(n)
    else:
        c = 2 ** int(math.floor(math.log2(n)))
        s = pow2(c) + pow2(2 * c)[0::2][: n - c]
    return np.asarray(s, np.float32)


_LOG_G = np.log1p(-(2.0 ** (-5.0 - np.arange(RET_HEADS)))).astype(np.float32).astype(np.float64)
_SLOPES = _alibi_slopes(SWA_HEADS).astype(np.float64)


def _retention_consts():
    idx = np.arange(CHUNK, dtype=np.float64)
    diff = idx[:, None] - idx[None, :]
    scale = RET_HD ** -0.5
    decay = np.where(diff >= 0, np.exp(np.maximum(diff, 0.0)[None] * _LOG_G[:, None, None]), 0.0) * scale
    w_q = np.exp((idx + 1.0)[None, :] * _LOG_G[:, None])
    w_k = np.exp((CHUNK - 1.0 - idx)[None, :] * _LOG_G[:, None]) * scale
    w_q = np.broadcast_to(w_q[:, :, None], (RET_HEADS, CHUNK, RET_HD))
    w_k = np.broadcast_to(w_k[:, :, None], (RET_HEADS, CHUNK, RET_HD))
    g_c = np.exp(CHUNK * _LOG_G)
    pair = lambda t: np.ascontiguousarray(np.concatenate([t[0::2], t[1::2]], axis=2), np.float32)
    return pair(decay), pair(w_q), pair(w_k), [float(v) for v in g_c]


def _swa_bias():
    i = np.arange(WINDOW)[:, None]
    j = np.arange(2 * WINDOW)[None, :]
    dist = i + WINDOW - j
    valid = (dist >= 0) & (dist <= WINDOW)
    out = np.zeros((SWA_KV_HEADS, SWA_REP * WINDOW, 2 * WINDOW), np.float32)
    for g in range(SWA_KV_HEADS):
        for r in range(SWA_REP):
            b = np.where(valid, -_SLOPES[g * SWA_REP + r] * dist, NEG)
            out[g, r * WINDOW:(r + 1) * WINDOW] = b
    return out


def _rmsnorm(x, g):
    ms = jnp.mean(x * x, axis=-1, keepdims=True)
    return (x * lax.rsqrt(ms + EPS)) * g


def _lane_block64(lane):
    return lax.shift_right_logical(lane, 6)


def _cparams(n_axes):
    return pltpu.CompilerParams(dimension_semantics=("arbitrary",) * n_axes, vmem_limit_bytes=VMEM_LIMIT)


def _layer_block(tail, layer, n_grid):
    zeros = (0,) * len(tail)
    if n_grid == 1:
        return pl.BlockSpec((None,) + tuple(tail), lambda i: (layer,) + zeros)
    return pl.BlockSpec((None,) + tuple(tail), lambda i, j: (layer,) + zeros)


def _norm_matmul_kernel(x_ref, g_ref, w_ref, o_ref, *, n_chunk):
    xn = _rmsnorm(x_ref[...], g_ref[...]).astype(BF16)
    n = w_ref.shape[1]
    for n0 in range(0, n, n_chunk):
        n1 = min(n0 + n_chunk, n)
        o_ref[:, n0:n1] = jnp.dot(xn, w_ref[:, n0:n1].astype(BF16), preferred_element_type=F32).astype(o_ref.dtype)


def norm_matmul(x, g, w, layer, out_dtype, tm, n_chunk=512):
    m, k = x.shape
    n = w.shape[2]
    return pl.pallas_call(
        functools.partial(_norm_matmul_kernel, n_chunk=n_chunk),
        grid=(m // tm,),
        in_specs=[pl.BlockSpec((tm, k), lambda i: (i, 0)),
                  _layer_block((1, k), layer, 1),
                  _layer_block((k, n), layer, 1)],
        out_specs=pl.BlockSpec((tm, n), lambda i: (i, 0)),
        out_shape=jax.ShapeDtypeStruct((m, n), out_dtype),
        compiler_params=_cparams(1),
        name="norm_matmul",
    )(x, g.reshape(g.shape[0], 1, k), w)


def _mem_kv_kernel(x_ref, g_ref, w_ref, o_ref, k16_ref, v16_ref):
    xn = _rmsnorm(x_ref[...], g_ref[...]).astype(BF16)
    kv = jnp.dot(xn, w_ref[...].astype(BF16), preferred_element_type=F32)
    o_ref[...] = kv
    k16_ref[...] = kv[:, :MEM_W].astype(BF16)
    v16_ref[...] = kv[:, MEM_W:].astype(BF16)


def mem_kv_proj(x, g, w):
    m, k = x.shape
    n_layers, _, n = w.shape
    half = pl.BlockSpec((None, m, MEM_W), lambda l: (l, 0, 0))
    return pl.pallas_call(
        _mem_kv_kernel,
        grid=(n_layers,),
        in_specs=[pl.BlockSpec((m, k), lambda l: (0, 0)),
                  pl.BlockSpec((None, 1, k), lambda l: (l, 0, 0)),
                  pl.BlockSpec((None, k, n), lambda l: (l, 0, 0))],
        out_specs=[pl.BlockSpec((None, m, n), lambda l: (l, 0, 0)), half, half],
        out_shape=[jax.ShapeDtypeStruct((n_layers, m, n), F32),
                   jax.ShapeDtypeStruct((n_layers, m, MEM_W), BF16),
                   jax.ShapeDtypeStruct((n_layers, m, MEM_W), BF16)],
        compiler_params=_cparams(1),
        name="mem_kv_proj",
    )(x, g.reshape(n_layers, 1, k), w)


def _mlp_kernel(x_ref, g_ref, wu_ref, wd_ref, gf_ref, *rest, fc, final_norm, with_kv):
    if with_kv:
        gkv_ref, wkv_ref, o_ref, kv_ref, kv16_ref, act_ref = rest
    else:
        o_ref, act_ref = rest
    xn = _rmsnorm(x_ref[...], g_ref[...]).astype(BF16)
    for c0 in range(0, D_FF, fc):
        h = jnp.dot(xn, wu_ref[:, c0:c0 + fc].astype(BF16), preferred_element_type=F32)
        act_ref[:, c0:c0 + fc] = jnp.square(jnp.maximum(h, 0.0)).astype(BF16)
    y = x_ref[...] + jnp.dot(act_ref[...], wd_ref[...].astype(BF16), preferred_element_type=F32)
    if with_kv:
        kv = jnp.dot(_rmsnorm(y, gkv_ref[...]).astype(BF16), wkv_ref[...], preferred_element_type=F32)
        kv_ref[...] = kv
        kv16_ref[...] = kv.astype(BF16)
    if final_norm:
        y = _rmsnorm(y, gf_ref[...])
    o_ref[...] = y


def mlp_block(x, g, w_up, w_down, layer, g_final, final_norm, tm, kv_proj=None, fc=512):
    m, d = x.shape
    with_kv = kv_proj is not None
    in_specs = [pl.BlockSpec((tm, d), lambda i: (i, 0)),
                _layer_block((1, d), layer, 1),
                pl.BlockSpec((None, d, D_FF), lambda i: (layer, 0, 0), pipeline_mode=pl.Buffered(1)),
                pl.BlockSpec((None, D_FF, d), lambda i: (layer, 0, 0), pipeline_mode=pl.Buffered(1)),
                pl.BlockSpec((1, d), lambda i: (0, 0))]
    args = [x, g.reshape(g.shape[0], 1, d), w_up, w_down, g_final.reshape(1, d)]
    out_specs = [pl.BlockSpec((tm, d), lambda i: (i, 0))]
    out_shape = [jax.ShapeDtypeStruct((m, d), F32)]
    if with_kv:
        in_specs += [pl.BlockSpec((1, d), lambda i: (0, 0)),
                     pl.BlockSpec((d, 2 * KV_W), lambda i: (0, 0), pipeline_mode=pl.Buffered(1))]
        args += list(kv_proj)
        out_specs += [pl.BlockSpec((tm, 2 * KV_W), lambda i: (i, 0))] * 2
        out_shape += [jax.ShapeDtypeStruct((m, 2 * KV_W), F32), jax.ShapeDtypeStruct((m, 2 * KV_W), BF16)]
    res = pl.pallas_call(
        functools.partial(_mlp_kernel, fc=fc, final_norm=final_norm, with_kv=with_kv),
        grid=(m // tm,),
        in_specs=in_specs,
        out_specs=out_specs,
        out_shape=out_shape,
        scratch_shapes=[pltpu.VMEM((tm, D_FF), BF16)],
        compiler_params=_cparams(1),
        name="mlp_block",
    )(*args)
    return res if with_kv else res[0]


def _post_decode_kernel(a_ref, wo_ref, x_ref, g_ref, wu_ref, wd_ref, gf_ref, o_ref, x1_scr, xn_scr, acc_scr, *,
                        final_norm):
    k = pl.program_id(0)

    @pl.when(k == 0)
    def _():
        x1 = x_ref[...] + jnp.dot(a_ref[...], wo_ref[...].astype(BF16), preferred_element_type=F32)
        x1_scr[...] = x1
        xn_scr[...] = _rmsnorm(x1, g_ref[...]).astype(BF16)
        acc_scr[...] = jnp.zeros_like(acc_scr)

    h = jnp.dot(xn_scr[...], wu_ref[...].astype(BF16), preferred_element_type=F32)
    act = jnp.square(jnp.maximum(h, 0.0)).astype(BF16)
    acc_scr[...] += jnp.dot(act, wd_ref[...].astype(BF16), preferred_element_type=F32)

    @pl.when(k == pl.num_programs(0) - 1)
    def _():
        y = x1_scr[...] + acc_scr[...]
        if final_norm:
            y = _rmsnorm(y, gf_ref[...])
        o_ref[...] = y


def post_decode(attn, w_out, out_layer, x, g, w_up, w_down, layer, g_final, final_norm, fc=1024):
    m, d = x.shape
    return pl.pallas_call(
        functools.partial(_post_decode_kernel, final_norm=final_norm),
        grid=(D_FF // fc,),
        in_specs=[pl.BlockSpec((m, d), lambda k: (0, 0)),
                  _layer_block((d, d), out_layer, 1),
                  pl.BlockSpec((m, d), lambda k: (0, 0)),
                  _layer_block((1, d), layer, 1),
                  pl.BlockSpec((None, d, fc), lambda k: (layer, 0, k)),
                  pl.BlockSpec((None, fc, d), lambda k: (layer, k, 0)),
                  pl.BlockSpec((1, d), lambda k: (0, 0))],
        out_specs=pl.BlockSpec((m, d), lambda k: (0, 0)),
        out_shape=jax.ShapeDtypeStruct((m, d), F32),
        scratch_shapes=[pltpu.VMEM((m, d), F32), pltpu.VMEM((m, d), BF16), pltpu.VMEM((m, d), F32)],
        compiler_params=_cparams(1),
        name="post_decode",
    )(attn, w_out, x, g.reshape(g.shape[0], 1, d), w_up, w_down, g_final.reshape(1, d))


def _mem_attention_tile(qm, mk, mv, fillers=()):
    t = qm.shape[0]
    lane_head = _lane_block64(lax.broadcasted_iota(jnp.int32, (t, MEM_W), 1))
    tok_head = _lane_block64(lax.broadcasted_iota(jnp.int32, (MEM_TOKENS, MEM_W), 1))
    probs = []
    vals = []
    for h in range(MEM_HEADS):
        qh = jnp.where(lane_head == h, qm, jnp.zeros_like(qm))
        s = lax.dot_general(qh, mk, NT_DIMS, preferred_element_type=F32)
        if h < len(fillers):
            fillers[h]()
        m = jnp.max(s, axis=-1, keepdims=True)
        e = jnp.exp(s - m)
        inv = 1.0 / jnp.sum(e, axis=-1, keepdims=True)
        probs.append((e * inv).astype(BF16))
        vals.append(jnp.where(tok_head == h, mv, jnp.zeros_like(mv)))
    return jnp.dot(jnp.concatenate(probs, axis=1), jnp.concatenate(vals, axis=0), preferred_element_type=F32)


def _project_tile(x_ref, g_ref, w_ref, p_ref, n_chunk):
    xn = _rmsnorm(x_ref[...], g_ref[...]).astype(BF16)
    n = w_ref.shape[1]
    for n0 in range(0, n, n_chunk):
        n1 = min(n0 + n_chunk, n)
        p_ref[:, n0:n1] = jnp.dot(xn, w_ref[:, n0:n1].astype(BF16), preferred_element_type=F32).astype(BF16)


def _projection_pieces(x_ref, g_ref, w_ref, p_ref):
    xn = _rmsnorm(x_ref[...], g_ref[...]).astype(BF16)

    def piece(c):
        def run():
            cols = slice(c * PIECE, (c + 1) * PIECE)
            p_ref[:, cols] = jnp.dot(xn, w_ref[:, cols].astype(BF16), preferred_element_type=F32).astype(BF16)
        return run

    return [piece(c) for c in range(w_ref.shape[1] // PIECE)]


def _out_projection_piece(x_ref, a_ref, w_ref, o_ref, c):
    def run():
        cols = slice(c * PIECE, (c + 1) * PIECE)
        o_ref[:, cols] = x_ref[:, cols] + jnp.dot(a_ref[:, :MIX_MAIN], w_ref[:MIX_MAIN, cols].astype(BF16),
                                                  preferred_element_type=F32)
    return run


def _both(f, g):
    def run():
        f()
        g()
    return run


def _block_diag2(a, b):
    z = jnp.zeros_like(a)
    return jnp.concatenate([jnp.concatenate([a, z], axis=1), jnp.concatenate([z, b], axis=1)], axis=0)


def _group_norm_gate(o, gate):
    mu = jnp.mean(o, axis=-1, keepdims=True)
    oc = o - mu
    var = jnp.mean(oc * oc, axis=-1, keepdims=True)
    return (gate * jax.nn.sigmoid(gate) * (oc * lax.rsqrt(var + EPS))).astype(BF16)


def _retention_tile(p_ref, a_ref, s_scr, decay_ref, wq_ref, wk_ref, g_c, fillers=()):
    pw = 2 * RET_HD
    pairs = range(RET_HEADS // 2)
    for c in range(TILE // CHUNK):
        rows = slice(c * CHUNK, (c + 1) * CHUNK)
        q2 = [p_ref[rows, hp * pw:(hp + 1) * pw] for hp in pairs]
        k2 = [p_ref[rows, MIX_MAIN + hp * pw:MIX_MAIN + (hp + 1) * pw] for hp in pairs]
        v2 = [p_ref[rows, 2 * MIX_MAIN + hp * pw:2 * MIX_MAIN + (hp + 1) * pw] for hp in pairs]
        sc = [lax.dot_general(q2[hp], _block_diag2(k2[hp][:, :RET_HD], k2[hp][:, RET_HD:]), NT_DIMS,
                              preferred_element_type=F32) for hp in pairs]
        kw = [(k2[hp].astype(F32) * wk_ref[hp]).astype(BF16) for hp in pairs]
        kv = [lax.dot_general(kw[hp], v2[hp], TN_DIMS, preferred_element_type=F32) for hp in pairs]
        if 2 * c < len(fillers):
            fillers[2 * c]()
        s_old = [s_scr[h] for h in range(RET_HEADS)]
        lhs = [jnp.concatenate([(sc[hp] * decay_ref[hp]).astype(BF16),
                                (q2[hp].astype(F32) * wq_ref[hp]).astype(BF16)], axis=1) for hp in pairs]
        rhs = [jnp.concatenate([_block_diag2(v2[hp][:, :RET_HD], v2[hp][:, RET_HD:]),
                                _block_diag2(s_old[2 * hp].astype(BF16), s_old[2 * hp + 1].astype(BF16))], axis=0)
               for hp in pairs]
        o2 = [jnp.dot(lhs[hp], rhs[hp], preferred_element_type=F32) for hp in pairs]
        if 2 * c + 1 < len(fillers):
            fillers[2 * c + 1]()
        for hp in pairs:
            s_scr[2 * hp] = g_c[2 * hp] * s_old[2 * hp] + kv[hp][:RET_HD, :RET_HD]
            s_scr[2 * hp + 1] = g_c[2 * hp + 1] * s_old[2 * hp + 1] + kv[hp][RET_HD:, RET_HD:]
        for hp in pairs:
            lo = hp * pw
            gate2 = p_ref[rows, 3 * MIX_MAIN + lo:3 * MIX_MAIN + lo + pw].astype(F32)
            a_ref[rows, lo:lo + RET_HD] = _group_norm_gate(o2[hp][:, :RET_HD], gate2[:, :RET_HD])
            a_ref[rows, lo + RET_HD:lo + pw] = _group_norm_gate(o2[hp][:, RET_HD:], gate2[:, RET_HD:])


def _fused_a_kernel(xc_ref, xn_ref, g_ref, win_ref, wout_ref, mk_ref, mv_ref, decay_ref, wq_ref, wk_ref,
                    o_ref, st_ref, p0, p1, a_scr, s_scr, *, g_c, tiles_per_seq):
    t = pl.program_id(0)

    @pl.when(t == 0)
    def _():
        _project_tile(xc_ref, g_ref, win_ref, p0, MIX_MAIN)

    @pl.when(t % tiles_per_seq == 0)
    def _():
        s_scr[...] = jnp.zeros_like(s_scr)

    def step(p_cur, p_next):
        pieces = _projection_pieces(xn_ref, g_ref, win_ref, p_next)
        n_slots = 2 * (TILE // CHUNK)
        slots = [_both(pieces[c], pieces[n_slots + c]) if n_slots + c < len(pieces) else pieces[c]
                 for c in range(n_slots)]
        _retention_tile(p_cur, a_scr, s_scr, decay_ref, wq_ref, wk_ref, g_c, slots)
        st_ref[...] = s_scr[...]
        qm = p_cur[:, 4 * MIX_MAIN:4 * MIX_MAIN + MEM_W] * ATTN_SCALE
        out_piece = functools.partial(_out_projection_piece, xc_ref, a_scr, wout_ref, o_ref)
        a_mem = _mem_attention_tile(qm, mk_ref[...], mv_ref[...],
                                    [out_piece(c) for c in range(D_MODEL // PIECE)]).astype(BF16)
        o_ref[...] += jnp.dot(a_mem, wout_ref[MIX_MAIN:, :].astype(BF16), preferred_element_type=F32)

    @pl.when(t % 2 == 0)
    def _():
        step(p0, p1)

    @pl.when(t % 2 == 1)
    def _():
        step(p1, p0)


def fused_mixer_a(x, g, w_in, w_out, mem_k, mem_v, layer, batch, seq):
    n_tiles = batch * seq // TILE
    tps = seq // TILE
    decay, w_q, w_k, g_c = _retention_consts()
    pw = w_in.shape[2]
    c3 = lambda t: (0, 0, 0)
    mem_spec = pl.BlockSpec((None, None, MEM_TOKENS, MEM_W), lambda t: (layer, t // tps, 0, 0))
    return pl.pallas_call(
        functools.partial(_fused_a_kernel, g_c=g_c, tiles_per_seq=tps),
        grid=(n_tiles,),
        in_specs=[pl.BlockSpec((TILE, D_MODEL), lambda t: (t, 0)),
                  pl.BlockSpec((TILE, D_MODEL), lambda t: (jnp.minimum(t + 1, n_tiles - 1), 0)),
                  _layer_block((1, D_MODEL), layer, 1),
                  pl.BlockSpec((None, D_MODEL, pw), lambda t: (layer, 0, 0), pipeline_mode=pl.Buffered(1)),
                  pl.BlockSpec((None, D_MODEL, D_MODEL), lambda t: (layer, 0, 0), pipeline_mode=pl.Buffered(1)),
                  mem_spec, mem_spec,
                  pl.BlockSpec((RET_HEADS // 2, CHUNK, 2 * CHUNK), c3),
                  pl.BlockSpec((RET_HEADS // 2, CHUNK, 2 * RET_HD), c3),
                  pl.BlockSpec((RET_HEADS // 2, CHUNK, 2 * RET_HD), c3)],
        out_specs=[pl.BlockSpec((TILE, D_MODEL), lambda t: (t, 0)),
                   pl.BlockSpec((None, RET_HEADS, RET_HD, RET_HD), lambda t: (t // tps, 0, 0, 0))],
        out_shape=[jax.ShapeDtypeStruct((batch * seq, D_MODEL), F32),
                   jax.ShapeDtypeStruct((batch, RET_HEADS, RET_HD, RET_HD), F32)],
        scratch_shapes=[pltpu.VMEM((TILE, pw), BF16), pltpu.VMEM((TILE, pw), BF16),
                        pltpu.VMEM((TILE, D_MODEL), BF16),
                        pltpu.VMEM((RET_HEADS, RET_HD, RET_HD), F32)],
        compiler_params=_cparams(1),
        name="fused_mixer_a",
    )(x, x, g.reshape(g.shape[0], 1, D_MODEL), w_in, w_out, mem_k, mem_v,
      jnp.asarray(decay), jnp.asarray(w_q), jnp.asarray(w_k))


def _swa_tile(p_ref, kvp_ref, kvc_ref, a_ref, bias_ref, sink_ref, is_first, fillers=()):
    lane_group = _lane_block64(lax.broadcasted_iota(jnp.int32, (WINDOW, KV_W), 1))
    key_group = _lane_block64(lax.broadcasted_iota(jnp.int32, (2 * WINDOW, KV_W), 1))
    for i in range(TILE // WINDOW):
        r0 = i * WINDOW
        if i == 0:
            kv2 = jnp.concatenate([kvp_ref[...], kvc_ref[0:WINDOW, :]], axis=0)
        else:
            kv2 = kvc_ref[r0 - WINDOW:r0 + WINDOW, :]
        k2 = kv2[:, :KV_W]
        v2 = kv2[:, KV_W:]
        qs = [p_ref[r0:r0 + WINDOW, r * KV_W:(r + 1) * KV_W] for r in range(SWA_REP)]
        probs = []
        vals = []
        for g in range(SWA_KV_HEADS):
            sel = lane_group == g
            qg = jnp.concatenate([jnp.where(sel, qr, jnp.zeros_like(qr)) for qr in qs], axis=0)
            s = lax.dot_general(qg, k2, NT_DIMS, preferred_element_type=F32) + bias_ref[g]
            if g == 1 and i < len(fillers):
                fillers[i]()
            s_prev = s[:, :WINDOW]
            s_cur = s[:, WINDOW:]
            if i == 0:
                s_prev = jnp.where(is_first, NEG, s_prev)
            sink = sink_ref[g]
            m = jnp.maximum(jnp.max(jnp.maximum(s_prev, s_cur), axis=-1, keepdims=True), sink)
            e_prev = jnp.exp(s_prev - m)
            e_cur = jnp.exp(s_cur - m)
            inv = 1.0 / (jnp.sum(e_prev + e_cur, axis=-1, keepdims=True) + jnp.exp(sink - m))
            probs.append((e_prev * inv).astype(BF16))
            probs.append((e_cur * inv).astype(BF16))
            vals.append(jnp.where(key_group == g, v2, jnp.zeros_like(v2)))
        o = jnp.dot(jnp.concatenate(probs, axis=1), jnp.concatenate(vals, axis=0), preferred_element_type=F32)
        for r in range(SWA_REP):
            a_ref[r0:r0 + WINDOW, r * KV_W:(r + 1) * KV_W] = o[r * WINDOW:(r + 1) * WINDOW].astype(BF16)


def _fused_b_kernel(xc_ref, xn_ref, g_ref, win_ref, wout_ref, kvp_ref, kvc_ref, mk_ref, mv_ref, bias_ref, sink_ref,
                    o_ref, p0, p1, a_scr, *, tiles_per_seq):
    t = pl.program_id(0)
    is_first = t % tiles_per_seq == 0

    @pl.when(t == 0)
    def _():
        _project_tile(xc_ref, g_ref, win_ref, p0, 2 * PIECE)

    out_piece = functools.partial(_out_projection_piece, xc_ref, a_scr, wout_ref, o_ref)

    def step(p_cur, p_next):
        pieces = _projection_pieces(xn_ref, g_ref, win_ref, p_next)
        _swa_tile(p_cur, kvp_ref, kvc_ref, a_scr, bias_ref, sink_ref, is_first, pieces)
        qm = p_cur[:, MIX_MAIN:MIX_MAIN + MEM_W]
        a_mem = _mem_attention_tile(qm, mk_ref[...], mv_ref[...],
                                    [out_piece(c) for c in range(D_MODEL // PIECE)]).astype(BF16)
        o_ref[...] += jnp.dot(a_mem, wout_ref[MIX_MAIN:, :], preferred_element_type=F32)

    @pl.when(t % 2 == 0)
    def _():
        step(p0, p1)

    @pl.when(t % 2 == 1)
    def _():
        step(p1, p0)


def fused_mixer_b(x, g, w_in, w_out, kv, mem_k, mem_v, layer, sink_rows, sub_layer, batch, seq):
    n_tiles = batch * seq // TILE
    tps = seq // TILE
    nw = TILE // WINDOW
    bias = jnp.asarray(_swa_bias())
    mem_spec = pl.BlockSpec((None, None, MEM_TOKENS, MEM_W), lambda t: (layer, t // tps, 0, 0))

    def prev_map(t):
        return (jnp.maximum(t * nw - 1, (t // tps) * (seq // WINDOW)), 0)

    return pl.pallas_call(
        functools.partial(_fused_b_kernel, tiles_per_seq=tps),
        grid=(n_tiles,),
        in_specs=[pl.BlockSpec((TILE, D_MODEL), lambda t: (t, 0)),
                  pl.BlockSpec((TILE, D_MODEL), lambda t: (jnp.minimum(t + 1, n_tiles - 1), 0)),
                  _layer_block((1, D_MODEL), layer, 1),
                  pl.BlockSpec((None, D_MODEL, D_MODEL), lambda t: (sub_layer, 0, 0), pipeline_mode=pl.Buffered(1)),
                  pl.BlockSpec((None, D_MODEL, D_MODEL), lambda t: (sub_layer, 0, 0), pipeline_mode=pl.Buffered(1)),
                  pl.BlockSpec((WINDOW, 2 * KV_W), prev_map),
                  pl.BlockSpec((TILE, 2 * KV_W), lambda t: (t, 0)),
                  mem_spec, mem_spec,
                  pl.BlockSpec((SWA_KV_HEADS, SWA_REP * WINDOW, 2 * WINDOW), lambda t: (0, 0, 0)),
                  pl.BlockSpec((None, SWA_KV_HEADS, SWA_REP * WINDOW, LANES), lambda t: (sub_layer, 0, 0, 0))],
        out_specs=pl.BlockSpec((TILE, D_MODEL), lambda t: (t, 0)),
        out_shape=jax.ShapeDtypeStruct((batch * seq, D_MODEL), F32),
        scratch_shapes=[pltpu.VMEM((TILE, D_MODEL), BF16), pltpu.VMEM((TILE, D_MODEL), BF16),
                        pltpu.VMEM((TILE, D_MODEL), BF16)],
        compiler_params=_cparams(1),
        name="fused_mixer_b",
    )(x, x, g.reshape(g.shape[0], 1, D_MODEL), w_in, w_out, kv, kv, mem_k, mem_v, bias, sink_rows)


def _ret_decode_kernel(proj_ref, st_ref, *rest, bb):
    o_ref, nst_ref = rest[-2:]
    scale = RET_HD ** -0.5
    row = lax.broadcasted_iota(jnp.int32, (bb, bb * RET_HD), 0)
    lane_seq = lax.shift_right_logical(lax.broadcasted_iota(jnp.int32, (bb, bb * RET_HD), 1), 7)
    own = row == lane_seq
    for h in range(RET_HEADS):
        lo = h * RET_HD
        q8 = proj_ref[:, lo:lo + RET_HD]
        k8 = proj_ref[:, MIX_MAIN + lo:MIX_MAIN + lo + RET_HD] * scale
        v8 = proj_ref[:, 2 * MIX_MAIN + lo:2 * MIX_MAIN + lo + RET_HD]
        gate = proj_ref[:, 3 * MIX_MAIN + lo:3 * MIX_MAIN + lo + RET_HD]
        g = float(np.exp(_LOG_G[h]))
        vexp = jnp.where(own, jnp.concatenate([v8] * bb, axis=1), 0.0).astype(BF16)
        outer = lax.dot_general(k8.astype(BF16), vexp, TN_DIMS, preferred_element_type=F32)
        cols = []
        for j in range(bb):
            s_new = g * st_ref[j, h] + outer[:, j * RET_HD:(j + 1) * RET_HD]
            nst_ref[j, h] = s_new
            cols.append(s_new.astype(BF16))
        o_all = jnp.dot(q8.astype(BF16), jnp.concatenate(cols, axis=1), preferred_element_type=F32)
        o_all = jnp.where(own, o_all, 0.0)
        o = o_all[:, :RET_HD]
        for j in range(1, bb):
            o = o + o_all[:, j * RET_HD:(j + 1) * RET_HD]
        mu = jnp.mean(o, axis=-1, keepdims=True)
        oc = o - mu
        var = jnp.mean(oc * oc, axis=-1, keepdims=True)
        on = oc * lax.rsqrt(var + EPS)
        o_ref[:, lo:lo + RET_HD] = gate * jax.nn.sigmoid(gate) * on


def ret_decode(proj, state, layer, stacked_out=None, bb=16):
    nb = proj.shape[0]
    pw = proj.shape[1]
    in_specs = [pl.BlockSpec((bb, pw), lambda i: (i, 0)),
                pl.BlockSpec((None, bb, RET_HEADS, RET_HD, RET_HD), lambda i: (layer, i, 0, 0, 0))]
    args = [proj, state]
    aliases = {}
    if stacked_out is not None:
        in_specs.append(pl.BlockSpec(memory_space=pl.ANY))
        args.append(stacked_out)
        aliases = {2: 1}
    return pl.pallas_call(
        functools.partial(_ret_decode_kernel, bb=bb),
        grid=(nb // bb,),
        in_specs=in_specs,
        out_specs=[pl.BlockSpec((bb, MIX_MAIN), lambda i: (i, 0)),
                   pl.BlockSpec((None, bb, RET_HEADS, RET_HD, RET_HD), lambda i: (layer, i, 0, 0, 0))],
        out_shape=[jax.ShapeDtypeStruct((nb, MIX_MAIN), F32),
                   jax.ShapeDtypeStruct(state.shape, F32)],
        input_output_aliases=aliases,
        compiler_params=_cparams(1),
        name="ret_decode",
    )(*args)


def _mem_decode_kernel(q_ref, kt_ref, vt_ref, o_ref, *, bb, q_off, q_scale):
    rows = SUBLANES
    row = lax.broadcasted_iota(jnp.int32, (rows, MEM_W), 0)
    lane_head = _lane_block64(lax.broadcasted_iota(jnp.int32, (rows, MEM_W), 1))
    sel = row == lane_head
    seqs = range(bb)
    qexp = [jnp.where(sel, jnp.broadcast_to(q_ref[j:j + 1, q_off:q_off + MEM_W] * q_scale, (rows, MEM_W)), 0.0)
            for j in seqs]
    s = [jnp.dot(qexp[j], kt_ref[j], preferred_element_type=F32) for j in seqs]
    e = [jnp.exp(s[j] - jnp.max(s[j], axis=-1, keepdims=True)) for j in seqs]
    den = [jnp.sum(e[j], axis=-1, keepdims=True) for j in seqs]
    o = [lax.dot_general(e[j], vt_ref[j], NT_DIMS, preferred_element_type=F32) / den[j] for j in seqs]
    o_ref[...] = jnp.concatenate([jnp.sum(jnp.where(sel, o[j], 0.0), axis=0, keepdims=True) for j in seqs], axis=0)


def mem_decode(proj, cache_kt, cache_vt, layer, q_off, q_scale, bb=32):
    nb = proj.shape[0]
    pw = proj.shape[1]
    cmap = lambda i: (layer, i, 0, 0)
    return pl.pallas_call(
        functools.partial(_mem_decode_kernel, bb=bb, q_off=q_off, q_scale=q_scale),
        grid=(nb // bb,),
        in_specs=[pl.BlockSpec((bb, pw), lambda i: (i, 0)),
                  pl.BlockSpec((None, bb, MEM_W, MEM_TOKENS), cmap),
                  pl.BlockSpec((None, bb, MEM_W, MEM_TOKENS), cmap)],
        out_specs=pl.BlockSpec((bb, MEM_W), lambda i: (i, 0)),
        out_shape=jax.ShapeDtypeStruct((nb, MEM_W), F32),
        compiler_params=_cparams(1),
        name="mem_decode",
    )(proj, cache_kt, cache_vt)


def _swa_decode_kernel(q_ref, kn_ref, vn_ref, ckt_ref, cvt_ref, slope_ref, sink_ref, o_ref, *cache_out, bb):
    row = lax.broadcasted_iota(jnp.int32, (DEC_ROWS, KV_W), 0)
    lane_group = _lane_block64(lax.broadcasted_iota(jnp.int32, (DEC_ROWS, KV_W), 1))
    dist = (WINDOW - lax.broadcasted_iota(jnp.int32, (DEC_ROWS, WINDOW), 1)).astype(F32)
    bias = -slope_ref[...] * dist
    sink = sink_ref[...]
    own = [row == SWA_KV_HEADS * r + lane_group for r in range(SWA_REP)]
    seqs = range(bb)

    def expand(j):
        qexp = jnp.zeros((DEC_ROWS, KV_W), F32)
        for r in range(SWA_REP):
            qr = jnp.broadcast_to(q_ref[j:j + 1, r * KV_W:(r + 1) * KV_W], (DEC_ROWS, KV_W))
            qexp = jnp.where(own[r], qr, qexp)
        return qexp

    qexp = [expand(j) for j in seqs]
    s = [jnp.dot(qexp[j], ckt_ref[j], preferred_element_type=F32) + bias for j in seqs]
    s_new = [jnp.sum(qexp[j] * kn_ref[j:j + 1, :], axis=-1, keepdims=True) for j in seqs]
    m = [jnp.maximum(jnp.maximum(jnp.max(s[j], axis=-1, keepdims=True), s_new[j]), sink) for j in seqs]
    e = [jnp.exp(s[j] - m[j]) for j in seqs]
    e_new = [jnp.exp(s_new[j] - m[j]) for j in seqs]
    inv = [1.0 / (jnp.sum(e[j], axis=-1, keepdims=True) + e_new[j] + jnp.exp(sink - m[j])) for j in seqs]
    o = [lax.dot_general(e[j], cvt_ref[j], NT_DIMS, preferred_element_type=F32) for j in seqs]
    o = [(o[j] + jnp.concatenate([e_new[j], e_new[j]], axis=1) * vn_ref[j:j + 1, :])
         * jnp.concatenate([inv[j], inv[j]], axis=1) for j in seqs]
    for r in range(SWA_REP):
        o_ref[:, r * KV_W:(r + 1) * KV_W] = jnp.concatenate(
            [jnp.sum(jnp.where(own[r], o[j], 0.0), axis=0, keepdims=True) for j in seqs], axis=0)
    if cache_out:
        nkt_ref, nvt_ref = cache_out
        last = lax.broadcasted_iota(jnp.int32, (KV_W, WINDOW), 1) == WINDOW - 1
        knt = kn_ref[...].T
        vnt = vn_ref[...].T
        for j in seqs:
            nkt_ref[j] = jnp.where(last, jnp.broadcast_to(knt[:, j:j + 1], (KV_W, WINDOW)),
                                   pltpu.roll(ckt_ref[j], WINDOW - 1, 1))
            nvt_ref[j] = jnp.where(last, jnp.broadcast_to(vnt[:, j:j + 1], (KV_W, WINDOW)),
                                   pltpu.roll(cvt_ref[j], WINDOW - 1, 1))


def swa_decode(qproj, k_new, v_new, cache_kt, cache_vt, slope_rows, sink_rows, sink_layer, emit_cache, bb=32):
    nb = qproj.shape[0]
    pw = qproj.shape[1]
    cache_spec = pl.BlockSpec((bb, KV_W, WINDOW), lambda i: (i, 0, 0))
    cache_shape = jax.ShapeDtypeStruct((nb, KV_W, WINDOW), F32)
    n_cache = 2 if emit_cache else 0
    return pl.pallas_call(
        functools.partial(_swa_decode_kernel, bb=bb),
        grid=(nb // bb,),
        in_specs=[pl.BlockSpec((bb, pw), lambda i: (i, 0)),
                  pl.BlockSpec((bb, KV_W), lambda i: (i, 0)),
                  pl.BlockSpec((bb, KV_W), lambda i: (i, 0)),
                  cache_spec, cache_spec,
                  pl.BlockSpec((DEC_ROWS, LANES), lambda i: (0, 0)),
                  pl.BlockSpec((None, DEC_ROWS, LANES), lambda i: (sink_layer, 0, 0))],
        out_specs=[pl.BlockSpec((bb, MIX_MAIN), lambda i: (i, 0))] + [cache_spec] * n_cache,
        out_shape=[jax.ShapeDtypeStruct((nb, MIX_MAIN), F32)] + [cache_shape] * n_cache,
        compiler_params=_cparams(1),
        name="swa_decode",
    )(qproj, k_new, v_new, cache_kt, cache_vt, slope_rows, sink_rows)


def _slot_source(t):
    r, g = divmod(t, SWA_KV_HEADS)
    return SWA_REP * g + r


def _prep_w_in_b_kernel(w_ref, o_ref):
    low = lax.broadcasted_iota(jnp.int32, (D_MODEL, LANES), 1) < SWA_HD

    def half(slot, want_high):
        col = w_ref[:, (slot // 2) * LANES:(slot // 2 + 1) * LANES]
        return pltpu.roll(col, SWA_HD, 1) if (slot % 2 == 1) != want_high else col

    for c in range(MIX_MAIN // LANES):
        tile = jnp.where(low, half(_slot_source(2 * c), False), half(_slot_source(2 * c + 1), True))
        o_ref[:, c * LANES:(c + 1) * LANES] = (tile * ATTN_SCALE).astype(BF16)
    o_ref[:, MIX_MAIN:] = (w_ref[:, MIX_MAIN:] * ATTN_SCALE).astype(BF16)


def _prep_w_out_b_kernel(w_ref, o_ref):
    for t in range(SWA_HEADS):
        s = _slot_source(t)
        o_ref[t * SWA_HD:(t + 1) * SWA_HD, :] = w_ref[s * SWA_HD:(s + 1) * SWA_HD, :].astype(BF16)
    o_ref[MIX_MAIN:, :] = w_ref[MIX_MAIN:, :].astype(BF16)


def _prep_layer_b_weights(w, body, name):
    n_layers = w.shape[0]
    spec = pl.BlockSpec((None, D_MODEL, D_MODEL), lambda l: (l, 0, 0))
    return pl.pallas_call(
        body, grid=(n_layers,), in_specs=[spec], out_specs=spec,
        out_shape=jax.ShapeDtypeStruct(w.shape, BF16), compiler_params=_cparams(1), name=name,
    )(w)


def kernel(x_prompt, x_sample, cache_mem_k, cache_mem_v, state_ret, cache_swa_k, cache_swa_v, mem_prompt,
           norm_mix, w_in_a, w_out_a, w_in_b, w_out_b, attn_sinks, norm_mem, w_mem_kv, norm_kv, w_kv,
           norm_mlp, w_up, w_down, norm_final):
    batch, seq, d = x_prompt.shape
    nb = x_sample.shape[0]
    tm = TILE

    w_in_b16 = _prep_layer_b_weights(w_in_b, _prep_w_in_b_kernel, "prep_w_in_b")
    w_out_b16 = _prep_layer_b_weights(w_out_b, _prep_w_out_b_kernel, "prep_w_out_b")
    w_kv16 = w_kv.astype(BF16).reshape(1, d, 2 * KV_W)
    norm_kv1 = norm_kv.reshape(1, d)

    sinks_gr = attn_sinks.reshape(N_B, SWA_KV_HEADS, SWA_REP)
    sink_prompt = jnp.broadcast_to(sinks_gr[:, :, :, None, None], (N_B, SWA_KV_HEADS, SWA_REP, WINDOW, LANES)
                                   ).reshape(N_B, SWA_KV_HEADS, SWA_REP * WINDOW, LANES)
    sink_dec = jnp.concatenate([sinks_gr.swapaxes(1, 2).reshape(N_B, SWA_HEADS),
                                jnp.zeros((N_B, DEC_ROWS - SWA_HEADS), F32)], axis=1)
    sink_dec = jnp.broadcast_to(sink_dec[:, :, None], (N_B, DEC_ROWS, LANES))
    slope_dec = np.zeros((DEC_ROWS, LANES), np.float32)
    for r in range(SWA_REP):
        for g in range(SWA_KV_HEADS):
            slope_dec[SWA_KV_HEADS * r + g, :] = _SLOPES[g * SWA_REP + r]
    slope_dec = jnp.asarray(slope_dec)

    memx = mem_prompt.reshape(batch * MEM_TOKENS, d)
    mem_kv, mk16, mv16 = mem_kv_proj(memx, norm_mem, w_mem_kv)
    mem_kv = mem_kv.reshape(DEPTH, batch, MEM_TOKENS, 2 * MEM_W)
    mem_k_prompt = mem_kv[..., :MEM_W].reshape(DEPTH, batch, MEM_TOKENS, MEM_HEADS, MEM_HD)
    mem_v_prompt = mem_kv[..., MEM_W:].reshape(DEPTH, batch, MEM_TOKENS, MEM_HEADS, MEM_HD)
    mk16 = mk16.reshape(DEPTH, batch, MEM_TOKENS, MEM_W)
    mv16 = mv16.reshape(DEPTH, batch, MEM_TOKENS, MEM_W)

    x = x_prompt.reshape(batch * seq, d)
    ret_states = []
    kv_p = kv_p16 = None
    for l in range(DEPTH):
        if l < N_A:
            x, st = fused_mixer_a(x, norm_mix, w_in_a, w_out_a, mk16, mv16, l, batch, seq)
            ret_states.append(st)
        else:
            j = l - N_A
            x = fused_mixer_b(x, norm_mix, w_in_b16, w_out_b16, kv_p16, mk16, mv16, l, sink_prompt, j, batch, seq)
        if l == N_A - 1:
            x, kv_p, kv_p16 = mlp_block(x, norm_mlp, w_up, w_down, l, norm_final, False, tm=tm,
                                        kv_proj=(norm_kv1, w_kv16[0]))
        else:
            x = mlp_block(x, norm_mlp, w_up, w_down, l, norm_final, l == DEPTH - 1, tm=tm)
    y_prompt = x.reshape(batch, seq, d)
    ret_prompt = jnp.stack(ret_states)
    kv_p3 = kv_p.reshape(batch, seq, 2 * KV_W)
    swa_k_prompt = kv_p3[:, -WINDOW:, :KV_W].reshape(batch, WINDOW, SWA_KV_HEADS, SWA_HD)
    swa_v_prompt = kv_p3[:, -WINDOW:, KV_W:].reshape(batch, WINDOW, SWA_KV_HEADS, SWA_HD)

    cmkt = jnp.transpose(cache_mem_k, (0, 1, 3, 4, 2)).reshape(DEPTH, nb, MEM_W, MEM_TOKENS)
    cmvt = jnp.transpose(cache_mem_v, (0, 1, 3, 4, 2)).reshape(DEPTH, nb, MEM_W, MEM_TOKENS)
    cskt = jnp.transpose(cache_swa_k, (0, 2, 3, 1)).reshape(nb, KV_W, WINDOW)
    csvt = jnp.transpose(cache_swa_v, (0, 2, 3, 1)).reshape(nb, KV_W, WINDOW)
    xs = x_sample.reshape(nb, d)
    ret_sample = None
    k_new = v_new = nkt = nvt = None
    for l in range(DEPTH):
        if l < N_A:
            proj = norm_matmul(xs, norm_mix, w_in_a, l, F32, tm=nb, n_chunk=MIX_MAIN)
            o_ret, ret_sample = ret_decode(proj, state_ret, l, stacked_out=ret_sample)
            o_mem = mem_decode(proj, cmkt, cmvt, l, 4 * MIX_MAIN, ATTN_SCALE)
            attn = jnp.concatenate([o_ret, o_mem], axis=-1).astype(BF16)
            w_out, out_layer = w_out_a, l
        else:
            j = l - N_A
            if j == 0:
                kv_s = norm_matmul(xs, norm_kv1, w_kv16, 0, F32, tm=nb)
                k_new, v_new = kv_s[:, :KV_W], kv_s[:, KV_W:]
            qproj = norm_matmul(xs, norm_mix[N_A:], w_in_b16, j, F32, tm=nb)
            res = swa_decode(qproj, k_new, v_new, cskt, csvt, slope_dec, sink_dec, j, emit_cache=(j == 0))
            o_swa = res[0]
            if j == 0:
                nkt, nvt = res[1], res[2]
            o_mem = mem_decode(qproj, cmkt, cmvt, l, MIX_MAIN, 1.0)
            attn = jnp.concatenate([o_swa, o_mem], axis=-1).astype(BF16)
            w_out, out_layer = w_out_b16, j
        xs = post_decode(attn, w_out, out_layer, xs, norm_mlp, w_up, w_down, l, norm_final, l == DEPTH - 1)
    y_sample = xs.reshape(nb, 1, d)
    swa_k_sample = jnp.transpose(nkt.reshape(nb, SWA_KV_HEADS, SWA_HD, WINDOW), (0, 3, 1, 2))
    swa_v_sample = jnp.transpose(nvt.reshape(nb, SWA_KV_HEADS, SWA_HD, WINDOW), (0, 3, 1, 2))

    return (y_prompt, y_sample, ret_prompt, ret_sample, swa_k_prompt, swa_v_prompt, swa_k_sample, swa_v_sample,
            mem_k_prompt, mem_v_prompt)
```

```python
import functools
import math

import jax
import jax.numpy as jnp
import numpy as np
from jax import lax
from jax.experimental import pallas as pl
from jax.experimental.pallas import tpu as pltpu

F32 = jnp.float32
BF16 = jnp.bfloat16

D_MODEL = 1024
DEPTH = 4
N_A = 2
N_B = 2
MEM_TOKENS = 256
MEM_HEADS = 4
MEM_W = 256
MEM_HD = 64
MIX_MAIN = 768
RET_HEADS = 6
RET_HD = 128
CHUNK = 128
SWA_HEADS = 12
SWA_KV_HEADS = 4
SWA_REP = SWA_HEADS // SWA_KV_HEADS
SWA_HD = 64
KV_W = SWA_KV_HEADS * SWA_HD
WINDOW = 128
D_FF = 4096
EPS = 1e-6
NEG = -1e30
ATTN_SCALE = 0.125
LANES = 128
SUBLANES = 8
PIECE = 256
TILE = 512
DEC_ROWS = 16

VMEM_LIMIT = 56 * 1024 * 1024

NT_DIMS = (((1,), (1,)), ((), ()))
TN_DIMS = (((0,), (0,)), ((), ()))


def _alibi_slopes(n):
    def pow2(m):
        start = 2.0 ** (-8.0 / m)
        return [start ** (i + 1) for i in range(m)]

    if math.log2(n).is_integer():
        s = pow2(n)
    else:
        c = 2 ** int(math.floor(math.log2(n)))
        s = pow2(c) + pow2(2 * c)[0::2][: n - c]
    return np.asarray(s, np.float32)


_LOG_G = np.log1p(-(2.0 ** (-5.0 - np.arange(RET_HEADS)))).astype(np.float32).astype(np.float64)
_SLOPES = _alibi_slopes(SWA_HEADS).astype(np.float64)


def _retention_consts():
    idx = np.arange(CHUNK, dtype=np.float64)
    diff = idx[:, None] - idx[None, :]
    scale = RET_HD ** -0.5
    decay = np.where(diff >= 0, np.exp(np.maximum(diff, 0.0)[None] * _LOG_G[:, None, None]), 0.0) * scale
    w_q = np.exp((idx + 1.0)[None, :] * _LOG_G[:, None])
    w_k = np.exp((CHUNK - 1.0 - idx)[None, :] * _LOG_G[:, None]) * scale
    w_q = np.broadcast_to(w_q[:, :, None], (RET_HEADS, CHUNK, RET_HD))
    w_k = np.broadcast_to(w_k[:, :, None], (RET_HEADS, CHUNK, RET_HD))
    g_c = np.exp(CHUNK * _LOG_G)
    pair = lambda t: np.ascontiguousarray(np.concatenate([t[0::2], t[1::2]], axis=2), np.float32)
    return pair(decay), pair(w_q), pair(w_k), [float(v) for v in g_c]


def _swa_bias():
    i = np.arange(WINDOW)[:, None]
    j = np.arange(2 * WINDOW)[None, :]
    dist = i + WINDOW - j
    valid = (dist >= 0) & (dist <= WINDOW)
    out = np.zeros((SWA_KV_HEADS, SWA_REP * WINDOW, 2 * WINDOW), np.float32)
    for g in range(SWA_KV_HEADS):
        for r in range(SWA_REP):
            b = np.where(valid, -_SLOPES[g * SWA_REP + r] * dist, NEG)
            out[g, r * WINDOW:(r + 1) * WINDOW] = b
    return out


def _rmsnorm(x, g):
    ms = jnp.mean(x * x, axis=-1, keepdims=True)
    return (x * lax.rsqrt(ms + EPS)) * g


def _lane_block64(lane):
    return lax.shift_right_logical(lane, 6)


def _cparams(n_axes):
    return pltpu.CompilerParams(dimension_semantics=("arbitrary",) * n_axes, vmem_limit_bytes=VMEM_LIMIT)


def _layer_block(tail, layer, n_grid):
    zeros = (0,) * len(tail)
    if n_grid == 1:
        return pl.BlockSpec((None,) + tuple(tail), lambda i: (layer,) + zeros)
    return pl.BlockSpec((None,) + tuple(tail), lambda i, j: (layer,) + zeros)


def _norm_matmul_kernel(x_ref, g_ref, w_ref, o_ref, *, n_chunk):
    xn = _rmsnorm(x_ref[...], g_ref[...]).astype(BF16)
    n = w_ref.shape[1]
    for n0 in range(0, n, n_chunk):
        n1 = min(n0 + n_chunk, n)
        o_ref[:, n0:n1] = jnp.dot(xn, w_ref[:, n0:n1].astype(BF16), preferred_element_type=F32).astype(o_ref.dtype)


def norm_matmul(x, g, w, layer, out_dtype, tm, n_chunk=512):
    m, k = x.shape
    n = w.shape[2]
    return pl.pallas_call(
        functools.partial(_norm_matmul_kernel, n_chunk=n_chunk),
        grid=(m // tm,),
        in_specs=[pl.BlockSpec((tm, k), lambda i: (i, 0)),
                  _layer_block((1, k), layer, 1),
                  _layer_block((k, n), layer, 1)],
        out_specs=pl.BlockSpec((tm, n), lambda i: (i, 0)),
        out_shape=jax.ShapeDtypeStruct((m, n), out_dtype),
        compiler_params=_cparams(1),
        name="norm_matmul",
    )(x, g.reshape(g.shape[0], 1, k), w)


def _mem_kv_kernel(x_ref, g_ref, w_ref, o_ref, k16_ref, v16_ref):
    xn = _rmsnorm(x_ref[...], g_ref[...]).astype(BF16)
    kv = jnp.dot(xn, w_ref[...].astype(BF16), preferred_element_type=F32)
    o_ref[...] = kv
    k16_ref[...] = kv[:, :MEM_W].astype(BF16)
    v16_ref[...] = kv[:, MEM_W:].astype(BF16)


def mem_kv_proj(x, g, w):
    m, k = x.shape
    n_layers, _, n = w.shape
    half = pl.BlockSpec((None, m, MEM_W), lambda l: (l, 0, 0))
    return pl.pallas_call(
        _mem_kv_kernel,
        grid=(n_layers,),
        in_specs=[pl.BlockSpec((m, k), lambda l: (0, 0)),
                  pl.BlockSpec((None, 1, k), lambda l: (l, 0, 0)),
                  pl.BlockSpec((None, k, n), lambda l: (l, 0, 0))],
        out_specs=[pl.BlockSpec((None, m, n), lambda l: (l, 0, 0)), half, half],
        out_shape=[jax.ShapeDtypeStruct((n_layers, m, n), F32),
                   jax.ShapeDtypeStruct((n_layers, m, MEM_W), BF16),
                   jax.ShapeDtypeStruct((n_layers, m, MEM_W), BF16)],
        compiler_params=_cparams(1),
        name="mem_kv_proj",
    )(x, g.reshape(n_layers, 1, k), w)


def _mlp_kernel(x_ref, g_ref, wu_ref, wd_ref, gf_ref, *rest, fc, final_norm, with_kv):
    if with_kv:
        gkv_ref, wkv_ref, o_ref, kv_ref, kv16_ref, act_ref = rest
    else:
        o_ref, act_ref = rest
    xn = _rmsnorm(x_ref[...], g_ref[...]).astype(BF16)
    for c0 in range(0, D_FF, fc):
        h = jnp.dot(xn, wu_ref[:, c0:c0 + fc].astype(BF16), preferred_element_type=F32)
        act_ref[:, c0:c0 + fc] = jnp.square(jnp.maximum(h, 0.0)).astype(BF16)
    y = x_ref[...] + jnp.dot(act_ref[...], wd_ref[...].astype(BF16), preferred_element_type=F32)
    if with_kv:
        kv = jnp.dot(_rmsnorm(y, gkv_ref[...]).astype(BF16), wkv_ref[...], preferred_element_type=F32)
        kv_ref[...] = kv
        kv16_ref[...] = kv.astype(BF16)
    if final_norm:
        y = _rmsnorm(y, gf_ref[...])
    o_ref[...] = y


def mlp_block(x, g, w_up, w_down, layer, g_final, final_norm, tm, kv_proj=None, fc=512):
    m, d = x.shape
    with_kv = kv_proj is not None
    in_specs = [pl.BlockSpec((tm, d), lambda i: (i, 0)),
                _layer_block((1, d), layer, 1),
                pl.BlockSpec((None, d, D_FF), lambda i: (layer, 0, 0), pipeline_mode=pl.Buffered(1)),
                pl.BlockSpec((None, D_FF, d), lambda i: (layer, 0, 0), pipeline_mode=pl.Buffered(1)),
                pl.BlockSpec((1, d), lambda i: (0, 0))]
    args = [x, g.reshape(g.shape[0], 1, d), w_up, w_down, g_final.reshape(1, d)]
    out_specs = [pl.BlockSpec((tm, d), lambda i: (i, 0))]
    out_shape = [jax.ShapeDtypeStruct((m, d), F32)]
    if with_kv:
        in_specs += [pl.BlockSpec((1, d), lambda i: (0, 0)),
                     pl.BlockSpec((d, 2 * KV_W), lambda i: (0, 0), pipeline_mode=pl.Buffered(1))]
        args += list(kv_proj)
        out_specs += [pl.BlockSpec((tm, 2 * KV_W), lambda i: (i, 0))] * 2
        out_shape += [jax.ShapeDtypeStruct((m, 2 * KV_W), F32), jax.ShapeDtypeStruct((m, 2 * KV_W), BF16)]
    res = pl.pallas_call(
        functools.partial(_mlp_kernel, fc=fc, final_norm=final_norm, with_kv=with_kv),
        grid=(m // tm,),
        in_specs=in_specs,
        out_specs=out_specs,
        out_shape=out_shape,
        scratch_shapes=[pltpu.VMEM((tm, D_FF), BF16)],
        compiler_params=_cparams(1),
        name="mlp_block",
    )(*args)
    return res if with_kv else res[0]


def _post_decode_kernel(a_ref, wo_ref, x_ref, g_ref, wu_ref, wd_ref, gf_ref, o_ref, x1_scr, xn_scr, acc_scr, *,
                        final_norm):
    k = pl.program_id(0)

    @pl.when(k == 0)
    def _():
        x1 = x_ref[...] + jnp.dot(a_ref[...], wo_ref[...].astype(BF16), preferred_element_type=F32)
        x1_scr[...] = x1
        xn_scr[...] = _rmsnorm(x1, g_ref[...]).astype(BF16)
        acc_scr[...] = jnp.zeros_like(acc_scr)

    h = jnp.dot(xn_scr[...], wu_ref[...].astype(BF16), preferred_element_type=F32)
    act = jnp.square(jnp.maximum(h, 0.0)).astype(BF16)
    acc_scr[...] += jnp.dot(act, wd_ref[...].astype(BF16), preferred_element_type=F32)

    @pl.when(k == pl.num_programs(0) - 1)
    def _():
        y = x1_scr[...] + acc_scr[...]
        if final_norm:
            y = _rmsnorm(y, gf_ref[...])
        o_ref[...] = y


def post_decode(attn, w_out, out_layer, x, g, w_up, w_down, layer, g_final, final_norm, fc=1024):
    m, d = x.shape
    return pl.pallas_call(
        functools.partial(_post_decode_kernel, final_norm=final_norm),
        grid=(D_FF // fc,),
        in_specs=[pl.BlockSpec((m, d), lambda k: (0, 0)),
                  _layer_block((d, d), out_layer, 1),
                  pl.BlockSpec((m, d), lambda k: (0, 0)),
                  _layer_block((1, d), layer, 1),
                  pl.BlockSpec((None, d, fc), lambda k: (layer, 0, k)),
                  pl.BlockSpec((None, fc, d), lambda k: (layer, k, 0)),
                  pl.BlockSpec((1, d), lambda k: (0, 0))],
        out_specs=pl.BlockSpec((m, d), lambda k: (0, 0)),
        out_shape=jax.ShapeDtypeStruct((m, d), F32),
        scratch_shapes=[pltpu.VMEM((m, d), F32), pltpu.VMEM((m, d), BF16), pltpu.VMEM((m, d), F32)],
        compiler_params=_cparams(1),
        name="post_decode",
    )(attn, w_out, x, g.reshape(g.shape[0], 1, d), w_up, w_down, g_final.reshape(1, d))


def _mem_attention_tile(qm, mk, mv, fillers=()):
    t = qm.shape[0]
    lane_head = _lane_block64(lax.broadcasted_iota(jnp.int32, (t, MEM_W), 1))
    tok_head = _lane_block64(lax.broadcasted_iota(jnp.int32, (MEM_TOKENS, MEM_W), 1))
    probs = []
    vals = []
    for h in range(MEM_HEADS):
        qh = jnp.where(lane_head == h, qm, jnp.zeros_like(qm))
        s = lax.dot_general(qh, mk, NT_DIMS, preferred_element_type=F32)
        if h < len(fillers):
            fillers[h]()
        m = jnp.max(s, axis=-1, keepdims=True)
        e = jnp.exp(s - m)
        inv = 1.0 / jnp.sum(e, axis=-1, keepdims=True)
        probs.append((e * inv).astype(BF16))
        vals.append(jnp.where(tok_head == h, mv, jnp.zeros_like(mv)))
    return jnp.dot(jnp.concatenate(probs, axis=1), jnp.concatenate(vals, axis=0), preferred_element_type=F32)


def _project_tile(x_ref, g_ref, w_ref, p_ref, n_chunk):
    xn = _rmsnorm(x_ref[...], g_ref[...]).astype(BF16)
    n = w_ref.shape[1]
    for n0 in range(0, n, n_chunk):
        n1 = min(n0 + n_chunk, n)
        p_ref[:, n0:n1] = jnp.dot(xn, w_ref[:, n0:n1].astype(BF16), preferred_element_type=F32).astype(BF16)


def _projection_pieces(x_ref, g_ref, w_ref, p_ref):
    xn = _rmsnorm(x_ref[...], g_ref[...]).astype(BF16)

    def piece(c):
        def run():
            cols = slice(c * PIECE, (c + 1) * PIECE)
            p_ref[:, cols] = jnp.dot(xn, w_ref[:, cols].astype(BF16), preferred_element_type=F32).astype(BF16)
        return run

    return [piece(c) for c in range(w_ref.shape[1] // PIECE)]


def _out_projection_piece(x_ref, a_ref, w_ref, o_ref, c):
    def run():
        cols = slice(c * PIECE, (c + 1) * PIECE)
        o_ref[:, cols] = x_ref[:, cols] + jnp.dot(a_ref[:, :MIX_MAIN], w_ref[:MIX_MAIN, cols].astype(BF16),
                                                  preferred_element_type=F32)
    return run


def _both(f, g):
    def run():
        f()
        g()
    return run


def _block_diag2(a, b):
    z = jnp.zeros_like(a)
    return jnp.concatenate([jnp.concatenate([a, z], axis=1), jnp.concatenate([z, b], axis=1)], axis=0)


def _group_norm_gate(o, gate):
    mu = jnp.mean(o, axis=-1, keepdims=True)
    oc = o - mu
    var = jnp.mean(oc * oc, axis=-1, keepdims=True)
    return (gate * jax.nn.sigmoid(gate) * (oc * lax.rsqrt(var + EPS))).astype(BF16)


def _retention_tile(p_ref, a_ref, s_scr, decay_ref, wq_ref, wk_ref, g_c, fillers=()):
    pw = 2 * RET_HD
    pairs = range(RET_HEADS // 2)
    for c in range(TILE // CHUNK):
        rows = slice(c * CHUNK, (c + 1) * CHUNK)
        q2 = [p_ref[rows, hp * pw:(hp + 1) * pw] for hp in pairs]
        k2 = [p_ref[rows, MIX_MAIN + hp * pw:MIX_MAIN + (hp + 1) * pw] for hp in pairs]
        v2 = [p_ref[rows, 2 * MIX_MAIN + hp * pw:2 * MIX_MAIN + (hp + 1) * pw] for hp in pairs]
        sc = [lax.dot_general(q2[hp], _block_diag2(k2[hp][:, :RET_HD], k2[hp][:, RET_HD:]), NT_DIMS,
                              preferred_element_type=F32) for hp in pairs]
        kw = [(k2[hp].astype(F32) * wk_ref[hp]).astype(BF16) for hp in pairs]
        kv = [lax.dot_general(kw[hp], v2[hp], TN_DIMS, preferred_element_type=F32) for hp in pairs]
        if 2 * c < len(fillers):
            fillers[2 * c]()
        s_old = [s_scr[h] for h in range(RET_HEADS)]
        lhs = [jnp.concatenate([(sc[hp] * decay_ref[hp]).astype(BF16),
                                (q2[hp].astype(F32) * wq_ref[hp]).astype(BF16)], axis=1) for hp in pairs]
        rhs = [jnp.concatenate([_block_diag2(v2[hp][:, :RET_HD], v2[hp][:, RET_HD:]),
                                _block_diag2(s_old[2 * hp].astype(BF16), s_old[2 * hp + 1].astype(BF16))], axis=0)
               for hp in pairs]
        o2 = [jnp.dot(lhs[hp], rhs[hp], preferred_element_type=F32) for hp in pairs]
        if 2 * c + 1 < len(fillers):
            fillers[2 * c + 1]()
        for hp in pairs:
            s_scr[2 * hp] = g_c[2 * hp] * s_old[2 * hp] + kv[hp][:RET_HD, :RET_HD]
            s_scr[2 * hp + 1] = g_c[2 * hp + 1] * s_old[2 * hp + 1] + kv[hp][RET_HD:, RET_HD:]
        for hp in pairs:
            lo = hp * pw
            gate2 = p_ref[rows, 3 * MIX_MAIN + lo:3 * MIX_MAIN + lo + pw].astype(F32)
            a_ref[rows, lo:lo + RET_HD] = _group_norm_gate(o2[hp][:, :RET_HD], gate2[:, :RET_HD])
            a_ref[rows, lo + RET_HD:lo + pw] = _group_norm_gate(o2[hp][:, RET_HD:], gate2[:, RET_HD:])


def _fused_a_kernel(xc_ref, xn_ref, g_ref, win_ref, wout_ref, mk_ref, mv_ref, decay_ref, wq_ref, wk_ref,
                    o_ref, st_ref, p0, p1, a_scr, s_scr, *, g_c, tiles_per_seq):
    t = pl.program_id(0)

    @pl.when(t == 0)
    def _():
        _project_tile(xc_ref, g_ref, win_ref, p0, MIX_MAIN)

    @pl.when(t % tiles_per_seq == 0)
    def _():
        s_scr[...] = jnp.zeros_like(s_scr)

    def step(p_cur, p_next):
        pieces = _projection_pieces(xn_ref, g_ref, win_ref, p_next)
        n_slots = 2 * (TILE // CHUNK)
        slots = [_both(pieces[c], pieces[n_slots + c]) if n_slots + c < len(pieces) else pieces[c]
                 for c in range(n_slots)]
        _retention_tile(p_cur, a_scr, s_scr, decay_ref, wq_ref, wk_ref, g_c, slots)
        st_ref[...] = s_scr[...]
        qm = p_cur[:, 4 * MIX_MAIN:4 * MIX_MAIN + MEM_W] * ATTN_SCALE
        out_piece = functools.partial(_out_projection_piece, xc_ref, a_scr, wout_ref, o_ref)
        a_mem = _mem_attention_tile(qm, mk_ref[...], mv_ref[...],
                                    [out_piece(c) for c in range(D_MODEL // PIECE)]).astype(BF16)
        o_ref[...] += jnp.dot(a_mem, wout_ref[MIX_MAIN:, :].astype(BF16), preferred_element_type=F32)

    @pl.when(t % 2 == 0)
    def _():
        step(p0, p1)

    @pl.when(t % 2 == 1)
    def _():
        step(p1, p0)


def fused_mixer_a(x, g, w_in, w_out, mem_k, mem_v, layer, batch, seq):
    n_tiles = batch * seq // TILE
    tps = seq // TILE
    decay, w_q, w_k, g_c = _retention_consts()
    pw = w_in.shape[2]
    c3 = lambda t: (0, 0, 0)
    mem_spec = pl.BlockSpec((None, None, MEM_TOKENS, MEM_W), lambda t: (layer, t // tps, 0, 0))
    return pl.pallas_call(
        functools.partial(_fused_a_kernel, g_c=g_c, tiles_per_seq=tps),
        grid=(n_tiles,),
        in_specs=[pl.BlockSpec((TILE, D_MODEL), lambda t: (t, 0)),
                  pl.BlockSpec((TILE, D_MODEL), lambda t: (jnp.minimum(t + 1, n_tiles - 1), 0)),
                  _layer_block((1, D_MODEL), layer, 1),
                  pl.BlockSpec((None, D_MODEL, pw), lambda t: (layer, 0, 0), pipeline_mode=pl.Buffered(1)),
                  pl.BlockSpec((None, D_MODEL, D_MODEL), lambda t: (layer, 0, 0), pipeline_mode=pl.Buffered(1)),
                  mem_spec, mem_spec,
                  pl.BlockSpec((RET_HEADS // 2, CHUNK, 2 * CHUNK), c3),
                  pl.BlockSpec((RET_HEADS // 2, CHUNK, 2 * RET_HD), c3),
                  pl.BlockSpec((RET_HEADS // 2, CHUNK, 2 * RET_HD), c3)],
        out_specs=[pl.BlockSpec((TILE, D_MODEL), lambda t: (t, 0)),
                   pl.BlockSpec((None, RET_HEADS, RET_HD, RET_HD), lambda t: (t // tps, 0, 0, 0))],
        out_shape=[jax.ShapeDtypeStruct((batch * seq, D_MODEL), F32),
                   jax.ShapeDtypeStruct((batch, RET_HEADS, RET_HD, RET_HD), F32)],
        scratch_shapes=[pltpu.VMEM((TILE, pw), BF16), pltpu.VMEM((TILE, pw), BF16),
                        pltpu.VMEM((TILE, D_MODEL), BF16),
                        pltpu.VMEM((RET_HEADS, RET_HD, RET_HD), F32)],
        compiler_params=_cparams(1),
        name="fused_mixer_a",
    )(x, x, g.reshape(g.shape[0], 1, D_MODEL), w_in, w_out, mem_k, mem_v,
      jnp.asarray(decay), jnp.asarray(w_q), jnp.asarray(w_k))


def _swa_tile(p_ref, kvp_ref, kvc_ref, a_ref, bias_ref, sink_ref, is_first, fillers=()):
    lane_group = _lane_block64(lax.broadcasted_iota(jnp.int32, (WINDOW, KV_W), 1))
    key_group = _lane_block64(lax.broadcasted_iota(jnp.int32, (2 * WINDOW, KV_W), 1))
    for i in range(TILE // WINDOW):
        r0 = i * WINDOW
        if i == 0:
            kv2 = jnp.concatenate([kvp_ref[...], kvc_ref[0:WINDOW, :]], axis=0)
        else:
            kv2 = kvc_ref[r0 - WINDOW:r0 + WINDOW, :]
        k2 = kv2[:, :KV_W]
        v2 = kv2[:, KV_W:]
        qs = [p_ref[r0:r0 + WINDOW, r * KV_W:(r + 1) * KV_W] for r in range(SWA_REP)]
        probs = []
        vals = []
        for g in range(SWA_KV_HEADS):
            sel = lane_group == g
            qg = jnp.concatenate([jnp.where(sel, qr, jnp.zeros_like(qr)) for qr in qs], axis=0)
            s = lax.dot_general(qg, k2, NT_DIMS, preferred_element_type=F32) + bias_ref[g]
            if g == 1 and i < len(fillers):
                fillers[i]()
            s_prev = s[:, :WINDOW]
            s_cur = s[:, WINDOW:]
            if i == 0:
                s_prev = jnp.where(is_first, NEG, s_prev)
            sink = sink_ref[g]
            m = jnp.maximum(jnp.max(jnp.maximum(s_prev, s_cur), axis=-1, keepdims=True), sink)
            e_prev = jnp.exp(s_prev - m)
            e_cur = jnp.exp(s_cur - m)
            inv = 1.0 / (jnp.sum(e_prev + e_cur, axis=-1, keepdims=True) + jnp.exp(sink - m))
            probs.append((e_prev * inv).astype(BF16))
            probs.append((e_cur * inv).astype(BF16))
            vals.append(jnp.where(key_group == g, v2, jnp.zeros_like(v2)))
        o = jnp.dot(jnp.concatenate(probs, axis=1), jnp.concatenate(vals, axis=0), preferred_element_type=F32)
        for r in range(SWA_REP):
            a_ref[r0:r0 + WINDOW, r * KV_W:(r + 1) * KV_W] = o[r * WINDOW:(r + 1) * WINDOW].astype(BF16)


def _fused_b_kernel(xc_ref, xn_ref, g_ref, win_ref, wout_ref, kvp_ref, kvc_ref, mk_ref, mv_ref, bias_ref, sink_ref,
                    o_ref, p0, p1, a_scr, *, tiles_per_seq):
    t = pl.program_id(0)
    is_first = t % tiles_per_seq == 0

    @pl.when(t == 0)
    def _():
        _project_tile(xc_ref, g_ref, win_ref, p0, 2 * PIECE)

    out_piece = functools.partial(_out_projection_piece, xc_ref, a_scr, wout_ref, o_ref)

    def step(p_cur, p_next):
        pieces = _projection_pieces(xn_ref, g_ref, win_ref, p_next)
        _swa_tile(p_cur, kvp_ref, kvc_ref, a_scr, bias_ref, sink_ref, is_first, pieces)
        qm = p_cur[:, MIX_MAIN:MIX_MAIN + MEM_W]
        a_mem = _mem_attention_tile(qm, mk_ref[...], mv_ref[...],
                                    [out_piece(c) for c in range(D_MODEL // PIECE)]).astype(BF16)
        o_ref[...] += jnp.dot(a_mem, wout_ref[MIX_MAIN:, :], preferred_element_type=F32)

    @pl.when(t % 2 == 0)
    def _():
        step(p0, p1)

    @pl.when(t % 2 == 1)
    def _():
        step(p1, p0)


def fused_mixer_b(x, g, w_in, w_out, kv, mem_k, mem_v, layer, sink_rows, sub_layer, batch, seq):
    n_tiles = batch * seq // TILE
    tps = seq // TILE
    nw = TILE // WINDOW
    bias = jnp.asarray(_swa_bias())
    mem_spec = pl.BlockSpec((None, None, MEM_TOKENS, MEM_W), lambda t: (layer, t // tps, 0, 0))

    def prev_map(t):
        return (jnp.maximum(t * nw - 1, (t // tps) * (seq // WINDOW)), 0)

    return pl.pallas_call(
        functools.partial(_fused_b_kernel, tiles_per_seq=tps),
        grid=(n_tiles,),
        in_specs=[pl.BlockSpec((TILE, D_MODEL), lambda t: (t, 0)),
                  pl.BlockSpec((TILE, D_MODEL), lambda t: (jnp.minimum(t + 1, n_tiles - 1), 0)),
                  _layer_block((1, D_MODEL), layer, 1),
                  pl.BlockSpec((None, D_MODEL, D_MODEL), lambda t: (sub_layer, 0, 0), pipeline_mode=pl.Buffered(1)),
                  pl.BlockSpec((None, D_MODEL, D_MODEL), lambda t: (sub_layer, 0, 0), pipeline_mode=pl.Buffered(1)),
                  pl.BlockSpec((WINDOW, 2 * KV_W), prev_map),
                  pl.BlockSpec((TILE, 2 * KV_W), lambda t: (t, 0)),
                  mem_spec, mem_spec,
                  pl.BlockSpec((SWA_KV_HEADS, SWA_REP * WINDOW, 2 * WINDOW), lambda t: (0, 0, 0)),
                  pl.BlockSpec((None, SWA_KV_HEADS, SWA_REP * WINDOW, LANES), lambda t: (sub_layer, 0, 0, 0))],
        out_specs=pl.BlockSpec((TILE, D_MODEL), lambda t: (t, 0)),
        out_shape=jax.ShapeDtypeStruct((batch * seq, D_MODEL), F32),
        scratch_shapes=[pltpu.VMEM((TILE, D_MODEL), BF16), pltpu.VMEM((TILE, D_MODEL), BF16),
                        pltpu.VMEM((TILE, D_MODEL), BF16)],
        compiler_params=_cparams(1),
        name="fused_mixer_b",
    )(x, x, g.reshape(g.shape[0], 1, D_MODEL), w_in, w_out, kv, kv, mem_k, mem_v, bias, sink_rows)


def _ret_decode_kernel(proj_ref, st_ref, o_ref, nst_ref, *, bb):
    scale = RET_HD ** -0.5
    row = lax.broadcasted_iota(jnp.int32, (bb, bb * RET_HD), 0)
    lane_seq = lax.shift_right_logical(lax.broadcasted_iota(jnp.int32, (bb, bb * RET_HD), 1), 7)
    own = row == lane_seq
    for h in range(RET_HEADS):
        lo = h * RET_HD
        q8 = proj_ref[:, lo:lo + RET_HD]
        k8 = proj_ref[:, MIX_MAIN + lo:MIX_MAIN + lo + RET_HD] * scale
        v8 = proj_ref[:, 2 * MIX_MAIN + lo:2 * MIX_MAIN + lo + RET_HD]
        gate = proj_ref[:, 3 * MIX_MAIN + lo:3 * MIX_MAIN + lo + RET_HD]
        g = float(np.exp(_LOG_G[h]))
        vexp = jnp.where(own, jnp.concatenate([v8] * bb, axis=1), 0.0).astype(BF16)
        outer = lax.dot_general(k8.astype(BF16), vexp, TN_DIMS, preferred_element_type=F32)
        cols = []
        for j in range(bb):
            s_new = g * st_ref[j, h] + outer[:, j * RET_HD:(j + 1) * RET_HD]
            nst_ref[j, h] = s_new
            cols.append(s_new.astype(BF16))
        o_all = jnp.dot(q8.astype(BF16), jnp.concatenate(cols, axis=1), preferred_element_type=F32)
        o_all = jnp.where(own, o_all, 0.0)
        o = o_all[:, :RET_HD]
        for j in range(1, bb):
            o = o + o_all[:, j * RET_HD:(j + 1) * RET_HD]
        mu = jnp.mean(o, axis=-1, keepdims=True)
        oc = o - mu
        var = jnp.mean(oc * oc, axis=-1, keepdims=True)
        on = oc * lax.rsqrt(var + EPS)
        o_ref[:, lo:lo + RET_HD] = gate * jax.nn.sigmoid(gate) * on


def ret_decode(proj, state, layer, bb=16):
    nb = proj.shape[0]
    pw = proj.shape[1]
    return pl.pallas_call(
        functools.partial(_ret_decode_kernel, bb=bb),
        grid=(nb // bb,),
        in_specs=[pl.BlockSpec((bb, pw), lambda i: (i, 0)),
                  pl.BlockSpec((None, bb, RET_HEADS, RET_HD, RET_HD), lambda i: (layer, i, 0, 0, 0))],
        out_specs=[pl.BlockSpec((bb, MIX_MAIN), lambda i: (i, 0)),
                   pl.BlockSpec((bb, RET_HEADS, RET_HD, RET_HD), lambda i: (i, 0, 0, 0))],
        out_shape=[jax.ShapeDtypeStruct((nb, MIX_MAIN), F32),
                   jax.ShapeDtypeStruct(state.shape[1:], F32)],
        compiler_params=_cparams(1),
        name="ret_decode",
    )(proj, state)


def _mem_decode_kernel(q_ref, kt_ref, vt_ref, o_ref, *, bb, q_off, q_scale):
    rows = SUBLANES
    row = lax.broadcasted_iota(jnp.int32, (rows, MEM_W), 0)
    lane_head = _lane_block64(lax.broadcasted_iota(jnp.int32, (rows, MEM_W), 1))
    sel = row == lane_head
    seqs = range(bb)
    qexp = [jnp.where(sel, jnp.broadcast_to(q_ref[j:j + 1, q_off:q_off + MEM_W] * q_scale, (rows, MEM_W)), 0.0)
            for j in seqs]
    s = [jnp.dot(qexp[j], kt_ref[j], preferred_element_type=F32) for j in seqs]
    e = [jnp.exp(s[j] - jnp.max(s[j], axis=-1, keepdims=True)) for j in seqs]
    den = [jnp.sum(e[j], axis=-1, keepdims=True) for j in seqs]
    o = [lax.dot_general(e[j], vt_ref[j], NT_DIMS, preferred_element_type=F32) / den[j] for j in seqs]
    o_ref[...] = jnp.concatenate([jnp.sum(jnp.where(sel, o[j], 0.0), axis=0, keepdims=True) for j in seqs], axis=0)


def mem_decode(proj, cache_kt, cache_vt, layer, q_off, q_scale, bb=32):
    nb = proj.shape[0]
    pw = proj.shape[1]
    cmap = lambda i: (layer, i, 0, 0)
    return pl.pallas_call(
        functools.partial(_mem_decode_kernel, bb=bb, q_off=q_off, q_scale=q_scale),
        grid=(nb // bb,),
        in_specs=[pl.BlockSpec((bb, pw), lambda i: (i, 0)),
                  pl.BlockSpec((None, bb, MEM_W, MEM_TOKENS), cmap),
                  pl.BlockSpec((None, bb, MEM_W, MEM_TOKENS), cmap)],
        out_specs=pl.BlockSpec((bb, MEM_W), lambda i: (i, 0)),
        out_shape=jax.ShapeDtypeStruct((nb, MEM_W), F32),
        compiler_params=_cparams(1),
        name="mem_decode",
    )(proj, cache_kt, cache_vt)


def _swa_decode_kernel(q_ref, kn_ref, vn_ref, ckt_ref, cvt_ref, slope_ref, sink_ref, o_ref, *cache_out, bb):
    row = lax.broadcasted_iota(jnp.int32, (DEC_ROWS, KV_W), 0)
    lane_group = _lane_block64(lax.broadcasted_iota(jnp.int32, (DEC_ROWS, KV_W), 1))
    dist = (WINDOW - lax.broadcasted_iota(jnp.int32, (DEC_ROWS, WINDOW), 1)).astype(F32)
    bias = -slope_ref[...] * dist
    sink = sink_ref[...]
    own = [row == SWA_KV_HEADS * r + lane_group for r in range(SWA_REP)]
    seqs = range(bb)

    def expand(j):
        qexp = jnp.zeros((DEC_ROWS, KV_W), F32)
        for r in range(SWA_REP):
            qr = jnp.broadcast_to(q_ref[j:j + 1, r * KV_W:(r + 1) * KV_W], (DEC_ROWS, KV_W))
            qexp = jnp.where(own[r], qr, qexp)
        return qexp

    qexp = [expand(j) for j in seqs]
    s = [jnp.dot(qexp[j], ckt_ref[j], preferred_element_type=F32) + bias for j in seqs]
    s_new = [jnp.sum(qexp[j] * kn_ref[j:j + 1, :], axis=-1, keepdims=True) for j in seqs]
    m = [jnp.maximum(jnp.maximum(jnp.max(s[j], axis=-1, keepdims=True), s_new[j]), sink) for j in seqs]
    e = [jnp.exp(s[j] - m[j]) for j in seqs]
    e_new = [jnp.exp(s_new[j] - m[j]) for j in seqs]
    inv = [1.0 / (jnp.sum(e[j], axis=-1, keepdims=True) + e_new[j] + jnp.exp(sink - m[j])) for j in seqs]
    o = [lax.dot_general(e[j], cvt_ref[j], NT_DIMS, preferred_element_type=F32) for j in seqs]
    o = [(o[j] + jnp.concatenate([e_new[j], e_new[j]], axis=1) * vn_ref[j:j + 1, :])
         * jnp.concatenate([inv[j], inv[j]], axis=1) for j in seqs]
    for r in range(SWA_REP):
        o_ref[:, r * KV_W:(r + 1) * KV_W] = jnp.concatenate(
            [jnp.sum(jnp.where(own[r], o[j], 0.0), axis=0, keepdims=True) for j in seqs], axis=0)
    if cache_out:
        nkt_ref, nvt_ref = cache_out
        last = lax.broadcasted_iota(jnp.int32, (KV_W, WINDOW), 1) == WINDOW - 1
        knt = kn_ref[...].T
        vnt = vn_ref[...].T
        for j in seqs:
            nkt_ref[j] = jnp.where(last, jnp.broadcast_to(knt[:, j:j + 1], (KV_W, WINDOW)),
                                   pltpu.roll(ckt_ref[j], WINDOW - 1, 1))
            nvt_ref[j] = jnp.where(last, jnp.broadcast_to(vnt[:, j:j + 1], (KV_W, WINDOW)),
                                   pltpu.roll(cvt_ref[j], WINDOW - 1, 1))


def swa_decode(qproj, k_new, v_new, cache_kt, cache_vt, slope_rows, sink_rows, sink_layer, emit_cache, bb=32):
    nb = qproj.shape[0]
    pw = qproj.shape[1]
    cache_spec = pl.BlockSpec((bb, KV_W, WINDOW), lambda i: (i, 0, 0))
    cache_shape = jax.ShapeDtypeStruct((nb, KV_W, WINDOW), F32)
    n_cache = 2 if emit_cache else 0
    return pl.pallas_call(
        functools.partial(_swa_decode_kernel, bb=bb),
        grid=(nb // bb,),
        in_specs=[pl.BlockSpec((bb, pw), lambda i: (i, 0)),
                  pl.BlockSpec((bb, KV_W), lambda i: (i, 0)),
                  pl.BlockSpec((bb, KV_W), lambda i: (i, 0)),
                  cache_spec, cache_spec,
                  pl.BlockSpec((DEC_ROWS, LANES), lambda i: (0, 0)),
                  pl.BlockSpec((None, DEC_ROWS, LANES), lambda i: (sink_layer, 0, 0))],
        out_specs=[pl.BlockSpec((bb, MIX_MAIN), lambda i: (i, 0))] + [cache_spec] * n_cache,
        out_shape=[jax.ShapeDtypeStruct((nb, MIX_MAIN), F32)] + [cache_shape] * n_cache,
        compiler_params=_cparams(1),
        name="swa_decode",
    )(qproj, k_new, v_new, cache_kt, cache_vt, slope_rows, sink_rows)


def _slot_source(t):
    r, g = divmod(t, SWA_KV_HEADS)
    return SWA_REP * g + r


def _prep_w_in_b_kernel(w_ref, o_ref):
    low = lax.broadcasted_iota(jnp.int32, (D_MODEL, LANES), 1) < SWA_HD

    def half(slot, want_high):
        col = w_ref[:, (slot // 2) * LANES:(slot // 2 + 1) * LANES]
        return pltpu.roll(col, SWA_HD, 1) if (slot % 2 == 1) != want_high else col

    for c in range(MIX_MAIN // LANES):
        tile = jnp.where(low, half(_slot_source(2 * c), False), half(_slot_source(2 * c + 1), True))
        o_ref[:, c * LANES:(c + 1) * LANES] = (tile * ATTN_SCALE).astype(BF16)
    o_ref[:, MIX_MAIN:] = (w_ref[:, MIX_MAIN:] * ATTN_SCALE).astype(BF16)


def _prep_w_out_b_kernel(w_ref, o_ref):
    for t in range(SWA_HEADS):
        s = _slot_source(t)
        o_ref[t * SWA_HD:(t + 1) * SWA_HD, :] = w_ref[s * SWA_HD:(s + 1) * SWA_HD, :].astype(BF16)
    o_ref[MIX_MAIN:, :] = w_ref[MIX_MAIN:, :].astype(BF16)


def _prep_layer_b_weights(w, body, name):
    n_layers = w.shape[0]
    spec = pl.BlockSpec((None, D_MODEL, D_MODEL), lambda l: (l, 0, 0))
    return pl.pallas_call(
        body, grid=(n_layers,), in_specs=[spec], out_specs=spec,
        out_shape=jax.ShapeDtypeStruct(w.shape, BF16), compiler_params=_cparams(1), name=name,
    )(w)


def kernel(x_prompt, x_sample, cache_mem_k, cache_mem_v, state_ret, cache_swa_k, cache_swa_v, mem_prompt,
           norm_mix, w_in_a, w_out_a, w_in_b, w_out_b, attn_sinks, norm_mem, w_mem_kv, norm_kv, w_kv,
           norm_mlp, w_up, w_down, norm_final):
    batch, seq, d = x_prompt.shape
    nb = x_sample.shape[0]
    tm = TILE

    w_in_b16 = _prep_layer_b_weights(w_in_b, _prep_w_in_b_kernel, "prep_w_in_b")
    w_out_b16 = _prep_layer_b_weights(w_out_b, _prep_w_out_b_kernel, "prep_w_out_b")
    w_kv16 = w_kv.astype(BF16).reshape(1, d, 2 * KV_W)
    norm_kv1 = norm_kv.reshape(1, d)

    sinks_gr = attn_sinks.reshape(N_B, SWA_KV_HEADS, SWA_REP)
    sink_prompt = jnp.broadcast_to(sinks_gr[:, :, :, None, None], (N_B, SWA_KV_HEADS, SWA_REP, WINDOW, LANES)
                                   ).reshape(N_B, SWA_KV_HEADS, SWA_REP * WINDOW, LANES)
    sink_dec = jnp.concatenate([sinks_gr.swapaxes(1, 2).reshape(N_B, SWA_HEADS),
                                jnp.zeros((N_B, DEC_ROWS - SWA_HEADS), F32)], axis=1)
    sink_dec = jnp.broadcast_to(sink_dec[:, :, None], (N_B, DEC_ROWS, LANES))
    slope_dec = np.zeros((DEC_ROWS, LANES), np.float32)
    for r in range(SWA_REP):
        for g in range(SWA_KV_HEADS):
            slope_dec[SWA_KV_HEADS * r + g, :] = _SLOPES[g * SWA_REP + r]
    slope_dec = jnp.asarray(slope_dec)

    memx = mem_prompt.reshape(batch * MEM_TOKENS, d)
    mem_kv, mk16, mv16 = mem_kv_proj(memx, norm_mem, w_mem_kv)
    mem_kv = mem_kv.reshape(DEPTH, batch, MEM_TOKENS, 2 * MEM_W)
    mem_k_prompt = mem_kv[..., :MEM_W].reshape(DEPTH, batch, MEM_TOKENS, MEM_HEADS, MEM_HD)
    mem_v_prompt = mem_kv[..., MEM_W:].reshape(DEPTH, batch, MEM_TOKENS, MEM_HEADS, MEM_HD)
    mk16 = mk16.reshape(DEPTH, batch, MEM_TOKENS, MEM_W)
    mv16 = mv16.reshape(DEPTH, batch, MEM_TOKENS, MEM_W)

    x = x_prompt.reshape(batch * seq, d)
    ret_states = []
    kv_p = kv_p16 = None
    for l in range(DEPTH):
        if l < N_A:
            x, st = fused_mixer_a(x, norm_mix, w_in_a, w_out_a, mk16, mv16, l, batch, seq)
            ret_states.append(st)
        else:
            j = l - N_A
            x = fused_mixer_b(x, norm_mix, w_in_b16, w_out_b16, kv_p16, mk16, mv16, l, sink_prompt, j, batch, seq)
        if l == N_A - 1:
            x, kv_p, kv_p16 = mlp_block(x, norm_mlp, w_up, w_down, l, norm_final, False, tm=tm,
                                        kv_proj=(norm_kv1, w_kv16[0]))
        else:
            x = mlp_block(x, norm_mlp, w_up, w_down, l, norm_final, l == DEPTH - 1, tm=tm)
    y_prompt = x.reshape(batch, seq, d)
    ret_prompt = jnp.stack(ret_states)
    kv_p3 = kv_p.reshape(batch, seq, 2 * KV_W)
    swa_k_prompt = kv_p3[:, -WINDOW:, :KV_W].reshape(batch, WINDOW, SWA_KV_HEADS, SWA_HD)
    swa_v_prompt = kv_p3[:, -WINDOW:, KV_W:].reshape(batch, WINDOW, SWA_KV_HEADS, SWA_HD)

    cmkt = jnp.transpose(cache_mem_k, (0, 1, 3, 4, 2)).reshape(DEPTH, nb, MEM_W, MEM_TOKENS)
    cmvt = jnp.transpose(cache_mem_v, (0, 1, 3, 4, 2)).reshape(DEPTH, nb, MEM_W, MEM_TOKENS)
    cskt = jnp.transpose(cache_swa_k, (0, 2, 3, 1)).reshape(nb, KV_W, WINDOW)
    csvt = jnp.transpose(cache_swa_v, (0, 2, 3, 1)).reshape(nb, KV_W, WINDOW)
    xs = x_sample.reshape(nb, d)
    ret_states_s = []
    k_new = v_new = nkt = nvt = None
    for l in range(DEPTH):
        if l < N_A:
            proj = norm_matmul(xs, norm_mix, w_in_a, l, F32, tm=nb, n_chunk=MIX_MAIN)
            o_ret, st = ret_decode(proj, state_ret, l)
            ret_states_s.append(st)
            o_mem = mem_decode(proj, cmkt, cmvt, l, 4 * MIX_MAIN, ATTN_SCALE)
            attn = jnp.concatenate([o_ret, o_mem], axis=-1).astype(BF16)
            w_out, out_layer = w_out_a, l
        else:
            j = l - N_A
            if j == 0:
                kv_s = norm_matmul(xs, norm_kv1, w_kv16, 0, F32, tm=nb)
                k_new, v_new = kv_s[:, :KV_W], kv_s[:, KV_W:]
            qproj = norm_matmul(xs, norm_mix[N_A:], w_in_b16, j, F32, tm=nb)
            res = swa_decode(qproj, k_new, v_new, cskt, csvt, slope_dec, sink_dec, j, emit_cache=(j == 0))
            o_swa = res[0]
            if j == 0:
                nkt, nvt = res[1], res[2]
            o_mem = mem_decode(qproj, cmkt, cmvt, l, MIX_MAIN, 1.0)
            attn = jnp.concatenate([o_swa, o_mem], axis=-1).astype(BF16)
            w_out, out_layer = w_out_b16, j
        xs = post_decode(attn, w_out, out_layer, xs, norm_mlp, w_up, w_down, l, norm_final, l == DEPTH - 1)
    y_sample = xs.reshape(nb, 1, d)
    ret_sample = jnp.stack(ret_states_s)
    swa_k_sample = jnp.transpose(nkt.reshape(nb, SWA_KV_HEADS, SWA_HD, WINDOW), (0, 3, 1, 2))
    swa_v_sample = jnp.transpose(nvt.reshape(nb, SWA_KV_HEADS, SWA_HD, WINDOW), (0, 3, 1, 2))

    return (y_prompt, y_sample, ret_prompt, ret_sample, swa_k_prompt, swa_v_prompt, swa_k_sample, swa_v_sample,
            mem_k_prompt, mem_v_prompt)
```

```python
import functools
import math

import jax
import jax.numpy as jnp
import numpy as np
from jax import lax
from jax.experimental import pallas as pl
from jax.experimental.pallas import tpu as pltpu

F32 = jnp.float32
BF16 = jnp.bfloat16

D_MODEL = 1024
DEPTH = 4
N_A = 2
N_B = 2
MEM_TOKENS = 256
MEM_HEADS = 4
MEM_W = 256
MEM_HD = 64
MIX_MAIN = 768
RET_HEADS = 6
RET_HD = 128
CHUNK = 128
SWA_HEADS = 12
SWA_KV_HEADS = 4
SWA_REP = SWA_HEADS // SWA_KV_HEADS
SWA_HD = 64
KV_W = SWA_KV_HEADS * SWA_HD
WINDOW = 128
D_FF = 4096
EPS = 1e-6
NEG = -1e30
ATTN_SCALE = 0.125
LANES = 128
SUBLANES = 8
PIECE = 256
TILE = 512
DEC_ROWS = 16

VMEM_LIMIT = 56 * 1024 * 1024

NT_DIMS = (((1,), (1,)), ((), ()))
TN_DIMS = (((0,), (0,)), ((), ()))


def _alibi_slopes(n):
    def pow2(m):
        start = 2.0 ** (-8.0 / m)
        return [start ** (i + 1) for i in range(m)]

    if math.log2(n).is_integer():
        s = pow2(n)
    else:
        c = 2 ** int(math.floor(math.log2(n)))
        s = pow2(c) + pow2(2 * c)[0::2][: n - c]
    return np.asarray(s, np.float32)


_LOG_G = np.log1p(-(2.0 ** (-5.0 - np.arange(RET_HEADS)))).astype(np.float32).astype(np.float64)
_SLOPES = _alibi_slopes(SWA_HEADS).astype(np.float64)


def _retention_consts():
    idx = np.arange(CHUNK, dtype=np.float64)
    diff = idx[:, None] - idx[None, :]
    scale = RET_HD ** -0.5
    decay = np.where(diff >= 0, np.exp(np.maximum(diff, 0.0)[None] * _LOG_G[:, None, None]), 0.0) * scale
    w_q = np.exp((idx + 1.0)[None, :] * _LOG_G[:, None])
    w_k = np.exp((CHUNK - 1.0 - idx)[None, :] * _LOG_G[:, None]) * scale
    w_q = np.broadcast_to(w_q[:, :, None], (RET_HEADS, CHUNK, RET_HD))
    w_k = np.broadcast_to(w_k[:, :, None], (RET_HEADS, CHUNK, RET_HD))
    g_c = np.exp(CHUNK * _LOG_G)
    pair = lambda t: np.ascontiguousarray(np.concatenate([t[0::2], t[1::2]], axis=2), np.float32)
    return pair(decay), pair(w_q), pair(w_k), [float(v) for v in g_c]


def _swa_bias():
    i = np.arange(WINDOW)[:, None]
    j = np.arange(2 * WINDOW)[None, :]
    dist = i + WINDOW - j
    valid = (dist >= 0) & (dist <= WINDOW)
    out = np.zeros((SWA_KV_HEADS, SWA_REP * WINDOW, 2 * WINDOW), np.float32)
    for g in range(SWA_KV_HEADS):
        for r in range(SWA_REP):
            b = np.where(valid, -_SLOPES[g * SWA_REP + r] * dist, NEG)
            out[g, r * WINDOW:(r + 1) * WINDOW] = b
    return out


def _rmsnorm(x, g):
    ms = jnp.mean(x * x, axis=-1, keepdims=True)
    return (x * lax.rsqrt(ms + EPS)) * g


def _lane_block64(lane):
    return lax.shift_right_logical(lane, 6)


def _cparams(n_axes):
    return pltpu.CompilerParams(dimension_semantics=("arbitrary",) * n_axes, vmem_limit_bytes=VMEM_LIMIT)


def _layer_block(tail, layer, n_grid):
    zeros = (0,) * len(tail)
    if n_grid == 1:
        return pl.BlockSpec((None,) + tuple(tail), lambda i: (layer,) + zeros)
    return pl.BlockSpec((None,) + tuple(tail), lambda i, j: (layer,) + zeros)


def _norm_matmul_kernel(x_ref, g_ref, w_ref, o_ref, *, n_chunk):
    xn = _rmsnorm(x_ref[...], g_ref[...]).astype(BF16)
    n = w_ref.shape[1]
    for n0 in range(0, n, n_chunk):
        n1 = min(n0 + n_chunk, n)
        o_ref[:, n0:n1] = jnp.dot(xn, w_ref[:, n0:n1].astype(BF16), preferred_element_type=F32).astype(o_ref.dtype)


def norm_matmul(x, g, w, layer, out_dtype, tm, n_chunk=512):
    m, k = x.shape
    n = w.shape[2]
    return pl.pallas_call(
        functools.partial(_norm_matmul_kernel, n_chunk=n_chunk),
        grid=(m // tm,),
        in_specs=[pl.BlockSpec((tm, k), lambda i: (i, 0)),
                  _layer_block((1, k), layer, 1),
                  _layer_block((k, n), layer, 1)],
        out_specs=pl.BlockSpec((tm, n), lambda i: (i, 0)),
        out_shape=jax.ShapeDtypeStruct((m, n), out_dtype),
        compiler_params=_cparams(1),
        name="norm_matmul",
    )(x, g.reshape(g.shape[0], 1, k), w)


def _mem_kv_kernel(x_ref, g_ref, w_ref, o_ref, k16_ref, v16_ref):
    xn = _rmsnorm(x_ref[...], g_ref[...]).astype(BF16)
    kv = jnp.dot(xn, w_ref[...].astype(BF16), preferred_element_type=F32)
    o_ref[...] = kv
    k16_ref[...] = kv[:, :MEM_W].astype(BF16)
    v16_ref[...] = kv[:, MEM_W:].astype(BF16)


def mem_kv_proj(x, g, w):
    m, k = x.shape
    n_layers, _, n = w.shape
    half = pl.BlockSpec((None, m, MEM_W), lambda l: (l, 0, 0))
    return pl.pallas_call(
        _mem_kv_kernel,
        grid=(n_layers,),
        in_specs=[pl.BlockSpec((m, k), lambda l: (0, 0)),
                  pl.BlockSpec((None, 1, k), lambda l: (l, 0, 0)),
                  pl.BlockSpec((None, k, n), lambda l: (l, 0, 0))],
        out_specs=[pl.BlockSpec((None, m, n), lambda l: (l, 0, 0)), half, half],
        out_shape=[jax.ShapeDtypeStruct((n_layers, m, n), F32),
                   jax.ShapeDtypeStruct((n_layers, m, MEM_W), BF16),
                   jax.ShapeDtypeStruct((n_layers, m, MEM_W), BF16)],
        compiler_params=_cparams(1),
        name="mem_kv_proj",
    )(x, g.reshape(n_layers, 1, k), w)


def _mlp_kernel(x_ref, g_ref, wu_ref, wd_ref, gf_ref, *rest, fc, final_norm, with_kv):
    if with_kv:
        gkv_ref, wkv_ref, o_ref, kv_ref, kv16_ref, act_ref = rest
    else:
        o_ref, act_ref = rest
    xn = _rmsnorm(x_ref[...], g_ref[...]).astype(BF16)
    for c0 in range(0, D_FF, fc):
        h = jnp.dot(xn, wu_ref[:, c0:c0 + fc].astype(BF16), preferred_element_type=F32)
        act_ref[:, c0:c0 + fc] = jnp.square(jnp.maximum(h, 0.0)).astype(BF16)
    y = x_ref[...] + jnp.dot(act_ref[...], wd_ref[...].astype(BF16), preferred_element_type=F32)
    if with_kv:
        kv = jnp.dot(_rmsnorm(y, gkv_ref[...]).astype(BF16), wkv_ref[...], preferred_element_type=F32)
        kv_ref[...] = kv
        kv16_ref[...] = kv.astype(BF16)
    if final_norm:
        y = _rmsnorm(y, gf_ref[...])
    o_ref[...] = y


def mlp_block(x, g, w_up, w_down, layer, g_final, final_norm, tm, kv_proj=None, fc=512):
    m, d = x.shape
    with_kv = kv_proj is not None
    in_specs = [pl.BlockSpec((tm, d), lambda i: (i, 0)),
                _layer_block((1, d), layer, 1),
                pl.BlockSpec((None, d, D_FF), lambda i: (layer, 0, 0), pipeline_mode=pl.Buffered(1)),
                pl.BlockSpec((None, D_FF, d), lambda i: (layer, 0, 0), pipeline_mode=pl.Buffered(1)),
                pl.BlockSpec((1, d), lambda i: (0, 0))]
    args = [x, g.reshape(g.shape[0], 1, d), w_up, w_down, g_final.reshape(1, d)]
    out_specs = [pl.BlockSpec((tm, d), lambda i: (i, 0))]
    out_shape = [jax.ShapeDtypeStruct((m, d), F32)]
    if with_kv:
        in_specs += [pl.BlockSpec((1, d), lambda i: (0, 0)),
                     pl.BlockSpec((d, 2 * KV_W), lambda i: (0, 0), pipeline_mode=pl.Buffered(1))]
        args += list(kv_proj)
        out_specs += [pl.BlockSpec((tm, 2 * KV_W), lambda i: (i, 0))] * 2
        out_shape += [jax.ShapeDtypeStruct((m, 2 * KV_W), F32), jax.ShapeDtypeStruct((m, 2 * KV_W), BF16)]
    res = pl.pallas_call(
        functools.partial(_mlp_kernel, fc=fc, final_norm=final_norm, with_kv=with_kv),
        grid=(m // tm,),
        in_specs=in_specs,
        out_specs=out_specs,
        out_shape=out_shape,
        scratch_shapes=[pltpu.VMEM((tm, D_FF), BF16)],
        compiler_params=_cparams(1),
        name="mlp_block",
    )(*args)
    return res if with_kv else res[0]


def _post_decode_kernel(a_ref, wo_ref, x_ref, g_ref, wu_ref, wd_ref, gf_ref, o_ref, x1_scr, xn_scr, acc_scr, *,
                        final_norm):
    k = pl.program_id(0)

    @pl.when(k == 0)
    def _():
        x1 = x_ref[...] + jnp.dot(a_ref[...], wo_ref[...].astype(BF16), preferred_element_type=F32)
        x1_scr[...] = x1
        xn_scr[...] = _rmsnorm(x1, g_ref[...]).astype(BF16)
        acc_scr[...] = jnp.zeros_like(acc_scr)

    h = jnp.dot(xn_scr[...], wu_ref[...].astype(BF16), preferred_element_type=F32)
    act = jnp.square(jnp.maximum(h, 0.0)).astype(BF16)
    acc_scr[...] += jnp.dot(act, wd_ref[...].astype(BF16), preferred_element_type=F32)

    @pl.when(k == pl.num_programs(0) - 1)
    def _():
        y = x1_scr[...] + acc_scr[...]
        if final_norm:
            y = _rmsnorm(y, gf_ref[...])
        o_ref[...] = y


def post_decode(attn, w_out, out_layer, x, g, w_up, w_down, layer, g_final, final_norm, fc=1024):
    m, d = x.shape
    return pl.pallas_call(
        functools.partial(_post_decode_kernel, final_norm=final_norm),
        grid=(D_FF // fc,),
        in_specs=[pl.BlockSpec((m, d), lambda k: (0, 0)),
                  _layer_block((d, d), out_layer, 1),
                  pl.BlockSpec((m, d), lambda k: (0, 0)),
                  _layer_block((1, d), layer, 1),
                  pl.BlockSpec((None, d, fc), lambda k: (layer, 0, k)),
                  pl.BlockSpec((None, fc, d), lambda k: (layer, k, 0)),
                  pl.BlockSpec((1, d), lambda k: (0, 0))],
        out_specs=pl.BlockSpec((m, d), lambda k: (0, 0)),
        out_shape=jax.ShapeDtypeStruct((m, d), F32),
        scratch_shapes=[pltpu.VMEM((m, d), F32), pltpu.VMEM((m, d), BF16), pltpu.VMEM((m, d), F32)],
        compiler_params=_cparams(1),
        name="post_decode",
    )(attn, w_out, x, g.reshape(g.shape[0], 1, d), w_up, w_down, g_final.reshape(1, d))


def _mem_attention_tile(qm, mk, mv, fillers=()):
    t = qm.shape[0]
    lane_head = _lane_block64(lax.broadcasted_iota(jnp.int32, (t, MEM_W), 1))
    tok_head = _lane_block64(lax.broadcasted_iota(jnp.int32, (MEM_TOKENS, MEM_W), 1))
    probs = []
    vals = []
    for h in range(MEM_HEADS):
        qh = jnp.where(lane_head == h, qm, jnp.zeros_like(qm))
        s = lax.dot_general(qh, mk, NT_DIMS, preferred_element_type=F32)
        if h < len(fillers):
            fillers[h]()
        m = jnp.max(s, axis=-1, keepdims=True)
        e = jnp.exp(s - m)
        inv = 1.0 / jnp.sum(e, axis=-1, keepdims=True)
        probs.append((e * inv).astype(BF16))
        vals.append(jnp.where(tok_head == h, mv, jnp.zeros_like(mv)))
    return jnp.dot(jnp.concatenate(probs, axis=1), jnp.concatenate(vals, axis=0), preferred_element_type=F32)


def _project_tile(x_ref, g_ref, w_ref, p_ref, n_chunk):
    xn = _rmsnorm(x_ref[...], g_ref[...]).astype(BF16)
    n = w_ref.shape[1]
    for n0 in range(0, n, n_chunk):
        n1 = min(n0 + n_chunk, n)
        p_ref[:, n0:n1] = jnp.dot(xn, w_ref[:, n0:n1].astype(BF16), preferred_element_type=F32).astype(BF16)


def _projection_pieces(x_ref, g_ref, w_ref, p_ref):
    xn = _rmsnorm(x_ref[...], g_ref[...]).astype(BF16)

    def piece(c):
        def run():
            cols = slice(c * PIECE, (c + 1) * PIECE)
            p_ref[:, cols] = jnp.dot(xn, w_ref[:, cols].astype(BF16), preferred_element_type=F32).astype(BF16)
        return run

    return [piece(c) for c in range(w_ref.shape[1] // PIECE)]


def _out_projection_piece(x_ref, a_ref, w_ref, o_ref, c):
    def run():
        cols = slice(c * PIECE, (c + 1) * PIECE)
        o_ref[:, cols] = x_ref[:, cols] + jnp.dot(a_ref[:, :MIX_MAIN], w_ref[:MIX_MAIN, cols].astype(BF16),
                                                  preferred_element_type=F32)
    return run


def _both(f, g):
    def run():
        f()
        g()
    return run


def _block_diag2(a, b):
    z = jnp.zeros_like(a)
    return jnp.concatenate([jnp.concatenate([a, z], axis=1), jnp.concatenate([z, b], axis=1)], axis=0)


def _group_norm_gate(o, gate):
    mu = jnp.mean(o, axis=-1, keepdims=True)
    oc = o - mu
    var = jnp.mean(oc * oc, axis=-1, keepdims=True)
    return (gate * jax.nn.sigmoid(gate) * (oc * lax.rsqrt(var + EPS))).astype(BF16)


def _retention_tile(p_ref, a_ref, s_scr, decay_ref, wq_ref, wk_ref, g_c, fillers=()):
    pw = 2 * RET_HD
    pairs = range(RET_HEADS // 2)
    for c in range(TILE // CHUNK):
        rows = slice(c * CHUNK, (c + 1) * CHUNK)
        q2 = [p_ref[rows, hp * pw:(hp + 1) * pw] for hp in pairs]
        k2 = [p_ref[rows, MIX_MAIN + hp * pw:MIX_MAIN + (hp + 1) * pw] for hp in pairs]
        v2 = [p_ref[rows, 2 * MIX_MAIN + hp * pw:2 * MIX_MAIN + (hp + 1) * pw] for hp in pairs]
        sc = [lax.dot_general(q2[hp], _block_diag2(k2[hp][:, :RET_HD], k2[hp][:, RET_HD:]), NT_DIMS,
                              preferred_element_type=F32) for hp in pairs]
        kw = [(k2[hp].astype(F32) * wk_ref[hp]).astype(BF16) for hp in pairs]
        kv = [lax.dot_general(kw[hp], v2[hp], TN_DIMS, preferred_element_type=F32) for hp in pairs]
        if 2 * c < len(fillers):
            fillers[2 * c]()
        s_old = [s_scr[h] for h in range(RET_HEADS)]
        lhs = [jnp.concatenate([(sc[hp] * decay_ref[hp]).astype(BF16),
                                (q2[hp].astype(F32) * wq_ref[hp]).astype(BF16)], axis=1) for hp in pairs]
        rhs = [jnp.concatenate([_block_diag2(v2[hp][:, :RET_HD], v2[hp][:, RET_HD:]),
                                _block_diag2(s_old[2 * hp].astype(BF16), s_old[2 * hp + 1].astype(BF16))], axis=0)
               for hp in pairs]
        o2 = [jnp.dot(lhs[hp], rhs[hp], preferred_element_type=F32) for hp in pairs]
        if 2 * c + 1 < len(fillers):
            fillers[2 * c + 1]()
        for hp in pairs:
            s_scr[2 * hp] = g_c[2 * hp] * s_old[2 * hp] + kv[hp][:RET_HD, :RET_HD]
            s_scr[2 * hp + 1] = g_c[2 * hp + 1] * s_old[2 * hp + 1] + kv[hp][RET_HD:, RET_HD:]
        for hp in pairs:
            lo = hp * pw
            gate2 = p_ref[rows, 3 * MIX_MAIN + lo:3 * MIX_MAIN + lo + pw].astype(F32)
            a_ref[rows, lo:lo + RET_HD] = _group_norm_gate(o2[hp][:, :RET_HD], gate2[:, :RET_HD])
            a_ref[rows, lo + RET_HD:lo + pw] = _group_norm_gate(o2[hp][:, RET_HD:], gate2[:, RET_HD:])


def _fused_a_kernel(xc_ref, xn_ref, g_ref, win_ref, wout_ref, mk_ref, mv_ref, decay_ref, wq_ref, wk_ref,
                    o_ref, st_ref, p0, p1, a_scr, s_scr, *, g_c, tiles_per_seq):
    t = pl.program_id(0)

    @pl.when(t == 0)
    def _():
        _project_tile(xc_ref, g_ref, win_ref, p0, MIX_MAIN)

    @pl.when(t % tiles_per_seq == 0)
    def _():
        s_scr[...] = jnp.zeros_like(s_scr)

    def step(p_cur, p_next):
        pieces = _projection_pieces(xn_ref, g_ref, win_ref, p_next)
        n_slots = 2 * (TILE // CHUNK)
        slots = [_both(pieces[c], pieces[n_slots + c]) if n_slots + c < len(pieces) else pieces[c]
                 for c in range(n_slots)]
        _retention_tile(p_cur, a_scr, s_scr, decay_ref, wq_ref, wk_ref, g_c, slots)
        st_ref[...] = s_scr[...]
        qm = p_cur[:, 4 * MIX_MAIN:4 * MIX_MAIN + MEM_W] * ATTN_SCALE
        out_piece = functools.partial(_out_projection_piece, xc_ref, a_scr, wout_ref, o_ref)
        a_mem = _mem_attention_tile(qm, mk_ref[...], mv_ref[...],
                                    [out_piece(c) for c in range(D_MODEL // PIECE)]).astype(BF16)
        o_ref[...] += jnp.dot(a_mem, wout_ref[MIX_MAIN:, :].astype(BF16), preferred_element_type=F32)

    @pl.when(t % 2 == 0)
    def _():
        step(p0, p1)

    @pl.when(t % 2 == 1)
    def _():
        step(p1, p0)


def fused_mixer_a(x, g, w_in, w_out, mem_k, mem_v, layer, batch, seq):
    n_tiles = batch * seq // TILE
    tps = seq // TILE
    decay, w_q, w_k, g_c = _retention_consts()
    pw = w_in.shape[2]
    c3 = lambda t: (0, 0, 0)
    mem_spec = pl.BlockSpec((None, None, MEM_TOKENS, MEM_W), lambda t: (layer, t // tps, 0, 0))
    return pl.pallas_call(
        functools.partial(_fused_a_kernel, g_c=g_c, tiles_per_seq=tps),
        grid=(n_tiles,),
        in_specs=[pl.BlockSpec((TILE, D_MODEL), lambda t: (t, 0)),
                  pl.BlockSpec((TILE, D_MODEL), lambda t: (jnp.minimum(t + 1, n_tiles - 1), 0)),
                  _layer_block((1, D_MODEL), layer, 1),
                  pl.BlockSpec((None, D_MODEL, pw), lambda t: (layer, 0, 0), pipeline_mode=pl.Buffered(1)),
                  pl.BlockSpec((None, D_MODEL, D_MODEL), lambda t: (layer, 0, 0), pipeline_mode=pl.Buffered(1)),
                  mem_spec, mem_spec,
                  pl.BlockSpec((RET_HEADS // 2, CHUNK, 2 * CHUNK), c3),
                  pl.BlockSpec((RET_HEADS // 2, CHUNK, 2 * RET_HD), c3),
                  pl.BlockSpec((RET_HEADS // 2, CHUNK, 2 * RET_HD), c3)],
        out_specs=[pl.BlockSpec((TILE, D_MODEL), lambda t: (t, 0)),
                   pl.BlockSpec((None, RET_HEADS, RET_HD, RET_HD), lambda t: (t // tps, 0, 0, 0))],
        out_shape=[jax.ShapeDtypeStruct((batch * seq, D_MODEL), F32),
                   jax.ShapeDtypeStruct((batch, RET_HEADS, RET_HD, RET_HD), F32)],
        scratch_shapes=[pltpu.VMEM((TILE, pw), BF16), pltpu.VMEM((TILE, pw), BF16),
                        pltpu.VMEM((TILE, D_MODEL), BF16),
                        pltpu.VMEM((RET_HEADS, RET_HD, RET_HD), F32)],
        compiler_params=_cparams(1),
        name="fused_mixer_a",
    )(x, x, g.reshape(g.shape[0], 1, D_MODEL), w_in, w_out, mem_k, mem_v,
      jnp.asarray(decay), jnp.asarray(w_q), jnp.asarray(w_k))


def _swa_tile(p_ref, kvp_ref, kvc_ref, a_ref, bias_ref, sink_ref, is_first, fillers=()):
    lane_group = _lane_block64(lax.broadcasted_iota(jnp.int32, (WINDOW, KV_W), 1))
    key_group = _lane_block64(lax.broadcasted_iota(jnp.int32, (2 * WINDOW, KV_W), 1))
    for i in range(TILE // WINDOW):
        r0 = i * WINDOW
        if i == 0:
            kv2 = jnp.concatenate([kvp_ref[...], kvc_ref[0:WINDOW, :]], axis=0)
        else:
            kv2 = kvc_ref[r0 - WINDOW:r0 + WINDOW, :]
        k2 = kv2[:, :KV_W]
        v2 = kv2[:, KV_W:]
        qs = [p_ref[r0:r0 + WINDOW, r * KV_W:(r + 1) * KV_W] for r in range(SWA_REP)]
        probs = []
        vals = []
        for g in range(SWA_KV_HEADS):
            sel = lane_group == g
            qg = jnp.concatenate([jnp.where(sel, qr, jnp.zeros_like(qr)) for qr in qs], axis=0)
            s = lax.dot_general(qg, k2, NT_DIMS, preferred_element_type=F32) + bias_ref[g]
            if g == 1 and i < len(fillers):
                fillers[i]()
            s_prev = s[:, :WINDOW]
            s_cur = s[:, WINDOW:]
            if i == 0:
                s_prev = jnp.where(is_first, NEG, s_prev)
            sink = sink_ref[g]
            m = jnp.maximum(jnp.max(jnp.maximum(s_prev, s_cur), axis=-1, keepdims=True), sink)
            e_prev = jnp.exp(s_prev - m)
            e_cur = jnp.exp(s_cur - m)
            inv = 1.0 / (jnp.sum(e_prev + e_cur, axis=-1, keepdims=True) + jnp.exp(sink - m))
            probs.append((e_prev * inv).astype(BF16))
            probs.append((e_cur * inv).astype(BF16))
            vals.append(jnp.where(key_group == g, v2, jnp.zeros_like(v2)))
        o = jnp.dot(jnp.concatenate(probs, axis=1), jnp.concatenate(vals, axis=0), preferred_element_type=F32)
        for r in range(SWA_REP):
            a_ref[r0:r0 + WINDOW, r * KV_W:(r + 1) * KV_W] = o[r * WINDOW:(r + 1) * WINDOW].astype(BF16)


def _fused_b_kernel(xc_ref, xn_ref, g_ref, win_ref, wout_ref, kvp_ref, kvc_ref, mk_ref, mv_ref, bias_ref, sink_ref,
                    o_ref, p0, p1, a_scr, *, tiles_per_seq):
    t = pl.program_id(0)
    is_first = t % tiles_per_seq == 0

    @pl.when(t == 0)
    def _():
        _project_tile(xc_ref, g_ref, win_ref, p0, 2 * PIECE)

    out_piece = functools.partial(_out_projection_piece, xc_ref, a_scr, wout_ref, o_ref)

    def step(p_cur, p_next):
        pieces = _projection_pieces(xn_ref, g_ref, win_ref, p_next)
        _swa_tile(p_cur, kvp_ref, kvc_ref, a_scr, bias_ref, sink_ref, is_first, pieces)
        qm = p_cur[:, MIX_MAIN:MIX_MAIN + MEM_W]
        a_mem = _mem_attention_tile(qm, mk_ref[...], mv_ref[...],
                                    [out_piece(c) for c in range(D_MODEL // PIECE)]).astype(BF16)
        o_ref[...] += jnp.dot(a_mem, wout_ref[MIX_MAIN:, :], preferred_element_type=F32)

    @pl.when(t % 2 == 0)
    def _():
        step(p0, p1)

    @pl.when(t % 2 == 1)
    def _():
        step(p1, p0)


def fused_mixer_b(x, g, w_in, w_out, kv, mem_k, mem_v, layer, sink_rows, sub_layer, batch, seq):
    n_tiles = batch * seq // TILE
    tps = seq // TILE
    nw = TILE // WINDOW
    bias = jnp.asarray(_swa_bias())
    mem_spec = pl.BlockSpec((None, None, MEM_TOKENS, MEM_W), lambda t: (layer, t // tps, 0, 0))

    def prev_map(t):
        return (jnp.maximum(t * nw - 1, (t // tps) * (seq // WINDOW)), 0)

    return pl.pallas_call(
        functools.partial(_fused_b_kernel, tiles_per_seq=tps),
        grid=(n_tiles,),
        in_specs=[pl.BlockSpec((TILE, D_MODEL), lambda t: (t, 0)),
                  pl.BlockSpec((TILE, D_MODEL), lambda t: (jnp.minimum(t + 1, n_tiles - 1), 0)),
                  _layer_block((1, D_MODEL), layer, 1),
                  pl.BlockSpec((None, D_MODEL, D_MODEL), lambda t: (sub_layer, 0, 0), pipeline_mode=pl.Buffered(1)),
                  pl.BlockSpec((None, D_MODEL, D_MODEL), lambda t: (sub_layer, 0, 0), pipeline_mode=pl.Buffered(1)),
                  pl.BlockSpec((WINDOW, 2 * KV_W), prev_map),
                  pl.BlockSpec((TILE, 2 * KV_W), lambda t: (t, 0)),
                  mem_spec, mem_spec,
                  pl.BlockSpec((SWA_KV_HEADS, SWA_REP * WINDOW, 2 * WINDOW), lambda t: (0, 0, 0)),
                  pl.BlockSpec((None, SWA_KV_HEADS, SWA_REP * WINDOW, LANES), lambda t: (sub_layer, 0, 0, 0))],
        out_specs=pl.BlockSpec((TILE, D_MODEL), lambda t: (t, 0)),
        out_shape=jax.ShapeDtypeStruct((batch * seq, D_MODEL), F32),
        scratch_shapes=[pltpu.VMEM((TILE, D_MODEL), BF16), pltpu.VMEM((TILE, D_MODEL), BF16),
                        pltpu.VMEM((TILE, D_MODEL), BF16)],
        compiler_params=_cparams(1),
        name="fused_mixer_b",
    )(x, x, g.reshape(g.shape[0], 1, D_MODEL), w_in, w_out, kv, kv, mem_k, mem_v, bias, sink_rows)


def _ret_decode_kernel(proj_ref, st_ref, *rest, bb, layer, first_call):
    o_ref, stacked_ref = rest[-2:]
    if first_call:
        nst_ref = stacked_ref.at[layer]
        for other in range(stacked_ref.shape[0]):
            if other != layer:
                stacked_ref[other] = jnp.zeros(stacked_ref.shape[1:], F32)
    else:
        nst_ref = stacked_ref
    scale = RET_HD ** -0.5
    row = lax.broadcasted_iota(jnp.int32, (bb, bb * RET_HD), 0)
    lane_seq = lax.shift_right_logical(lax.broadcasted_iota(jnp.int32, (bb, bb * RET_HD), 1), 7)
    own = row == lane_seq
    for h in range(RET_HEADS):
        lo = h * RET_HD
        q8 = proj_ref[:, lo:lo + RET_HD]
        k8 = proj_ref[:, MIX_MAIN + lo:MIX_MAIN + lo + RET_HD] * scale
        v8 = proj_ref[:, 2 * MIX_MAIN + lo:2 * MIX_MAIN + lo + RET_HD]
        gate = proj_ref[:, 3 * MIX_MAIN + lo:3 * MIX_MAIN + lo + RET_HD]
        g = float(np.exp(_LOG_G[h]))
        vexp = jnp.where(own, jnp.concatenate([v8] * bb, axis=1), 0.0).astype(BF16)
        outer = lax.dot_general(k8.astype(BF16), vexp, TN_DIMS, preferred_element_type=F32)
        cols = []
        for j in range(bb):
            s_new = g * st_ref[j, h] + outer[:, j * RET_HD:(j + 1) * RET_HD]
            nst_ref[j, h] = s_new
            cols.append(s_new.astype(BF16))
        o_all = jnp.dot(q8.astype(BF16), jnp.concatenate(cols, axis=1), preferred_element_type=F32)
        o_all = jnp.where(own, o_all, 0.0)
        o = o_all[:, :RET_HD]
        for j in range(1, bb):
            o = o + o_all[:, j * RET_HD:(j + 1) * RET_HD]
        mu = jnp.mean(o, axis=-1, keepdims=True)
        oc = o - mu
        var = jnp.mean(oc * oc, axis=-1, keepdims=True)
        on = oc * lax.rsqrt(var + EPS)
        o_ref[:, lo:lo + RET_HD] = gate * jax.nn.sigmoid(gate) * on


def ret_decode(proj, state, layer, stacked_out=None, bb=16):
    nb = proj.shape[0]
    pw = proj.shape[1]
    n_layers = state.shape[0]
    tail = (RET_HEADS, RET_HD, RET_HD)
    in_specs = [pl.BlockSpec((bb, pw), lambda i: (i, 0)),
                pl.BlockSpec((None, bb) + tail, lambda i: (layer, i, 0, 0, 0))]
    args = [proj, state]
    first_call = stacked_out is None
    if first_call:
        aliases = {}
        stacked_spec = pl.BlockSpec((n_layers, bb) + tail, lambda i: (0, i, 0, 0, 0))
    else:
        in_specs.append(pl.BlockSpec(memory_space=pl.ANY))
        args.append(stacked_out)
        aliases = {2: 1}
        stacked_spec = pl.BlockSpec((None, bb) + tail, lambda i: (layer, i, 0, 0, 0))
    return pl.pallas_call(
        functools.partial(_ret_decode_kernel, bb=bb, layer=layer, first_call=first_call),
        grid=(nb // bb,),
        in_specs=in_specs,
        out_specs=[pl.BlockSpec((bb, MIX_MAIN), lambda i: (i, 0)), stacked_spec],
        out_shape=[jax.ShapeDtypeStruct((nb, MIX_MAIN), F32),
                   jax.ShapeDtypeStruct(state.shape, F32)],
        input_output_aliases=aliases,
        compiler_params=_cparams(1),
        name="ret_decode",
    )(*args)


def _mem_decode_kernel(q_ref, kt_ref, vt_ref, o_ref, *, bb, q_off, q_scale):
    rows = SUBLANES
    row = lax.broadcasted_iota(jnp.int32, (rows, MEM_W), 0)
    lane_head = _lane_block64(lax.broadcasted_iota(jnp.int32, (rows, MEM_W), 1))
    sel = row == lane_head
    seqs = range(bb)
    qexp = [jnp.where(sel, jnp.broadcast_to(q_ref[j:j + 1, q_off:q_off + MEM_W] * q_scale, (rows, MEM_W)), 0.0)
            for j in seqs]
    s = [jnp.dot(qexp[j], kt_ref[j], preferred_element_type=F32) for j in seqs]
    e = [jnp.exp(s[j] - jnp.max(s[j], axis=-1, keepdims=True)) for j in seqs]
    den = [jnp.sum(e[j], axis=-1, keepdims=True) for j in seqs]
    o = [lax.dot_general(e[j], vt_ref[j], NT_DIMS, preferred_element_type=F32) / den[j] for j in seqs]
    o_ref[...] = jnp.concatenate([jnp.sum(jnp.where(sel, o[j], 0.0), axis=0, keepdims=True) for j in seqs], axis=0)


def mem_decode(proj, cache_kt, cache_vt, layer, q_off, q_scale, bb=32):
    nb = proj.shape[0]
    pw = proj.shape[1]
    cmap = lambda i: (layer, i, 0, 0)
    return pl.pallas_call(
        functools.partial(_mem_decode_kernel, bb=bb, q_off=q_off, q_scale=q_scale),
        grid=(nb // bb,),
        in_specs=[pl.BlockSpec((bb, pw), lambda i: (i, 0)),
                  pl.BlockSpec((None, bb, MEM_W, MEM_TOKENS), cmap),
                  pl.BlockSpec((None, bb, MEM_W, MEM_TOKENS), cmap)],
        out_specs=pl.BlockSpec((bb, MEM_W), lambda i: (i, 0)),
        out_shape=jax.ShapeDtypeStruct((nb, MEM_W), F32),
        compiler_params=_cparams(1),
        name="mem_decode",
    )(proj, cache_kt, cache_vt)


def _swa_decode_kernel(q_ref, kn_ref, vn_ref, ckt_ref, cvt_ref, slope_ref, sink_ref, o_ref, *cache_out, bb):
    row = lax.broadcasted_iota(jnp.int32, (DEC_ROWS, KV_W), 0)
    lane_group = _lane_block64(lax.broadcasted_iota(jnp.int32, (DEC_ROWS, KV_W), 1))
    dist = (WINDOW - lax.broadcasted_iota(jnp.int32, (DEC_ROWS, WINDOW), 1)).astype(F32)
    bias = -slope_ref[...] * dist
    sink = sink_ref[...]
    own = [row == SWA_KV_HEADS * r + lane_group for r in range(SWA_REP)]
    seqs = range(bb)

    def expand(j):
        qexp = jnp.zeros((DEC_ROWS, KV_W), F32)
        for r in range(SWA_REP):
            qr = jnp.broadcast_to(q_ref[j:j + 1, r * KV_W:(r + 1) * KV_W], (DEC_ROWS, KV_W))
            qexp = jnp.where(own[r], qr, qexp)
        return qexp

    qexp = [expand(j) for j in seqs]
    s = [jnp.dot(qexp[j], ckt_ref[j], preferred_element_type=F32) + bias for j in seqs]
    s_new = [jnp.sum(qexp[j] * kn_ref[j:j + 1, :], axis=-1, keepdims=True) for j in seqs]
    m = [jnp.maximum(jnp.maximum(jnp.max(s[j], axis=-1, keepdims=True), s_new[j]), sink) for j in seqs]
    e = [jnp.exp(s[j] - m[j]) for j in seqs]
    e_new = [jnp.exp(s_new[j] - m[j]) for j in seqs]
    inv = [1.0 / (jnp.sum(e[j], axis=-1, keepdims=True) + e_new[j] + jnp.exp(sink - m[j])) for j in seqs]
    o = [lax.dot_general(e[j], cvt_ref[j], NT_DIMS, preferred_element_type=F32) for j in seqs]
    o = [(o[j] + jnp.concatenate([e_new[j], e_new[j]], axis=1) * vn_ref[j:j + 1, :])
         * jnp.concatenate([inv[j], inv[j]], axis=1) for j in seqs]
    for r in range(SWA_REP):
        o_ref[:, r * KV_W:(r + 1) * KV_W] = jnp.concatenate(
            [jnp.sum(jnp.where(own[r], o[j], 0.0), axis=0, keepdims=True) for j in seqs], axis=0)
    if cache_out:
        nkt_ref, nvt_ref = cache_out
        last = lax.broadcasted_iota(jnp.int32, (KV_W, WINDOW), 1) == WINDOW - 1
        knt = kn_ref[...].T
        vnt = vn_ref[...].T
        for j in seqs:
            nkt_ref[j] = jnp.where(last, jnp.broadcast_to(knt[:, j:j + 1], (KV_W, WINDOW)),
                                   pltpu.roll(ckt_ref[j], WINDOW - 1, 1))
            nvt_ref[j] = jnp.where(last, jnp.broadcast_to(vnt[:, j:j + 1], (KV_W, WINDOW)),
                                   pltpu.roll(cvt_ref[j], WINDOW - 1, 1))


def swa_decode(qproj, k_new, v_new, cache_kt, cache_vt, slope_rows, sink_rows, sink_layer, emit_cache, bb=32):
    nb = qproj.shape[0]
    pw = qproj.shape[1]
    cache_spec = pl.BlockSpec((bb, KV_W, WINDOW), lambda i: (i, 0, 0))
    cache_shape = jax.ShapeDtypeStruct((nb, KV_W, WINDOW), F32)
    n_cache = 2 if emit_cache else 0
    return pl.pallas_call(
        functools.partial(_swa_decode_kernel, bb=bb),
        grid=(nb // bb,),
        in_specs=[pl.BlockSpec((bb, pw), lambda i: (i, 0)),
                  pl.BlockSpec((bb, KV_W), lambda i: (i, 0)),
                  pl.BlockSpec((bb, KV_W), lambda i: (i, 0)),
                  cache_spec, cache_spec,
                  pl.BlockSpec((DEC_ROWS, LANES), lambda i: (0, 0)),
                  pl.BlockSpec((None, DEC_ROWS, LANES), lambda i: (sink_layer, 0, 0))],
        out_specs=[pl.BlockSpec((bb, MIX_MAIN), lambda i: (i, 0))] + [cache_spec] * n_cache,
        out_shape=[jax.ShapeDtypeStruct((nb, MIX_MAIN), F32)] + [cache_shape] * n_cache,
        compiler_params=_cparams(1),
        name="swa_decode",
    )(qproj, k_new, v_new, cache_kt, cache_vt, slope_rows, sink_rows)


def _slot_source(t):
    r, g = divmod(t, SWA_KV_HEADS)
    return SWA_REP * g + r


def _prep_w_in_b_kernel(w_ref, o_ref):
    low = lax.broadcasted_iota(jnp.int32, (D_MODEL, LANES), 1) < SWA_HD

    def half(slot, want_high):
        col = w_ref[:, (slot // 2) * LANES:(slot // 2 + 1) * LANES]
        return pltpu.roll(col, SWA_HD, 1) if (slot % 2 == 1) != want_high else col

    for c in range(MIX_MAIN // LANES):
        tile = jnp.where(low, half(_slot_source(2 * c), False), half(_slot_source(2 * c + 1), True))
        o_ref[:, c * LANES:(c + 1) * LANES] = (tile * ATTN_SCALE).astype(BF16)
    o_ref[:, MIX_MAIN:] = (w_ref[:, MIX_MAIN:] * ATTN_SCALE).astype(BF16)


def _prep_w_out_b_kernel(w_ref, o_ref):
    for t in range(SWA_HEADS):
        s = _slot_source(t)
        o_ref[t * SWA_HD:(t + 1) * SWA_HD, :] = w_ref[s * SWA_HD:(s + 1) * SWA_HD, :].astype(BF16)
    o_ref[MIX_MAIN:, :] = w_ref[MIX_MAIN:, :].astype(BF16)


def _prep_layer_b_weights(w, body, name):
    n_layers = w.shape[0]
    spec = pl.BlockSpec((None, D_MODEL, D_MODEL), lambda l: (l, 0, 0))
    return pl.pallas_call(
        body, grid=(n_layers,), in_specs=[spec], out_specs=spec,
        out_shape=jax.ShapeDtypeStruct(w.shape, BF16), compiler_params=_cparams(1), name=name,
    )(w)


def kernel(x_prompt, x_sample, cache_mem_k, cache_mem_v, state_ret, cache_swa_k, cache_swa_v, mem_prompt,
           norm_mix, w_in_a, w_out_a, w_in_b, w_out_b, attn_sinks, norm_mem, w_mem_kv, norm_kv, w_kv,
           norm_mlp, w_up, w_down, norm_final):
    batch, seq, d = x_prompt.shape
    nb = x_sample.shape[0]
    tm = TILE

    w_in_b16 = _prep_layer_b_weights(w_in_b, _prep_w_in_b_kernel, "prep_w_in_b")
    w_out_b16 = _prep_layer_b_weights(w_out_b, _prep_w_out_b_kernel, "prep_w_out_b")
    w_kv16 = w_kv.astype(BF16).reshape(1, d, 2 * KV_W)
    norm_kv1 = norm_kv.reshape(1, d)

    sinks_gr = attn_sinks.reshape(N_B, SWA_KV_HEADS, SWA_REP)
    sink_prompt = jnp.broadcast_to(sinks_gr[:, :, :, None, None], (N_B, SWA_KV_HEADS, SWA_REP, WINDOW, LANES)
                                   ).reshape(N_B, SWA_KV_HEADS, SWA_REP * WINDOW, LANES)
    sink_dec = jnp.concatenate([sinks_gr.swapaxes(1, 2).reshape(N_B, SWA_HEADS),
                                jnp.zeros((N_B, DEC_ROWS - SWA_HEADS), F32)], axis=1)
    sink_dec = jnp.broadcast_to(sink_dec[:, :, None], (N_B, DEC_ROWS, LANES))
    slope_dec = np.zeros((DEC_ROWS, LANES), np.float32)
    for r in range(SWA_REP):
        for g in range(SWA_KV_HEADS):
            slope_dec[SWA_KV_HEADS * r + g, :] = _SLOPES[g * SWA_REP + r]
    slope_dec = jnp.asarray(slope_dec)

    memx = mem_prompt.reshape(batch * MEM_TOKENS, d)
    mem_kv, mk16, mv16 = mem_kv_proj(memx, norm_mem, w_mem_kv)
    mem_kv = mem_kv.reshape(DEPTH, batch, MEM_TOKENS, 2 * MEM_W)
    mem_k_prompt = mem_kv[..., :MEM_W].reshape(DEPTH, batch, MEM_TOKENS, MEM_HEADS, MEM_HD)
    mem_v_prompt = mem_kv[..., MEM_W:].reshape(DEPTH, batch, MEM_TOKENS, MEM_HEADS, MEM_HD)
    mk16 = mk16.reshape(DEPTH, batch, MEM_TOKENS, MEM_W)
    mv16 = mv16.reshape(DEPTH, batch, MEM_TOKENS, MEM_W)

    x = x_prompt.reshape(batch * seq, d)
    ret_states = []
    kv_p = kv_p16 = None
    for l in range(DEPTH):
        if l < N_A:
            x, st = fused_mixer_a(x, norm_mix, w_in_a, w_out_a, mk16, mv16, l, batch, seq)
            ret_states.append(st)
        else:
            j = l - N_A
            x = fused_mixer_b(x, norm_mix, w_in_b16, w_out_b16, kv_p16, mk16, mv16, l, sink_prompt, j, batch, seq)
        if l == N_A - 1:
            x, kv_p, kv_p16 = mlp_block(x, norm_mlp, w_up, w_down, l, norm_final, False, tm=tm,
                                        kv_proj=(norm_kv1, w_kv16[0]))
        else:
            x = mlp_block(x, norm_mlp, w_up, w_down, l, norm_final, l == DEPTH - 1, tm=tm)
    y_prompt = x.reshape(batch, seq, d)
    ret_prompt = jnp.stack(ret_states)
    kv_p3 = kv_p.reshape(batch, seq, 2 * KV_W)
    swa_k_prompt = kv_p3[:, -WINDOW:, :KV_W].reshape(batch, WINDOW, SWA_KV_HEADS, SWA_HD)
    swa_v_prompt = kv_p3[:, -WINDOW:, KV_W:].reshape(batch, WINDOW, SWA_KV_HEADS, SWA_HD)

    cmkt = jnp.transpose(cache_mem_k, (0, 1, 3, 4, 2)).reshape(DEPTH, nb, MEM_W, MEM_TOKENS)
    cmvt = jnp.transpose(cache_mem_v, (0, 1, 3, 4, 2)).reshape(DEPTH, nb, MEM_W, MEM_TOKENS)
    cskt = jnp.transpose(cache_swa_k, (0, 2, 3, 1)).reshape(nb, KV_W, WINDOW)
    csvt = jnp.transpose(cache_swa_v, (0, 2, 3, 1)).reshape(nb, KV_W, WINDOW)
    xs = x_sample.reshape(nb, d)
    ret_sample = None
    k_new = v_new = nkt = nvt = None
    for l in range(DEPTH):
        if l < N_A:
            proj = norm_matmul(xs, norm_mix, w_in_a, l, F32, tm=nb, n_chunk=MIX_MAIN)
            o_ret, ret_sample = ret_decode(proj, state_ret, l, stacked_out=ret_sample)
            o_mem = mem_decode(proj, cmkt, cmvt, l, 4 * MIX_MAIN, ATTN_SCALE)
            attn = jnp.concatenate([o_ret, o_mem], axis=-1).astype(BF16)
            w_out, out_layer = w_out_a, l
        else:
            j = l - N_A
            if j == 0:
                kv_s = norm_matmul(xs, norm_kv1, w_kv16, 0, F32, tm=nb)
                k_new, v_new = kv_s[:, :KV_W], kv_s[:, KV_W:]
            qproj = norm_matmul(xs, norm_mix[N_A:], w_in_b16, j, F32, tm=nb)
            res = swa_decode(qproj, k_new, v_new, cskt, csvt, slope_dec, sink_dec, j, emit_cache=(j == 0))
            o_swa = res[0]
            if j == 0:
                nkt, nvt = res[1], res[2]
            o_mem = mem_decode(qproj, cmkt, cmvt, l, MIX_MAIN, 1.0)
            attn = jnp.concatenate([o_swa, o_mem], axis=-1).astype(BF16)
            w_out, out_layer = w_out_b16, j
        xs = post_decode(attn, w_out, out_layer, xs, norm_mlp, w_up, w_down, l, norm_final, l == DEPTH - 1)
    y_sample = xs.reshape(nb, 1, d)
    swa_k_sample = jnp.transpose(nkt.reshape(nb, SWA_KV_HEADS, SWA_HD, WINDOW), (0, 3, 1, 2))
    swa_v_sample = jnp.transpose(nvt.reshape(nb, SWA_KV_HEADS, SWA_HD, WINDOW), (0, 3, 1, 2))

    return (y_prompt, y_sample, ret_prompt, ret_sample, swa_k_prompt, swa_v_prompt, swa_k_sample, swa_v_sample,
            mem_k_prompt, mem_v_prompt)
```

```python
import functools
import math

import jax
import jax.numpy as jnp
import numpy as np
from jax import lax
from jax.experimental import pallas as pl
from jax.experimental.pallas import tpu as pltpu

F32 = jnp.float32
BF16 = jnp.bfloat16

D_MODEL = 1024
DEPTH = 4
N_A = 2
N_B = 2
MEM_TOKENS = 256
MEM_HEADS = 4
MEM_W = 256
MEM_HD = 64
MIX_MAIN = 768
RET_HEADS = 6
RET_HD = 128
CHUNK = 128
SWA_HEADS = 12
SWA_KV_HEADS = 4
SWA_REP = SWA_HEADS // SWA_KV_HEADS
SWA_HD = 64
KV_W = SWA_KV_HEADS * SWA_HD
WINDOW = 128
D_FF = 4096
EPS = 1e-6
NEG = -1e30
ATTN_SCALE = 0.125
LANES = 128
SUBLANES = 8
PIECE = 256
TILE = 512
DEC_ROWS = 16

VMEM_LIMIT = 56 * 1024 * 1024

NT_DIMS = (((1,), (1,)), ((), ()))
TN_DIMS = (((0,), (0,)), ((), ()))


def _alibi_slopes(n):
    def pow2(m):
        start = 2.0 ** (-8.0 / m)
        return [start ** (i + 1) for i in range(m)]

    if math.log2(n).is_integer():
        s = pow2(n)
    else:
        c = 2 ** int(math.floor(math.log2(n)))
        s = pow2(c) + pow2(2 * c)[0::2][: n - c]
    return np.asarray(s, np.float32)


_LOG_G = np.log1p(-(2.0 ** (-5.0 - np.arange(RET_HEADS)))).astype(np.float32).astype(np.float64)
_SLOPES = _alibi_slopes(SWA_HEADS).astype(np.float64)


def _retention_consts():
    idx = np.arange(CHUNK, dtype=np.float64)
    diff = idx[:, None] - idx[None, :]
    scale = RET_HD ** -0.5
    decay = np.where(diff >= 0, np.exp(np.maximum(diff, 0.0)[None] * _LOG_G[:, None, None]), 0.0) * scale
    w_q = np.exp((idx + 1.0)[None, :] * _LOG_G[:, None])
    w_k = np.exp((CHUNK - 1.0 - idx)[None, :] * _LOG_G[:, None]) * scale
    w_q = np.broadcast_to(w_q[:, :, None], (RET_HEADS, CHUNK, RET_HD))
    w_k = np.broadcast_to(w_k[:, :, None], (RET_HEADS, CHUNK, RET_HD))
    g_c = np.exp(CHUNK * _LOG_G)
    pair = lambda t: np.ascontiguousarray(np.concatenate([t[0::2], t[1::2]], axis=2), np.float32)
    return pair(decay), pair(w_q), pair(w_k), [float(v) for v in g_c]


def _swa_bias():
    i = np.arange(WINDOW)[:, None]
    j = np.arange(2 * WINDOW)[None, :]
    dist = i + WINDOW - j
    valid = (dist >= 0) & (dist <= WINDOW)
    out = np.zeros((SWA_KV_HEADS, SWA_REP * WINDOW, 2 * WINDOW), np.float32)
    for g in range(SWA_KV_HEADS):
        for r in range(SWA_REP):
            b = np.where(valid, -_SLOPES[g * SWA_REP + r] * dist, NEG)
            out[g, r * WINDOW:(r + 1) * WINDOW] = b
    return out


def _rmsnorm(x, g):
    ms = jnp.mean(x * x, axis=-1, keepdims=True)
    return (x * lax.rsqrt(ms + EPS)) * g


def _lane_block64(lane):
    return lax.shift_right_logical(lane, 6)


def _cparams(n_axes):
    return pltpu.CompilerParams(dimension_semantics=("arbitrary",) * n_axes, vmem_limit_bytes=VMEM_LIMIT)


def _layer_block(tail, layer, n_grid):
    zeros = (0,) * len(tail)
    if n_grid == 1:
        return pl.BlockSpec((None,) + tuple(tail), lambda i: (layer,) + zeros)
    return pl.BlockSpec((None,) + tuple(tail), lambda i, j: (layer,) + zeros)


def _norm_matmul_kernel(x_ref, g_ref, w_ref, o_ref, *, n_chunk):
    xn = _rmsnorm(x_ref[...], g_ref[...]).astype(BF16)
    n = w_ref.shape[1]
    for n0 in range(0, n, n_chunk):
        n1 = min(n0 + n_chunk, n)
        o_ref[:, n0:n1] = jnp.dot(xn, w_ref[:, n0:n1].astype(BF16), preferred_element_type=F32).astype(o_ref.dtype)


def norm_matmul(x, g, w, layer, out_dtype, tm, n_chunk=512):
    m, k = x.shape
    n = w.shape[2]
    return pl.pallas_call(
        functools.partial(_norm_matmul_kernel, n_chunk=n_chunk),
        grid=(m // tm,),
        in_specs=[pl.BlockSpec((tm, k), lambda i: (i, 0)),
                  _layer_block((1, k), layer, 1),
                  _layer_block((k, n), layer, 1)],
        out_specs=pl.BlockSpec((tm, n), lambda i: (i, 0)),
        out_shape=jax.ShapeDtypeStruct((m, n), out_dtype),
        compiler_params=_cparams(1),
        name="norm_matmul",
    )(x, g.reshape(g.shape[0], 1, k), w)


def _mem_kv_kernel(x_ref, g_ref, w_ref, o_ref, k16_ref, v16_ref):
    xn = _rmsnorm(x_ref[...], g_ref[...]).astype(BF16)
    kv = jnp.dot(xn, w_ref[...].astype(BF16), preferred_element_type=F32)
    o_ref[...] = kv
    k16_ref[...] = kv[:, :MEM_W].astype(BF16)
    v16_ref[...] = kv[:, MEM_W:].astype(BF16)


def mem_kv_proj(x, g, w):
    m, k = x.shape
    n_layers, _, n = w.shape
    half = pl.BlockSpec((None, m, MEM_W), lambda l: (l, 0, 0))
    return pl.pallas_call(
        _mem_kv_kernel,
        grid=(n_layers,),
        in_specs=[pl.BlockSpec((m, k), lambda l: (0, 0)),
                  pl.BlockSpec((None, 1, k), lambda l: (l, 0, 0)),
                  pl.BlockSpec((None, k, n), lambda l: (l, 0, 0))],
        out_specs=[pl.BlockSpec((None, m, n), lambda l: (l, 0, 0)), half, half],
        out_shape=[jax.ShapeDtypeStruct((n_layers, m, n), F32),
                   jax.ShapeDtypeStruct((n_layers, m, MEM_W), BF16),
                   jax.ShapeDtypeStruct((n_layers, m, MEM_W), BF16)],
        compiler_params=_cparams(1),
        name="mem_kv_proj",
    )(x, g.reshape(n_layers, 1, k), w)


def _mlp_rows(x, g_ref, wu_ref, wd_ref, act_ref, fc):
    xn = _rmsnorm(x, g_ref[...]).astype(BF16)
    for c0 in range(0, D_FF, fc):
        h = jnp.dot(xn, wu_ref[:, c0:c0 + fc].astype(BF16), preferred_element_type=F32)
        act_ref[:, c0:c0 + fc] = jnp.square(jnp.maximum(h, 0.0)).astype(BF16)
    return x + jnp.dot(act_ref[...], wd_ref[...].astype(BF16), preferred_element_type=F32)


def _mlp_kernel(x_ref, g_ref, wu_ref, wd_ref, gf_ref, *rest, fc, final_norm, with_kv, with_rows):
    rest = list(rest)
    gkv_ref, wkv_ref = (rest.pop(0), rest.pop(0)) if with_kv else (None, None)
    xs_ref = rest.pop(0) if with_rows else None
    o_ref = rest.pop(0)
    kv_ref, kv16_ref = (rest.pop(0), rest.pop(0)) if with_kv else (None, None)
    os_ref = rest.pop(0) if with_rows else None
    act_ref = rest.pop(0)
    if with_rows:
        acts_ref = rest.pop(0)

        @pl.when(pl.program_id(0) == 0)
        def _():
            ys = _mlp_rows(xs_ref[...], g_ref, wu_ref, wd_ref, acts_ref, fc)
            os_ref[...] = _rmsnorm(ys, gf_ref[...]) if final_norm else ys

    y = _mlp_rows(x_ref[...], g_ref, wu_ref, wd_ref, act_ref, fc)
    if with_kv:
        kv = jnp.dot(_rmsnorm(y, gkv_ref[...]).astype(BF16), wkv_ref[...], preferred_element_type=F32)
        kv_ref[...] = kv
        kv16_ref[...] = kv.astype(BF16)
    if final_norm:
        y = _rmsnorm(y, gf_ref[...])
    o_ref[...] = y


def mlp_block(x, g, w_up, w_down, layer, g_final, final_norm, tm, kv_proj=None, extra_rows=None, fc=512):
    m, d = x.shape
    with_kv = kv_proj is not None
    with_rows = extra_rows is not None
    in_specs = [pl.BlockSpec((tm, d), lambda i: (i, 0)),
                _layer_block((1, d), layer, 1),
                pl.BlockSpec((None, d, D_FF), lambda i: (layer, 0, 0), pipeline_mode=pl.Buffered(1)),
                pl.BlockSpec((None, D_FF, d), lambda i: (layer, 0, 0), pipeline_mode=pl.Buffered(1)),
                pl.BlockSpec((1, d), lambda i: (0, 0))]
    args = [x, g.reshape(g.shape[0], 1, d), w_up, w_down, g_final.reshape(1, d)]
    out_specs = [pl.BlockSpec((tm, d), lambda i: (i, 0))]
    out_shape = [jax.ShapeDtypeStruct((m, d), F32)]
    if with_kv:
        in_specs += [pl.BlockSpec((1, d), lambda i: (0, 0)),
                     pl.BlockSpec((d, 2 * KV_W), lambda i: (0, 0), pipeline_mode=pl.Buffered(1))]
        args += list(kv_proj)
        out_specs += [pl.BlockSpec((tm, 2 * KV_W), lambda i: (i, 0))] * 2
        out_shape += [jax.ShapeDtypeStruct((m, 2 * KV_W), F32), jax.ShapeDtypeStruct((m, 2 * KV_W), BF16)]
    scratch = [pltpu.VMEM((tm, D_FF), BF16)]
    if with_rows:
        rows = extra_rows.shape[0]
        in_specs.append(pl.BlockSpec((rows, d), lambda i: (0, 0)))
        args.append(extra_rows)
        out_specs.append(pl.BlockSpec((rows, d), lambda i: (0, 0)))
        out_shape.append(jax.ShapeDtypeStruct((rows, d), F32))
        scratch.append(pltpu.VMEM((rows, D_FF), BF16))
    res = pl.pallas_call(
        functools.partial(_mlp_kernel, fc=fc, final_norm=final_norm, with_kv=with_kv, with_rows=with_rows),
        grid=(m // tm,),
        in_specs=in_specs,
        out_specs=out_specs,
        out_shape=out_shape,
        scratch_shapes=scratch,
        compiler_params=_cparams(1),
        name="mlp_block",
    )(*args)
    return res if len(res) > 1 else res[0]


def _out_residual_kernel(a_ref, w_ref, x_ref, o_ref):
    o_ref[...] = x_ref[...] + jnp.dot(a_ref[...], w_ref[...].astype(BF16), preferred_element_type=F32)


def out_residual(a, w, layer, x):
    m, k = a.shape
    n = w.shape[2]
    return pl.pallas_call(
        _out_residual_kernel,
        grid=(1,),
        in_specs=[pl.BlockSpec((m, k), lambda i: (0, 0)),
                  _layer_block((k, n), layer, 1),
                  pl.BlockSpec((m, n), lambda i: (0, 0))],
        out_specs=pl.BlockSpec((m, n), lambda i: (0, 0)),
        out_shape=jax.ShapeDtypeStruct((m, n), F32),
        compiler_params=_cparams(1),
        name="out_residual",
    )(a, w, x)


def _mem_attention_tile(qm, mk, mv, fillers=()):
    t = qm.shape[0]
    lane_head = _lane_block64(lax.broadcasted_iota(jnp.int32, (t, MEM_W), 1))
    tok_head = _lane_block64(lax.broadcasted_iota(jnp.int32, (MEM_TOKENS, MEM_W), 1))
    probs = []
    vals = []
    for h in range(MEM_HEADS):
        qh = jnp.where(lane_head == h, qm, jnp.zeros_like(qm))
        s = lax.dot_general(qh, mk, NT_DIMS, preferred_element_type=F32)
        if h < len(fillers):
            fillers[h]()
        m = jnp.max(s, axis=-1, keepdims=True)
        e = jnp.exp(s - m)
        inv = 1.0 / jnp.sum(e, axis=-1, keepdims=True)
        probs.append((e * inv).astype(BF16))
        vals.append(jnp.where(tok_head == h, mv, jnp.zeros_like(mv)))
    return jnp.dot(jnp.concatenate(probs, axis=1), jnp.concatenate(vals, axis=0), preferred_element_type=F32)


def _project_tile(x_ref, g_ref, w_ref, p_ref, n_chunk):
    xn = _rmsnorm(x_ref[...], g_ref[...]).astype(BF16)
    n = w_ref.shape[1]
    for n0 in range(0, n, n_chunk):
        n1 = min(n0 + n_chunk, n)
        p_ref[:, n0:n1] = jnp.dot(xn, w_ref[:, n0:n1].astype(BF16), preferred_element_type=F32).astype(BF16)


def _projection_pieces(x_ref, g_ref, w_ref, p_ref):
    xn = _rmsnorm(x_ref[...], g_ref[...]).astype(BF16)

    def piece(c):
        def run():
            cols = slice(c * PIECE, (c + 1) * PIECE)
            p_ref[:, cols] = jnp.dot(xn, w_ref[:, cols].astype(BF16), preferred_element_type=F32).astype(BF16)
        return run

    return [piece(c) for c in range(w_ref.shape[1] // PIECE)]


def _out_projection_piece(x_ref, a_ref, w_ref, o_ref, c):
    def run():
        cols = slice(c * PIECE, (c + 1) * PIECE)
        o_ref[:, cols] = x_ref[:, cols] + jnp.dot(a_ref[:, :MIX_MAIN], w_ref[:MIX_MAIN, cols].astype(BF16),
                                                  preferred_element_type=F32)
    return run


def _both(f, g):
    def run():
        f()
        g()
    return run


def _block_diag2(a, b):
    z = jnp.zeros_like(a)
    return jnp.concatenate([jnp.concatenate([a, z], axis=1), jnp.concatenate([z, b], axis=1)], axis=0)


def _group_norm_gate(o, gate):
    mu = jnp.mean(o, axis=-1, keepdims=True)
    oc = o - mu
    var = jnp.mean(oc * oc, axis=-1, keepdims=True)
    return (gate * jax.nn.sigmoid(gate) * (oc * lax.rsqrt(var + EPS))).astype(BF16)


def _retention_tile(p_ref, a_ref, s_scr, decay_ref, wq_ref, wk_ref, g_c, fillers=()):
    pw = 2 * RET_HD
    pairs = range(RET_HEADS // 2)
    for c in range(TILE // CHUNK):
        rows = slice(c * CHUNK, (c + 1) * CHUNK)
        q2 = [p_ref[rows, hp * pw:(hp + 1) * pw] for hp in pairs]
        k2 = [p_ref[rows, MIX_MAIN + hp * pw:MIX_MAIN + (hp + 1) * pw] for hp in pairs]
        v2 = [p_ref[rows, 2 * MIX_MAIN + hp * pw:2 * MIX_MAIN + (hp + 1) * pw] for hp in pairs]
        sc = [lax.dot_general(q2[hp], _block_diag2(k2[hp][:, :RET_HD], k2[hp][:, RET_HD:]), NT_DIMS,
                              preferred_element_type=F32) for hp in pairs]
        kw = [(k2[hp].astype(F32) * wk_ref[hp]).astype(BF16) for hp in pairs]
        kv = [lax.dot_general(kw[hp], v2[hp], TN_DIMS, preferred_element_type=F32) for hp in pairs]
        if 2 * c < len(fillers):
            fillers[2 * c]()
        s_old = [s_scr[h] for h in range(RET_HEADS)]
        lhs = [jnp.concatenate([(sc[hp] * decay_ref[hp]).astype(BF16),
                                (q2[hp].astype(F32) * wq_ref[hp]).astype(BF16)], axis=1) for hp in pairs]
        rhs = [jnp.concatenate([_block_diag2(v2[hp][:, :RET_HD], v2[hp][:, RET_HD:]),
                                _block_diag2(s_old[2 * hp].astype(BF16), s_old[2 * hp + 1].astype(BF16))], axis=0)
               for hp in pairs]
        o2 = [jnp.dot(lhs[hp], rhs[hp], preferred_element_type=F32) for hp in pairs]
        if 2 * c + 1 < len(fillers):
            fillers[2 * c + 1]()
        for hp in pairs:
            s_scr[2 * hp] = g_c[2 * hp] * s_old[2 * hp] + kv[hp][:RET_HD, :RET_HD]
            s_scr[2 * hp + 1] = g_c[2 * hp + 1] * s_old[2 * hp + 1] + kv[hp][RET_HD:, RET_HD:]
        for hp in pairs:
            lo = hp * pw
            gate2 = p_ref[rows, 3 * MIX_MAIN + lo:3 * MIX_MAIN + lo + pw].astype(F32)
            a_ref[rows, lo:lo + RET_HD] = _group_norm_gate(o2[hp][:, :RET_HD], gate2[:, :RET_HD])
            a_ref[rows, lo + RET_HD:lo + pw] = _group_norm_gate(o2[hp][:, RET_HD:], gate2[:, RET_HD:])


def _fused_a_kernel(xc_ref, xn_ref, g_ref, win_ref, wout_ref, mk_ref, mv_ref, decay_ref, wq_ref, wk_ref,
                    o_ref, st_ref, p0, p1, a_scr, s_scr, *, g_c, tiles_per_seq):
    t = pl.program_id(0)

    @pl.when(t == 0)
    def _():
        _project_tile(xc_ref, g_ref, win_ref, p0, MIX_MAIN)

    @pl.when(t % tiles_per_seq == 0)
    def _():
        s_scr[...] = jnp.zeros_like(s_scr)

    def step(p_cur, p_next):
        pieces = _projection_pieces(xn_ref, g_ref, win_ref, p_next)
        n_slots = 2 * (TILE // CHUNK)
        slots = [_both(pieces[c], pieces[n_slots + c]) if n_slots + c < len(pieces) else pieces[c]
                 for c in range(n_slots)]
        _retention_tile(p_cur, a_scr, s_scr, decay_ref, wq_ref, wk_ref, g_c, slots)
        st_ref[...] = s_scr[...]
        qm = p_cur[:, 4 * MIX_MAIN:4 * MIX_MAIN + MEM_W] * ATTN_SCALE
        out_piece = functools.partial(_out_projection_piece, xc_ref, a_scr, wout_ref, o_ref)
        a_mem = _mem_attention_tile(qm, mk_ref[...], mv_ref[...],
                                    [out_piece(c) for c in range(D_MODEL // PIECE)]).astype(BF16)
        o_ref[...] += jnp.dot(a_mem, wout_ref[MIX_MAIN:, :].astype(BF16), preferred_element_type=F32)

    @pl.when(t % 2 == 0)
    def _():
        step(p0, p1)

    @pl.when(t % 2 == 1)
    def _():
        step(p1, p0)


def fused_mixer_a(x, g, w_in, w_out, mem_k, mem_v, layer, batch, seq):
    n_tiles = batch * seq // TILE
    tps = seq // TILE
    decay, w_q, w_k, g_c = _retention_consts()
    pw = w_in.shape[2]
    c3 = lambda t: (0, 0, 0)
    mem_spec = pl.BlockSpec((None, None, MEM_TOKENS, MEM_W), lambda t: (layer, t // tps, 0, 0))
    return pl.pallas_call(
        functools.partial(_fused_a_kernel, g_c=g_c, tiles_per_seq=tps),
        grid=(n_tiles,),
        in_specs=[pl.BlockSpec((TILE, D_MODEL), lambda t: (t, 0)),
                  pl.BlockSpec((TILE, D_MODEL), lambda t: (jnp.minimum(t + 1, n_tiles - 1), 0)),
                  _layer_block((1, D_MODEL), layer, 1),
                  pl.BlockSpec((None, D_MODEL, pw), lambda t: (layer, 0, 0), pipeline_mode=pl.Buffered(1)),
                  pl.BlockSpec((None, D_MODEL, D_MODEL), lambda t: (layer, 0, 0), pipeline_mode=pl.Buffered(1)),
                  mem_spec, mem_spec,
                  pl.BlockSpec((RET_HEADS // 2, CHUNK, 2 * CHUNK), c3),
                  pl.BlockSpec((RET_HEADS // 2, CHUNK, 2 * RET_HD), c3),
                  pl.BlockSpec((RET_HEADS // 2, CHUNK, 2 * RET_HD), c3)],
        out_specs=[pl.BlockSpec((TILE, D_MODEL), lambda t: (t, 0)),
                   pl.BlockSpec((None, RET_HEADS, RET_HD, RET_HD), lambda t: (t // tps, 0, 0, 0))],
        out_shape=[jax.ShapeDtypeStruct((batch * seq, D_MODEL), F32),
                   jax.ShapeDtypeStruct((batch, RET_HEADS, RET_HD, RET_HD), F32)],
        scratch_shapes=[pltpu.VMEM((TILE, pw), BF16), pltpu.VMEM((TILE, pw), BF16),
                        pltpu.VMEM((TILE, D_MODEL), BF16),
                        pltpu.VMEM((RET_HEADS, RET_HD, RET_HD), F32)],
        compiler_params=_cparams(1),
        name="fused_mixer_a",
    )(x, x, g.reshape(g.shape[0], 1, D_MODEL), w_in, w_out, mem_k, mem_v,
      jnp.asarray(decay), jnp.asarray(w_q), jnp.asarray(w_k))


def _swa_tile(p_ref, kvp_ref, kvc_ref, a_ref, bias_ref, sink_ref, is_first, fillers=()):
    lane_group = _lane_block64(lax.broadcasted_iota(jnp.int32, (WINDOW, KV_W), 1))
    key_group = _lane_block64(lax.broadcasted_iota(jnp.int32, (2 * WINDOW, KV_W), 1))
    for i in range(TILE // WINDOW):
        r0 = i * WINDOW
        if i == 0:
            kv2 = jnp.concatenate([kvp_ref[...], kvc_ref[0:WINDOW, :]], axis=0)
        else:
            kv2 = kvc_ref[r0 - WINDOW:r0 + WINDOW, :]
        k2 = kv2[:, :KV_W]
        v2 = kv2[:, KV_W:]
        qs = [p_ref[r0:r0 + WINDOW, r * KV_W:(r + 1) * KV_W] for r in range(SWA_REP)]
        probs = []
        vals = []
        for g in range(SWA_KV_HEADS):
            sel = lane_group == g
            qg = jnp.concatenate([jnp.where(sel, qr, jnp.zeros_like(qr)) for qr in qs], axis=0)
            s = lax.dot_general(qg, k2, NT_DIMS, preferred_element_type=F32) + bias_ref[g]
            if g == 1 and i < len(fillers):
                fillers[i]()
            s_prev = s[:, :WINDOW]
            s_cur = s[:, WINDOW:]
            if i == 0:
                s_prev = jnp.where(is_first, NEG, s_prev)
            sink = sink_ref[g]
            m = jnp.maximum(jnp.max(jnp.maximum(s_prev, s_cur), axis=-1, keepdims=True), sink)
            e_prev = jnp.exp(s_prev - m)
            e_cur = jnp.exp(s_cur - m)
            inv = 1.0 / (jnp.sum(e_prev + e_cur, axis=-1, keepdims=True) + jnp.exp(sink - m))
            probs.append((e_prev * inv).astype(BF16))
            probs.append((e_cur * inv).astype(BF16))
            vals.append(jnp.where(key_group == g, v2, jnp.zeros_like(v2)))
        o = jnp.dot(jnp.concatenate(probs, axis=1), jnp.concatenate(vals, axis=0), preferred_element_type=F32)
        for r in range(SWA_REP):
            a_ref[r0:r0 + WINDOW, r * KV_W:(r + 1) * KV_W] = o[r * WINDOW:(r + 1) * WINDOW].astype(BF16)


def _fused_b_kernel(xc_ref, xn_ref, g_ref, win_ref, wout_ref, kvp_ref, kvc_ref, mk_ref, mv_ref, bias_ref, sink_ref,
                    o_ref, p0, p1, a_scr, *, tiles_per_seq):
    t = pl.program_id(0)
    is_first = t % tiles_per_seq == 0

    @pl.when(t == 0)
    def _():
        _project_tile(xc_ref, g_ref, win_ref, p0, 2 * PIECE)

    out_piece = functools.partial(_out_projection_piece, xc_ref, a_scr, wout_ref, o_ref)

    def step(p_cur, p_next):
        pieces = _projection_pieces(xn_ref, g_ref, win_ref, p_next)
        _swa_tile(p_cur, kvp_ref, kvc_ref, a_scr, bias_ref, sink_ref, is_first, pieces)
        qm = p_cur[:, MIX_MAIN:MIX_MAIN + MEM_W]
        a_mem = _mem_attention_tile(qm, mk_ref[...], mv_ref[...],
                                    [out_piece(c) for c in range(D_MODEL // PIECE)]).astype(BF16)
        o_ref[...] += jnp.dot(a_mem, wout_ref[MIX_MAIN:, :], preferred_element_type=F32)

    @pl.when(t % 2 == 0)
    def _():
        step(p0, p1)

    @pl.when(t % 2 == 1)
    def _():
        step(p1, p0)


def fused_mixer_b(x, g, w_in, w_out, kv, mem_k, mem_v, layer, sink_rows, sub_layer, batch, seq):
    n_tiles = batch * seq // TILE
    tps = seq // TILE
    nw = TILE // WINDOW
    bias = jnp.asarray(_swa_bias())
    mem_spec = pl.BlockSpec((None, None, MEM_TOKENS, MEM_W), lambda t: (layer, t // tps, 0, 0))

    def prev_map(t):
        return (jnp.maximum(t * nw - 1, (t // tps) * (seq // WINDOW)), 0)

    return pl.pallas_call(
        functools.partial(_fused_b_kernel, tiles_per_seq=tps),
        grid=(n_tiles,),
        in_specs=[pl.BlockSpec((TILE, D_MODEL), lambda t: (t, 0)),
                  pl.BlockSpec((TILE, D_MODEL), lambda t: (jnp.minimum(t + 1, n_tiles - 1), 0)),
                  _layer_block((1, D_MODEL), layer, 1),
                  pl.BlockSpec((None, D_MODEL, D_MODEL), lambda t: (sub_layer, 0, 0), pipeline_mode=pl.Buffered(1)),
                  pl.BlockSpec((None, D_MODEL, D_MODEL), lambda t: (sub_layer, 0, 0), pipeline_mode=pl.Buffered(1)),
                  pl.BlockSpec((WINDOW, 2 * KV_W), prev_map),
                  pl.BlockSpec((TILE, 2 * KV_W), lambda t: (t, 0)),
                  mem_spec, mem_spec,
                  pl.BlockSpec((SWA_KV_HEADS, SWA_REP * WINDOW, 2 * WINDOW), lambda t: (0, 0, 0)),
                  pl.BlockSpec((None, SWA_KV_HEADS, SWA_REP * WINDOW, LANES), lambda t: (sub_layer, 0, 0, 0))],
        out_specs=pl.BlockSpec((TILE, D_MODEL), lambda t: (t, 0)),
        out_shape=jax.ShapeDtypeStruct((batch * seq, D_MODEL), F32),
        scratch_shapes=[pltpu.VMEM((TILE, D_MODEL), BF16), pltpu.VMEM((TILE, D_MODEL), BF16),
                        pltpu.VMEM((TILE, D_MODEL), BF16)],
        compiler_params=_cparams(1),
        name="fused_mixer_b",
    )(x, x, g.reshape(g.shape[0], 1, D_MODEL), w_in, w_out, kv, kv, mem_k, mem_v, bias, sink_rows)


def _ret_decode_kernel(proj_ref, st_ref, *rest, bb, layer, first_call):
    o_ref, stacked_ref = rest[-2:]
    if first_call:
        nst_ref = stacked_ref.at[layer]
        for other in range(stacked_ref.shape[0]):
            if other != layer:
                stacked_ref[other] = jnp.zeros(stacked_ref.shape[1:], F32)
    else:
        nst_ref = stacked_ref
    scale = RET_HD ** -0.5
    row = lax.broadcasted_iota(jnp.int32, (bb, bb * RET_HD), 0)
    lane_seq = lax.shift_right_logical(lax.broadcasted_iota(jnp.int32, (bb, bb * RET_HD), 1), 7)
    own = row == lane_seq
    for h in range(RET_HEADS):
        lo = h * RET_HD
        q8 = proj_ref[:, lo:lo + RET_HD]
        k8 = proj_ref[:, MIX_MAIN + lo:MIX_MAIN + lo + RET_HD] * scale
        v8 = proj_ref[:, 2 * MIX_MAIN + lo:2 * MIX_MAIN + lo + RET_HD]
        gate = proj_ref[:, 3 * MIX_MAIN + lo:3 * MIX_MAIN + lo + RET_HD]
        g = float(np.exp(_LOG_G[h]))
        vexp = jnp.where(own, jnp.concatenate([v8] * bb, axis=1), 0.0).astype(BF16)
        outer = lax.dot_general(k8.astype(BF16), vexp, TN_DIMS, preferred_element_type=F32)
        cols = []
        for j in range(bb):
            s_new = g * st_ref[j, h] + outer[:, j * RET_HD:(j + 1) * RET_HD]
            nst_ref[j, h] = s_new
            cols.append(s_new.astype(BF16))
        o_all = jnp.dot(q8.astype(BF16), jnp.concatenate(cols, axis=1), preferred_element_type=F32)
        o_all = jnp.where(own, o_all, 0.0)
        o = o_all[:, :RET_HD]
        for j in range(1, bb):
            o = o + o_all[:, j * RET_HD:(j + 1) * RET_HD]
        mu = jnp.mean(o, axis=-1, keepdims=True)
        oc = o - mu
        var = jnp.mean(oc * oc, axis=-1, keepdims=True)
        on = oc * lax.rsqrt(var + EPS)
        o_ref[:, lo:lo + RET_HD] = gate * jax.nn.sigmoid(gate) * on


def ret_decode(proj, state, layer, stacked_out=None, bb=16):
    nb = proj.shape[0]
    pw = proj.shape[1]
    n_layers = state.shape[0]
    tail = (RET_HEADS, RET_HD, RET_HD)
    in_specs = [pl.BlockSpec((bb, pw), lambda i: (i, 0)),
                pl.BlockSpec((None, bb) + tail, lambda i: (layer, i, 0, 0, 0))]
    args = [proj, state]
    first_call = stacked_out is None
    if first_call:
        aliases = {}
        stacked_spec = pl.BlockSpec((n_layers, bb) + tail, lambda i: (0, i, 0, 0, 0))
    else:
        in_specs.append(pl.BlockSpec(memory_space=pl.ANY))
        args.append(stacked_out)
        aliases = {2: 1}
        stacked_spec = pl.BlockSpec((None, bb) + tail, lambda i: (layer, i, 0, 0, 0))
    return pl.pallas_call(
        functools.partial(_ret_decode_kernel, bb=bb, layer=layer, first_call=first_call),
        grid=(nb // bb,),
        in_specs=in_specs,
        out_specs=[pl.BlockSpec((bb, MIX_MAIN), lambda i: (i, 0)), stacked_spec],
        out_shape=[jax.ShapeDtypeStruct((nb, MIX_MAIN), F32),
                   jax.ShapeDtypeStruct(state.shape, F32)],
        input_output_aliases=aliases,
        compiler_params=_cparams(1),
        name="ret_decode",
    )(*args)


def _mem_decode_kernel(q_ref, kt_ref, vt_ref, o_ref, *, bb, q_off, q_scale):
    rows = SUBLANES
    row = lax.broadcasted_iota(jnp.int32, (rows, MEM_W), 0)
    lane_head = _lane_block64(lax.broadcasted_iota(jnp.int32, (rows, MEM_W), 1))
    sel = row == lane_head
    seqs = range(bb)
    qexp = [jnp.where(sel, jnp.broadcast_to(q_ref[j:j + 1, q_off:q_off + MEM_W] * q_scale, (rows, MEM_W)), 0.0)
            for j in seqs]
    s = [jnp.dot(qexp[j], kt_ref[j], preferred_element_type=F32) for j in seqs]
    e = [jnp.exp(s[j] - jnp.max(s[j], axis=-1, keepdims=True)) for j in seqs]
    den = [jnp.sum(e[j], axis=-1, keepdims=True) for j in seqs]
    o = [lax.dot_general(e[j], vt_ref[j], NT_DIMS, preferred_element_type=F32) / den[j] for j in seqs]
    o_ref[...] = jnp.concatenate([jnp.sum(jnp.where(sel, o[j], 0.0), axis=0, keepdims=True) for j in seqs], axis=0)


def mem_decode(proj, cache_kt, cache_vt, layer, q_off, q_scale, bb=32):
    nb = proj.shape[0]
    pw = proj.shape[1]
    cmap = lambda i: (layer, i, 0, 0)
    return pl.pallas_call(
        functools.partial(_mem_decode_kernel, bb=bb, q_off=q_off, q_scale=q_scale),
        grid=(nb // bb,),
        in_specs=[pl.BlockSpec((bb, pw), lambda i: (i, 0)),
                  pl.BlockSpec((None, bb, MEM_W, MEM_TOKENS), cmap),
                  pl.BlockSpec((None, bb, MEM_W, MEM_TOKENS), cmap)],
        out_specs=pl.BlockSpec((bb, MEM_W), lambda i: (i, 0)),
        out_shape=jax.ShapeDtypeStruct((nb, MEM_W), F32),
        compiler_params=_cparams(1),
        name="mem_decode",
    )(proj, cache_kt, cache_vt)


def _swa_decode_kernel(q_ref, kn_ref, vn_ref, ckt_ref, cvt_ref, slope_ref, sink_ref, o_ref, *cache_out, bb):
    row = lax.broadcasted_iota(jnp.int32, (DEC_ROWS, KV_W), 0)
    lane_group = _lane_block64(lax.broadcasted_iota(jnp.int32, (DEC_ROWS, KV_W), 1))
    dist = (WINDOW - lax.broadcasted_iota(jnp.int32, (DEC_ROWS, WINDOW), 1)).astype(F32)
    bias = -slope_ref[...] * dist
    sink = sink_ref[...]
    own = [row == SWA_KV_HEADS * r + lane_group for r in range(SWA_REP)]
    seqs = range(bb)

    def expand(j):
        qexp = jnp.zeros((DEC_ROWS, KV_W), F32)
        for r in range(SWA_REP):
            qr = jnp.broadcast_to(q_ref[j:j + 1, r * KV_W:(r + 1) * KV_W], (DEC_ROWS, KV_W))
            qexp = jnp.where(own[r], qr, qexp)
        return qexp

    qexp = [expand(j) for j in seqs]
    s = [jnp.dot(qexp[j], ckt_ref[j], preferred_element_type=F32) + bias for j in seqs]
    s_new = [jnp.sum(qexp[j] * kn_ref[j:j + 1, :], axis=-1, keepdims=True) for j in seqs]
    m = [jnp.maximum(jnp.maximum(jnp.max(s[j], axis=-1, keepdims=True), s_new[j]), sink) for j in seqs]
    e = [jnp.exp(s[j] - m[j]) for j in seqs]
    e_new = [jnp.exp(s_new[j] - m[j]) for j in seqs]
    inv = [1.0 / (jnp.sum(e[j], axis=-1, keepdims=True) + e_new[j] + jnp.exp(sink - m[j])) for j in seqs]
    o = [lax.dot_general(e[j], cvt_ref[j], NT_DIMS, preferred_element_type=F32) for j in seqs]
    o = [(o[j] + jnp.concatenate([e_new[j], e_new[j]], axis=1) * vn_ref[j:j + 1, :])
         * jnp.concatenate([inv[j], inv[j]], axis=1) for j in seqs]
    for r in range(SWA_REP):
        o_ref[:, r * KV_W:(r + 1) * KV_W] = jnp.concatenate(
            [jnp.sum(jnp.where(own[r], o[j], 0.0), axis=0, keepdims=True) for j in seqs], axis=0)
    if cache_out:
        nkt_ref, nvt_ref = cache_out
        last = lax.broadcasted_iota(jnp.int32, (KV_W, WINDOW), 1) == WINDOW - 1
        knt = kn_ref[...].T
        vnt = vn_ref[...].T
        for j in seqs:
            nkt_ref[j] = jnp.where(last, jnp.broadcast_to(knt[:, j:j + 1], (KV_W, WINDOW)),
                                   pltpu.roll(ckt_ref[j], WINDOW - 1, 1))
            nvt_ref[j] = jnp.where(last, jnp.broadcast_to(vnt[:, j:j + 1], (KV_W, WINDOW)),
                                   pltpu.roll(cvt_ref[j], WINDOW - 1, 1))


def swa_decode(qproj, k_new, v_new, cache_kt, cache_vt, slope_rows, sink_rows, sink_layer, emit_cache, bb=32):
    nb = qproj.shape[0]
    pw = qproj.shape[1]
    cache_spec = pl.BlockSpec((bb, KV_W, WINDOW), lambda i: (i, 0, 0))
    cache_shape = jax.ShapeDtypeStruct((nb, KV_W, WINDOW), F32)
    n_cache = 2 if emit_cache else 0
    return pl.pallas_call(
        functools.partial(_swa_decode_kernel, bb=bb),
        grid=(nb // bb,),
        in_specs=[pl.BlockSpec((bb, pw), lambda i: (i, 0)),
                  pl.BlockSpec((bb, KV_W), lambda i: (i, 0)),
                  pl.BlockSpec((bb, KV_W), lambda i: (i, 0)),
                  cache_spec, cache_spec,
                  pl.BlockSpec((DEC_ROWS, LANES), lambda i: (0, 0)),
                  pl.BlockSpec((None, DEC_ROWS, LANES), lambda i: (sink_layer, 0, 0))],
        out_specs=[pl.BlockSpec((bb, MIX_MAIN), lambda i: (i, 0))] + [cache_spec] * n_cache,
        out_shape=[jax.ShapeDtypeStruct((nb, MIX_MAIN), F32)] + [cache_shape] * n_cache,
        compiler_params=_cparams(1),
        name="swa_decode",
    )(qproj, k_new, v_new, cache_kt, cache_vt, slope_rows, sink_rows)


def _slot_source(t):
    r, g = divmod(t, SWA_KV_HEADS)
    return SWA_REP * g + r


def _prep_w_in_b_kernel(w_ref, o_ref):
    low = lax.broadcasted_iota(jnp.int32, (D_MODEL, LANES), 1) < SWA_HD

    def half(slot, want_high):
        col = w_ref[:, (slot // 2) * LANES:(slot // 2 + 1) * LANES]
        return pltpu.roll(col, SWA_HD, 1) if (slot % 2 == 1) != want_high else col

    for c in range(MIX_MAIN // LANES):
        tile = jnp.where(low, half(_slot_source(2 * c), False), half(_slot_source(2 * c + 1), True))
        o_ref[:, c * LANES:(c + 1) * LANES] = (tile * ATTN_SCALE).astype(BF16)
    o_ref[:, MIX_MAIN:] = (w_ref[:, MIX_MAIN:] * ATTN_SCALE).astype(BF16)


def _prep_w_out_b_kernel(w_ref, o_ref):
    for t in range(SWA_HEADS):
        s = _slot_source(t)
        o_ref[t * SWA_HD:(t + 1) * SWA_HD, :] = w_ref[s * SWA_HD:(s + 1) * SWA_HD, :].astype(BF16)
    o_ref[MIX_MAIN:, :] = w_ref[MIX_MAIN:, :].astype(BF16)


def _prep_layer_b_weights(w, body, name):
    n_layers = w.shape[0]
    spec = pl.BlockSpec((None, D_MODEL, D_MODEL), lambda l: (l, 0, 0))
    return pl.pallas_call(
        body, grid=(n_layers,), in_specs=[spec], out_specs=spec,
        out_shape=jax.ShapeDtypeStruct(w.shape, BF16), compiler_params=_cparams(1), name=name,
    )(w)


def kernel(x_prompt, x_sample, cache_mem_k, cache_mem_v, state_ret, cache_swa_k, cache_swa_v, mem_prompt,
           norm_mix, w_in_a, w_out_a, w_in_b, w_out_b, attn_sinks, norm_mem, w_mem_kv, norm_kv, w_kv,
           norm_mlp, w_up, w_down, norm_final):
    batch, seq, d = x_prompt.shape
    nb = x_sample.shape[0]
    tm = TILE

    w_in_b16 = _prep_layer_b_weights(w_in_b, _prep_w_in_b_kernel, "prep_w_in_b")
    w_out_b16 = _prep_layer_b_weights(w_out_b, _prep_w_out_b_kernel, "prep_w_out_b")
    w_kv16 = w_kv.astype(BF16).reshape(1, d, 2 * KV_W)
    norm_kv1 = norm_kv.reshape(1, d)

    sinks_gr = attn_sinks.reshape(N_B, SWA_KV_HEADS, SWA_REP)
    sink_prompt = jnp.broadcast_to(sinks_gr[:, :, :, None, None], (N_B, SWA_KV_HEADS, SWA_REP, WINDOW, LANES)
                                   ).reshape(N_B, SWA_KV_HEADS, SWA_REP * WINDOW, LANES)
    sink_dec = jnp.concatenate([sinks_gr.swapaxes(1, 2).reshape(N_B, SWA_HEADS),
                                jnp.zeros((N_B, DEC_ROWS - SWA_HEADS), F32)], axis=1)
    sink_dec = jnp.broadcast_to(sink_dec[:, :, None], (N_B, DEC_ROWS, LANES))
    slope_dec = np.zeros((DEC_ROWS, LANES), np.float32)
    for r in range(SWA_REP):
        for g in range(SWA_KV_HEADS):
            slope_dec[SWA_KV_HEADS * r + g, :] = _SLOPES[g * SWA_REP + r]
    slope_dec = jnp.asarray(slope_dec)

    memx = mem_prompt.reshape(batch * MEM_TOKENS, d)
    mem_kv, mk16, mv16 = mem_kv_proj(memx, norm_mem, w_mem_kv)
    mem_kv = mem_kv.reshape(DEPTH, batch, MEM_TOKENS, 2 * MEM_W)
    mem_k_prompt = mem_kv[..., :MEM_W].reshape(DEPTH, batch, MEM_TOKENS, MEM_HEADS, MEM_HD)
    mem_v_prompt = mem_kv[..., MEM_W:].reshape(DEPTH, batch, MEM_TOKENS, MEM_HEADS, MEM_HD)
    mk16 = mk16.reshape(DEPTH, batch, MEM_TOKENS, MEM_W)
    mv16 = mv16.reshape(DEPTH, batch, MEM_TOKENS, MEM_W)

    cmkt = jnp.transpose(cache_mem_k, (0, 1, 3, 4, 2)).reshape(DEPTH, nb, MEM_W, MEM_TOKENS)
    cmvt = jnp.transpose(cache_mem_v, (0, 1, 3, 4, 2)).reshape(DEPTH, nb, MEM_W, MEM_TOKENS)
    cskt = jnp.transpose(cache_swa_k, (0, 2, 3, 1)).reshape(nb, KV_W, WINDOW)
    csvt = jnp.transpose(cache_swa_v, (0, 2, 3, 1)).reshape(nb, KV_W, WINDOW)

    x = x_prompt.reshape(batch * seq, d)
    xs = x_sample.reshape(nb, d)
    ret_states = []
    ret_sample = None
    kv_p = kv_p16 = k_new = v_new = nkt = nvt = None
    for l in range(DEPTH):
        j = l - N_A
        if l < N_A:
            x, st = fused_mixer_a(x, norm_mix, w_in_a, w_out_a, mk16, mv16, l, batch, seq)
            ret_states.append(st)
        else:
            x = fused_mixer_b(x, norm_mix, w_in_b16, w_out_b16, kv_p16, mk16, mv16, l, sink_prompt, j, batch, seq)
        if l < N_A:
            proj = norm_matmul(xs, norm_mix, w_in_a, l, F32, tm=nb, n_chunk=MIX_MAIN)
            o_ret, ret_sample = ret_decode(proj, state_ret, l, stacked_out=ret_sample)
            o_mem = mem_decode(proj, cmkt, cmvt, l, 4 * MIX_MAIN, ATTN_SCALE)
            attn = jnp.concatenate([o_ret, o_mem], axis=-1).astype(BF16)
            xs = out_residual(attn, w_out_a, l, xs)
        else:
            if j == 0:
                kv_s = norm_matmul(xs, norm_kv1, w_kv16, 0, F32, tm=nb)
                k_new, v_new = kv_s[:, :KV_W], kv_s[:, KV_W:]
            qproj = norm_matmul(xs, norm_mix[N_A:], w_in_b16, j, F32, tm=nb)
            res = swa_decode(qproj, k_new, v_new, cskt, csvt, slope_dec, sink_dec, j, emit_cache=(j == 0))
            o_swa = res[0]
            if j == 0:
                nkt, nvt = res[1], res[2]
            o_mem = mem_decode(qproj, cmkt, cmvt, l, MIX_MAIN, 1.0)
            attn = jnp.concatenate([o_swa, o_mem], axis=-1).astype(BF16)
            xs = out_residual(attn, w_out_b16, j, xs)
        if l == N_A - 1:
            x, kv_p, kv_p16, xs = mlp_block(x, norm_mlp, w_up, w_down, l, norm_final, False, tm=tm,
                                            kv_proj=(norm_kv1, w_kv16[0]), extra_rows=xs)
        else:
            x, xs = mlp_block(x, norm_mlp, w_up, w_down, l, norm_final, l == DEPTH - 1, tm=tm, extra_rows=xs)
    y_prompt = x.reshape(batch, seq, d)
    ret_prompt = jnp.stack(ret_states)
    kv_p3 = kv_p.reshape(batch, seq, 2 * KV_W)
    swa_k_prompt = kv_p3[:, -WINDOW:, :KV_W].reshape(batch, WINDOW, SWA_KV_HEADS, SWA_HD)
    swa_v_prompt = kv_p3[:, -WINDOW:, KV_W:].reshape(batch, WINDOW, SWA_KV_HEADS, SWA_HD)
    y_sample = xs.reshape(nb, 1, d)
    swa_k_sample = jnp.transpose(nkt.reshape(nb, SWA_KV_HEADS, SWA_HD, WINDOW), (0, 3, 1, 2))
    swa_v_sample = jnp.transpose(nvt.reshape(nb, SWA_KV_HEADS, SWA_HD, WINDOW), (0, 3, 1, 2))

    return (y_prompt, y_sample, ret_prompt, ret_sample, swa_k_prompt, swa_v_prompt, swa_k_sample, swa_v_sample,
            mem_k_prompt, mem_v_prompt)
```

```python
import functools
import math

import jax
import jax.numpy as jnp
import numpy as np
from jax import lax
from jax.experimental import pallas as pl
from jax.experimental.pallas import tpu as pltpu

F32 = jnp.float32
BF16 = jnp.bfloat16

D_MODEL = 1024
DEPTH = 4
N_A = 2
N_B = 2
MEM_TOKENS = 256
MEM_HEADS = 4
MEM_W = 256
MEM_HD = 64
MIX_MAIN = 768
RET_HEADS = 6
RET_HD = 128
CHUNK = 128
SWA_HEADS = 12
SWA_KV_HEADS = 4
SWA_REP = SWA_HEADS // SWA_KV_HEADS
SWA_HD = 64
KV_W = SWA_KV_HEADS * SWA_HD
WINDOW = 128
D_FF = 4096
EPS = 1e-6
NEG = -1e30
ATTN_SCALE = 0.125
LANES = 128
SUBLANES = 8
PIECE = 256
TILE = 512
DEC_ROWS = 16

VMEM_LIMIT = 56 * 1024 * 1024

NT_DIMS = (((1,), (1,)), ((), ()))
TN_DIMS = (((0,), (0,)), ((), ()))


def _alibi_slopes(n):
    def pow2(m):
        start = 2.0 ** (-8.0 / m)
        return [start ** (i + 1) for i in range(m)]

    if math.log2(n).is_integer():
        s = pow2(n)
    else:
        c = 2 ** int(math.floor(math.log2(n)))
        s = pow2(c) + pow2(2 * c)[0::2][: n - c]
    return np.asarray(s, np.float32)


_LOG_G = np.log1p(-(2.0 ** (-5.0 - np.arange(RET_HEADS)))).astype(np.float32).astype(np.float64)
_SLOPES = _alibi_slopes(SWA_HEADS).astype(np.float64)


def _retention_consts():
    idx = np.arange(CHUNK, dtype=np.float64)
    diff = idx[:, None] - idx[None, :]
    scale = RET_HD ** -0.5
    decay = np.where(diff >= 0, np.exp(np.maximum(diff, 0.0)[None] * _LOG_G[:, None, None]), 0.0) * scale
    w_q = np.exp((idx + 1.0)[None, :] * _LOG_G[:, None])
    w_k = np.exp((CHUNK - 1.0 - idx)[None, :] * _LOG_G[:, None]) * scale
    w_q = np.broadcast_to(w_q[:, :, None], (RET_HEADS, CHUNK, RET_HD))
    w_k = np.broadcast_to(w_k[:, :, None], (RET_HEADS, CHUNK, RET_HD))
    g_c = np.exp(CHUNK * _LOG_G)
    pair = lambda t: np.ascontiguousarray(np.concatenate([t[0::2], t[1::2]], axis=2), np.float32)
    return pair(decay), pair(w_q), pair(w_k), [float(v) for v in g_c]


def _swa_bias():
    i = np.arange(WINDOW)[:, None]
    j = np.arange(2 * WINDOW)[None, :]
    dist = i + WINDOW - j
    valid = (dist >= 0) & (dist <= WINDOW)
    out = np.zeros((SWA_KV_HEADS, SWA_REP * WINDOW, 2 * WINDOW), np.float32)
    for g in range(SWA_KV_HEADS):
        for r in range(SWA_REP):
            b = np.where(valid, -_SLOPES[g * SWA_REP + r] * dist, NEG)
            out[g, r * WINDOW:(r + 1) * WINDOW] = b
    return out


def _rmsnorm(x, g):
    ms = jnp.mean(x * x, axis=-1, keepdims=True)
    return (x * lax.rsqrt(ms + EPS)) * g


def _lane_block64(lane):
    return lax.shift_right_logical(lane, 6)


def _cparams(n_axes):
    return pltpu.CompilerParams(dimension_semantics=("arbitrary",) * n_axes, vmem_limit_bytes=VMEM_LIMIT)


def _layer_block(tail, layer, n_grid):
    zeros = (0,) * len(tail)
    if n_grid == 1:
        return pl.BlockSpec((None,) + tuple(tail), lambda i: (layer,) + zeros)
    return pl.BlockSpec((None,) + tuple(tail), lambda i, j: (layer,) + zeros)


def _norm_matmul_kernel(x_ref, g_ref, w_ref, o_ref, *, n_chunk):
    xn = _rmsnorm(x_ref[...], g_ref[...]).astype(BF16)
    n = w_ref.shape[1]
    for n0 in range(0, n, n_chunk):
        n1 = min(n0 + n_chunk, n)
        o_ref[:, n0:n1] = jnp.dot(xn, w_ref[:, n0:n1].astype(BF16), preferred_element_type=F32).astype(o_ref.dtype)


def norm_matmul(x, g, w, layer, out_dtype, tm, n_chunk=512):
    m, k = x.shape
    n = w.shape[2]
    return pl.pallas_call(
        functools.partial(_norm_matmul_kernel, n_chunk=n_chunk),
        grid=(m // tm,),
        in_specs=[pl.BlockSpec((tm, k), lambda i: (i, 0)),
                  _layer_block((1, k), layer, 1),
                  _layer_block((k, n), layer, 1)],
        out_specs=pl.BlockSpec((tm, n), lambda i: (i, 0)),
        out_shape=jax.ShapeDtypeStruct((m, n), out_dtype),
        compiler_params=_cparams(1),
        name="norm_matmul",
    )(x, g.reshape(g.shape[0], 1, k), w)


def _mem_kv_kernel(x_ref, g_ref, w_ref, o_ref, k16_ref, v16_ref):
    xn = _rmsnorm(x_ref[...], g_ref[...]).astype(BF16)
    kv = jnp.dot(xn, w_ref[...].astype(BF16), preferred_element_type=F32)
    o_ref[...] = kv
    k16_ref[...] = kv[:, :MEM_W].astype(BF16)
    v16_ref[...] = kv[:, MEM_W:].astype(BF16)


def mem_kv_proj(x, g, w):
    m, k = x.shape
    n_layers, _, n = w.shape
    half = pl.BlockSpec((None, m, MEM_W), lambda l: (l, 0, 0))
    return pl.pallas_call(
        _mem_kv_kernel,
        grid=(n_layers,),
        in_specs=[pl.BlockSpec((m, k), lambda l: (0, 0)),
                  pl.BlockSpec((None, 1, k), lambda l: (l, 0, 0)),
                  pl.BlockSpec((None, k, n), lambda l: (l, 0, 0))],
        out_specs=[pl.BlockSpec((None, m, n), lambda l: (l, 0, 0)), half, half],
        out_shape=[jax.ShapeDtypeStruct((n_layers, m, n), F32),
                   jax.ShapeDtypeStruct((n_layers, m, MEM_W), BF16),
                   jax.ShapeDtypeStruct((n_layers, m, MEM_W), BF16)],
        compiler_params=_cparams(1),
        name="mem_kv_proj",
    )(x, g.reshape(n_layers, 1, k), w)


def _mlp_rows(x, g_ref, wu_ref, wd_ref, act_ref, fc):
    xn = _rmsnorm(x, g_ref[...]).astype(BF16)
    for c0 in range(0, D_FF, fc):
        h = jnp.dot(xn, wu_ref[:, c0:c0 + fc].astype(BF16), preferred_element_type=F32)
        act_ref[:, c0:c0 + fc] = jnp.square(jnp.maximum(h, 0.0)).astype(BF16)
    return x + jnp.dot(act_ref[...], wd_ref[...].astype(BF16), preferred_element_type=F32)


def _mlp_kernel(x_ref, g_ref, wu_ref, wd_ref, gf_ref, *rest, fc, final_norm, with_kv, with_rows):
    rest = list(rest)
    gkv_ref, wkv_ref = (rest.pop(0), rest.pop(0)) if with_kv else (None, None)
    xs_ref = rest.pop(0) if with_rows else None
    o_ref = rest.pop(0)
    kv_ref, kv16_ref = (rest.pop(0), rest.pop(0)) if with_kv else (None, None)
    os_ref = rest.pop(0) if with_rows else None
    act_ref = rest.pop(0)
    if with_rows:
        acts_ref = rest.pop(0)

        @pl.when(pl.program_id(0) == 0)
        def _():
            ys = _mlp_rows(xs_ref[...], g_ref, wu_ref, wd_ref, acts_ref, fc)
            os_ref[...] = _rmsnorm(ys, gf_ref[...]) if final_norm else ys

    y = _mlp_rows(x_ref[...], g_ref, wu_ref, wd_ref, act_ref, fc)
    if with_kv:
        kv = jnp.dot(_rmsnorm(y, gkv_ref[...]).astype(BF16), wkv_ref[...], preferred_element_type=F32)
        kv_ref[...] = kv
        kv16_ref[...] = kv.astype(BF16)
    if final_norm:
        y = _rmsnorm(y, gf_ref[...])
    o_ref[...] = y


def mlp_block(x, g, w_up, w_down, layer, g_final, final_norm, tm, kv_proj=None, extra_rows=None, fc=512):
    m, d = x.shape
    with_kv = kv_proj is not None
    with_rows = extra_rows is not None
    in_specs = [pl.BlockSpec((tm, d), lambda i: (i, 0)),
                _layer_block((1, d), layer, 1),
                pl.BlockSpec((None, d, D_FF), lambda i: (layer, 0, 0), pipeline_mode=pl.Buffered(1)),
                pl.BlockSpec((None, D_FF, d), lambda i: (layer, 0, 0), pipeline_mode=pl.Buffered(1)),
                pl.BlockSpec((1, d), lambda i: (0, 0))]
    args = [x, g.reshape(g.shape[0], 1, d), w_up, w_down, g_final.reshape(1, d)]
    out_specs = [pl.BlockSpec((tm, d), lambda i: (i, 0))]
    out_shape = [jax.ShapeDtypeStruct((m, d), F32)]
    if with_kv:
        in_specs += [pl.BlockSpec((1, d), lambda i: (0, 0)),
                     pl.BlockSpec((d, 2 * KV_W), lambda i: (0, 0), pipeline_mode=pl.Buffered(1))]
        args += list(kv_proj)
        out_specs += [pl.BlockSpec((tm, 2 * KV_W), lambda i: (i, 0))] * 2
        out_shape += [jax.ShapeDtypeStruct((m, 2 * KV_W), F32), jax.ShapeDtypeStruct((m, 2 * KV_W), BF16)]
    scratch = [pltpu.VMEM((tm, D_FF), BF16)]
    if with_rows:
        rows = extra_rows.shape[0]
        in_specs.append(pl.BlockSpec((rows, d), lambda i: (0, 0)))
        args.append(extra_rows)
        out_specs.append(pl.BlockSpec((rows, d), lambda i: (0, 0)))
        out_shape.append(jax.ShapeDtypeStruct((rows, d), F32))
        scratch.append(pltpu.VMEM((rows, D_FF), BF16))
    res = pl.pallas_call(
        functools.partial(_mlp_kernel, fc=fc, final_norm=final_norm, with_kv=with_kv, with_rows=with_rows),
        grid=(m // tm,),
        in_specs=in_specs,
        out_specs=out_specs,
        out_shape=out_shape,
        scratch_shapes=scratch,
        compiler_params=_cparams(1),
        name="mlp_block",
    )(*args)
    return res if len(res) > 1 else res[0]


def _out_residual_kernel(a_ref, w_ref, x_ref, o_ref):
    o_ref[...] = x_ref[...] + jnp.dot(a_ref[...], w_ref[...].astype(BF16), preferred_element_type=F32)


def out_residual(a, w, layer, x):
    m, k = a.shape
    n = w.shape[2]
    return pl.pallas_call(
        _out_residual_kernel,
        grid=(1,),
        in_specs=[pl.BlockSpec((m, k), lambda i: (0, 0)),
                  _layer_block((k, n), layer, 1),
                  pl.BlockSpec((m, n), lambda i: (0, 0))],
        out_specs=pl.BlockSpec((m, n), lambda i: (0, 0)),
        out_shape=jax.ShapeDtypeStruct((m, n), F32),
        compiler_params=_cparams(1),
        name="out_residual",
    )(a, w, x)


def _mem_attention_tile(qm, mk, mv, fillers=()):
    t = qm.shape[0]
    lane_head = _lane_block64(lax.broadcasted_iota(jnp.int32, (t, MEM_W), 1))
    tok_head = _lane_block64(lax.broadcasted_iota(jnp.int32, (MEM_TOKENS, MEM_W), 1))
    probs = []
    vals = []
    for h in range(MEM_HEADS):
        qh = jnp.where(lane_head == h, qm, jnp.zeros_like(qm))
        s = lax.dot_general(qh, mk, NT_DIMS, preferred_element_type=F32)
        if h < len(fillers):
            fillers[h]()
        m = jnp.max(s, axis=-1, keepdims=True)
        e = jnp.exp(s - m)
        inv = 1.0 / jnp.sum(e, axis=-1, keepdims=True)
        probs.append((e * inv).astype(BF16))
        vals.append(jnp.where(tok_head == h, mv, jnp.zeros_like(mv)))
    return jnp.dot(jnp.concatenate(probs, axis=1), jnp.concatenate(vals, axis=0), preferred_element_type=F32)


def _project_tile(x_ref, g_ref, w_ref, p_ref, n_chunk):
    xn = _rmsnorm(x_ref[...], g_ref[...]).astype(BF16)
    n = w_ref.shape[1]
    for n0 in range(0, n, n_chunk):
        n1 = min(n0 + n_chunk, n)
        p_ref[:, n0:n1] = jnp.dot(xn, w_ref[:, n0:n1].astype(BF16), preferred_element_type=F32).astype(BF16)


def _projection_pieces(x_ref, g_ref, w_ref, p_ref):
    xn = _rmsnorm(x_ref[...], g_ref[...]).astype(BF16)

    def piece(c):
        def run():
            cols = slice(c * PIECE, (c + 1) * PIECE)
            p_ref[:, cols] = jnp.dot(xn, w_ref[:, cols].astype(BF16), preferred_element_type=F32).astype(BF16)
        return run

    return [piece(c) for c in range(w_ref.shape[1] // PIECE)]


def _out_projection_piece(x_ref, a_ref, w_ref, o_ref, c):
    def run():
        cols = slice(c * PIECE, (c + 1) * PIECE)
        o_ref[:, cols] = x_ref[:, cols] + jnp.dot(a_ref[:, :MIX_MAIN], w_ref[:MIX_MAIN, cols].astype(BF16),
                                                  preferred_element_type=F32)
    return run


def _both(f, g):
    def run():
        f()
        g()
    return run


def _block_diag2(a, b):
    z = jnp.zeros_like(a)
    return jnp.concatenate([jnp.concatenate([a, z], axis=1), jnp.concatenate([z, b], axis=1)], axis=0)


def _group_norm_gate(o, gate):
    mu = jnp.mean(o, axis=-1, keepdims=True)
    oc = o - mu
    var = jnp.mean(oc * oc, axis=-1, keepdims=True)
    return (gate * jax.nn.sigmoid(gate) * (oc * lax.rsqrt(var + EPS))).astype(BF16)


def _retention_tile(p_ref, a_ref, s_scr, decay_ref, wq_ref, wk_ref, g_c, fillers=()):
    pw = 2 * RET_HD
    pairs = range(RET_HEADS // 2)
    for c in range(TILE // CHUNK):
        rows = slice(c * CHUNK, (c + 1) * CHUNK)
        q2 = [p_ref[rows, hp * pw:(hp + 1) * pw] for hp in pairs]
        k2 = [p_ref[rows, MIX_MAIN + hp * pw:MIX_MAIN + (hp + 1) * pw] for hp in pairs]
        v2 = [p_ref[rows, 2 * MIX_MAIN + hp * pw:2 * MIX_MAIN + (hp + 1) * pw] for hp in pairs]
        sc = [lax.dot_general(q2[hp], _block_diag2(k2[hp][:, :RET_HD], k2[hp][:, RET_HD:]), NT_DIMS,
                              preferred_element_type=F32) for hp in pairs]
        kw = [(k2[hp].astype(F32) * wk_ref[hp]).astype(BF16) for hp in pairs]
        kv = [lax.dot_general(kw[hp], v2[hp], TN_DIMS, preferred_element_type=F32) for hp in pairs]
        if 2 * c < len(fillers):
            fillers[2 * c]()
        s_old = [s_scr[h] for h in range(RET_HEADS)]
        lhs = [jnp.concatenate([(sc[hp] * decay_ref[hp]).astype(BF16),
                                (q2[hp].astype(F32) * wq_ref[hp]).astype(BF16)], axis=1) for hp in pairs]
        rhs = [jnp.concatenate([_block_diag2(v2[hp][:, :RET_HD], v2[hp][:, RET_HD:]),
                                _block_diag2(s_old[2 * hp].astype(BF16), s_old[2 * hp + 1].astype(BF16))], axis=0)
               for hp in pairs]
        o2 = [jnp.dot(lhs[hp], rhs[hp], preferred_element_type=F32) for hp in pairs]
        if 2 * c + 1 < len(fillers):
            fillers[2 * c + 1]()
        for hp in pairs:
            s_scr[2 * hp] = g_c[2 * hp] * s_old[2 * hp] + kv[hp][:RET_HD, :RET_HD]
            s_scr[2 * hp + 1] = g_c[2 * hp + 1] * s_old[2 * hp + 1] + kv[hp][RET_HD:, RET_HD:]
        for hp in pairs:
            lo = hp * pw
            gate2 = p_ref[rows, 3 * MIX_MAIN + lo:3 * MIX_MAIN + lo + pw].astype(F32)
            a_ref[rows, lo:lo + RET_HD] = _group_norm_gate(o2[hp][:, :RET_HD], gate2[:, :RET_HD])
            a_ref[rows, lo + RET_HD:lo + pw] = _group_norm_gate(o2[hp][:, RET_HD:], gate2[:, RET_HD:])


def _fused_a_kernel(xc_ref, xn_ref, g_ref, win_ref, wout_ref, mk_ref, mv_ref, decay_ref, wq_ref, wk_ref, xs_ref,
                    o_ref, st_ref, ps_ref, p0, p1, a_scr, s_scr, *, g_c, tiles_per_seq):
    t = pl.program_id(0)

    @pl.when(t == 0)
    def _():
        _project_tile(xc_ref, g_ref, win_ref, p0, MIX_MAIN)
        xsn = _rmsnorm(xs_ref[...], g_ref[...]).astype(BF16)
        for n0 in range(0, win_ref.shape[1], PIECE):
            ps_ref[:, n0:n0 + PIECE] = jnp.dot(xsn, win_ref[:, n0:n0 + PIECE].astype(BF16),
                                               preferred_element_type=F32)

    @pl.when(t % tiles_per_seq == 0)
    def _():
        s_scr[...] = jnp.zeros_like(s_scr)

    def step(p_cur, p_next):
        pieces = _projection_pieces(xn_ref, g_ref, win_ref, p_next)
        n_slots = 2 * (TILE // CHUNK)
        slots = [_both(pieces[c], pieces[n_slots + c]) if n_slots + c < len(pieces) else pieces[c]
                 for c in range(n_slots)]
        _retention_tile(p_cur, a_scr, s_scr, decay_ref, wq_ref, wk_ref, g_c, slots)
        st_ref[...] = s_scr[...]
        qm = p_cur[:, 4 * MIX_MAIN:4 * MIX_MAIN + MEM_W] * ATTN_SCALE
        out_piece = functools.partial(_out_projection_piece, xc_ref, a_scr, wout_ref, o_ref)
        a_mem = _mem_attention_tile(qm, mk_ref[...], mv_ref[...],
                                    [out_piece(c) for c in range(D_MODEL // PIECE)]).astype(BF16)
        o_ref[...] += jnp.dot(a_mem, wout_ref[MIX_MAIN:, :].astype(BF16), preferred_element_type=F32)

    @pl.when(t % 2 == 0)
    def _():
        step(p0, p1)

    @pl.when(t % 2 == 1)
    def _():
        step(p1, p0)


def fused_mixer_a(x, g, w_in, w_out, mem_k, mem_v, layer, batch, seq, extra_rows):
    rows = extra_rows.shape[0]
    n_tiles = batch * seq // TILE
    tps = seq // TILE
    decay, w_q, w_k, g_c = _retention_consts()
    pw = w_in.shape[2]
    c3 = lambda t: (0, 0, 0)
    mem_spec = pl.BlockSpec((None, None, MEM_TOKENS, MEM_W), lambda t: (layer, t // tps, 0, 0))
    return pl.pallas_call(
        functools.partial(_fused_a_kernel, g_c=g_c, tiles_per_seq=tps),
        grid=(n_tiles,),
        in_specs=[pl.BlockSpec((TILE, D_MODEL), lambda t: (t, 0)),
                  pl.BlockSpec((TILE, D_MODEL), lambda t: (jnp.minimum(t + 1, n_tiles - 1), 0)),
                  _layer_block((1, D_MODEL), layer, 1),
                  pl.BlockSpec((None, D_MODEL, pw), lambda t: (layer, 0, 0), pipeline_mode=pl.Buffered(1)),
                  pl.BlockSpec((None, D_MODEL, D_MODEL), lambda t: (layer, 0, 0), pipeline_mode=pl.Buffered(1)),
                  mem_spec, mem_spec,
                  pl.BlockSpec((RET_HEADS // 2, CHUNK, 2 * CHUNK), c3),
                  pl.BlockSpec((RET_HEADS // 2, CHUNK, 2 * RET_HD), c3),
                  pl.BlockSpec((RET_HEADS // 2, CHUNK, 2 * RET_HD), c3),
                  pl.BlockSpec((rows, D_MODEL), lambda t: (0, 0))],
        out_specs=[pl.BlockSpec((TILE, D_MODEL), lambda t: (t, 0)),
                   pl.BlockSpec((None, RET_HEADS, RET_HD, RET_HD), lambda t: (t // tps, 0, 0, 0)),
                   pl.BlockSpec((rows, pw), lambda t: (0, 0))],
        out_shape=[jax.ShapeDtypeStruct((batch * seq, D_MODEL), F32),
                   jax.ShapeDtypeStruct((batch, RET_HEADS, RET_HD, RET_HD), F32),
                   jax.ShapeDtypeStruct((rows, pw), F32)],
        scratch_shapes=[pltpu.VMEM((TILE, pw), BF16), pltpu.VMEM((TILE, pw), BF16),
                        pltpu.VMEM((TILE, D_MODEL), BF16),
                        pltpu.VMEM((RET_HEADS, RET_HD, RET_HD), F32)],
        compiler_params=_cparams(1),
        name="fused_mixer_a",
    )(x, x, g.reshape(g.shape[0], 1, D_MODEL), w_in, w_out, mem_k, mem_v,
      jnp.asarray(decay), jnp.asarray(w_q), jnp.asarray(w_k), extra_rows)


def _swa_tile(p_ref, kvp_ref, kvc_ref, a_ref, bias_ref, sink_ref, is_first, fillers=()):
    lane_group = _lane_block64(lax.broadcasted_iota(jnp.int32, (WINDOW, KV_W), 1))
    key_group = _lane_block64(lax.broadcasted_iota(jnp.int32, (2 * WINDOW, KV_W), 1))
    for i in range(TILE // WINDOW):
        r0 = i * WINDOW
        if i == 0:
            kv2 = jnp.concatenate([kvp_ref[...], kvc_ref[0:WINDOW, :]], axis=0)
        else:
            kv2 = kvc_ref[r0 - WINDOW:r0 + WINDOW, :]
        k2 = kv2[:, :KV_W]
        v2 = kv2[:, KV_W:]
        qs = [p_ref[r0:r0 + WINDOW, r * KV_W:(r + 1) * KV_W] for r in range(SWA_REP)]
        probs = []
        vals = []
        for g in range(SWA_KV_HEADS):
            sel = lane_group == g
            qg = jnp.concatenate([jnp.where(sel, qr, jnp.zeros_like(qr)) for qr in qs], axis=0)
            s = lax.dot_general(qg, k2, NT_DIMS, preferred_element_type=F32) + bias_ref[g]
            if g == 1 and i < len(fillers):
                fillers[i]()
            s_prev = s[:, :WINDOW]
            s_cur = s[:, WINDOW:]
            if i == 0:
                s_prev = jnp.where(is_first, NEG, s_prev)
            sink = sink_ref[g]
            m = jnp.maximum(jnp.max(jnp.maximum(s_prev, s_cur), axis=-1, keepdims=True), sink)
            e_prev = jnp.exp(s_prev - m)
            e_cur = jnp.exp(s_cur - m)
            inv = 1.0 / (jnp.sum(e_prev + e_cur, axis=-1, keepdims=True) + jnp.exp(sink - m))
            probs.append((e_prev * inv).astype(BF16))
            probs.append((e_cur * inv).astype(BF16))
            vals.append(jnp.where(key_group == g, v2, jnp.zeros_like(v2)))
        o = jnp.dot(jnp.concatenate(probs, axis=1), jnp.concatenate(vals, axis=0), preferred_element_type=F32)
        for r in range(SWA_REP):
            a_ref[r0:r0 + WINDOW, r * KV_W:(r + 1) * KV_W] = o[r * WINDOW:(r + 1) * WINDOW].astype(BF16)


def _fused_b_kernel(xc_ref, xn_ref, g_ref, win_ref, wout_ref, kvp_ref, kvc_ref, mk_ref, mv_ref, bias_ref, sink_ref,
                    o_ref, p0, p1, a_scr, *, tiles_per_seq):
    t = pl.program_id(0)
    is_first = t % tiles_per_seq == 0

    @pl.when(t == 0)
    def _():
        _project_tile(xc_ref, g_ref, win_ref, p0, 2 * PIECE)

    out_piece = functools.partial(_out_projection_piece, xc_ref, a_scr, wout_ref, o_ref)

    def step(p_cur, p_next):
        pieces = _projection_pieces(xn_ref, g_ref, win_ref, p_next)
        _swa_tile(p_cur, kvp_ref, kvc_ref, a_scr, bias_ref, sink_ref, is_first, pieces)
        qm = p_cur[:, MIX_MAIN:MIX_MAIN + MEM_W]
        a_mem = _mem_attention_tile(qm, mk_ref[...], mv_ref[...],
                                    [out_piece(c) for c in range(D_MODEL // PIECE)]).astype(BF16)
        o_ref[...] += jnp.dot(a_mem, wout_ref[MIX_MAIN:, :], preferred_element_type=F32)

    @pl.when(t % 2 == 0)
    def _():
        step(p0, p1)

    @pl.when(t % 2 == 1)
    def _():
        step(p1, p0)


def fused_mixer_b(x, g, w_in, w_out, kv, mem_k, mem_v, layer, sink_rows, sub_layer, batch, seq):
    n_tiles = batch * seq // TILE
    tps = seq // TILE
    nw = TILE // WINDOW
    bias = jnp.asarray(_swa_bias())
    mem_spec = pl.BlockSpec((None, None, MEM_TOKENS, MEM_W), lambda t: (layer, t // tps, 0, 0))

    def prev_map(t):
        return (jnp.maximum(t * nw - 1, (t // tps) * (seq // WINDOW)), 0)

    return pl.pallas_call(
        functools.partial(_fused_b_kernel, tiles_per_seq=tps),
        grid=(n_tiles,),
        in_specs=[pl.BlockSpec((TILE, D_MODEL), lambda t: (t, 0)),
                  pl.BlockSpec((TILE, D_MODEL), lambda t: (jnp.minimum(t + 1, n_tiles - 1), 0)),
                  _layer_block((1, D_MODEL), layer, 1),
                  pl.BlockSpec((None, D_MODEL, D_MODEL), lambda t: (sub_layer, 0, 0), pipeline_mode=pl.Buffered(1)),
                  pl.BlockSpec((None, D_MODEL, D_MODEL), lambda t: (sub_layer, 0, 0), pipeline_mode=pl.Buffered(1)),
                  pl.BlockSpec((WINDOW, 2 * KV_W), prev_map),
                  pl.BlockSpec((TILE, 2 * KV_W), lambda t: (t, 0)),
                  mem_spec, mem_spec,
                  pl.BlockSpec((SWA_KV_HEADS, SWA_REP * WINDOW, 2 * WINDOW), lambda t: (0, 0, 0)),
                  pl.BlockSpec((None, SWA_KV_HEADS, SWA_REP * WINDOW, LANES), lambda t: (sub_layer, 0, 0, 0))],
        out_specs=pl.BlockSpec((TILE, D_MODEL), lambda t: (t, 0)),
        out_shape=jax.ShapeDtypeStruct((batch * seq, D_MODEL), F32),
        scratch_shapes=[pltpu.VMEM((TILE, D_MODEL), BF16), pltpu.VMEM((TILE, D_MODEL), BF16),
                        pltpu.VMEM((TILE, D_MODEL), BF16)],
        compiler_params=_cparams(1),
        name="fused_mixer_b",
    )(x, x, g.reshape(g.shape[0], 1, D_MODEL), w_in, w_out, kv, kv, mem_k, mem_v, bias, sink_rows)


def _ret_decode_kernel(proj_ref, st_ref, *rest, bb, layer, first_call):
    o_ref, stacked_ref = rest[-2:]
    if first_call:
        nst_ref = stacked_ref.at[layer]
        for other in range(stacked_ref.shape[0]):
            if other != layer:
                stacked_ref[other] = jnp.zeros(stacked_ref.shape[1:], F32)
    else:
        nst_ref = stacked_ref
    scale = RET_HD ** -0.5
    row = lax.broadcasted_iota(jnp.int32, (bb, bb * RET_HD), 0)
    lane_seq = lax.shift_right_logical(lax.broadcasted_iota(jnp.int32, (bb, bb * RET_HD), 1), 7)
    own = row == lane_seq
    for h in range(RET_HEADS):
        lo = h * RET_HD
        q8 = proj_ref[:, lo:lo + RET_HD]
        k8 = proj_ref[:, MIX_MAIN + lo:MIX_MAIN + lo + RET_HD] * scale
        v8 = proj_ref[:, 2 * MIX_MAIN + lo:2 * MIX_MAIN + lo + RET_HD]
        gate = proj_ref[:, 3 * MIX_MAIN + lo:3 * MIX_MAIN + lo + RET_HD]
        g = float(np.exp(_LOG_G[h]))
        vexp = jnp.where(own, jnp.concatenate([v8] * bb, axis=1), 0.0).astype(BF16)
        outer = lax.dot_general(k8.astype(BF16), vexp, TN_DIMS, preferred_element_type=F32)
        cols = []
        for j in range(bb):
            s_new = g * st_ref[j, h] + outer[:, j * RET_HD:(j + 1) * RET_HD]
            nst_ref[j, h] = s_new
            cols.append(s_new.astype(BF16))
        o_all = jnp.dot(q8.astype(BF16), jnp.concatenate(cols, axis=1), preferred_element_type=F32)
        o_all = jnp.where(own, o_all, 0.0)
        o = o_all[:, :RET_HD]
        for j in range(1, bb):
            o = o + o_all[:, j * RET_HD:(j + 1) * RET_HD]
        mu = jnp.mean(o, axis=-1, keepdims=True)
        oc = o - mu
        var = jnp.mean(oc * oc, axis=-1, keepdims=True)
        on = oc * lax.rsqrt(var + EPS)
        o_ref[:, lo:lo + RET_HD] = gate * jax.nn.sigmoid(gate) * on


def ret_decode(proj, state, layer, stacked_out=None, bb=16):
    nb = proj.shape[0]
    pw = proj.shape[1]
    n_layers = state.shape[0]
    tail = (RET_HEADS, RET_HD, RET_HD)
    in_specs = [pl.BlockSpec((bb, pw), lambda i: (i, 0)),
                pl.BlockSpec((None, bb) + tail, lambda i: (layer, i, 0, 0, 0))]
    args = [proj, state]
    first_call = stacked_out is None
    if first_call:
        aliases = {}
        stacked_spec = pl.BlockSpec((n_layers, bb) + tail, lambda i: (0, i, 0, 0, 0))
    else:
        in_specs.append(pl.BlockSpec(memory_space=pl.ANY))
        args.append(stacked_out)
        aliases = {2: 1}
        stacked_spec = pl.BlockSpec((None, bb) + tail, lambda i: (layer, i, 0, 0, 0))
    return pl.pallas_call(
        functools.partial(_ret_decode_kernel, bb=bb, layer=layer, first_call=first_call),
        grid=(nb // bb,),
        in_specs=in_specs,
        out_specs=[pl.BlockSpec((bb, MIX_MAIN), lambda i: (i, 0)), stacked_spec],
        out_shape=[jax.ShapeDtypeStruct((nb, MIX_MAIN), F32),
                   jax.ShapeDtypeStruct(state.shape, F32)],
        input_output_aliases=aliases,
        compiler_params=_cparams(1),
        name="ret_decode",
    )(*args)


def _mem_decode_kernel(q_ref, kt_ref, vt_ref, o_ref, *, bb, q_off, q_scale):
    rows = SUBLANES
    row = lax.broadcasted_iota(jnp.int32, (rows, MEM_W), 0)
    lane_head = _lane_block64(lax.broadcasted_iota(jnp.int32, (rows, MEM_W), 1))
    sel = row == lane_head
    seqs = range(bb)
    qexp = [jnp.where(sel, jnp.broadcast_to(q_ref[j:j + 1, q_off:q_off + MEM_W] * q_scale, (rows, MEM_W)), 0.0)
            for j in seqs]
    s = [jnp.dot(qexp[j], kt_ref[j], preferred_element_type=F32) for j in seqs]
    e = [jnp.exp(s[j] - jnp.max(s[j], axis=-1, keepdims=True)) for j in seqs]
    den = [jnp.sum(e[j], axis=-1, keepdims=True) for j in seqs]
    o = [lax.dot_general(e[j], vt_ref[j], NT_DIMS, preferred_element_type=F32) / den[j] for j in seqs]
    o_ref[...] = jnp.concatenate([jnp.sum(jnp.where(sel, o[j], 0.0), axis=0, keepdims=True) for j in seqs], axis=0)


def mem_decode(proj, cache_kt, cache_vt, layer, q_off, q_scale, bb=32):
    nb = proj.shape[0]
    pw = proj.shape[1]
    cmap = lambda i: (layer, i, 0, 0)
    return pl.pallas_call(
        functools.partial(_mem_decode_kernel, bb=bb, q_off=q_off, q_scale=q_scale),
        grid=(nb // bb,),
        in_specs=[pl.BlockSpec((bb, pw), lambda i: (i, 0)),
                  pl.BlockSpec((None, bb, MEM_W, MEM_TOKENS), cmap),
                  pl.BlockSpec((None, bb, MEM_W, MEM_TOKENS), cmap)],
        out_specs=pl.BlockSpec((bb, MEM_W), lambda i: (i, 0)),
        out_shape=jax.ShapeDtypeStruct((nb, MEM_W), F32),
        compiler_params=_cparams(1),
        name="mem_decode",
    )(proj, cache_kt, cache_vt)


def _swa_decode_kernel(q_ref, kn_ref, vn_ref, ckt_ref, cvt_ref, slope_ref, sink_ref, o_ref, *cache_out, bb):
    row = lax.broadcasted_iota(jnp.int32, (DEC_ROWS, KV_W), 0)
    lane_group = _lane_block64(lax.broadcasted_iota(jnp.int32, (DEC_ROWS, KV_W), 1))
    dist = (WINDOW - lax.broadcasted_iota(jnp.int32, (DEC_ROWS, WINDOW), 1)).astype(F32)
    bias = -slope_ref[...] * dist
    sink = sink_ref[...]
    own = [row == SWA_KV_HEADS * r + lane_group for r in range(SWA_REP)]
    seqs = range(bb)

    def expand(j):
        qexp = jnp.zeros((DEC_ROWS, KV_W), F32)
        for r in range(SWA_REP):
            qr = jnp.broadcast_to(q_ref[j:j + 1, r * KV_W:(r + 1) * KV_W], (DEC_ROWS, KV_W))
            qexp = jnp.where(own[r], qr, qexp)
        return qexp

    qexp = [expand(j) for j in seqs]
    s = [jnp.dot(qexp[j], ckt_ref[j], preferred_element_type=F32) + bias for j in seqs]
    s_new = [jnp.sum(qexp[j] * kn_ref[j:j + 1, :], axis=-1, keepdims=True) for j in seqs]
    m = [jnp.maximum(jnp.maximum(jnp.max(s[j], axis=-1, keepdims=True), s_new[j]), sink) for j in seqs]
    e = [jnp.exp(s[j] - m[j]) for j in seqs]
    e_new = [jnp.exp(s_new[j] - m[j]) for j in seqs]
    inv = [1.0 / (jnp.sum(e[j], axis=-1, keepdims=True) + e_new[j] + jnp.exp(sink - m[j])) for j in seqs]
    o = [lax.dot_general(e[j], cvt_ref[j], NT_DIMS, preferred_element_type=F32) for j in seqs]
    o = [(o[j] + jnp.concatenate([e_new[j], e_new[j]], axis=1) * vn_ref[j:j + 1, :])
         * jnp.concatenate([inv[j], inv[j]], axis=1) for j in seqs]
    for r in range(SWA_REP):
        o_ref[:, r * KV_W:(r + 1) * KV_W] = jnp.concatenate(
            [jnp.sum(jnp.where(own[r], o[j], 0.0), axis=0, keepdims=True) for j in seqs], axis=0)
    if cache_out:
        nkt_ref, nvt_ref = cache_out
        last = lax.broadcasted_iota(jnp.int32, (KV_W, WINDOW), 1) == WINDOW - 1
        knt = kn_ref[...].T
        vnt = vn_ref[...].T
        for j in seqs:
            nkt_ref[j] = jnp.where(last, jnp.broadcast_to(knt[:, j:j + 1], (KV_W, WINDOW)),
                                   pltpu.roll(ckt_ref[j], WINDOW - 1, 1))
            nvt_ref[j] = jnp.where(last, jnp.broadcast_to(vnt[:, j:j + 1], (KV_W, WINDOW)),
                                   pltpu.roll(cvt_ref[j], WINDOW - 1, 1))


def swa_decode(qproj, k_new, v_new, cache_kt, cache_vt, slope_rows, sink_rows, sink_layer, emit_cache, bb=32):
    nb = qproj.shape[0]
    pw = qproj.shape[1]
    cache_spec = pl.BlockSpec((bb, KV_W, WINDOW), lambda i: (i, 0, 0))
    cache_shape = jax.ShapeDtypeStruct((nb, KV_W, WINDOW), F32)
    n_cache = 2 if emit_cache else 0
    return pl.pallas_call(
        functools.partial(_swa_decode_kernel, bb=bb),
        grid=(nb // bb,),
        in_specs=[pl.BlockSpec((bb, pw), lambda i: (i, 0)),
                  pl.BlockSpec((bb, KV_W), lambda i: (i, 0)),
                  pl.BlockSpec((bb, KV_W), lambda i: (i, 0)),
                  cache_spec, cache_spec,
                  pl.BlockSpec((DEC_ROWS, LANES), lambda i: (0, 0)),
                  pl.BlockSpec((None, DEC_ROWS, LANES), lambda i: (sink_layer, 0, 0))],
        out_specs=[pl.BlockSpec((bb, MIX_MAIN), lambda i: (i, 0))] + [cache_spec] * n_cache,
        out_shape=[jax.ShapeDtypeStruct((nb, MIX_MAIN), F32)] + [cache_shape] * n_cache,
        compiler_params=_cparams(1),
        name="swa_decode",
    )(qproj, k_new, v_new, cache_kt, cache_vt, slope_rows, sink_rows)


def _slot_source(t):
    r, g = divmod(t, SWA_KV_HEADS)
    return SWA_REP * g + r


def _prep_w_in_b_kernel(w_ref, o_ref):
    low = lax.broadcasted_iota(jnp.int32, (D_MODEL, LANES), 1) < SWA_HD

    def half(slot, want_high):
        col = w_ref[:, (slot // 2) * LANES:(slot // 2 + 1) * LANES]
        return pltpu.roll(col, SWA_HD, 1) if (slot % 2 == 1) != want_high else col

    for c in range(MIX_MAIN // LANES):
        tile = jnp.where(low, half(_slot_source(2 * c), False), half(_slot_source(2 * c + 1), True))
        o_ref[:, c * LANES:(c + 1) * LANES] = (tile * ATTN_SCALE).astype(BF16)
    o_ref[:, MIX_MAIN:] = (w_ref[:, MIX_MAIN:] * ATTN_SCALE).astype(BF16)


def _prep_w_out_b_kernel(w_ref, o_ref):
    for t in range(SWA_HEADS):
        s = _slot_source(t)
        o_ref[t * SWA_HD:(t + 1) * SWA_HD, :] = w_ref[s * SWA_HD:(s + 1) * SWA_HD, :].astype(BF16)
    o_ref[MIX_MAIN:, :] = w_ref[MIX_MAIN:, :].astype(BF16)


def _prep_layer_b_weights(w, body, name):
    n_layers = w.shape[0]
    spec = pl.BlockSpec((None, D_MODEL, D_MODEL), lambda l: (l, 0, 0))
    return pl.pallas_call(
        body, grid=(n_layers,), in_specs=[spec], out_specs=spec,
        out_shape=jax.ShapeDtypeStruct(w.shape, BF16), compiler_params=_cparams(1), name=name,
    )(w)


def kernel(x_prompt, x_sample, cache_mem_k, cache_mem_v, state_ret, cache_swa_k, cache_swa_v, mem_prompt,
           norm_mix, w_in_a, w_out_a, w_in_b, w_out_b, attn_sinks, norm_mem, w_mem_kv, norm_kv, w_kv,
           norm_mlp, w_up, w_down, norm_final):
    batch, seq, d = x_prompt.shape
    nb = x_sample.shape[0]
    tm = TILE

    w_in_b16 = _prep_layer_b_weights(w_in_b, _prep_w_in_b_kernel, "prep_w_in_b")
    w_out_b16 = _prep_layer_b_weights(w_out_b, _prep_w_out_b_kernel, "prep_w_out_b")
    w_kv16 = w_kv.astype(BF16).reshape(1, d, 2 * KV_W)
    norm_kv1 = norm_kv.reshape(1, d)

    sinks_gr = attn_sinks.reshape(N_B, SWA_KV_HEADS, SWA_REP)
    sink_prompt = jnp.broadcast_to(sinks_gr[:, :, :, None, None], (N_B, SWA_KV_HEADS, SWA_REP, WINDOW, LANES)
                                   ).reshape(N_B, SWA_KV_HEADS, SWA_REP * WINDOW, LANES)
    sink_dec = jnp.concatenate([sinks_gr.swapaxes(1, 2).reshape(N_B, SWA_HEADS),
                                jnp.zeros((N_B, DEC_ROWS - SWA_HEADS), F32)], axis=1)
    sink_dec = jnp.broadcast_to(sink_dec[:, :, None], (N_B, DEC_ROWS, LANES))
    slope_dec = np.zeros((DEC_ROWS, LANES), np.float32)
    for r in range(SWA_REP):
        for g in range(SWA_KV_HEADS):
            slope_dec[SWA_KV_HEADS * r + g, :] = _SLOPES[g * SWA_REP + r]
    slope_dec = jnp.asarray(slope_dec)

    memx = mem_prompt.reshape(batch * MEM_TOKENS, d)
    mem_kv, mk16, mv16 = mem_kv_proj(memx, norm_mem, w_mem_kv)
    mem_kv = mem_kv.reshape(DEPTH, batch, MEM_TOKENS, 2 * MEM_W)
    mem_k_prompt = mem_kv[..., :MEM_W].reshape(DEPTH, batch, MEM_TOKENS, MEM_HEADS, MEM_HD)
    mem_v_prompt = mem_kv[..., MEM_W:].reshape(DEPTH, batch, MEM_TOKENS, MEM_HEADS, MEM_HD)
    mk16 = mk16.reshape(DEPTH, batch, MEM_TOKENS, MEM_W)
    mv16 = mv16.reshape(DEPTH, batch, MEM_TOKENS, MEM_W)

    cmkt = jnp.transpose(cache_mem_k, (0, 1, 3, 4, 2)).reshape(DEPTH, nb, MEM_W, MEM_TOKENS)
    cmvt = jnp.transpose(cache_mem_v, (0, 1, 3, 4, 2)).reshape(DEPTH, nb, MEM_W, MEM_TOKENS)
    cskt = jnp.transpose(cache_swa_k, (0, 2, 3, 1)).reshape(nb, KV_W, WINDOW)
    csvt = jnp.transpose(cache_swa_v, (0, 2, 3, 1)).reshape(nb, KV_W, WINDOW)

    x = x_prompt.reshape(batch * seq, d)
    xs = x_sample.reshape(nb, d)
    ret_states = []
    ret_sample = None
    kv_p = kv_p16 = k_new = v_new = nkt = nvt = None
    for l in range(DEPTH):
        j = l - N_A
        if l < N_A:
            x, st, proj = fused_mixer_a(x, norm_mix, w_in_a, w_out_a, mk16, mv16, l, batch, seq, xs)
            ret_states.append(st)
        else:
            x = fused_mixer_b(x, norm_mix, w_in_b16, w_out_b16, kv_p16, mk16, mv16, l, sink_prompt, j, batch, seq)
        if l < N_A:
            o_ret, ret_sample = ret_decode(proj, state_ret, l, stacked_out=ret_sample)
            o_mem = mem_decode(proj, cmkt, cmvt, l, 4 * MIX_MAIN, ATTN_SCALE)
            attn = jnp.concatenate([o_ret, o_mem], axis=-1).astype(BF16)
            xs = out_residual(attn, w_out_a, l, xs)
        else:
            if j == 0:
                kv_s = norm_matmul(xs, norm_kv1, w_kv16, 0, F32, tm=nb)
                k_new, v_new = kv_s[:, :KV_W], kv_s[:, KV_W:]
            qproj = norm_matmul(xs, norm_mix[N_A:], w_in_b16, j, F32, tm=nb)
            res = swa_decode(qproj, k_new, v_new, cskt, csvt, slope_dec, sink_dec, j, emit_cache=(j == 0))
            o_swa = res[0]
            if j == 0:
                nkt, nvt = res[1], res[2]
            o_mem = mem_decode(qproj, cmkt, cmvt, l, MIX_MAIN, 1.0)
            attn = jnp.concatenate([o_swa, o_mem], axis=-1).astype(BF16)
            xs = out_residual(attn, w_out_b16, j, xs)
        if l == N_A - 1:
            x, kv_p, kv_p16, xs = mlp_block(x, norm_mlp, w_up, w_down, l, norm_final, False, tm=tm,
                                            kv_proj=(norm_kv1, w_kv16[0]), extra_rows=xs)
        else:
            x, xs = mlp_block(x, norm_mlp, w_up, w_down, l, norm_final, l == DEPTH - 1, tm=tm, extra_rows=xs)
    y_prompt = x.reshape(batch, seq, d)
    ret_prompt = jnp.stack(ret_states)
    kv_p3 = kv_p.reshape(batch, seq, 2 * KV_W)
    swa_k_prompt = kv_p3[:, -WINDOW:, :KV_W].reshape(batch, WINDOW, SWA_KV_HEADS, SWA_HD)
    swa_v_prompt = kv_p3[:, -WINDOW:, KV_W:].reshape(batch, WINDOW, SWA_KV_HEADS, SWA_HD)
    y_sample = xs.reshape(nb, 1, d)
    swa_k_sample = jnp.transpose(nkt.reshape(nb, SWA_KV_HEADS, SWA_HD, WINDOW), (0, 3, 1, 2))
    swa_v_sample = jnp.transpose(nvt.reshape(nb, SWA_KV_HEADS, SWA_HD, WINDOW), (0, 3, 1, 2))

    return (y_prompt, y_sample, ret_prompt, ret_sample, swa_k_prompt, swa_v_prompt, swa_k_sample, swa_v_sample,
            mem_k_prompt, mem_v_prompt)
```

```python
import functools
import math

import jax
import jax.numpy as jnp
import numpy as np
from jax import lax
from jax.experimental import pallas as pl
from jax.experimental.pallas import tpu as pltpu

F32 = jnp.float32
BF16 = jnp.bfloat16

D_MODEL = 1024
DEPTH = 4
N_A = 2
N_B = 2
MEM_TOKENS = 256
MEM_HEADS = 4
MEM_W = 256
MEM_HD = 64
MIX_MAIN = 768
RET_HEADS = 6
RET_HD = 128
CHUNK = 128
SWA_HEADS = 12
SWA_KV_HEADS = 4
SWA_REP = SWA_HEADS // SWA_KV_HEADS
SWA_HD = 64
KV_W = SWA_KV_HEADS * SWA_HD
WINDOW = 128
D_FF = 4096
EPS = 1e-6
NEG = -1e30
ATTN_SCALE = 0.125
LANES = 128
SUBLANES = 8
PIECE = 256
TILE = 512
DEC_ROWS = 16

VMEM_LIMIT = 56 * 1024 * 1024

NT_DIMS = (((1,), (1,)), ((), ()))
TN_DIMS = (((0,), (0,)), ((), ()))


def _alibi_slopes(n):
    def pow2(m):
        start = 2.0 ** (-8.0 / m)
        return [start ** (i + 1) for i in range(m)]

    if math.log2(n).is_integer():
        s = pow2(n)
    else:
        c = 2 ** int(math.floor(math.log2(n)))
        s = pow2(c) + pow2(2 * c)[0::2][: n - c]
    return np.asarray(s, np.float32)


_LOG_G = np.log1p(-(2.0 ** (-5.0 - np.arange(RET_HEADS)))).astype(np.float32).astype(np.float64)
_SLOPES = _alibi_slopes(SWA_HEADS).astype(np.float64)


def _retention_consts():
    idx = np.arange(CHUNK, dtype=np.float64)
    diff = idx[:, None] - idx[None, :]
    scale = RET_HD ** -0.5
    decay = np.where(diff >= 0, np.exp(np.maximum(diff, 0.0)[None] * _LOG_G[:, None, None]), 0.0) * scale
    w_q = np.exp((idx + 1.0)[None, :] * _LOG_G[:, None])
    w_k = np.exp((CHUNK - 1.0 - idx)[None, :] * _LOG_G[:, None]) * scale
    w_q = np.broadcast_to(w_q[:, :, None], (RET_HEADS, CHUNK, RET_HD))
    w_k = np.broadcast_to(w_k[:, :, None], (RET_HEADS, CHUNK, RET_HD))
    g_c = np.exp(CHUNK * _LOG_G)
    pair = lambda t: np.ascontiguousarray(np.concatenate([t[0::2], t[1::2]], axis=2), np.float32)
    return pair(decay), pair(w_q), pair(w_k), [float(v) for v in g_c]


def _swa_bias():
    i = np.arange(WINDOW)[:, None]
    j = np.arange(2 * WINDOW)[None, :]
    dist = i + WINDOW - j
    valid = (dist >= 0) & (dist <= WINDOW)
    out = np.zeros((SWA_KV_HEADS, SWA_REP * WINDOW, 2 * WINDOW), np.float32)
    for g in range(SWA_KV_HEADS):
        for r in range(SWA_REP):
            b = np.where(valid, -_SLOPES[g * SWA_REP + r] * dist, NEG)
            out[g, r * WINDOW:(r + 1) * WINDOW] = b
    return out


def _rmsnorm(x, g):
    ms = jnp.mean(x * x, axis=-1, keepdims=True)
    return (x * lax.rsqrt(ms + EPS)) * g


def _lane_block64(lane):
    return lax.shift_right_logical(lane, 6)


def _cparams(n_axes):
    return pltpu.CompilerParams(dimension_semantics=("arbitrary",) * n_axes, vmem_limit_bytes=VMEM_LIMIT)


def _layer_block(tail, layer, n_grid):
    zeros = (0,) * len(tail)
    if n_grid == 1:
        return pl.BlockSpec((None,) + tuple(tail), lambda i: (layer,) + zeros)
    return pl.BlockSpec((None,) + tuple(tail), lambda i, j: (layer,) + zeros)


def _norm_matmul_kernel(x_ref, g_ref, w_ref, o_ref, *, n_chunk):
    xn = _rmsnorm(x_ref[...], g_ref[...]).astype(BF16)
    n = w_ref.shape[1]
    for n0 in range(0, n, n_chunk):
        n1 = min(n0 + n_chunk, n)
        o_ref[:, n0:n1] = jnp.dot(xn, w_ref[:, n0:n1].astype(BF16), preferred_element_type=F32).astype(o_ref.dtype)


def norm_matmul(x, g, w, layer, out_dtype, tm, n_chunk=512):
    m, k = x.shape
    n = w.shape[2]
    return pl.pallas_call(
        functools.partial(_norm_matmul_kernel, n_chunk=n_chunk),
        grid=(m // tm,),
        in_specs=[pl.BlockSpec((tm, k), lambda i: (i, 0)),
                  _layer_block((1, k), layer, 1),
                  _layer_block((k, n), layer, 1)],
        out_specs=pl.BlockSpec((tm, n), lambda i: (i, 0)),
        out_shape=jax.ShapeDtypeStruct((m, n), out_dtype),
        compiler_params=_cparams(1),
        name="norm_matmul",
    )(x, g.reshape(g.shape[0], 1, k), w)


def _mem_kv_kernel(x_ref, g_ref, w_ref, o_ref, k16_ref, v16_ref):
    xn = _rmsnorm(x_ref[...], g_ref[...]).astype(BF16)
    kv = jnp.dot(xn, w_ref[...].astype(BF16), preferred_element_type=F32)
    o_ref[...] = kv
    k16_ref[...] = kv[:, :MEM_W].astype(BF16)
    v16_ref[...] = kv[:, MEM_W:].astype(BF16)


def mem_kv_proj(x, g, w):
    m, k = x.shape
    n_layers, _, n = w.shape
    half = pl.BlockSpec((None, m, MEM_W), lambda l: (l, 0, 0))
    return pl.pallas_call(
        _mem_kv_kernel,
        grid=(n_layers,),
        in_specs=[pl.BlockSpec((m, k), lambda l: (0, 0)),
                  pl.BlockSpec((None, 1, k), lambda l: (l, 0, 0)),
                  pl.BlockSpec((None, k, n), lambda l: (l, 0, 0))],
        out_specs=[pl.BlockSpec((None, m, n), lambda l: (l, 0, 0)), half, half],
        out_shape=[jax.ShapeDtypeStruct((n_layers, m, n), F32),
                   jax.ShapeDtypeStruct((n_layers, m, MEM_W), BF16),
                   jax.ShapeDtypeStruct((n_layers, m, MEM_W), BF16)],
        compiler_params=_cparams(1),
        name="mem_kv_proj",
    )(x, g.reshape(n_layers, 1, k), w)


def _mlp_rows(x, g_ref, wu_ref, wd_ref, act_ref, fc):
    xn = _rmsnorm(x, g_ref[...]).astype(BF16)
    for c0 in range(0, D_FF, fc):
        h = jnp.dot(xn, wu_ref[:, c0:c0 + fc].astype(BF16), preferred_element_type=F32)
        act_ref[:, c0:c0 + fc] = jnp.square(jnp.maximum(h, 0.0)).astype(BF16)
    return x + jnp.dot(act_ref[...], wd_ref[...].astype(BF16), preferred_element_type=F32)


def _mlp_kernel(x_ref, g_ref, wu_ref, wd_ref, gf_ref, *rest, fc, final_norm, with_kv, with_rows):
    rest = list(rest)
    gkv_ref, wkv_ref = (rest.pop(0), rest.pop(0)) if with_kv else (None, None)
    xs_ref = rest.pop(0) if with_rows else None
    o_ref = rest.pop(0)
    kv_ref, kv16_ref = (rest.pop(0), rest.pop(0)) if with_kv else (None, None)
    os_ref = rest.pop(0) if with_rows else None
    act_ref = rest.pop(0)
    if with_rows:
        acts_ref = rest.pop(0)

        @pl.when(pl.program_id(0) == 0)
        def _():
            ys = _mlp_rows(xs_ref[...], g_ref, wu_ref, wd_ref, acts_ref, fc)
            os_ref[...] = _rmsnorm(ys, gf_ref[...]) if final_norm else ys

    y = _mlp_rows(x_ref[...], g_ref, wu_ref, wd_ref, act_ref, fc)
    if with_kv:
        kv = jnp.dot(_rmsnorm(y, gkv_ref[...]).astype(BF16), wkv_ref[...], preferred_element_type=F32)
        kv_ref[...] = kv
        kv16_ref[...] = kv.astype(BF16)
    if final_norm:
        y = _rmsnorm(y, gf_ref[...])
    o_ref[...] = y


def mlp_block(x, g, w_up, w_down, layer, g_final, final_norm, tm, kv_proj=None, extra_rows=None, fc=512):
    m, d = x.shape
    with_kv = kv_proj is not None
    with_rows = extra_rows is not None
    in_specs = [pl.BlockSpec((tm, d), lambda i: (i, 0)),
                _layer_block((1, d), layer, 1),
                pl.BlockSpec((None, d, D_FF), lambda i: (layer, 0, 0), pipeline_mode=pl.Buffered(1)),
                pl.BlockSpec((None, D_FF, d), lambda i: (layer, 0, 0), pipeline_mode=pl.Buffered(1)),
                pl.BlockSpec((1, d), lambda i: (0, 0))]
    args = [x, g.reshape(g.shape[0], 1, d), w_up, w_down, g_final.reshape(1, d)]
    out_specs = [pl.BlockSpec((tm, d), lambda i: (i, 0))]
    out_shape = [jax.ShapeDtypeStruct((m, d), F32)]
    if with_kv:
        in_specs += [pl.BlockSpec((1, d), lambda i: (0, 0)),
                     pl.BlockSpec((d, 2 * KV_W), lambda i: (0, 0), pipeline_mode=pl.Buffered(1))]
        args += list(kv_proj)
        out_specs += [pl.BlockSpec((tm, 2 * KV_W), lambda i: (i, 0))] * 2
        out_shape += [jax.ShapeDtypeStruct((m, 2 * KV_W), F32), jax.ShapeDtypeStruct((m, 2 * KV_W), BF16)]
    scratch = [pltpu.VMEM((tm, D_FF), BF16)]
    if with_rows:
        rows = extra_rows.shape[0]
        in_specs.append(pl.BlockSpec((rows, d), lambda i: (0, 0)))
        args.append(extra_rows)
        out_specs.append(pl.BlockSpec((rows, d), lambda i: (0, 0)))
        out_shape.append(jax.ShapeDtypeStruct((rows, d), F32))
        scratch.append(pltpu.VMEM((rows, D_FF), BF16))
    res = pl.pallas_call(
        functools.partial(_mlp_kernel, fc=fc, final_norm=final_norm, with_kv=with_kv, with_rows=with_rows),
        grid=(m // tm,),
        in_specs=in_specs,
        out_specs=out_specs,
        out_shape=out_shape,
        scratch_shapes=scratch,
        compiler_params=_cparams(1),
        name="mlp_block",
    )(*args)
    return res if len(res) > 1 else res[0]


def _out_residual_kernel(a_ref, w_ref, x_ref, o_ref):
    o_ref[...] = x_ref[...] + jnp.dot(a_ref[...], w_ref[...].astype(BF16), preferred_element_type=F32)


def out_residual(a, w, layer, x):
    m, k = a.shape
    n = w.shape[2]
    return pl.pallas_call(
        _out_residual_kernel,
        grid=(1,),
        in_specs=[pl.BlockSpec((m, k), lambda i: (0, 0)),
                  _layer_block((k, n), layer, 1),
                  pl.BlockSpec((m, n), lambda i: (0, 0))],
        out_specs=pl.BlockSpec((m, n), lambda i: (0, 0)),
        out_shape=jax.ShapeDtypeStruct((m, n), F32),
        compiler_params=_cparams(1),
        name="out_residual",
    )(a, w, x)


def _mem_attention_tile(qm, mk, mv, fillers=()):
    t = qm.shape[0]
    lane_head = _lane_block64(lax.broadcasted_iota(jnp.int32, (t, MEM_W), 1))
    tok_head = _lane_block64(lax.broadcasted_iota(jnp.int32, (MEM_TOKENS, MEM_W), 1))
    probs = []
    vals = []
    for h in range(MEM_HEADS):
        qh = jnp.where(lane_head == h, qm, jnp.zeros_like(qm))
        s = lax.dot_general(qh, mk, NT_DIMS, preferred_element_type=F32)
        if h < len(fillers):
            fillers[h]()
        m = jnp.max(s, axis=-1, keepdims=True)
        e = jnp.exp(s - m)
        inv = 1.0 / jnp.sum(e, axis=-1, keepdims=True)
        probs.append((e * inv).astype(BF16))
        vals.append(jnp.where(tok_head == h, mv, jnp.zeros_like(mv)))
    return jnp.dot(jnp.concatenate(probs, axis=1), jnp.concatenate(vals, axis=0), preferred_element_type=F32)


def _project_tile(x_ref, g_ref, w_ref, p_ref, n_chunk):
    xn = _rmsnorm(x_ref[...], g_ref[...]).astype(BF16)
    n = w_ref.shape[1]
    for n0 in range(0, n, n_chunk):
        n1 = min(n0 + n_chunk, n)
        p_ref[:, n0:n1] = jnp.dot(xn, w_ref[:, n0:n1].astype(BF16), preferred_element_type=F32).astype(BF16)


def _projection_pieces(x_ref, g_ref, w_ref, p_ref):
    xn = _rmsnorm(x_ref[...], g_ref[...]).astype(BF16)

    def piece(c):
        def run():
            cols = slice(c * PIECE, (c + 1) * PIECE)
            p_ref[:, cols] = jnp.dot(xn, w_ref[:, cols].astype(BF16), preferred_element_type=F32).astype(BF16)
        return run

    return [piece(c) for c in range(w_ref.shape[1] // PIECE)]


def _out_projection_piece(x_ref, a_ref, w_ref, o_ref, c):
    def run():
        cols = slice(c * PIECE, (c + 1) * PIECE)
        o_ref[:, cols] = x_ref[:, cols] + jnp.dot(a_ref[:, :MIX_MAIN], w_ref[:MIX_MAIN, cols].astype(BF16),
                                                  preferred_element_type=F32)
    return run


def _both(f, g):
    def run():
        f()
        g()
    return run


def _block_diag2(a, b):
    z = jnp.zeros_like(a)
    return jnp.concatenate([jnp.concatenate([a, z], axis=1), jnp.concatenate([z, b], axis=1)], axis=0)


def _group_norm_gate(o, gate):
    mu = jnp.mean(o, axis=-1, keepdims=True)
    oc = o - mu
    var = jnp.mean(oc * oc, axis=-1, keepdims=True)
    return (gate * jax.nn.sigmoid(gate) * (oc * lax.rsqrt(var + EPS))).astype(BF16)


def _retention_tile(p_ref, a_ref, s_scr, decay_ref, wq_ref, wk_ref, g_c, fillers=()):
    pw = 2 * RET_HD
    pairs = range(RET_HEADS // 2)
    for c in range(TILE // CHUNK):
        rows = slice(c * CHUNK, (c + 1) * CHUNK)
        q2 = [p_ref[rows, hp * pw:(hp + 1) * pw] for hp in pairs]
        k2 = [p_ref[rows, MIX_MAIN + hp * pw:MIX_MAIN + (hp + 1) * pw] for hp in pairs]
        v2 = [p_ref[rows, 2 * MIX_MAIN + hp * pw:2 * MIX_MAIN + (hp + 1) * pw] for hp in pairs]
        sc = [lax.dot_general(q2[hp], _block_diag2(k2[hp][:, :RET_HD], k2[hp][:, RET_HD:]), NT_DIMS,
                              preferred_element_type=F32) for hp in pairs]
        kw = [(k2[hp].astype(F32) * wk_ref[hp]).astype(BF16) for hp in pairs]
        kv = [lax.dot_general(kw[hp], v2[hp], TN_DIMS, preferred_element_type=F32) for hp in pairs]
        if 2 * c < len(fillers):
            fillers[2 * c]()
        s_old = [s_scr[h] for h in range(RET_HEADS)]
        lhs = [jnp.concatenate([(sc[hp] * decay_ref[hp]).astype(BF16),
                                (q2[hp].astype(F32) * wq_ref[hp]).astype(BF16)], axis=1) for hp in pairs]
        rhs = [jnp.concatenate([_block_diag2(v2[hp][:, :RET_HD], v2[hp][:, RET_HD:]),
                                _block_diag2(s_old[2 * hp].astype(BF16), s_old[2 * hp + 1].astype(BF16))], axis=0)
               for hp in pairs]
        o2 = [jnp.dot(lhs[hp], rhs[hp], preferred_element_type=F32) for hp in pairs]
        if 2 * c + 1 < len(fillers):
            fillers[2 * c + 1]()
        for hp in pairs:
            s_scr[2 * hp] = g_c[2 * hp] * s_old[2 * hp] + kv[hp][:RET_HD, :RET_HD]
            s_scr[2 * hp + 1] = g_c[2 * hp + 1] * s_old[2 * hp + 1] + kv[hp][RET_HD:, RET_HD:]
        for hp in pairs:
            lo = hp * pw
            gate2 = p_ref[rows, 3 * MIX_MAIN + lo:3 * MIX_MAIN + lo + pw].astype(F32)
            a_ref[rows, lo:lo + RET_HD] = _group_norm_gate(o2[hp][:, :RET_HD], gate2[:, :RET_HD])
            a_ref[rows, lo + RET_HD:lo + pw] = _group_norm_gate(o2[hp][:, RET_HD:], gate2[:, RET_HD:])


def _fused_a_kernel(xc_ref, xn_ref, g_ref, win_ref, wout_ref, mk_ref, mv_ref, decay_ref, wq_ref, wk_ref, xs_ref,
                    o_ref, st_ref, ps_ref, p0, p1, a_scr, s_scr, *, g_c, tiles_per_seq):
    t = pl.program_id(0)

    @pl.when(t == 0)
    def _():
        _project_tile(xc_ref, g_ref, win_ref, p0, MIX_MAIN)
        xsn = _rmsnorm(xs_ref[...], g_ref[...]).astype(BF16)
        for n0 in range(0, win_ref.shape[1], PIECE):
            ps_ref[:, n0:n0 + PIECE] = jnp.dot(xsn, win_ref[:, n0:n0 + PIECE].astype(BF16),
                                               preferred_element_type=F32)

    @pl.when(t % tiles_per_seq == 0)
    def _():
        s_scr[...] = jnp.zeros_like(s_scr)

    def step(p_cur, p_next):
        pieces = _projection_pieces(xn_ref, g_ref, win_ref, p_next)
        n_slots = 2 * (TILE // CHUNK)
        slots = [_both(pieces[c], pieces[n_slots + c]) if n_slots + c < len(pieces) else pieces[c]
                 for c in range(n_slots)]
        _retention_tile(p_cur, a_scr, s_scr, decay_ref, wq_ref, wk_ref, g_c, slots)
        st_ref[...] = s_scr[...]
        qm = p_cur[:, 4 * MIX_MAIN:4 * MIX_MAIN + MEM_W] * ATTN_SCALE
        out_piece = functools.partial(_out_projection_piece, xc_ref, a_scr, wout_ref, o_ref)
        a_mem = _mem_attention_tile(qm, mk_ref[...], mv_ref[...],
                                    [out_piece(c) for c in range(D_MODEL // PIECE)]).astype(BF16)
        o_ref[...] += jnp.dot(a_mem, wout_ref[MIX_MAIN:, :].astype(BF16), preferred_element_type=F32)

    @pl.when(t % 2 == 0)
    def _():
        step(p0, p1)

    @pl.when(t % 2 == 1)
    def _():
        step(p1, p0)


def fused_mixer_a(x, g, w_in, w_out, mem_k, mem_v, layer, batch, seq, extra_rows):
    rows = extra_rows.shape[0]
    n_tiles = batch * seq // TILE
    tps = seq // TILE
    decay, w_q, w_k, g_c = _retention_consts()
    pw = w_in.shape[2]
    c3 = lambda t: (0, 0, 0)
    mem_spec = pl.BlockSpec((None, None, MEM_TOKENS, MEM_W), lambda t: (layer, t // tps, 0, 0))
    return pl.pallas_call(
        functools.partial(_fused_a_kernel, g_c=g_c, tiles_per_seq=tps),
        grid=(n_tiles,),
        in_specs=[pl.BlockSpec((TILE, D_MODEL), lambda t: (t, 0)),
                  pl.BlockSpec((TILE, D_MODEL), lambda t: (jnp.minimum(t + 1, n_tiles - 1), 0)),
                  _layer_block((1, D_MODEL), layer, 1),
                  pl.BlockSpec((None, D_MODEL, pw), lambda t: (layer, 0, 0), pipeline_mode=pl.Buffered(1)),
                  pl.BlockSpec((None, D_MODEL, D_MODEL), lambda t: (layer, 0, 0), pipeline_mode=pl.Buffered(1)),
                  mem_spec, mem_spec,
                  pl.BlockSpec((RET_HEADS // 2, CHUNK, 2 * CHUNK), c3),
                  pl.BlockSpec((RET_HEADS // 2, CHUNK, 2 * RET_HD), c3),
                  pl.BlockSpec((RET_HEADS // 2, CHUNK, 2 * RET_HD), c3),
                  pl.BlockSpec((rows, D_MODEL), lambda t: (0, 0))],
        out_specs=[pl.BlockSpec((TILE, D_MODEL), lambda t: (t, 0)),
                   pl.BlockSpec((None, RET_HEADS, RET_HD, RET_HD), lambda t: (t // tps, 0, 0, 0)),
                   pl.BlockSpec((rows, pw), lambda t: (0, 0))],
        out_shape=[jax.ShapeDtypeStruct((batch * seq, D_MODEL), F32),
                   jax.ShapeDtypeStruct((batch, RET_HEADS, RET_HD, RET_HD), F32),
                   jax.ShapeDtypeStruct((rows, pw), F32)],
        scratch_shapes=[pltpu.VMEM((TILE, pw), BF16), pltpu.VMEM((TILE, pw), BF16),
                        pltpu.VMEM((TILE, D_MODEL), BF16),
                        pltpu.VMEM((RET_HEADS, RET_HD, RET_HD), F32)],
        compiler_params=_cparams(1),
        name="fused_mixer_a",
    )(x, x, g.reshape(g.shape[0], 1, D_MODEL), w_in, w_out, mem_k, mem_v,
      jnp.asarray(decay), jnp.asarray(w_q), jnp.asarray(w_k), extra_rows)


def _swa_tile(p_ref, kvp_ref, kvc_ref, a_ref, bias_ref, sink_ref, is_first, fillers=()):
    lane_group = _lane_block64(lax.broadcasted_iota(jnp.int32, (WINDOW, KV_W), 1))
    key_group = _lane_block64(lax.broadcasted_iota(jnp.int32, (2 * WINDOW, KV_W), 1))
    for i in range(TILE // WINDOW):
        r0 = i * WINDOW
        if i == 0:
            kv2 = jnp.concatenate([kvp_ref[...], kvc_ref[0:WINDOW, :]], axis=0)
        else:
            kv2 = kvc_ref[r0 - WINDOW:r0 + WINDOW, :]
        k2 = kv2[:, :KV_W]
        v2 = kv2[:, KV_W:]
        qs = [p_ref[r0:r0 + WINDOW, r * KV_W:(r + 1) * KV_W] for r in range(SWA_REP)]
        probs = []
        vals = []
        for g in range(SWA_KV_HEADS):
            sel = lane_group == g
            qg = jnp.concatenate([jnp.where(sel, qr, jnp.zeros_like(qr)) for qr in qs], axis=0)
            s = lax.dot_general(qg, k2, NT_DIMS, preferred_element_type=F32) + bias_ref[g]
            if g == 1 and i < len(fillers):
                fillers[i]()
            s_prev = s[:, :WINDOW]
            s_cur = s[:, WINDOW:]
            if i == 0:
                s_prev = jnp.where(is_first, NEG, s_prev)
            sink = sink_ref[g]
            m = jnp.maximum(jnp.max(jnp.maximum(s_prev, s_cur), axis=-1, keepdims=True), sink)
            e_prev = jnp.exp(s_prev - m)
            e_cur = jnp.exp(s_cur - m)
            inv = 1.0 / (jnp.sum(e_prev + e_cur, axis=-1, keepdims=True) + jnp.exp(sink - m))
            probs.append((e_prev * inv).astype(BF16))
            probs.append((e_cur * inv).astype(BF16))
            vals.append(jnp.where(key_group == g, v2, jnp.zeros_like(v2)))
        o = jnp.dot(jnp.concatenate(probs, axis=1), jnp.concatenate(vals, axis=0), preferred_element_type=F32)
        for r in range(SWA_REP):
            a_ref[r0:r0 + WINDOW, r * KV_W:(r + 1) * KV_W] = o[r * WINDOW:(r + 1) * WINDOW].astype(BF16)


def _fused_b_kernel(xc_ref, xn_ref, g_ref, win_ref, wout_ref, kvp_ref, kvc_ref, mk_ref, mv_ref, bias_ref, sink_ref,
                    xs_ref, o_ref, ps_ref, p0, p1, a_scr, *, tiles_per_seq):
    t = pl.program_id(0)
    is_first = t % tiles_per_seq == 0

    @pl.when(t == 0)
    def _():
        _project_tile(xc_ref, g_ref, win_ref, p0, 2 * PIECE)
        xsn = _rmsnorm(xs_ref[...], g_ref[...]).astype(BF16)
        ps_ref[...] = jnp.dot(xsn, win_ref[...].astype(BF16), preferred_element_type=F32)

    out_piece = functools.partial(_out_projection_piece, xc_ref, a_scr, wout_ref, o_ref)

    def step(p_cur, p_next):
        pieces = _projection_pieces(xn_ref, g_ref, win_ref, p_next)
        _swa_tile(p_cur, kvp_ref, kvc_ref, a_scr, bias_ref, sink_ref, is_first, pieces)
        qm = p_cur[:, MIX_MAIN:MIX_MAIN + MEM_W]
        a_mem = _mem_attention_tile(qm, mk_ref[...], mv_ref[...],
                                    [out_piece(c) for c in range(D_MODEL // PIECE)]).astype(BF16)
        o_ref[...] += jnp.dot(a_mem, wout_ref[MIX_MAIN:, :], preferred_element_type=F32)

    @pl.when(t % 2 == 0)
    def _():
        step(p0, p1)

    @pl.when(t % 2 == 1)
    def _():
        step(p1, p0)


def fused_mixer_b(x, g, w_in, w_out, kv, mem_k, mem_v, layer, sink_rows, sub_layer, batch, seq, extra_rows):
    rows = extra_rows.shape[0]
    n_tiles = batch * seq // TILE
    tps = seq // TILE
    nw = TILE // WINDOW
    bias = jnp.asarray(_swa_bias())
    mem_spec = pl.BlockSpec((None, None, MEM_TOKENS, MEM_W), lambda t: (layer, t // tps, 0, 0))

    def prev_map(t):
        return (jnp.maximum(t * nw - 1, (t // tps) * (seq // WINDOW)), 0)

    return pl.pallas_call(
        functools.partial(_fused_b_kernel, tiles_per_seq=tps),
        grid=(n_tiles,),
        in_specs=[pl.BlockSpec((TILE, D_MODEL), lambda t: (t, 0)),
                  pl.BlockSpec((TILE, D_MODEL), lambda t: (jnp.minimum(t + 1, n_tiles - 1), 0)),
                  _layer_block((1, D_MODEL), layer, 1),
                  pl.BlockSpec((None, D_MODEL, D_MODEL), lambda t: (sub_layer, 0, 0), pipeline_mode=pl.Buffered(1)),
                  pl.BlockSpec((None, D_MODEL, D_MODEL), lambda t: (sub_layer, 0, 0), pipeline_mode=pl.Buffered(1)),
                  pl.BlockSpec((WINDOW, 2 * KV_W), prev_map),
                  pl.BlockSpec((TILE, 2 * KV_W), lambda t: (t, 0)),
                  mem_spec, mem_spec,
                  pl.BlockSpec((SWA_KV_HEADS, SWA_REP * WINDOW, 2 * WINDOW), lambda t: (0, 0, 0)),
                  pl.BlockSpec((None, SWA_KV_HEADS, SWA_REP * WINDOW, LANES), lambda t: (sub_layer, 0, 0, 0)),
                  pl.BlockSpec((rows, D_MODEL), lambda t: (0, 0))],
        out_specs=[pl.BlockSpec((TILE, D_MODEL), lambda t: (t, 0)),
                   pl.BlockSpec((rows, D_MODEL), lambda t: (0, 0))],
        out_shape=[jax.ShapeDtypeStruct((batch * seq, D_MODEL), F32),
                   jax.ShapeDtypeStruct((rows, D_MODEL), F32)],
        scratch_shapes=[pltpu.VMEM((TILE, D_MODEL), BF16), pltpu.VMEM((TILE, D_MODEL), BF16),
                        pltpu.VMEM((TILE, D_MODEL), BF16)],
        compiler_params=_cparams(1),
        name="fused_mixer_b",
    )(x, x, g.reshape(g.shape[0], 1, D_MODEL), w_in, w_out, kv, kv, mem_k, mem_v, bias, sink_rows, extra_rows)


def _ret_decode_kernel(proj_ref, st_ref, *rest, bb, layer, first_call):
    o_ref, stacked_ref = rest[-2:]
    if first_call:
        nst_ref = stacked_ref.at[layer]
        for other in range(stacked_ref.shape[0]):
            if other != layer:
                stacked_ref[other] = jnp.zeros(stacked_ref.shape[1:], F32)
    else:
        nst_ref = stacked_ref
    scale = RET_HD ** -0.5
    row = lax.broadcasted_iota(jnp.int32, (bb, bb * RET_HD), 0)
    lane_seq = lax.shift_right_logical(lax.broadcasted_iota(jnp.int32, (bb, bb * RET_HD), 1), 7)
    own = row == lane_seq
    for h in range(RET_HEADS):
        lo = h * RET_HD
        q8 = proj_ref[:, lo:lo + RET_HD]
        k8 = proj_ref[:, MIX_MAIN + lo:MIX_MAIN + lo + RET_HD] * scale
        v8 = proj_ref[:, 2 * MIX_MAIN + lo:2 * MIX_MAIN + lo + RET_HD]
        gate = proj_ref[:, 3 * MIX_MAIN + lo:3 * MIX_MAIN + lo + RET_HD]
        g = float(np.exp(_LOG_G[h]))
        vexp = jnp.where(own, jnp.concatenate([v8] * bb, axis=1), 0.0).astype(BF16)
        outer = lax.dot_general(k8.astype(BF16), vexp, TN_DIMS, preferred_element_type=F32)
        cols = []
        for j in range(bb):
            s_new = g * st_ref[j, h] + outer[:, j * RET_HD:(j + 1) * RET_HD]
            nst_ref[j, h] = s_new
            cols.append(s_new.astype(BF16))
        o_all = jnp.dot(q8.astype(BF16), jnp.concatenate(cols, axis=1), preferred_element_type=F32)
        o_all = jnp.where(own, o_all, 0.0)
        o = o_all[:, :RET_HD]
        for j in range(1, bb):
            o = o + o_all[:, j * RET_HD:(j + 1) * RET_HD]
        mu = jnp.mean(o, axis=-1, keepdims=True)
        oc = o - mu
        var = jnp.mean(oc * oc, axis=-1, keepdims=True)
        on = oc * lax.rsqrt(var + EPS)
        o_ref[:, lo:lo + RET_HD] = gate * jax.nn.sigmoid(gate) * on


def ret_decode(proj, state, layer, stacked_out=None, bb=16):
    nb = proj.shape[0]
    pw = proj.shape[1]
    n_layers = state.shape[0]
    tail = (RET_HEADS, RET_HD, RET_HD)
    in_specs = [pl.BlockSpec((bb, pw), lambda i: (i, 0)),
                pl.BlockSpec((None, bb) + tail, lambda i: (layer, i, 0, 0, 0))]
    args = [proj, state]
    first_call = stacked_out is None
    if first_call:
        aliases = {}
        stacked_spec = pl.BlockSpec((n_layers, bb) + tail, lambda i: (0, i, 0, 0, 0))
    else:
        in_specs.append(pl.BlockSpec(memory_space=pl.ANY))
        args.append(stacked_out)
        aliases = {2: 1}
        stacked_spec = pl.BlockSpec((None, bb) + tail, lambda i: (layer, i, 0, 0, 0))
    return pl.pallas_call(
        functools.partial(_ret_decode_kernel, bb=bb, layer=layer, first_call=first_call),
        grid=(nb // bb,),
        in_specs=in_specs,
        out_specs=[pl.BlockSpec((bb, MIX_MAIN), lambda i: (i, 0)), stacked_spec],
        out_shape=[jax.ShapeDtypeStruct((nb, MIX_MAIN), F32),
                   jax.ShapeDtypeStruct(state.shape, F32)],
        input_output_aliases=aliases,
        compiler_params=_cparams(1),
        name="ret_decode",
    )(*args)


def _mem_decode_kernel(q_ref, kt_ref, vt_ref, o_ref, *, bb, q_off, q_scale):
    rows = SUBLANES
    row = lax.broadcasted_iota(jnp.int32, (rows, MEM_W), 0)
    lane_head = _lane_block64(lax.broadcasted_iota(jnp.int32, (rows, MEM_W), 1))
    sel = row == lane_head
    seqs = range(bb)
    qexp = [jnp.where(sel, jnp.broadcast_to(q_ref[j:j + 1, q_off:q_off + MEM_W] * q_scale, (rows, MEM_W)), 0.0)
            for j in seqs]
    s = [jnp.dot(qexp[j], kt_ref[j], preferred_element_type=F32) for j in seqs]
    e = [jnp.exp(s[j] - jnp.max(s[j], axis=-1, keepdims=True)) for j in seqs]
    den = [jnp.sum(e[j], axis=-1, keepdims=True) for j in seqs]
    o = [lax.dot_general(e[j], vt_ref[j], NT_DIMS, preferred_element_type=F32) / den[j] for j in seqs]
    o_ref[...] = jnp.concatenate([jnp.sum(jnp.where(sel, o[j], 0.0), axis=0, keepdims=True) for j in seqs], axis=0)


def mem_decode(proj, cache_kt, cache_vt, layer, q_off, q_scale, bb=32):
    nb = proj.shape[0]
    pw = proj.shape[1]
    cmap = lambda i: (layer, i, 0, 0)
    return pl.pallas_call(
        functools.partial(_mem_decode_kernel, bb=bb, q_off=q_off, q_scale=q_scale),
        grid=(nb // bb,),
        in_specs=[pl.BlockSpec((bb, pw), lambda i: (i, 0)),
                  pl.BlockSpec((None, bb, MEM_W, MEM_TOKENS), cmap),
                  pl.BlockSpec((None, bb, MEM_W, MEM_TOKENS), cmap)],
        out_specs=pl.BlockSpec((bb, MEM_W), lambda i: (i, 0)),
        out_shape=jax.ShapeDtypeStruct((nb, MEM_W), F32),
        compiler_params=_cparams(1),
        name="mem_decode",
    )(proj, cache_kt, cache_vt)


def _swa_decode_kernel(q_ref, kn_ref, vn_ref, ckt_ref, cvt_ref, slope_ref, sink_ref, o_ref, *cache_out, bb):
    row = lax.broadcasted_iota(jnp.int32, (DEC_ROWS, KV_W), 0)
    lane_group = _lane_block64(lax.broadcasted_iota(jnp.int32, (DEC_ROWS, KV_W), 1))
    dist = (WINDOW - lax.broadcasted_iota(jnp.int32, (DEC_ROWS, WINDOW), 1)).astype(F32)
    bias = -slope_ref[...] * dist
    sink = sink_ref[...]
    own = [row == SWA_KV_HEADS * r + lane_group for r in range(SWA_REP)]
    seqs = range(bb)

    def expand(j):
        qexp = jnp.zeros((DEC_ROWS, KV_W), F32)
        for r in range(SWA_REP):
            qr = jnp.broadcast_to(q_ref[j:j + 1, r * KV_W:(r + 1) * KV_W], (DEC_ROWS, KV_W))
            qexp = jnp.where(own[r], qr, qexp)
        return qexp

    qexp = [expand(j) for j in seqs]
    s = [jnp.dot(qexp[j], ckt_ref[j], preferred_element_type=F32) + bias for j in seqs]
    s_new = [jnp.sum(qexp[j] * kn_ref[j:j + 1, :], axis=-1, keepdims=True) for j in seqs]
    m = [jnp.maximum(jnp.maximum(jnp.max(s[j], axis=-1, keepdims=True), s_new[j]), sink) for j in seqs]
    e = [jnp.exp(s[j] - m[j]) for j in seqs]
    e_new = [jnp.exp(s_new[j] - m[j]) for j in seqs]
    inv = [1.0 / (jnp.sum(e[j], axis=-1, keepdims=True) + e_new[j] + jnp.exp(sink - m[j])) for j in seqs]
    o = [lax.dot_general(e[j], cvt_ref[j], NT_DIMS, preferred_element_type=F32) for j in seqs]
    o = [(o[j] + jnp.concatenate([e_new[j], e_new[j]], axis=1) * vn_ref[j:j + 1, :])
         * jnp.concatenate([inv[j], inv[j]], axis=1) for j in seqs]
    for r in range(SWA_REP):
        o_ref[:, r * KV_W:(r + 1) * KV_W] = jnp.concatenate(
            [jnp.sum(jnp.where(own[r], o[j], 0.0), axis=0, keepdims=True) for j in seqs], axis=0)
    if cache_out:
        nkt_ref, nvt_ref = cache_out
        last = lax.broadcasted_iota(jnp.int32, (KV_W, WINDOW), 1) == WINDOW - 1
        knt = kn_ref[...].T
        vnt = vn_ref[...].T
        for j in seqs:
            nkt_ref[j] = jnp.where(last, jnp.broadcast_to(knt[:, j:j + 1], (KV_W, WINDOW)),
                                   pltpu.roll(ckt_ref[j], WINDOW - 1, 1))
            nvt_ref[j] = jnp.where(last, jnp.broadcast_to(vnt[:, j:j + 1], (KV_W, WINDOW)),
                                   pltpu.roll(cvt_ref[j], WINDOW - 1, 1))


def swa_decode(qproj, k_new, v_new, cache_kt, cache_vt, slope_rows, sink_rows, sink_layer, emit_cache, bb=32):
    nb = qproj.shape[0]
    pw = qproj.shape[1]
    cache_spec = pl.BlockSpec((bb, KV_W, WINDOW), lambda i: (i, 0, 0))
    cache_shape = jax.ShapeDtypeStruct((nb, KV_W, WINDOW), F32)
    n_cache = 2 if emit_cache else 0
    return pl.pallas_call(
        functools.partial(_swa_decode_kernel, bb=bb),
        grid=(nb // bb,),
        in_specs=[pl.BlockSpec((bb, pw), lambda i: (i, 0)),
                  pl.BlockSpec((bb, KV_W), lambda i: (i, 0)),
                  pl.BlockSpec((bb, KV_W), lambda i: (i, 0)),
                  cache_spec, cache_spec,
                  pl.BlockSpec((DEC_ROWS, LANES), lambda i: (0, 0)),
                  pl.BlockSpec((None, DEC_ROWS, LANES), lambda i: (sink_layer, 0, 0))],
        out_specs=[pl.BlockSpec((bb, MIX_MAIN), lambda i: (i, 0))] + [cache_spec] * n_cache,
        out_shape=[jax.ShapeDtypeStruct((nb, MIX_MAIN), F32)] + [cache_shape] * n_cache,
        compiler_params=_cparams(1),
        name="swa_decode",
    )(qproj, k_new, v_new, cache_kt, cache_vt, slope_rows, sink_rows)


def _slot_source(t):
    r, g = divmod(t, SWA_KV_HEADS)
    return SWA_REP * g + r


def _prep_w_in_b_kernel(w_ref, o_ref):
    low = lax.broadcasted_iota(jnp.int32, (D_MODEL, LANES), 1) < SWA_HD

    def half(slot, want_high):
        col = w_ref[:, (slot // 2) * LANES:(slot // 2 + 1) * LANES]
        return pltpu.roll(col, SWA_HD, 1) if (slot % 2 == 1) != want_high else col

    for c in range(MIX_MAIN // LANES):
        tile = jnp.where(low, half(_slot_source(2 * c), False), half(_slot_source(2 * c + 1), True))
        o_ref[:, c * LANES:(c + 1) * LANES] = (tile * ATTN_SCALE).astype(BF16)
    o_ref[:, MIX_MAIN:] = (w_ref[:, MIX_MAIN:] * ATTN_SCALE).astype(BF16)


def _prep_w_out_b_kernel(w_ref, o_ref):
    for t in range(SWA_HEADS):
        s = _slot_source(t)
        o_ref[t * SWA_HD:(t + 1) * SWA_HD, :] = w_ref[s * SWA_HD:(s + 1) * SWA_HD, :].astype(BF16)
    o_ref[MIX_MAIN:, :] = w_ref[MIX_MAIN:, :].astype(BF16)


def _prep_layer_b_weights(w, body, name):
    n_layers = w.shape[0]
    spec = pl.BlockSpec((None, D_MODEL, D_MODEL), lambda l: (l, 0, 0))
    return pl.pallas_call(
        body, grid=(n_layers,), in_specs=[spec], out_specs=spec,
        out_shape=jax.ShapeDtypeStruct(w.shape, BF16), compiler_params=_cparams(1), name=name,
    )(w)


def kernel(x_prompt, x_sample, cache_mem_k, cache_mem_v, state_ret, cache_swa_k, cache_swa_v, mem_prompt,
           norm_mix, w_in_a, w_out_a, w_in_b, w_out_b, attn_sinks, norm_mem, w_mem_kv, norm_kv, w_kv,
           norm_mlp, w_up, w_down, norm_final):
    batch, seq, d = x_prompt.shape
    nb = x_sample.shape[0]
    tm = TILE

    w_in_b16 = _prep_layer_b_weights(w_in_b, _prep_w_in_b_kernel, "prep_w_in_b")
    w_out_b16 = _prep_layer_b_weights(w_out_b, _prep_w_out_b_kernel, "prep_w_out_b")
    w_kv16 = w_kv.astype(BF16).reshape(1, d, 2 * KV_W)
    norm_kv1 = norm_kv.reshape(1, d)

    sinks_gr = attn_sinks.reshape(N_B, SWA_KV_HEADS, SWA_REP)
    sink_prompt = jnp.broadcast_to(sinks_gr[:, :, :, None, None], (N_B, SWA_KV_HEADS, SWA_REP, WINDOW, LANES)
                                   ).reshape(N_B, SWA_KV_HEADS, SWA_REP * WINDOW, LANES)
    sink_dec = jnp.concatenate([sinks_gr.swapaxes(1, 2).reshape(N_B, SWA_HEADS),
                                jnp.zeros((N_B, DEC_ROWS - SWA_HEADS), F32)], axis=1)
    sink_dec = jnp.broadcast_to(sink_dec[:, :, None], (N_B, DEC_ROWS, LANES))
    slope_dec = np.zeros((DEC_ROWS, LANES), np.float32)
    for r in range(SWA_REP):
        for g in range(SWA_KV_HEADS):
            slope_dec[SWA_KV_HEADS * r + g, :] = _SLOPES[g * SWA_REP + r]
    slope_dec = jnp.asarray(slope_dec)

    memx = mem_prompt.reshape(batch * MEM_TOKENS, d)
    mem_kv, mk16, mv16 = mem_kv_proj(memx, norm_mem, w_mem_kv)
    mem_kv = mem_kv.reshape(DEPTH, batch, MEM_TOKENS, 2 * MEM_W)
    mem_k_prompt = mem_kv[..., :MEM_W].reshape(DEPTH, batch, MEM_TOKENS, MEM_HEADS, MEM_HD)
    mem_v_prompt = mem_kv[..., MEM_W:].reshape(DEPTH, batch, MEM_TOKENS, MEM_HEADS, MEM_HD)
    mk16 = mk16.reshape(DEPTH, batch, MEM_TOKENS, MEM_W)
    mv16 = mv16.reshape(DEPTH, batch, MEM_TOKENS, MEM_W)

    cmkt = jnp.transpose(cache_mem_k, (0, 1, 3, 4, 2)).reshape(DEPTH, nb, MEM_W, MEM_TOKENS)
    cmvt = jnp.transpose(cache_mem_v, (0, 1, 3, 4, 2)).reshape(DEPTH, nb, MEM_W, MEM_TOKENS)
    cskt = jnp.transpose(cache_swa_k, (0, 2, 3, 1)).reshape(nb, KV_W, WINDOW)
    csvt = jnp.transpose(cache_swa_v, (0, 2, 3, 1)).reshape(nb, KV_W, WINDOW)

    x = x_prompt.reshape(batch * seq, d)
    xs = x_sample.reshape(nb, d)
    ret_states = []
    ret_sample = None
    kv_p = kv_p16 = k_new = v_new = nkt = nvt = None
    for l in range(DEPTH):
        j = l - N_A
        if l < N_A:
            x, st, proj = fused_mixer_a(x, norm_mix, w_in_a, w_out_a, mk16, mv16, l, batch, seq, xs)
            ret_states.append(st)
        else:
            x, qproj = fused_mixer_b(x, norm_mix, w_in_b16, w_out_b16, kv_p16, mk16, mv16, l, sink_prompt, j, batch,
                                     seq, xs)
        if l < N_A:
            o_ret, ret_sample = ret_decode(proj, state_ret, l, stacked_out=ret_sample)
            o_mem = mem_decode(proj, cmkt, cmvt, l, 4 * MIX_MAIN, ATTN_SCALE)
            attn = jnp.concatenate([o_ret, o_mem], axis=-1).astype(BF16)
            xs = out_residual(attn, w_out_a, l, xs)
        else:
            if j == 0:
                kv_s = norm_matmul(xs, norm_kv1, w_kv16, 0, F32, tm=nb)
                k_new, v_new = kv_s[:, :KV_W], kv_s[:, KV_W:]
            res = swa_decode(qproj, k_new, v_new, cskt, csvt, slope_dec, sink_dec, j, emit_cache=(j == 0))
            o_swa = res[0]
            if j == 0:
                nkt, nvt = res[1], res[2]
            o_mem = mem_decode(qproj, cmkt, cmvt, l, MIX_MAIN, 1.0)
            attn = jnp.concatenate([o_swa, o_mem], axis=-1).astype(BF16)
            xs = out_residual(attn, w_out_b16, j, xs)
        if l == N_A - 1:
            x, kv_p, kv_p16, xs = mlp_block(x, norm_mlp, w_up, w_down, l, norm_final, False, tm=tm,
                                            kv_proj=(norm_kv1, w_kv16[0]), extra_rows=xs)
        else:
            x, xs = mlp_block(x, norm_mlp, w_up, w_down, l, norm_final, l == DEPTH - 1, tm=tm, extra_rows=xs)
    y_prompt = x.reshape(batch, seq, d)
    ret_prompt = jnp.stack(ret_states)
    kv_p3 = kv_p.reshape(batch, seq, 2 * KV_W)
    swa_k_prompt = kv_p3[:, -WINDOW:, :KV_W].reshape(batch, WINDOW, SWA_KV_HEADS, SWA_HD)
    swa_v_prompt = kv_p3[:, -WINDOW:, KV_W:].reshape(batch, WINDOW, SWA_KV_HEADS, SWA_HD)
    y_sample = xs.reshape(nb, 1, d)
    swa_k_sample = jnp.transpose(nkt.reshape(nb, SWA_KV_HEADS, SWA_HD, WINDOW), (0, 3, 1, 2))
    swa_v_sample = jnp.transpose(nvt.reshape(nb, SWA_KV_HEADS, SWA_HD, WINDOW), (0, 3, 1, 2))

    return (y_prompt, y_sample, ret_prompt, ret_sample, swa_k_prompt, swa_v_prompt, swa_k_sample, swa_v_sample,
            mem_k_prompt, mem_v_prompt)
```
